```python
import math
import jax, jax.numpy as jnp
from jax import lax
import numpy as np

D_MODEL = 2048
BATCH = 4
SEQ = 2048
DEPTH = 4
DEC_BATCH = 128
DEC_SEQ = 4
PAST_LEN = 16384
PAGE_SIZE = 128

N_EVEN = (DEPTH + 1) // 2
N_ODD = DEPTH // 2
CHUNK = 64
EPS = 1e-6

H_A = 32
P_A = 64
G_A = 4
N_A = 128
CONV_K_A = 4
D_A = H_A * P_A
CONV_DIM_A = D_A + 2 * G_A * N_A
H_B = 8
DK_B = 128
DV_B = 256
D_B = H_B * DV_B
H_C = 16
DK_C = 128
DV_C = 128
D_C = H_C * DV_C
D_FF = 5632
CONV_K_F = 3

D_IN_AB = D_A + CONV_DIM_A + H_A + 2 * H_B * DK_B + 2 * D_B + 2 * H_B
D_IN_C = 2 * H_C * DK_C + H_C * DV_C + D_C

kernel_name = 'hybrid_ssd_mlstm_hgrn2_adaln_convffn_step'


def _rmsnorm(x, w):
    xf = x.astype(jnp.float32)
    xf = xf * lax.rsqrt(jnp.mean(xf * xf, axis=-1, keepdims=True) + EPS)
    return (xf * w.astype(jnp.float32)).astype(x.dtype)


def _group_rmsnorm(x, w, groups):
    shp = x.shape
    xf = x.astype(jnp.float32).reshape(*shp[:-1], groups, shp[-1] // groups)
    xf = xf * lax.rsqrt(jnp.mean(xf * xf, axis=-1, keepdims=True) + EPS)
    return (xf.reshape(shp) * w.astype(jnp.float32)).astype(x.dtype)


def _causal_dwconv(x, buf, w, b):
    k = w.shape[0]
    L = x.shape[1]
    xp = jnp.concatenate([buf.astype(x.dtype), x], axis=1)
    y = b + w[0] * xp[:, 0:L]
    for j in range(1, k):
        y = y + w[j] * xp[:, j:j + L]
    return y, xp[:, L:]


def _chunks(t, lc):
    b, L = t.shape[0], t.shape[1]
    return jnp.moveaxis(t.reshape(b, L // lc, lc, *t.shape[2:]), 1, 0)


def _unchunk(t):
    t = jnp.moveaxis(t, 0, 1)
    return t.reshape(t.shape[0], t.shape[1] * t.shape[2], *t.shape[3:])


def _ssd_scan(x, dt, a_log, bm, cm, s0, lc):
    f32 = jnp.float32
    bsz, L, H, P = x.shape
    G, N = bm.shape[2], bm.shape[3]
    hg = H // G
    la = dt * (-jnp.exp(a_log.astype(f32)))
    xs = (_chunks(x.astype(f32).reshape(bsz, L, G, hg, P), lc),
          _chunks(dt.reshape(bsz, L, G, hg), lc),
          _chunks(la.reshape(bsz, L, G, hg), lc),
          _chunks(bm.astype(f32), lc),
          _chunks(cm.astype(f32), lc))
    mask = jnp.tril(jnp.ones((lc, lc), dtype=bool))[None, :, :, None, None]

    def step(s, inp):
        xc, dtc, lac, bc, cc = inp
        cum = jnp.cumsum(lac, axis=1)
        decay = jnp.exp(jnp.where(mask, cum[:, :, None] - cum[:, None], -jnp.inf))
        cb = jnp.einsum('btgn,bsgn->btsg', cc, bc)
        w_ts = cb[..., None] * decay * dtc[:, None]
        y_intra = jnp.einsum('btsgh,bsghp->btghp', w_ts, xc)
        y_inter = jnp.einsum('btgn,bghpn->btghp', cc, s) * jnp.exp(cum)[..., None]
        tail = jnp.exp(cum[:, -1:] - cum) * dtc
        s_new = s * jnp.exp(cum[:, -1])[..., None, None] + jnp.einsum('bsgh,bsghp,bsgn->bghpn', tail, xc, bc)
        return s_new, y_intra + y_inter

    s_fin, ys = lax.scan(step, s0.astype(f32).reshape(bsz, G, hg, P, N), xs)
    return _unchunk(ys).reshape(bsz, L, H, P), s_fin.reshape(bsz, H, P, N)


def _mlstm_scan(q, k, v, logi, logf, c0, n0, m0, lc):
    f32 = jnp.float32
    xs = tuple(_chunks(t.astype(f32), lc) for t in (q, k, v, logi, logf))
    mask = jnp.tril(jnp.ones((lc, lc), dtype=bool))[None, :, :, None]

    def step(carry, inp):
        c, n, m = carry
        qc, kc, vc, lic, lfc = inp
        bcum = jnp.cumsum(lfc, axis=1)
        dmat = jnp.where(mask, bcum[:, :, None] - bcum[:, None] + lic[:, None], -jnp.inf)
        inter = bcum + m[:, None]
        m_t = jnp.maximum(inter, jnp.max(dmat, axis=2))
        w_in = jnp.exp(inter - m_t)
        w_ts = jnp.exp(dmat - m_t[:, :, None]) * jnp.einsum('bthd,bshd->btsh', qc, kc)
        num = jnp.einsum('btsh,bshv->bthv', w_ts, vc) + w_in[..., None] * jnp.einsum('bthd,bhdv->bthv', qc, c)
        den = jnp.sum(w_ts, axis=2) + w_in * jnp.einsum('bthd,bhd->bth', qc, n)
        h = num / jnp.maximum(jnp.abs(den), jnp.exp(-m_t))[..., None]
        m_new = m_t[:, -1]
        w_s = jnp.exp(bcum[:, -1:] - bcum + lic - m_new[:, None])
        w_c = jnp.exp(bcum[:, -1] + m - m_new)
        c_new = w_c[..., None, None] * c + jnp.einsum('bsh,bshd,bshv->bhdv', w_s, kc, vc)
        n_new = w_c[..., None] * n + jnp.einsum('bsh,bshd->bhd', w_s, kc)
        return (c_new, n_new, m_new), h

    (c_f, n_f, m_f), hs = lax.scan(step, (c0.astype(f32), n0.astype(f32), m0.astype(f32)), xs)
    return _unchunk(hs), c_f, n_f, m_f


def _hgrn2_scan(q, k, v, logf, s0, lc):
    f32 = jnp.float32
    xs = tuple(_chunks(t.astype(f32), lc) for t in (q, k, v, logf))
    mask = jnp.tril(jnp.ones((lc, lc), dtype=bool))[None, :, :, None, None]

    def step(s, inp):
        qc, kc, vc, gc = inp
        gcum = jnp.cumsum(gc, axis=1)
        decay = jnp.exp(jnp.where(mask, gcum[:, :, None] - gcum[:, None], -jnp.inf))
        att = jnp.einsum('bthd,bshd,btshd->btsh', qc, kc, decay)
        o = jnp.einsum('btsh,bshv->bthv', att, vc) + jnp.einsum('bthd,bhdv->bthv', qc * jnp.exp(gcum), s)
        s_new = jnp.exp(gcum[:, -1])[..., None] * s + jnp.einsum('bshd,bshv->bhdv', kc * jnp.exp(gcum[:, -1:] - gcum), vc)
        return s_new, o

    s_fin, os_ = lax.scan(step, s0.astype(f32), xs)
    return _unchunk(os_), s_fin


def _mixer_ab(h, p, e, conv_buf, ssd_s, mem_c, mem_n, mem_m, lc):
    f32 = jnp.float32
    bsz, L, _ = h.shape
    sizes = [D_A, CONV_DIM_A, H_A, H_B * DK_B, H_B * DK_B, D_B, D_B, H_B, H_B]
    cuts = []
    acc = 0
    for s in sizes[:-1]:
        acc += s
        cuts.append(acc)
    proj = h @ p['w_in_ab'][e]
    z, xbc, dt, q, k, v, og, ig, fg = jnp.split(proj, cuts, axis=-1)
    xbc, conv_new = _causal_dwconv(xbc, conv_buf, p['conv_w_a'][e], p['conv_b_a'][e])
    xbc = jax.nn.silu(xbc)
    xa, bm, cm = jnp.split(xbc, [D_A, D_A + G_A * N_A], axis=-1)
    xa = xa.reshape(bsz, L, H_A, P_A)
    bm = bm.reshape(bsz, L, G_A, N_A)
    cm = cm.reshape(bsz, L, G_A, N_A)
    dt = jax.nn.softplus(dt.astype(f32) + p['dt_bias'][e].astype(f32))
    ya, ssd_new = _ssd_scan(xa, dt, p['a_log'][e], bm, cm, ssd_s, lc)
    ya = ya + p['d_skip'][e].astype(f32)[:, None] * xa.astype(f32)
    ya = ya.reshape(bsz, L, D_A) * jax.nn.silu(z.astype(f32))
    ya = _group_rmsnorm(ya, p['norm_a'][e], G_A)
    q = q.reshape(bsz, L, H_B, DK_B) * (DK_B ** -0.5)
    k = k.reshape(bsz, L, H_B, DK_B)
    v = v.reshape(bsz, L, H_B, DV_B)
    logi = ig.astype(f32) + p['i_bias'][e].astype(f32)
    logf = jax.nn.log_sigmoid(fg.astype(f32) + p['f_bias'][e].astype(f32))
    hb, c_new, n_new, m_new = _mlstm_scan(q, k, v, logi, logf, mem_c, mem_n, mem_m, lc)
    hb = _group_rmsnorm(hb.reshape(bsz, L, D_B), p['norm_b'][e], H_B) * jax.nn.sigmoid(og.astype(f32))
    out = jnp.concatenate([ya, hb], axis=-1).astype(h.dtype) @ p['w_out_ab'][e]
    return out, conv_new, ssd_new, c_new, n_new, m_new


def _mixer_c(h, p, o, hgrn_s, lc):
    f32 = jnp.float32
    bsz, L, _ = h.shape
    proj = h @ p['w_in_c'][o]
    q, fx, iv, g = jnp.split(proj, [H_C * DK_C, 2 * H_C * DK_C, 2 * H_C * DK_C + H_C * DV_C], axis=-1)
    lb_all = jnp.cumsum(jax.nn.softmax(p['lb_logits'].astype(f32), axis=0), axis=0)
    lb = lb_all[o] - lb_all[0]
    fx = fx.astype(f32)
    logf = jnp.logaddexp(jnp.log(lb), jnp.log1p(-lb) + jax.nn.log_sigmoid(fx))
    kk = (1.0 - lb) * jax.nn.sigmoid(-fx)
    oc, s_new = _hgrn2_scan(q.reshape(bsz, L, H_C, DK_C), kk.reshape(bsz, L, H_C, DK_C),
                            iv.reshape(bsz, L, H_C, DV_C), logf.reshape(bsz, L, H_C, DK_C), hgrn_s, lc)
    oc = _group_rmsnorm(oc.reshape(bsz, L, D_C), p['norm_c'][o], H_C) * jax.nn.silu(g.astype(f32))
    return oc.astype(h.dtype) @ p['w_out_c'][o], s_new


def _conv_ffn(h, buf, p, layer):
    g = h @ p['w_ffn_g'][layer]
    g, buf_new = _causal_dwconv(g, buf, p['conv_w_f'][layer], p['conv_b_f'][layer])
    u = h @ p['w_ffn_u'][layer]
    return (jax.nn.silu(g) * u) @ p['w_ffn_d'][layer], buf_new


def _trunk(x, c, lc, conv_a, ssd, mem_c, mem_n, mem_m, hgrn, ffn_buf, p):
    n_conv, n_ssd, n_c, n_n, n_m, n_hgrn, n_ffn = [], [], [], [], [], [], []
    c_act = jax.nn.silu(c)
    for layer in range(DEPTH):
        mod = c_act @ p['w_ada'][layer] + p['b_ada'][layer]
        sh1, sc1, g1, sh2, sc2, g2 = [m[:, None, :] for m in jnp.split(mod, 6, axis=-1)]
        hmix = _rmsnorm(x, p['norm_mix'][layer]) * (1 + sc1) + sh1
        if layer % 2 == 0:
            e = layer // 2
            out, cv, ss, cc, nn, mm = _mixer_ab(hmix, p, e, conv_a[e], ssd[e], mem_c[e], mem_n[e], mem_m[e], lc)
            n_conv.append(cv)
            n_ssd.append(ss)
            n_c.append(cc)
            n_n.append(nn)
            n_m.append(mm)
        else:
            o = layer // 2
            out, hs = _mixer_c(hmix, p, o, hgrn[o], lc)
            n_hgrn.append(hs)
        x = x + g1 * out
        hffn = _rmsnorm(x, p['norm_ffn'][layer]) * (1 + sc2) + sh2
        f, fb = _conv_ffn(hffn, ffn_buf[layer], p, layer)
        n_ffn.append(fb)
        x = x + g2 * f
    y = _rmsnorm(x, p['norm_f'])
    dt = x.dtype
    return (y, jnp.stack(n_conv).astype(dt), jnp.stack(n_ssd).astype(dt), jnp.stack(n_c).astype(dt),
            jnp.stack(n_n).astype(dt), jnp.stack(n_m).astype(dt), jnp.stack(n_hgrn).astype(dt),
            jnp.stack(n_ffn).astype(dt))


def setup_inputs(seed: int = 0) -> dict:
    key = jax.random.key(seed)
    ks = jax.random.split(key, 40)
    f32 = jnp.float32

    def nrm(i, shape, scale):
        return scale * jax.random.normal(ks[i], shape, f32)

    def gain(i, shape):
        return 1.0 + 0.02 * jax.random.normal(ks[i], shape, f32)

    dt0 = jnp.exp(jax.random.uniform(ks[17], (N_EVEN, H_A), f32, math.log(1e-3), math.log(1e-1)))
    return {
        'x_prompt': nrm(0, (BATCH, SEQ, D_MODEL), 1.0),
        'x_sample': nrm(1, (DEC_BATCH, DEC_SEQ, D_MODEL), 1.0),
        'c_prompt': nrm(2, (BATCH, D_MODEL), 1.0),
        'c_sample': nrm(3, (DEC_BATCH, D_MODEL), 1.0),
        'state_ssd_conv': nrm(4, (N_EVEN, DEC_BATCH, CONV_K_A - 1, CONV_DIM_A), 1.0),
        'state_ssd': nrm(5, (N_EVEN, DEC_BATCH, H_A, P_A, N_A), 0.1),
        'state_mlstm_c': nrm(6, (N_EVEN, DEC_BATCH, H_B, DK_B, DV_B), 0.1),
        'state_mlstm_n': nrm(7, (N_EVEN, DEC_BATCH, H_B, DK_B), 0.1),
        'state_mlstm_m': nrm(8, (N_EVEN, DEC_BATCH, H_B), 1.0),
        'state_hgrn': nrm(9, (N_ODD, DEC_BATCH, H_C, DK_C, DV_C), 0.5),
        'state_ffn_conv': nrm(10, (DEPTH, DEC_BATCH, CONV_K_F - 1, D_FF), 1.0),
        'w_ada': nrm(11, (DEPTH, D_MODEL, 6 * D_MODEL), 0.5 * D_MODEL ** -0.5),
        'b_ada': nrm(12, (DEPTH, 6 * D_MODEL), 0.02),
        'norm_mix': gain(13, (DEPTH, D_MODEL)),
        'norm_ffn': gain(14, (DEPTH, D_MODEL)),
        'w_in_ab': nrm(15, (N_EVEN, D_MODEL, D_IN_AB), D_MODEL ** -0.5),
        'conv_w_a': nrm(16, (N_EVEN, CONV_K_A, CONV_DIM_A), CONV_K_A ** -0.5),
        'conv_b_a': nrm(18, (N_EVEN, CONV_DIM_A), 0.02),
        'dt_bias': dt0 + jnp.log(-jnp.expm1(-dt0)),
        'a_log': jnp.log(jax.random.uniform(ks[19], (N_EVEN, H_A), f32, 1.0, 16.0)),
        'd_skip': 1.0 + nrm(20, (N_EVEN, H_A), 0.1),
        'norm_a': gain(21, (N_EVEN, D_A)),
        'i_bias': nrm(22, (N_EVEN, H_B), 0.5),
        'f_bias': jax.random.uniform(ks[23], (N_EVEN, H_B), f32, 3.0, 6.0),
        'norm_b': gain(24, (N_EVEN, D_B)),
        'w_out_ab': nrm(25, (N_EVEN, D_A + D_B, D_MODEL), (D_A + D_B) ** -0.5),
        'w_in_c': nrm(26, (N_ODD, D_MODEL, D_IN_C), D_MODEL ** -0.5),
        'lb_logits': nrm(27, (N_ODD, H_C * DK_C), 0.5),
        'norm_c': gain(28, (N_ODD, D_C)),
        'w_out_c': nrm(29, (N_ODD, D_C, D_MODEL), D_C ** -0.5),
        'w_ffn_g': nrm(30, (DEPTH, D_MODEL, D_FF), D_MODEL ** -0.5),
        'w_ffn_u': nrm(31, (DEPTH, D_MODEL, D_FF), D_MODEL ** -0.5),
        'conv_w_f': nrm(32, (DEPTH, CONV_K_F, D_FF), CONV_K_F ** -0.5),
        'conv_b_f': nrm(33, (DEPTH, D_FF), 0.02),
        'w_ffn_d': nrm(34, (DEPTH, D_FF, D_MODEL), D_FF ** -0.5),
        'norm_f': gain(35, (D_MODEL,)),
    }


def reference(x_prompt, x_sample, c_prompt, c_sample, state_ssd_conv, state_ssd, state_mlstm_c,
              state_mlstm_n, state_mlstm_m, state_hgrn, state_ffn_conv, w_ada, b_ada, norm_mix,
              norm_ffn, w_in_ab, conv_w_a, conv_b_a, dt_bias, a_log, d_skip, norm_a, i_bias, f_bias,
              norm_b, w_out_ab, w_in_c, lb_logits, norm_c, w_out_c, w_ffn_g, w_ffn_u, conv_w_f,
              conv_b_f, w_ffn_d, norm_f):
    p = dict(w_ada=w_ada, b_ada=b_ada, norm_mix=norm_mix, norm_ffn=norm_ffn, w_in_ab=w_in_ab,
             conv_w_a=conv_w_a, conv_b_a=conv_b_a, dt_bias=dt_bias, a_log=a_log, d_skip=d_skip,
             norm_a=norm_a, i_bias=i_bias, f_bias=f_bias, norm_b=norm_b, w_out_ab=w_out_ab,
             w_in_c=w_in_c, lb_logits=lb_logits, norm_c=norm_c, w_out_c=w_out_c, w_ffn_g=w_ffn_g,
             w_ffn_u=w_ffn_u, conv_w_f=conv_w_f, conv_b_f=conv_b_f, w_ffn_d=w_ffn_d, norm_f=norm_f)
    dt = x_prompt.dtype
    (y_prompt, p_ssd_conv, p_ssd, p_mlstm_c, p_mlstm_n, p_mlstm_m, p_hgrn, p_ffn_conv) = _trunk(
        x_prompt, c_prompt, math.gcd(SEQ, CHUNK),
        jnp.zeros((N_EVEN, BATCH, CONV_K_A - 1, CONV_DIM_A), dt),
        jnp.zeros((N_EVEN, BATCH, H_A, P_A, N_A), dt),
        jnp.zeros((N_EVEN, BATCH, H_B, DK_B, DV_B), dt),
        jnp.zeros((N_EVEN, BATCH, H_B, DK_B), dt),
        jnp.zeros((N_EVEN, BATCH, H_B), dt),
        jnp.zeros((N_ODD, BATCH, H_C, DK_C, DV_C), dt),
        jnp.zeros((DEPTH, BATCH, CONV_K_F - 1, D_FF), dt),
        p)
    (y_sample, s_ssd_conv, s_ssd, s_mlstm_c, s_mlstm_n, s_mlstm_m, s_hgrn, s_ffn_conv) = _trunk(
        x_sample, c_sample, DEC_SEQ, state_ssd_conv, state_ssd, state_mlstm_c, state_mlstm_n,
        state_mlstm_m, state_hgrn, state_ffn_conv, p)
    return (y_prompt, y_sample, p_ssd_conv, p_ssd, p_mlstm_c, p_mlstm_n, p_mlstm_m, p_hgrn, p_ffn_conv,
            s_ssd_conv, s_ssd, s_mlstm_c, s_mlstm_n, s_mlstm_m, s_hgrn, s_ffn_conv)
```

```python
import functools
import math

import jax
import jax.numpy as jnp
from jax import lax
from jax.experimental import pallas as pl
from jax.experimental.pallas import tpu as pltpu

F32 = jnp.float32
MXU_DTYPE = jnp.bfloat16
HI = lax.Precision.HIGHEST
NEG_BIG = -1e30

D_MODEL = 2048
DEPTH = 4
EPS = 1e-6
PROMPT_CHUNK = 64
H_A, P_A, G_A, N_A, CONV_K_A = 32, 64, 4, 128, 4
D_A = H_A * P_A
CONV_DIM_A = D_A + 2 * G_A * N_A
H_B, DK_B, DV_B = 8, 128, 256
D_B = H_B * DV_B
H_C, DK_C, DV_C = 16, 128, 128
D_C = H_C * DV_C
D_FF, CONV_K_F = 5632, 3
GATE_W = 128
SAMPLE_PAD = 8
VMEM_LIMIT = 56 * 1024 * 1024

_NT = (((1,), (1,)), ((), ()))
_TN = (((0,), (0,)), ((), ()))


def _cparams(sem):
    return pltpu.CompilerParams(dimension_semantics=sem, vmem_limit_bytes=VMEM_LIMIT)


def _sigmoid(x):
    return 1.0 / (1.0 + jnp.exp(-x))


def _softplus(x):
    return jnp.maximum(x, 0.0) + jnp.log1p(jnp.exp(-jnp.abs(x)))


def _log_sigmoid(x):
    return jnp.minimum(x, 0.0) - jnp.log1p(jnp.exp(-jnp.abs(x)))


def _tri(n):
    r = lax.broadcasted_iota(jnp.int32, (n, n), 0)
    c = lax.broadcasted_iota(jnp.int32, (n, n), 1)
    mask = c <= r
    return mask.astype(F32), (r <= c).astype(F32), mask


def _shifted(x, tails, k):
    row = lax.broadcasted_iota(jnp.int32, x.shape, 0)
    out = pltpu.roll(x, k, 0)
    for r in range(k):
        out = jnp.where(row == r, tails[len(tails) - k + r], out)
    return out


def _ada_kernel(c_ref, w_ref, b_ref, o_ref):
    c = c_ref[...]
    ca = (c * _sigmoid(c)).astype(MXU_DTYPE)
    o_ref[0] = jnp.dot(ca, w_ref[0].astype(MXU_DTYPE), preferred_element_type=F32) + b_ref[0]


def _ada(c_all, w_ada, b_ada, tn=1024):
    rows = c_all.shape[0]
    n = w_ada.shape[2]
    return pl.pallas_call(
        _ada_kernel,
        grid=(DEPTH, n // tn),
        in_specs=[
            pl.BlockSpec((rows, D_MODEL), lambda l, j: (0, 0)),
            pl.BlockSpec((1, D_MODEL, tn), lambda l, j: (l, 0, j)),
            pl.BlockSpec((1, 1, tn), lambda l, j: (l, 0, j)),
        ],
        out_specs=pl.BlockSpec((1, rows, tn), lambda l, j: (l, 0, j)),
        out_shape=jax.ShapeDtypeStruct((DEPTH, rows, n), F32),
        compiler_params=_cparams(("parallel", "parallel")),
        name="ada",
    )(c_all, w_ada, b_ada.reshape(DEPTH, 1, n))


def _norm_mod_kernel(x_ref, nw_ref, sc_ref, sh_ref, o_ref):
    x = x_ref[...]
    xn = x * lax.rsqrt(jnp.mean(x * x, axis=-1, keepdims=True) + EPS)
    o_ref[...] = ((xn * nw_ref[...]) * (1.0 + sc_ref[0]) + sh_ref[0]).astype(o_ref.dtype)


def _norm_kernel(x_ref, nw_ref, o_ref):
    x = x_ref[...]
    xn = x * lax.rsqrt(jnp.mean(x * x, axis=-1, keepdims=True) + EPS)
    o_ref[...] = (xn * nw_ref[...]).astype(o_ref.dtype)


def _mod_spec(mod, tm, rows_per_mod, width, col_of):
    r = mod.shape[1]
    return pl.BlockSpec((1, r, width), lambda i, j: ((i * tm) // rows_per_mod, 0, col_of(j)))


def _norm_mod(x, nw, mod, k_sc, k_sh, tm, rows_per_mod):
    t = x.shape[0]
    return pl.pallas_call(
        _norm_mod_kernel,
        grid=(t // tm, 1),
        in_specs=[
            pl.BlockSpec((tm, D_MODEL), lambda i, j: (i, 0)),
            pl.BlockSpec((1, D_MODEL), lambda i, j: (0, 0)),
            _mod_spec(mod, tm, rows_per_mod, D_MODEL, lambda j: k_sc),
            _mod_spec(mod, tm, rows_per_mod, D_MODEL, lambda j: k_sh),
        ],
        out_specs=pl.BlockSpec((tm, D_MODEL), lambda i, j: (i, 0)),
        out_shape=jax.ShapeDtypeStruct((t, D_MODEL), MXU_DTYPE),
        compiler_params=_cparams(("parallel", "arbitrary")),
        name="norm_mod",
    )(x, nw.reshape(1, D_MODEL), mod, mod)


def _final_norm(x, nw, tm):
    t = x.shape[0]
    return pl.pallas_call(
        _norm_kernel,
        grid=(t // tm,),
        in_specs=[
            pl.BlockSpec((tm, D_MODEL), lambda i: (i, 0)),
            pl.BlockSpec((1, D_MODEL), lambda i: (0, 0)),
        ],
        out_specs=pl.BlockSpec((tm, D_MODEL), lambda i: (i, 0)),
        out_shape=jax.ShapeDtypeStruct((t, D_MODEL), F32),
        compiler_params=_cparams(("parallel",)),
        name="final_norm",
    )(x, nw.reshape(1, D_MODEL))


def _mm_kernel(*refs, n_lhs, has_res):
    a_refs, w_refs = refs[:n_lhs], refs[n_lhs:2 * n_lhs]
    acc = None
    for a_ref, w_ref in zip(a_refs, w_refs):
        d = jnp.dot(a_ref[...], w_ref[...], preferred_element_type=F32)
        acc = d if acc is None else acc + d
    if has_res:
        xres_ref, gate_ref, o_ref = refs[2 * n_lhs:]
        o_ref[...] = xres_ref[...] + gate_ref[0] * acc
    else:
        (o_ref,) = refs[2 * n_lhs:]
        o_ref[...] = acc.astype(o_ref.dtype)


def _mm(a_list, w_list, tm, tn, *, weights_outer, out_dtype=F32, res=None):
    m = a_list[0].shape[0]
    n = w_list[0].shape[1]
    if weights_outer:
        grid = (n // tn, m // tm)
        swap = lambda f: (lambda j, i: f(i, j))
    else:
        grid = (m // tm, n // tn)
        swap = lambda f: f
    in_specs = [pl.BlockSpec((tm, a.shape[1]), swap(lambda i, j: (i, 0))) for a in a_list]
    in_specs += [pl.BlockSpec((w.shape[0], tn), swap(lambda i, j: (0, j))) for w in w_list]
    args = list(a_list) + list(w_list)
    if res is not None:
        xres, mod, k_gate, rows_per_mod = res
        r = mod.shape[1]
        per = D_MODEL // tn
        in_specs.append(pl.BlockSpec((tm, tn), swap(lambda i, j: (i, j))))
        in_specs.append(pl.BlockSpec(
            (1, r, tn), swap(lambda i, j: ((i * tm) // rows_per_mod, 0, k_gate * per + j))))
        args += [xres, mod]
    return pl.pallas_call(
        functools.partial(_mm_kernel, n_lhs=len(a_list), has_res=res is not None),
        grid=grid,
        in_specs=in_specs,
        out_specs=pl.BlockSpec((tm, tn), swap(lambda i, j: (i, j))),
        out_shape=jax.ShapeDtypeStruct((m, n), out_dtype),
        compiler_params=_cparams(("parallel", "parallel")),
        name="mm_res" if res is not None else "mm",
    )(*args)


def _ffn_act_kernel(g_ref, u_ref, cs_ref, cw_ref, cb_ref, a_ref, cso_ref, tail_scr, *, lv):
    @pl.when(pl.program_id(1) == 0)
    def _():
        tail_scr[6:8, :] = cs_ref[0]

    g = g_ref[...]
    t1, t2 = tail_scr[7:8, :], tail_scr[6:7, :]
    g1 = _shifted(g, [t2, t1], 1)
    g2 = _shifted(g, [t2, t1], 2)
    w = cw_ref[...]
    y = cb_ref[...] + w[0:1] * g2 + w[1:2] * g1 + w[2:3] * g
    a_ref[...] = (y * _sigmoid(y) * u_ref[...]).astype(a_ref.dtype)
    new_tail = g[lv - 2:lv, :]
    tail_scr[6:8, :] = new_tail
    cso_ref[0] = new_tail


def _ffn_act(gu, conv_state, conv_w, conv_b, b, l, lc, lv):
    nc = l // lc
    return pl.pallas_call(
        functools.partial(_ffn_act_kernel, lv=lv),
        grid=(b, nc),
        in_specs=[
            pl.BlockSpec((lc, D_FF), lambda i, c: (i * nc + c, 0)),
            pl.BlockSpec((lc, D_FF), lambda i, c: (i * nc + c, 1)),
            pl.BlockSpec((1, CONV_K_F - 1, D_FF), lambda i, c: (i, 0, 0)),
            pl.BlockSpec((CONV_K_F, D_FF), lambda i, c: (0, 0)),
            pl.BlockSpec((1, D_FF), lambda i, c: (0, 0)),
        ],
        out_specs=[
            pl.BlockSpec((lc, D_FF), lambda i, c: (i * nc + c, 0)),
            pl.BlockSpec((1, CONV_K_F - 1, D_FF), lambda i, c: (i, 0, 0)),
        ],
        out_shape=[
            jax.ShapeDtypeStruct((b * l, D_FF), MXU_DTYPE),
            jax.ShapeDtypeStruct((b, CONV_K_F - 1, D_FF), F32),
        ],
        scratch_shapes=[pltpu.VMEM((8, D_FF), F32)],
        compiler_params=_cparams(("parallel", "arbitrary")),
        name="ffn_act",
    )(gu, gu, conv_state, conv_w, conv_b.reshape(1, D_FF))


def _ssd_kernel(z_ref, xbc_ref, gc_ref, gr_ref, cs_ref, s0_ref, cw_ref, cb_ref,
                dtb_r_ref, dtb_c_ref, al_r_ref, al_c_ref, dsk_ref, nw_ref, exp_ref,
                y_ref, cso_ref, so_ref, tail_scr, *, lc, lv):
    @pl.when(pl.program_id(1) == 0)
    def _():
        tail_scr[5:8, :] = cs_ref[0]
        so_ref[0] = s0_ref[0]

    x = xbc_ref[...]
    tails = [tail_scr[5:6, :], tail_scr[6:7, :], tail_scr[7:8, :]]
    w = cw_ref[...]
    xc = (cb_ref[...] + w[0:1] * _shifted(x, tails, 3) + w[1:2] * _shifted(x, tails, 2)
          + w[2:3] * _shifted(x, tails, 1) + w[3:4] * x)
    xc = xc * _sigmoid(xc)
    new_tail = x[lv - 3:lv, :]
    tail_scr[5:8, :] = new_tail
    cso_ref[0] = new_tail
    xa = xc[:, :D_A]
    bm = xc[:, D_A:D_A + G_A * N_A]
    cm = xc[:, D_A + G_A * N_A:]

    dt_c = _softplus(gc_ref[:, 0:H_A] + dtb_r_ref[...])
    dt_r = _softplus(gr_ref[0, 0:H_A, :] + dtb_c_ref[...])
    if lv < lc:
        dt_c = jnp.where(lax.broadcasted_iota(jnp.int32, dt_c.shape, 0) < lv, dt_c, 0.0)
        dt_r = jnp.where(lax.broadcasted_iota(jnp.int32, dt_r.shape, 1) < lv, dt_r, 0.0)
    lower, upper, mask = _tri(lc)
    cum_c = jnp.dot(lower, dt_c * (-jnp.exp(al_r_ref[...])), precision=HI)
    cum_r = jnp.dot(dt_r * (-jnp.exp(al_c_ref[...])), upper, precision=HI)
    cum_last = cum_c[lc - 1:lc, :]
    expand = exp_ref[...]
    ecum_x = jnp.dot(jnp.exp(cum_c), expand, precision=HI)
    tail_x = jnp.dot(jnp.exp(cum_last - cum_c) * dt_c, expand, precision=HI)

    hg = H_A // G_A
    gw = hg * P_A
    ys = []
    for g in range(G_A):
        cg = cm[:, g * N_A:(g + 1) * N_A]
        bg = bm[:, g * N_A:(g + 1) * N_A]
        cb_ts = lax.dot_general(cg, bg, _NT, preferred_element_type=F32)
        s_g = so_ref[0, g]
        y_inter = lax.dot_general(cg, s_g, _NT, preferred_element_type=F32)
        xg = xa[:, g * gw:(g + 1) * gw]
        pieces = []
        for j in range(hg // 2):
            xp = xg[:, j * 128:(j + 1) * 128]
            lane = lax.broadcasted_iota(jnp.int32, xp.shape, 1)
            acc = None
            for half in range(2):
                h = g * hg + 2 * j + half
                dec = jnp.exp(jnp.where(mask, cum_c[:, h:h + 1] - cum_r[h:h + 1, :], -jnp.inf))
                w_ts = cb_ts * dec * dt_r[h:h + 1, :]
                xh = jnp.where((lane >= P_A) if half else (lane < P_A), xp, 0.0)
                d = jnp.dot(w_ts, xh, preferred_element_type=F32)
                acc = d if acc is None else acc + d
            pieces.append(acc)
        y_intra = jnp.concatenate(pieces, axis=1)
        ys.append(y_intra + y_inter * ecum_x[:, g * gw:(g + 1) * gw])
        upd = lax.dot_general(xg * tail_x[:, g * gw:(g + 1) * gw], bg, _TN,
                              preferred_element_type=F32)
        for hh in range(hg):
            h = g * hg + hh
            rows = slice(hh * P_A, (hh + 1) * P_A)
            so_ref[0, g, rows, :] = s_g[rows, :] * jnp.exp(cum_r[h:h + 1, lc - 1:lc]) + upd[rows, :]

    y = jnp.concatenate(ys, axis=1) + dsk_ref[...] * xa
    z = z_ref[...]
    y = y * (z * _sigmoid(z))
    outs = []
    for g in range(G_A):
        yg = y[:, g * gw:(g + 1) * gw]
        outs.append(yg * lax.rsqrt(jnp.mean(yg * yg, axis=-1, keepdims=True) + EPS))
    y_ref[...] = (jnp.concatenate(outs, axis=1) * nw_ref[...]).astype(y_ref.dtype)


def _ssd(p1, gates, gates_t, conv_state, ssd_state, prm, e, b, l, lc, lv):
    nc = l // lc
    rowblk = lambda w, col: pl.BlockSpec((lc, w), lambda i, c: (i * nc + c, col))
    const2 = lambda shape: pl.BlockSpec(shape, lambda i, c: (0, 0))
    s4 = ssd_state.reshape(b, G_A, (H_A // G_A) * P_A, N_A)
    expand = (jnp.arange(D_A)[None, :] // P_A == jnp.arange(H_A)[:, None]).astype(F32)
    y, cso, so = pl.pallas_call(
        functools.partial(_ssd_kernel, lc=lc, lv=lv),
        grid=(b, nc),
        in_specs=[
            rowblk(D_A, 0),
            rowblk(CONV_DIM_A, 2),
            rowblk(GATE_W, 0),
            pl.BlockSpec((1, GATE_W, lc), lambda i, c: (i * nc + c, 0, 0)),
            pl.BlockSpec((1, CONV_K_A - 1, CONV_DIM_A), lambda i, c: (i, 0, 0)),
            pl.BlockSpec((1,) + s4.shape[1:], lambda i, c: (i, 0, 0, 0)),
            const2((CONV_K_A, CONV_DIM_A)), const2((1, CONV_DIM_A)),
            const2((1, H_A)), const2((H_A, 1)), const2((1, H_A)), const2((H_A, 1)),
            const2((1, D_A)), const2((1, D_A)), const2((H_A, D_A)),
        ],
        out_specs=[
            rowblk(D_A, 0),
            pl.BlockSpec((1, CONV_K_A - 1, CONV_DIM_A), lambda i, c: (i, 0, 0)),
            pl.BlockSpec((1,) + s4.shape[1:], lambda i, c: (i, 0, 0, 0)),
        ],
        out_shape=[
            jax.ShapeDtypeStruct((b * l, D_A), MXU_DTYPE),
            jax.ShapeDtypeStruct((b, CONV_K_A - 1, CONV_DIM_A), F32),
            jax.ShapeDtypeStruct(s4.shape, F32),
        ],
        scratch_shapes=[pltpu.VMEM((8, CONV_DIM_A), F32)],
        compiler_params=_cparams(("parallel", "arbitrary")),
        name="ssd",
    )(p1, p1, gates, gates_t, conv_state, s4,
      prm['conv_w_a'][e], prm['conv_b_a'][e].reshape(1, CONV_DIM_A),
      prm['dt_bias'][e].reshape(1, H_A), prm['dt_bias'][e].reshape(H_A, 1),
      prm['a_log'][e].reshape(1, H_A), prm['a_log'][e].reshape(H_A, 1),
      jnp.repeat(prm['d_skip'][e], P_A).reshape(1, D_A), prm['norm_a'][e].reshape(1, D_A), expand)
    return y, cso, so.reshape(ssd_state.shape)


def _mlstm_kernel(q_ref, k_ref, v_ref, og_ref, gc_ref, gr_ref, c0_ref, n0_ref, m0_ref,
                  ib_r_ref, ib_c_ref, fb_r_ref, fb_c_ref, nw_ref,
                  h_ref, co_ref, no_ref, mo_ref, *, lc, lv):
    @pl.when(pl.program_id(1) == 0)
    def _():
        co_ref[0] = c0_ref[0]
        no_ref[0] = n0_ref[0]
        mo_ref[0] = m0_ref[0]

    i0, f0 = H_A, H_A + H_B
    li_c = gc_ref[:, i0:i0 + H_B] + ib_r_ref[...]
    lf_c = _log_sigmoid(gc_ref[:, f0:f0 + H_B] + fb_r_ref[...])
    li_r = gr_ref[0, i0:i0 + H_B, :] + ib_c_ref[...]
    lf_r = _log_sigmoid(gr_ref[0, f0:f0 + H_B, :] + fb_c_ref[...])
    if lv < lc:
        vc = lax.broadcasted_iota(jnp.int32, li_c.shape, 0) < lv
        vr = lax.broadcasted_iota(jnp.int32, li_r.shape, 1) < lv
        li_c, lf_c = jnp.where(vc, li_c, NEG_BIG), jnp.where(vc, lf_c, 0.0)
        li_r, lf_r = jnp.where(vr, li_r, NEG_BIG), jnp.where(vr, lf_r, 0.0)
    lower, upper, mask = _tri(lc)
    bc_c = jnp.dot(lower, lf_c, precision=HI)
    bc_r = jnp.dot(lf_r, upper, precision=HI)

    m_old = mo_ref[0]
    n_old = no_ref[0]
    lane_h = lax.broadcasted_iota(jnp.int32, (1, H_B), 1)
    m_out = jnp.zeros((1, H_B), F32)
    hs = []
    for h in range(H_B):
        qh = q_ref[:, h * DK_B:(h + 1) * DK_B] * (DK_B ** -0.5)
        kh = k_ref[:, h * DK_B:(h + 1) * DK_B]
        vh = v_ref[:, h * DV_B:(h + 1) * DV_B]
        m_prev = m_old[:, h:h + 1]
        bcc = bc_c[:, h:h + 1]
        dmat = jnp.where(mask, bcc - bc_r[h:h + 1, :] + li_r[h:h + 1, :], -jnp.inf)
        inter = bcc + m_prev
        m_t = jnp.maximum(inter, jnp.max(dmat, axis=1, keepdims=True))
        w_in = jnp.exp(inter - m_t)
        qk = lax.dot_general(qh, kh, _NT, preferred_element_type=F32)
        w_ts = jnp.exp(dmat - m_t) * qk
        c_h = co_ref[0, h]
        n_h = n_old[h:h + 1, :]
        num = (jnp.dot(w_ts, vh, preferred_element_type=F32)
               + w_in * jnp.dot(qh, c_h, preferred_element_type=F32))
        den = (jnp.sum(w_ts, axis=1, keepdims=True)
               + w_in * jnp.sum(qh * n_h, axis=1, keepdims=True))
        hs.append(num / jnp.maximum(jnp.abs(den), jnp.exp(-m_t)))
        m_new = m_t[lv - 1:lv, :]
        bc_last = bcc[lc - 1:lc, :]
        w_s = jnp.exp(bc_last - bcc + li_c[:, h:h + 1] - m_new)
        w_c = jnp.exp(bc_last + m_prev - m_new)
        ks = kh * w_s
        co_ref[0, h] = w_c * c_h + lax.dot_general(ks, vh, _TN, preferred_element_type=F32)
        no_ref[0, h:h + 1, :] = w_c * n_h + jnp.sum(ks, axis=0, keepdims=True)
        m_out = jnp.where(lane_h == h, m_new, m_out)
    mo_ref[0] = m_out

    outs = []
    for h in range(H_B):
        hh = hs[h]
        outs.append(hh * lax.rsqrt(jnp.mean(hh * hh, axis=-1, keepdims=True) + EPS))
    hn = jnp.concatenate(outs, axis=1) * nw_ref[...]
    h_ref[...] = (hn * _sigmoid(og_ref[...])).astype(h_ref.dtype)


def _mlstm(p1, gates, gates_t, c_state, n_state, m_state, prm, e, b, l, lc, lv):
    nc = l // lc
    rowblk = lambda w, col: pl.BlockSpec((lc, w), lambda i, c: (i * nc + c, col))
    const2 = lambda shape: pl.BlockSpec(shape, lambda i, c: (0, 0))
    qk_w = H_B * DK_B
    h, co, no, mo = pl.pallas_call(
        functools.partial(_mlstm_kernel, lc=lc, lv=lv),
        grid=(b, nc),
        in_specs=[
            rowblk(qk_w, 9), rowblk(qk_w, 10),
            rowblk(D_B, 1), rowblk(D_B, 2),
            rowblk(GATE_W, 0),
            pl.BlockSpec((1, GATE_W, lc), lambda i, c: (i * nc + c, 0, 0)),
            pl.BlockSpec((1, H_B, DK_B, DV_B), lambda i, c: (i, 0, 0, 0)),
            pl.BlockSpec((1, H_B, DK_B), lambda i, c: (i, 0, 0)),
            pl.BlockSpec((1, 1, H_B), lambda i, c: (i, 0, 0)),
            const2((1, H_B)), const2((H_B, 1)), const2((1, H_B)), const2((H_B, 1)),
            const2((1, D_B)),
        ],
        out_specs=[
            rowblk(D_B, 0),
            pl.BlockSpec((1, H_B, DK_B, DV_B), lambda i, c: (i, 0, 0, 0)),
            pl.BlockSpec((1, H_B, DK_B), lambda i, c: (i, 0, 0)),
            pl.BlockSpec((1, 1, H_B), lambda i, c: (i, 0, 0)),
        ],
        out_shape=[
            jax.ShapeDtypeStruct((b * l, D_B), MXU_DTYPE),
            jax.ShapeDtypeStruct((b, H_B, DK_B, DV_B), F32),
            jax.ShapeDtypeStruct((b, H_B, DK_B), F32),
            jax.ShapeDtypeStruct((b, 1, H_B), F32),
        ],
        compiler_params=_cparams(("parallel", "arbitrary")),
        name="mlstm",
    )(p1, p1, p1, p1, gates, gates_t, c_state, n_state, m_state.reshape(b, 1, H_B),
      prm['i_bias'][e].reshape(1, H_B), prm['i_bias'][e].reshape(H_B, 1),
      prm['f_bias'][e].reshape(1, H_B), prm['f_bias'][e].reshape(H_B, 1),
      prm['norm_b'][e].reshape(1, D_B))
    return h, co, no, mo.reshape(b, H_B)


def _hgrn_kernel(q_ref, f_ref, i_ref, g_ref, lbl_ref, s0_ref, nw_ref, o_ref, so_ref,
                 *, lc, lv, layer_o, bs):
    @pl.when(pl.program_id(1) == 0)
    def _():
        so_ref[0] = s0_ref[0]

    lbl = lbl_ref[...]
    ex = jnp.exp(lbl - jnp.max(lbl, axis=0, keepdims=True))
    sm = ex / jnp.sum(ex, axis=0, keepdims=True)
    lb_all = [sm[0:1, :]]
    for r in range(1, lbl.shape[0]):
        lb_all.append(lb_all[-1] + sm[r:r + 1, :])
    lb = lb_all[layer_o] - lb_all[0]

    fx = f_ref[...]
    e1 = jnp.exp(-jnp.abs(fx))
    log_sig = jnp.minimum(fx, 0.0) - jnp.log1p(e1)
    la = jnp.log(lb)
    lb_ = jnp.log1p(-lb) + log_sig
    logf = jnp.maximum(la, lb_) + jnp.log1p(jnp.exp(-jnp.abs(la - lb_)))
    kk = (1.0 - lb) * (jnp.where(fx >= 0.0, e1, 1.0) / (1.0 + e1))
    if lv < lc:
        valid = lax.broadcasted_iota(jnp.int32, fx.shape, 0) < lv
        logf = jnp.where(valid, logf, 0.0)
        kk = jnp.where(valid, kk, 0.0)

    nb = lc // bs
    r_i = lax.broadcasted_iota(jnp.int32, (lc, lc), 0)
    c_i = lax.broadcasted_iota(jnp.int32, (lc, lc), 1)
    sh = int(math.log2(bs))
    blk_lower = ((c_i <= r_i) & ((c_i >> sh) == (r_i >> sh))).astype(F32)
    gw = jnp.dot(blk_lower, logf, precision=HI)
    q = q_ref[...]
    v = i_ref[...]
    blk = lambda a, i: a[i * bs:(i + 1) * bs, :]
    tots = [gw[(i + 1) * bs - 1:(i + 1) * bs, :] for i in range(nb)]
    before = [jnp.zeros_like(tots[0])]
    for i in range(nb):
        before.append(before[-1] + tots[i])
    g_tot = before[nb]
    qt = q * jnp.exp(gw)
    kt = [blk(kk, j) * jnp.exp(tots[j] - blk(gw, j)) for j in range(nb)]
    q_in = jnp.concatenate([blk(qt, i) * jnp.exp(before[i]) for i in range(nb)], axis=0)
    k_out = jnp.concatenate([kt[j] * jnp.exp(g_tot - before[j + 1]) for j in range(nb)], axis=0)

    hsl = lambda h: slice(h * DK_C, (h + 1) * DK_C)
    row_b = lax.broadcasted_iota(jnp.int32, (bs, D_C), 0)
    o_blocks = []
    for i in range(nb):
        g_i, q_i, k_i, v_i = blk(gw, i), blk(q, i), blk(kk, i), blk(v, i)
        acc = [jnp.zeros((bs, DV_C), F32) for _ in range(H_C)]
        for s in range(bs):
            dec = jnp.exp(jnp.where(row_b >= s, g_i - g_i[s:s + 1, :], -jnp.inf))
            p = q_i * k_i[s:s + 1, :] * dec
            for h in range(H_C):
                a = jnp.sum(p[:, hsl(h)], axis=1, keepdims=True)
                acc[h] = acc[h] + a * v_i[s:s + 1, hsl(h)]
        if i > 0:
            k_hat = jnp.concatenate(
                [kt[j] if j == i - 1 else kt[j] * jnp.exp(before[i] - before[j + 1])
                 for j in range(i)], axis=0)
            q_ti = blk(qt, i)
            v_prev = v[0:i * bs, :]
            for h in range(H_C):
                att = lax.dot_general(q_ti[:, hsl(h)], k_hat[:, hsl(h)], _NT,
                                      preferred_element_type=F32)
                acc[h] = acc[h] + jnp.dot(att, v_prev[:, hsl(h)], preferred_element_type=F32)
        o_blocks.append(jnp.concatenate(acc, axis=1))
    o_intra = jnp.concatenate(o_blocks, axis=0) if nb > 1 else o_blocks[0]

    outs = []
    for h in range(H_C):
        s_h = so_ref[0, h]
        o_h = o_intra[:, hsl(h)] + jnp.dot(q_in[:, hsl(h)], s_h, preferred_element_type=F32)
        outs.append(o_h * lax.rsqrt(jnp.mean(o_h * o_h, axis=-1, keepdims=True) + EPS))
        dec_col = jnp.transpose(jnp.broadcast_to(jnp.exp(g_tot[:, hsl(h)]), (DK_C, DK_C)))
        so_ref[0, h] = dec_col * s_h + lax.dot_general(k_out[:, hsl(h)], v[:, hsl(h)], _TN,
                                                       preferred_element_type=F32)
    gate = g_ref[...]
    o_ref[...] = (jnp.concatenate(outs, axis=1) * nw_ref[...]
                  * (gate * _sigmoid(gate))).astype(o_ref.dtype)


def _hgrn(p, state, prm, o, b, l, lc, lv):
    nc = l // lc
    rowblk = lambda col: pl.BlockSpec((lc, D_C), lambda i, c: (i * nc + c, col))
    n_odd = prm['lb_logits'].shape[0]
    out, so = pl.pallas_call(
        functools.partial(_hgrn_kernel, lc=lc, lv=lv, layer_o=o, bs=min(16, lc)),
        grid=(b, nc),
        in_specs=[
            rowblk(0), rowblk(1), rowblk(2), rowblk(3),
            pl.BlockSpec((n_odd, D_C), lambda i, c: (0, 0)),
            pl.BlockSpec((1, H_C, DK_C, DV_C), lambda i, c: (i, 0, 0, 0)),
            pl.BlockSpec((1, D_C), lambda i, c: (0, 0)),
        ],
        out_specs=[
            rowblk(0),
            pl.BlockSpec((1, H_C, DK_C, DV_C), lambda i, c: (i, 0, 0, 0)),
        ],
        out_shape=[
            jax.ShapeDtypeStruct((b * l, D_C), MXU_DTYPE),
            jax.ShapeDtypeStruct(state.shape, F32),
        ],
        compiler_params=_cparams(("parallel", "arbitrary")),
        name="hgrn",
    )(p, p, p, p, prm['lb_logits'], state, prm['norm_c'][o].reshape(1, D_C))
    return out, so


def _prep_weights(prm):
    w = {}
    a0, a1, a2 = D_A, D_A + CONV_DIM_A, D_A + CONV_DIM_A + H_A
    q1 = a2 + H_B * DK_B
    k1 = q1 + H_B * DK_B
    v1 = k1 + D_B
    o1 = v1 + D_B
    wab = prm['w_in_ab']
    w['in_ab'] = jnp.concatenate(
        [wab[:, :, :a0], wab[:, :, k1:v1], wab[:, :, v1:o1], wab[:, :, a0:a1],
         wab[:, :, a2:q1], wab[:, :, q1:k1]], axis=2).astype(MXU_DTYPE)
    n_gate = H_A + 2 * H_B
    w['gates_ab'] = jnp.pad(
        jnp.concatenate([wab[:, :, a1:a2], wab[:, :, o1:]], axis=2),
        ((0, 0), (0, 0), (0, GATE_W - n_gate))).astype(MXU_DTYPE)
    w['out_a'] = prm['w_out_ab'][:, :D_A].astype(MXU_DTYPE)
    w['out_b'] = prm['w_out_ab'][:, D_A:].astype(MXU_DTYPE)
    w['in_c'] = prm['w_in_c'].astype(MXU_DTYPE)
    w['out_c'] = prm['w_out_c'].astype(MXU_DTYPE)
    w['ffn_gu'] = jnp.concatenate([prm['w_ffn_g'], prm['w_ffn_u']], axis=2).astype(MXU_DTYPE)
    w['ffn_d'] = prm['w_ffn_d'].astype(MXU_DTYPE)
    return w


def _trunk(x, mod, states, prm, w, *, b, l, lc, lv, tm, tn, ffn_lc, weights_outer):
    conv_a, ssd, mem_c, mem_n, mem_m, hgrn, ffn_buf = states
    rows_per_mod = l if mod.shape[2] == 1 else tm
    mm = functools.partial(_mm, tm=tm, tn=tn, weights_outer=weights_outer)
    n_conv, n_ssd, n_c, n_n, n_m, n_hgrn, n_ffn = [], [], [], [], [], [], []
    nc = l // lc
    for layer in range(DEPTH):
        modl = mod[layer]
        hmix = _norm_mod(x, prm['norm_mix'][layer], modl, 1, 0, tm, rows_per_mod)
        if layer % 2 == 0:
            e = layer // 2
            p1 = mm([hmix], [w['in_ab'][e]])
            gates = _mm([hmix], [w['gates_ab'][e]], tm, GATE_W, weights_outer=weights_outer)
            gates_t = gates.reshape(b * nc, lc, GATE_W).transpose(0, 2, 1)
            ya, cv, ss = _ssd(p1, gates, gates_t, conv_a[e], ssd[e], prm, e, b, l, lc, lv)
            hb, cc, nn, mmm = _mlstm(p1, gates, gates_t, mem_c[e], mem_n[e], mem_m[e],
                                     prm, e, b, l, lc, lv)
            n_conv.append(cv); n_ssd.append(ss); n_c.append(cc); n_n.append(nn); n_m.append(mmm)
            x = mm([ya, hb], [w['out_a'][e], w['out_b'][e]], res=(x, modl, 2, rows_per_mod))
        else:
            o = layer // 2
            p = mm([hmix], [w['in_c'][o]])
            oc, hs = _hgrn(p, hgrn[o], prm, o, b, l, lc, lv)
            n_hgrn.append(hs)
            x = mm([oc], [w['out_c'][o]], res=(x, modl, 2, rows_per_mod))
        hffn = _norm_mod(x, prm['norm_ffn'][layer], modl, 4, 3, tm, rows_per_mod)
        gu = mm([hffn], [w['ffn_gu'][layer]])
        act, fb = _ffn_act(gu, ffn_buf[layer], prm['conv_w_f'][layer], prm['conv_b_f'][layer],
                           b, l, ffn_lc, lv if ffn_lc == lc else ffn_lc)
        n_ffn.append(fb)
        x = mm([act], [w['ffn_d'][layer]], res=(x, modl, 5, rows_per_mod))
    y = _final_norm(x, prm['norm_f'], tm)
    return (y, jnp.stack(n_conv), jnp.stack(n_ssd), jnp.stack(n_c), jnp.stack(n_n),
            jnp.stack(n_m), jnp.stack(n_hgrn), jnp.stack(n_ffn))


def kernel(x_prompt, x_sample, c_prompt, c_sample, state_ssd_conv, state_ssd, state_mlstm_c, state_mlstm_n, state_mlstm_m, state_hgrn, state_ffn_conv, w_ada, b_ada, norm_mix, norm_ffn, w_in_ab, conv_w_a, conv_b_a, dt_bias, a_log, d_skip, norm_a, i_bias, f_bias, norm_b, w_out_ab, w_in_c, lb_logits, norm_c, w_out_c, w_ffn_g, w_ffn_u, conv_w_f, conv_b_f, w_ffn_d, norm_f):
    prm = dict(norm_mix=norm_mix, norm_ffn=norm_ffn, w_in_ab=w_in_ab, conv_w_a=conv_w_a,
               conv_b_a=conv_b_a, dt_bias=dt_bias, a_log=a_log, d_skip=d_skip, norm_a=norm_a,
               i_bias=i_bias, f_bias=f_bias, norm_b=norm_b, w_out_ab=w_out_ab, w_in_c=w_in_c,
               lb_logits=lb_logits, norm_c=norm_c, w_out_c=w_out_c, w_ffn_g=w_ffn_g,
               w_ffn_u=w_ffn_u, conv_w_f=conv_w_f, conv_b_f=conv_b_f, w_ffn_d=w_ffn_d,
               norm_f=norm_f)
    bp, lp, _ = x_prompt.shape
    bs, ls, _ = x_sample.shape
    n_even, n_odd = state_ssd.shape[0], state_hgrn.shape[0]
    w = _prep_weights(prm)

    n_c = bp + bs
    n_c_pad = -(-n_c // 8) * 8
    c_all = jnp.pad(jnp.concatenate([c_prompt, c_sample], axis=0), ((0, n_c_pad - n_c), (0, 0)))
    mod_all = _ada(c_all, w_ada, b_ada)
    mod_p = mod_all[:, :bp].reshape(DEPTH, bp, 1, 6 * D_MODEL)

    zeros = lambda *s: jnp.zeros(s, F32)
    st_p = (zeros(n_even, bp, CONV_K_A - 1, CONV_DIM_A), zeros(n_even, bp, H_A, P_A, N_A),
            zeros(n_even, bp, H_B, DK_B, DV_B), zeros(n_even, bp, H_B, DK_B),
            zeros(n_even, bp, H_B), zeros(n_odd, bp, H_C, DK_C, DV_C),
            zeros(DEPTH, bp, CONV_K_F - 1, D_FF))
    lc_p = math.gcd(lp, PROMPT_CHUNK)
    out_p = _trunk(x_prompt.reshape(bp * lp, D_MODEL), mod_p, st_p, prm, w,
                   b=bp, l=lp, lc=lc_p, lv=lc_p, tm=1024, tn=512, ffn_lc=256,
                   weights_outer=False)

    tm_s = 256
    xs = jnp.pad(x_sample, ((0, 0), (0, SAMPLE_PAD - ls), (0, 0))).reshape(bs * SAMPLE_PAD, D_MODEL)
    mod_s = jnp.repeat(mod_all[:, bp:n_c], SAMPLE_PAD, axis=1)
    mod_s = mod_s.reshape(DEPTH, bs * SAMPLE_PAD // tm_s, tm_s, 6 * D_MODEL)
    st_s = (state_ssd_conv, state_ssd, state_mlstm_c, state_mlstm_n, state_mlstm_m, state_hgrn,
            state_ffn_conv)
    out_s = _trunk(xs, mod_s, st_s, prm, w, b=bs, l=SAMPLE_PAD, lc=SAMPLE_PAD, lv=ls,
                   tm=tm_s, tn=512, ffn_lc=SAMPLE_PAD, weights_outer=True)

    y_p = out_p[0].reshape(bp, lp, D_MODEL)
    y_s = out_s[0].reshape(bs, SAMPLE_PAD, D_MODEL)[:, :ls]
    return (y_p, y_s) + tuple(out_p[1:]) + tuple(out_s[1:])
```

```python
import collections
import functools
import math

import jax
import jax.numpy as jnp
from jax import lax
from jax.experimental import pallas as pl
from jax.experimental.pallas import tpu as pltpu

F32 = jnp.float32
MXU_DTYPE = jnp.bfloat16
HI = lax.Precision.HIGHEST
NEG_BIG = -1e30

D_MODEL = 2048
DEPTH = 4
EPS = 1e-6
PROMPT_CHUNK = 64
H_A, P_A, G_A, N_A, CONV_K_A = 32, 64, 4, 128, 4
D_A = H_A * P_A
CONV_DIM_A = D_A + 2 * G_A * N_A
H_B, DK_B, DV_B = 8, 128, 256
D_B = H_B * DV_B
H_C, DK_C, DV_C = 16, 128, 128
D_C = H_C * DV_C
D_FF, CONV_K_F = 5632, 3
GATE_W = 128
SUBLANES = 8
VMEM_LIMIT = 56 * 1024 * 1024

_NT = (((1,), (1,)), ((), ()))
_TN = (((0,), (0,)), ((), ()))

Geo = collections.namedtuple("Geo", "n_seq seq_len seq_blk row_blk chunk valid mod_off")


def _cparams(sem):
    return pltpu.CompilerParams(dimension_semantics=sem, vmem_limit_bytes=VMEM_LIMIT)


def _sigmoid(x):
    return 1.0 / (1.0 + jnp.exp(-x))


def _softplus(x):
    return jnp.maximum(x, 0.0) + jnp.log1p(jnp.exp(-jnp.abs(x)))


def _log_sigmoid(x):
    return jnp.minimum(x, 0.0) - jnp.log1p(jnp.exp(-jnp.abs(x)))


def _tri(n):
    r = lax.broadcasted_iota(jnp.int32, (n, n), 0)
    c = lax.broadcasted_iota(jnp.int32, (n, n), 1)
    mask = c <= r
    return mask.astype(F32), (r <= c).astype(F32), mask


def _shifted(x, tails, k, axis=0):
    row = lax.broadcasted_iota(jnp.int32, x.shape, axis)
    out = pltpu.roll(x, k, axis)
    for r in range(k):
        out = jnp.where(row == r, tails[len(tails) - k + r], out)
    return out


def _drop_ref(fn, idx):
    def wrapped(*refs):
        return fn(*refs[:idx], *refs[idx + 1:])
    return wrapped


def _rmsnorm_rows(x):
    return x * lax.rsqrt(jnp.mean(x * x, axis=-1, keepdims=True) + EPS)


def _ada_kernel(c_ref, w_ref, b_ref, o_ref):
    c = c_ref[...]
    ca = (c * _sigmoid(c)).astype(MXU_DTYPE)
    o_ref[0] = jnp.dot(ca, w_ref[0].astype(MXU_DTYPE), preferred_element_type=F32) + b_ref[0]


def _ada(c_all, w_ada, b_ada, tn=1024):
    rows = c_all.shape[0]
    n = w_ada.shape[2]
    return pl.pallas_call(
        _ada_kernel,
        grid=(DEPTH, n // tn),
        in_specs=[
            pl.BlockSpec((rows, D_MODEL), lambda l, j: (0, 0)),
            pl.BlockSpec((1, D_MODEL, tn), lambda l, j: (l, 0, j)),
            pl.BlockSpec((1, 1, tn), lambda l, j: (l, 0, j)),
        ],
        out_specs=pl.BlockSpec((1, rows, tn), lambda l, j: (l, 0, j)),
        out_shape=jax.ShapeDtypeStruct((DEPTH, rows, n), F32),
        compiler_params=_cparams(("parallel", "parallel")),
        name="ada",
    )(c_all, w_ada, b_ada.reshape(DEPTH, 1, n))


def _tiles(geo):
    nrt = geo.seq_len // geo.row_blk
    return nrt, (geo.n_seq // geo.seq_blk) * nrt, geo.seq_blk * geo.row_blk


def _x_spec(geo, width, col_of):
    nrt = geo.seq_len // geo.row_blk
    return pl.BlockSpec((geo.seq_blk, geo.row_blk, width),
                        lambda i, j: (i // nrt, i % nrt, col_of(j)))


def _mod_spec(geo, layer, width, col_of):
    nrt = geo.seq_len // geo.row_blk
    return pl.BlockSpec((1, geo.seq_blk, 1, width),
                        lambda i, j: (layer, geo.mod_off + i // nrt, 0, col_of(j)))


def _norm_mod_to_scratch(x_ref, nw_ref, sc_ref, sh_ref, h_scr):
    h = (_rmsnorm_rows(x_ref[...]) * nw_ref[...]) * (1.0 + sc_ref[0]) + sh_ref[0]
    h_scr[...] = h.reshape(h_scr.shape).astype(h_scr.dtype)


def _mm_norm_kernel(x_ref, nw_ref, sc_ref, sh_ref, w_ref, o_ref, h_scr):
    @pl.when(pl.program_id(1) == 0)
    def _():
        _norm_mod_to_scratch(x_ref, nw_ref, sc_ref, sh_ref, h_scr)

    o_ref[...] = jnp.dot(h_scr[...], w_ref[...], preferred_element_type=F32).astype(o_ref.dtype)


def _mm_norm(x3, nw, mod, layer, k_sc, k_sh, w, geo, tn):
    _, n_i, tm = _tiles(geo)
    n = w.shape[1]
    return pl.pallas_call(
        _mm_norm_kernel,
        grid=(n_i, n // tn),
        in_specs=[
            _x_spec(geo, D_MODEL, lambda j: 0),
            pl.BlockSpec((1, D_MODEL), lambda i, j: (0, 0)),
            _mod_spec(geo, layer, D_MODEL, lambda j: k_sc),
            _mod_spec(geo, layer, D_MODEL, lambda j: k_sh),
            pl.BlockSpec((D_MODEL, tn), lambda i, j: (0, j)),
        ],
        out_specs=pl.BlockSpec((tm, tn), lambda i, j: (i, j)),
        out_shape=jax.ShapeDtypeStruct((geo.n_seq * geo.seq_len, n), F32),
        scratch_shapes=[pltpu.VMEM((tm, D_MODEL), MXU_DTYPE)],
        compiler_params=_cparams(("parallel", "arbitrary")),
        name="mm_norm",
    )(x3, nw.reshape(1, D_MODEL), mod, mod, w)


def _mm_res_kernel(*refs, n_lhs):
    a_refs, w_refs = refs[:n_lhs], refs[n_lhs:2 * n_lhs]
    xres_ref, gate_ref, o_ref = refs[2 * n_lhs:]
    acc = None
    for a_ref, w_ref in zip(a_refs, w_refs):
        d = jnp.dot(a_ref[...], w_ref[...], preferred_element_type=F32)
        acc = d if acc is None else acc + d
    o_ref[...] = xres_ref[...] + gate_ref[0] * acc.reshape(o_ref.shape)


def _mm_res(a_list, w_list, x3, mod, layer, k_gate, geo, tn):
    _, n_i, tm = _tiles(geo)
    per = D_MODEL // tn
    in_specs = [pl.BlockSpec((tm, a.shape[1]), lambda i, j: (i, 0)) for a in a_list]
    in_specs += [pl.BlockSpec((w.shape[0], tn), lambda i, j: (0, j)) for w in w_list]
    in_specs += [_x_spec(geo, tn, lambda j: j),
                 _mod_spec(geo, layer, tn, lambda j: k_gate * per + j)]
    return pl.pallas_call(
        functools.partial(_mm_res_kernel, n_lhs=len(a_list)),
        grid=(n_i, per),
        in_specs=in_specs,
        out_specs=_x_spec(geo, tn, lambda j: j),
        out_shape=jax.ShapeDtypeStruct(x3.shape, F32),
        compiler_params=_cparams(("parallel", "parallel")),
        name="mm_res",
    )(*a_list, *w_list, x3, mod)


def _ffn_in_kernel(x_ref, nw_ref, sc_ref, sh_ref, wg_ref, wu_ref, cs_ref, cw_ref, cb_ref,
                   a_ref, cso_ref, h_scr, tail_scr, *, nrt, valid):
    i, j = pl.program_id(0), pl.program_id(1)

    @pl.when(j == 0)
    def _():
        _norm_mod_to_scratch(x_ref, nw_ref, sc_ref, sh_ref, h_scr)

    h = h_scr[...]
    seq_blk, row_blk, tn = x_ref.shape[0], x_ref.shape[1], a_ref.shape[1]
    g = jnp.dot(h, wg_ref[...], preferred_element_type=F32).reshape(seq_blk, row_blk, tn)
    u = jnp.dot(h, wu_ref[...], preferred_element_type=F32).reshape(seq_blk, row_blk, tn)
    prev = cs_ref[0]
    if nrt > 1:
        prev = jnp.where(i % nrt == 0, prev, tail_scr[j])
    t2, t1 = prev[:, 0:1, :], prev[:, 1:2, :]
    w = cw_ref[...]
    y = (cb_ref[...] + w[0:1] * _shifted(g, [t2, t1], 2, axis=1)
         + w[1:2] * _shifted(g, [t2, t1], 1, axis=1) + w[2:3] * g)
    a_ref[...] = (y * _sigmoid(y) * u).reshape(a_ref.shape).astype(a_ref.dtype)
    new_tail = g[:, valid - 2:valid, :]
    if nrt > 1:
        tail_scr[j] = new_tail
    cso_ref[0] = new_tail


def _ffn_in(x3, nw, mod, layer, w_g, w_u, conv_state, conv_w, conv_b, geo, tn):
    nrt, n_i, tm = _tiles(geo)
    n_j = D_FF // tn
    valid = geo.row_blk if geo.valid == geo.chunk else geo.valid
    tail_shape = (n_j, geo.seq_blk, CONV_K_F - 1, tn) if nrt > 1 else (1, 1, CONV_K_F - 1, 128)
    cs_spec = pl.BlockSpec((1, geo.seq_blk, CONV_K_F - 1, tn), lambda i, j: (layer, i // nrt, 0, j))
    act, tails = pl.pallas_call(
        functools.partial(_ffn_in_kernel, nrt=nrt, valid=valid),
        grid=(n_i, n_j),
        in_specs=[
            _x_spec(geo, D_MODEL, lambda j: 0),
            pl.BlockSpec((1, D_MODEL), lambda i, j: (0, 0)),
            _mod_spec(geo, layer, D_MODEL, lambda j: 4),
            _mod_spec(geo, layer, D_MODEL, lambda j: 3),
            pl.BlockSpec((D_MODEL, tn), lambda i, j: (0, j)),
            pl.BlockSpec((D_MODEL, tn), lambda i, j: (0, j)),
            cs_spec,
            pl.BlockSpec((CONV_K_F, tn), lambda i, j: (0, j)),
            pl.BlockSpec((1, tn), lambda i, j: (0, j)),
        ],
        out_specs=[
            pl.BlockSpec((tm, tn), lambda i, j: (i, j)),
            pl.BlockSpec((1, geo.seq_blk, CONV_K_F - 1, tn), lambda i, j: (i, 0, 0, j)),
        ],
        out_shape=[
            jax.ShapeDtypeStruct((geo.n_seq * geo.seq_len, D_FF), MXU_DTYPE),
            jax.ShapeDtypeStruct((n_i, geo.seq_blk, CONV_K_F - 1, D_FF), F32),
        ],
        scratch_shapes=[pltpu.VMEM((tm, D_MODEL), MXU_DTYPE),
                        pltpu.VMEM(tail_shape, F32)],
        compiler_params=_cparams(("arbitrary", "arbitrary")),
        name="ffn_in",
    )(x3, nw.reshape(1, D_MODEL), mod, mod, w_g, w_u, conv_state, conv_w, conv_b.reshape(1, D_FF))
    last = tails.reshape(n_i // nrt, nrt, geo.seq_blk, CONV_K_F - 1, D_FF)[:, nrt - 1]
    return act, last.reshape(1, geo.n_seq, CONV_K_F - 1, D_FF)


def _norm_kernel(x_ref, nw_ref, o_ref):
    o_ref[...] = _rmsnorm_rows(x_ref[...]) * nw_ref[...]


def _final_norm(x3, nw, geo):
    _, n_i, _ = _tiles(geo)
    return pl.pallas_call(
        _norm_kernel,
        grid=(n_i, 1),
        in_specs=[_x_spec(geo, D_MODEL, lambda j: 0),
                  pl.BlockSpec((1, D_MODEL), lambda i, j: (0, 0))],
        out_specs=_x_spec(geo, D_MODEL, lambda j: 0),
        out_shape=jax.ShapeDtypeStruct(x3.shape, F32),
        compiler_params=_cparams(("parallel", "arbitrary")),
        name="final_norm",
    )(x3, nw.reshape(1, D_MODEL))


def _state_spec(shape, layer):
    rest = tuple(shape[2:])
    zeros = (0,) * len(rest)
    return pl.BlockSpec((1, 1) + rest, lambda i, c: (layer, i) + zeros)


def _seq_call(kernel_fn, geo, name, in_specs, args, out_specs, out_shapes, scratch, stacked_prev):
    in_specs, args = list(in_specs), list(args)
    aliases = {}
    for out_idx, arr in sorted(stacked_prev.items(), reverse=True):
        kernel_fn = _drop_ref(kernel_fn, len(in_specs))
        aliases[len(in_specs)] = out_idx
        in_specs.append(pl.BlockSpec(memory_space=pl.ANY))
        args.append(arr)
    if len(stacked_prev) > 1:
        raise NotImplementedError("one aliased state per call")
    return pl.pallas_call(
        kernel_fn,
        grid=(geo.n_seq, geo.seq_len // geo.chunk),
        in_specs=in_specs,
        out_specs=out_specs,
        out_shape=out_shapes,
        scratch_shapes=scratch,
        input_output_aliases=aliases,
        compiler_params=_cparams(("parallel", "arbitrary")),
        name=name,
    )(*args)


def _ssd_kernel(z_ref, xbc_ref, gc_ref, gr_ref, cs_ref, s0_ref, cw_ref, cb_ref,
                dtb_r_ref, dtb_c_ref, al_r_ref, al_c_ref, dsk_ref, nw_ref, exp_ref,
                y_ref, cso_ref, so_ref, tail_scr, *, lc, lv):
    @pl.when(pl.program_id(1) == 0)
    def _():
        tail_scr[5:8, :] = cs_ref[0, 0]
        so_ref[0, 0] = s0_ref[0, 0]

    x = xbc_ref[...]
    tails = [tail_scr[5:6, :], tail_scr[6:7, :], tail_scr[7:8, :]]
    w = cw_ref[0]
    xc = (cb_ref[0] + w[0:1] * _shifted(x, tails, 3) + w[1:2] * _shifted(x, tails, 2)
          + w[2:3] * _shifted(x, tails, 1) + w[3:4] * x)
    xc = xc * _sigmoid(xc)
    new_tail = x[lv - 3:lv, :]
    tail_scr[5:8, :] = new_tail
    cso_ref[0, 0] = new_tail
    xa = xc[:, :D_A]
    bm = xc[:, D_A:D_A + G_A * N_A]
    cm = xc[:, D_A + G_A * N_A:]

    dt_c = _softplus(gc_ref[:, 0:H_A] + dtb_r_ref[...])
    dt_r = _softplus(gr_ref[0, 0:H_A, :] + dtb_c_ref[...])
    if lv < lc:
        dt_c = jnp.where(lax.broadcasted_iota(jnp.int32, dt_c.shape, 0) < lv, dt_c, 0.0)
        dt_r = jnp.where(lax.broadcasted_iota(jnp.int32, dt_r.shape, 1) < lv, dt_r, 0.0)
    lower, upper, mask = _tri(lc)
    cum_c = jnp.dot(lower, dt_c * (-jnp.exp(al_r_ref[...])), precision=HI)
    cum_r = jnp.dot(dt_r * (-jnp.exp(al_c_ref[...])), upper, precision=HI)
    cum_last = cum_c[lc - 1:lc, :]
    expand = exp_ref[...]
    ecum_x = jnp.dot(jnp.exp(cum_c), expand, precision=HI)
    tail_x = jnp.dot(jnp.exp(cum_last - cum_c) * dt_c, expand, precision=HI)

    hg = H_A // G_A
    gw = hg * P_A
    ys = []
    for g in range(G_A):
        cg = cm[:, g * N_A:(g + 1) * N_A]
        bg = bm[:, g * N_A:(g + 1) * N_A]
        cb_ts = lax.dot_general(cg, bg, _NT, preferred_element_type=F32)
        s_g = so_ref[0, 0, g]
        y_inter = lax.dot_general(cg, s_g, _NT, preferred_element_type=F32)
        xg = xa[:, g * gw:(g + 1) * gw]
        pieces = []
        for j in range(hg // 2):
            xp = xg[:, j * 128:(j + 1) * 128]
            lane = lax.broadcasted_iota(jnp.int32, xp.shape, 1)
            acc = None
            for half in range(2):
                h = g * hg + 2 * j + half
                dec = jnp.exp(jnp.where(mask, cum_c[:, h:h + 1] - cum_r[h:h + 1, :], -jnp.inf))
                w_ts = cb_ts * dec * dt_r[h:h + 1, :]
                xh = jnp.where((lane >= P_A) if half else (lane < P_A), xp, 0.0)
                d = jnp.dot(w_ts, xh, preferred_element_type=F32)
                acc = d if acc is None else acc + d
            pieces.append(acc)
        y_intra = jnp.concatenate(pieces, axis=1)
        ys.append(y_intra + y_inter * ecum_x[:, g * gw:(g + 1) * gw])
        upd = lax.dot_general(xg * tail_x[:, g * gw:(g + 1) * gw], bg, _TN,
                              preferred_element_type=F32)
        for hh in range(hg):
            h = g * hg + hh
            rows = slice(hh * P_A, (hh + 1) * P_A)
            so_ref[0, 0, g, rows, :] = (s_g[rows, :] * jnp.exp(cum_r[h:h + 1, lc - 1:lc])
                                        + upd[rows, :])

    y = jnp.concatenate(ys, axis=1) + dsk_ref[...] * xa
    z = z_ref[...]
    y = y * (z * _sigmoid(z))
    y = jnp.concatenate([_rmsnorm_rows(y[:, g * gw:(g + 1) * gw]) for g in range(G_A)], axis=1)
    y_ref[...] = (y * nw_ref[...]).astype(y_ref.dtype)


def _ssd(p1, gates, gates_t, conv_state, ssd_state, prm, e, geo, prev_state_out):
    lc, lv, b = geo.chunk, geo.valid, geo.n_seq
    nc = geo.seq_len // lc
    rowblk = lambda w, col: pl.BlockSpec((lc, w), lambda i, c: (i * nc + c, col))
    const2 = lambda shape: pl.BlockSpec(shape, lambda i, c: (0, 0))
    n_even = ssd_state.shape[0]
    s5 = ssd_state.reshape(n_even, b, G_A, (H_A // G_A) * P_A, N_A)
    expand = (jnp.arange(D_A)[None, :] // P_A == jnp.arange(H_A)[:, None]).astype(F32)
    y, cso, so = _seq_call(
        functools.partial(_ssd_kernel, lc=lc, lv=lv), geo, "ssd",
        in_specs=[
            rowblk(D_A, 0),
            rowblk(CONV_DIM_A, 2),
            rowblk(GATE_W, 0),
            pl.BlockSpec((1, GATE_W, lc), lambda i, c: (i * nc + c, 0, 0)),
            _state_spec(conv_state.shape, e),
            _state_spec(s5.shape, e),
            pl.BlockSpec((1, CONV_K_A, CONV_DIM_A), lambda i, c: (e, 0, 0)),
            pl.BlockSpec((1, 1, CONV_DIM_A), lambda i, c: (e, 0, 0)),
            const2((1, H_A)), const2((H_A, 1)), const2((1, H_A)), const2((H_A, 1)),
            const2((1, D_A)), const2((1, D_A)), const2((H_A, D_A)),
        ],
        args=(p1, p1, gates, gates_t, conv_state, s5,
              prm['conv_w_a'], prm['conv_b_a'].reshape(n_even, 1, CONV_DIM_A),
              prm['dt_bias'][e].reshape(1, H_A), prm['dt_bias'][e].reshape(H_A, 1),
              prm['a_log'][e].reshape(1, H_A), prm['a_log'][e].reshape(H_A, 1),
              jnp.repeat(prm['d_skip'][e], P_A).reshape(1, D_A),
              prm['norm_a'][e].reshape(1, D_A), expand),
        out_specs=[rowblk(D_A, 0), _state_spec((1,) + conv_state.shape[1:], 0),
                   _state_spec(s5.shape, e)],
        out_shapes=[
            jax.ShapeDtypeStruct((b * geo.seq_len, D_A), MXU_DTYPE),
            jax.ShapeDtypeStruct((1,) + conv_state.shape[1:], F32),
            jax.ShapeDtypeStruct(s5.shape, F32),
        ],
        scratch=[pltpu.VMEM((SUBLANES, CONV_DIM_A), F32)],
        stacked_prev={} if prev_state_out is None else {2: prev_state_out},
    )
    return y, cso, so


def _mlstm_kernel(q_ref, k_ref, v_ref, og_ref, gc_ref, gr_ref, c0_ref, n0_ref, m0_ref,
                  ib_r_ref, ib_c_ref, fb_r_ref, fb_c_ref, nw_ref,
                  h_ref, co_ref, no_ref, mo_ref, *, lc, lv):
    @pl.when(pl.program_id(1) == 0)
    def _():
        co_ref[0, 0] = c0_ref[0, 0]
        no_ref[0, 0] = n0_ref[0, 0]
        mo_ref[0, 0] = m0_ref[0, 0]

    i0, f0 = H_A, H_A + H_B
    li_c = gc_ref[:, i0:i0 + H_B] + ib_r_ref[...]
    lf_c = _log_sigmoid(gc_ref[:, f0:f0 + H_B] + fb_r_ref[...])
    li_r = gr_ref[0, i0:i0 + H_B, :] + ib_c_ref[...]
    lf_r = _log_sigmoid(gr_ref[0, f0:f0 + H_B, :] + fb_c_ref[...])
    if lv < lc:
        vc = lax.broadcasted_iota(jnp.int32, li_c.shape, 0) < lv
        vr = lax.broadcasted_iota(jnp.int32, li_r.shape, 1) < lv
        li_c, lf_c = jnp.where(vc, li_c, NEG_BIG), jnp.where(vc, lf_c, 0.0)
        li_r, lf_r = jnp.where(vr, li_r, NEG_BIG), jnp.where(vr, lf_r, 0.0)
    lower, upper, mask = _tri(lc)
    bc_c = jnp.dot(lower, lf_c, precision=HI)
    bc_r = jnp.dot(lf_r, upper, precision=HI)

    m_old = mo_ref[0, 0]
    n_old = no_ref[0, 0]
    lane_h = lax.broadcasted_iota(jnp.int32, (1, H_B), 1)
    m_out = jnp.zeros((1, H_B), F32)
    hs = []
    for h in range(H_B):
        qh = q_ref[:, h * DK_B:(h + 1) * DK_B] * (DK_B ** -0.5)
        kh = k_ref[:, h * DK_B:(h + 1) * DK_B]
        vh = v_ref[:, h * DV_B:(h + 1) * DV_B]
        m_prev = m_old[:, h:h + 1]
        bcc = bc_c[:, h:h + 1]
        dmat = jnp.where(mask, bcc - bc_r[h:h + 1, :] + li_r[h:h + 1, :], -jnp.inf)
        inter = bcc + m_prev
        m_t = jnp.maximum(inter, jnp.max(dmat, axis=1, keepdims=True))
        w_in = jnp.exp(inter - m_t)
        qk = lax.dot_general(qh, kh, _NT, preferred_element_type=F32)
        w_ts = jnp.exp(dmat - m_t) * qk
        c_h = co_ref[0, 0, h]
        n_h = n_old[h:h + 1, :]
        num = (jnp.dot(w_ts, vh, preferred_element_type=F32)
               + w_in * jnp.dot(qh, c_h, preferred_element_type=F32))
        den = (jnp.sum(w_ts, axis=1, keepdims=True)
               + w_in * jnp.sum(qh * n_h, axis=1, keepdims=True))
        hs.append(_rmsnorm_rows(num / jnp.maximum(jnp.abs(den), jnp.exp(-m_t))))
        m_new = m_t[lv - 1:lv, :]
        bc_last = bcc[lc - 1:lc, :]
        w_s = jnp.exp(bc_last - bcc + li_c[:, h:h + 1] - m_new)
        w_c = jnp.exp(bc_last + m_prev - m_new)
        ks = kh * w_s
        co_ref[0, 0, h] = w_c * c_h + lax.dot_general(ks, vh, _TN, preferred_element_type=F32)
        no_ref[0, 0, h:h + 1, :] = w_c * n_h + jnp.sum(ks, axis=0, keepdims=True)
        m_out = jnp.where(lane_h == h, m_new, m_out)
    mo_ref[0, 0] = m_out
    hn = jnp.concatenate(hs, axis=1) * nw_ref[...]
    h_ref[...] = (hn * _sigmoid(og_ref[...])).astype(h_ref.dtype)


def _mlstm(p1, gates, gates_t, c_state, n_state, m_state, prm, e, geo, prev_state_out):
    lc, lv, b = geo.chunk, geo.valid, geo.n_seq
    nc = geo.seq_len // lc
    rowblk = lambda w, col: pl.BlockSpec((lc, w), lambda i, c: (i * nc + c, col))
    const2 = lambda shape: pl.BlockSpec(shape, lambda i, c: (0, 0))
    qk_w = H_B * DK_B
    m4 = m_state.reshape(m_state.shape[0], b, 1, H_B)
    one = lambda shape: (1,) + tuple(shape[1:])
    h, co, no, mo = _seq_call(
        functools.partial(_mlstm_kernel, lc=lc, lv=lv), geo, "mlstm",
        in_specs=[
            rowblk(qk_w, 9), rowblk(qk_w, 10),
            rowblk(D_B, 1), rowblk(D_B, 2),
            rowblk(GATE_W, 0),
            pl.BlockSpec((1, GATE_W, lc), lambda i, c: (i * nc + c, 0, 0)),
            _state_spec(c_state.shape, e), _state_spec(n_state.shape, e), _state_spec(m4.shape, e),
            const2((1, H_B)), const2((H_B, 1)), const2((1, H_B)), const2((H_B, 1)),
            const2((1, D_B)),
        ],
        args=(p1, p1, p1, p1, gates, gates_t, c_state, n_state, m4,
              prm['i_bias'][e].reshape(1, H_B), prm['i_bias'][e].reshape(H_B, 1),
              prm['f_bias'][e].reshape(1, H_B), prm['f_bias'][e].reshape(H_B, 1),
              prm['norm_b'][e].reshape(1, D_B)),
        out_specs=[rowblk(D_B, 0), _state_spec(c_state.shape, e),
                   _state_spec(one(n_state.shape), 0), _state_spec(one(m4.shape), 0)],
        out_shapes=[
            jax.ShapeDtypeStruct((b * geo.seq_len, D_B), MXU_DTYPE),
            jax.ShapeDtypeStruct(c_state.shape, F32),
            jax.ShapeDtypeStruct(one(n_state.shape), F32),
            jax.ShapeDtypeStruct(one(m4.shape), F32),
        ],
        scratch=[],
        stacked_prev={} if prev_state_out is None else {1: prev_state_out},
    )
    return h, co, no, mo.reshape(1, b, H_B)


def _hgrn_kernel(q_ref, f_ref, i_ref, g_ref, lbl_ref, s0_ref, nw_ref, o_ref, so_ref,
                 *, lc, lv, layer_o, bs):
    @pl.when(pl.program_id(1) == 0)
    def _():
        so_ref[0, 0] = s0_ref[0, 0]

    lbl = lbl_ref[...]
    ex = jnp.exp(lbl - jnp.max(lbl, axis=0, keepdims=True))
    sm = ex / jnp.sum(ex, axis=0, keepdims=True)
    lb_all = [sm[0:1, :]]
    for r in range(1, lbl.shape[0]):
        lb_all.append(lb_all[-1] + sm[r:r + 1, :])
    lb = lb_all[layer_o] - lb_all[0]

    fx = f_ref[...]
    e1 = jnp.exp(-jnp.abs(fx))
    log_sig = jnp.minimum(fx, 0.0) - jnp.log1p(e1)
    la = jnp.log(lb)
    lb_ = jnp.log1p(-lb) + log_sig
    logf = jnp.maximum(la, lb_) + jnp.log1p(jnp.exp(-jnp.abs(la - lb_)))
    kk = (1.0 - lb) * (jnp.where(fx >= 0.0, e1, 1.0) / (1.0 + e1))
    if lv < lc:
        valid = lax.broadcasted_iota(jnp.int32, fx.shape, 0) < lv
        logf = jnp.where(valid, logf, 0.0)
        kk = jnp.where(valid, kk, 0.0)

    nb = lc // bs
    r_i = lax.broadcasted_iota(jnp.int32, (lc, lc), 0)
    c_i = lax.broadcasted_iota(jnp.int32, (lc, lc), 1)
    sh = int(math.log2(bs))
    blk_lower = ((c_i <= r_i) & ((c_i >> sh) == (r_i >> sh))).astype(F32)
    gw = jnp.dot(blk_lower, logf, precision=HI)
    q = q_ref[...]
    v = i_ref[...]
    blk = lambda a, i: a[i * bs:(i + 1) * bs, :]
    tots = [gw[(i + 1) * bs - 1:(i + 1) * bs, :] for i in range(nb)]
    before = [jnp.zeros_like(tots[0])]
    for i in range(nb):
        before.append(before[-1] + tots[i])
    g_tot = before[nb]
    qt = q * jnp.exp(gw)
    kt = [blk(kk, j) * jnp.exp(tots[j] - blk(gw, j)) for j in range(nb)]
    q_in = jnp.concatenate([blk(qt, i) * jnp.exp(before[i]) for i in range(nb)], axis=0)
    k_out = jnp.concatenate([kt[j] * jnp.exp(g_tot - before[j + 1]) for j in range(nb)], axis=0)

    hsl = lambda h: slice(h * DK_C, (h + 1) * DK_C)
    row_b = lax.broadcasted_iota(jnp.int32, (bs, D_C), 0)
    o_blocks = []
    for i in range(nb):
        g_i, q_i, k_i, v_i = blk(gw, i), blk(q, i), blk(kk, i), blk(v, i)
        acc = [jnp.zeros((bs, DV_C), F32) for _ in range(H_C)]
        for s in range(bs):
            dec = jnp.exp(jnp.where(row_b >= s, g_i - g_i[s:s + 1, :], -jnp.inf))
            p = q_i * k_i[s:s + 1, :] * dec
            for h in range(H_C):
                a = jnp.sum(p[:, hsl(h)], axis=1, keepdims=True)
                acc[h] = acc[h] + a * v_i[s:s + 1, hsl(h)]
        if i > 0:
            k_hat = jnp.concatenate(
                [kt[j] if j == i - 1 else kt[j] * jnp.exp(before[i] - before[j + 1])
                 for j in range(i)], axis=0)
            q_ti = blk(qt, i)
            v_prev = v[0:i * bs, :]
            for h in range(H_C):
                att = lax.dot_general(q_ti[:, hsl(h)], k_hat[:, hsl(h)], _NT,
                                      preferred_element_type=F32)
                acc[h] = acc[h] + jnp.dot(att, v_prev[:, hsl(h)], preferred_element_type=F32)
        o_blocks.append(jnp.concatenate(acc, axis=1))
    o_intra = jnp.concatenate(o_blocks, axis=0) if nb > 1 else o_blocks[0]

    outs = []
    for h in range(H_C):
        s_h = so_ref[0, 0, h]
        o_h = o_intra[:, hsl(h)] + jnp.dot(q_in[:, hsl(h)], s_h, preferred_element_type=F32)
        outs.append(_rmsnorm_rows(o_h))
        dec_col = jnp.transpose(jnp.broadcast_to(jnp.exp(g_tot[:, hsl(h)]), (DK_C, DK_C)))
        so_ref[0, 0, h] = dec_col * s_h + lax.dot_general(k_out[:, hsl(h)], v[:, hsl(h)], _TN,
                                                          preferred_element_type=F32)
    gate = g_ref[...]
    o_ref[...] = (jnp.concatenate(outs, axis=1) * nw_ref[...]
                  * (gate * _sigmoid(gate))).astype(o_ref.dtype)


def _hgrn(p, state, prm, o, geo, prev_state_out):
    lc, lv, b = geo.chunk, geo.valid, geo.n_seq
    nc = geo.seq_len // lc
    rowblk = lambda col: pl.BlockSpec((lc, D_C), lambda i, c: (i * nc + c, col))
    n_odd = prm['lb_logits'].shape[0]
    return _seq_call(
        functools.partial(_hgrn_kernel, lc=lc, lv=lv, layer_o=o, bs=min(16, lc)), geo, "hgrn",
        in_specs=[
            rowblk(0), rowblk(1), rowblk(2), rowblk(3),
            pl.BlockSpec((n_odd, D_C), lambda i, c: (0, 0)),
            _state_spec(state.shape, o),
            pl.BlockSpec((1, D_C), lambda i, c: (0, 0)),
        ],
        args=(p, p, p, p, prm['lb_logits'], state, prm['norm_c'][o].reshape(1, D_C)),
        out_specs=[rowblk(0), _state_spec(state.shape, o)],
        out_shapes=[
            jax.ShapeDtypeStruct((b * geo.seq_len, D_C), MXU_DTYPE),
            jax.ShapeDtypeStruct(state.shape, F32),
        ],
        scratch=[],
        stacked_prev={} if prev_state_out is None else {1: prev_state_out},
    )


def _prep_weights(prm):
    w = {}
    a0, a1, a2 = D_A, D_A + CONV_DIM_A, D_A + CONV_DIM_A + H_A
    q1 = a2 + H_B * DK_B
    k1 = q1 + H_B * DK_B
    v1 = k1 + D_B
    o1 = v1 + D_B
    wab = prm['w_in_ab']
    w['in_ab'] = jnp.concatenate(
        [wab[:, :, :a0], wab[:, :, k1:v1], wab[:, :, v1:o1], wab[:, :, a0:a1],
         wab[:, :, a2:q1], wab[:, :, q1:k1]], axis=2).astype(MXU_DTYPE)
    n_gate = H_A + 2 * H_B
    w['gates_ab'] = jnp.pad(
        jnp.concatenate([wab[:, :, a1:a2], wab[:, :, o1:]], axis=2),
        ((0, 0), (0, 0), (0, GATE_W - n_gate))).astype(MXU_DTYPE)
    w['out_a'] = prm['w_out_ab'][:, :D_A].astype(MXU_DTYPE)
    w['out_b'] = prm['w_out_ab'][:, D_A:].astype(MXU_DTYPE)
    w['in_c'] = prm['w_in_c'].astype(MXU_DTYPE)
    w['out_c'] = prm['w_out_c'].astype(MXU_DTYPE)
    w['ffn_g'] = prm['w_ffn_g'].astype(MXU_DTYPE)
    w['ffn_u'] = prm['w_ffn_u'].astype(MXU_DTYPE)
    w['ffn_d'] = prm['w_ffn_d'].astype(MXU_DTYPE)
    return w


def _trunk(x3, mod, states, prm, w, geo, tn=512):
    conv_a, ssd, mem_c, mem_n, mem_m, hgrn, ffn_buf = states
    n_conv, n_n, n_m, n_ffn = [], [], [], []
    new_ssd = new_c = new_hgrn = None
    nc = geo.seq_len // geo.chunk
    for layer in range(DEPTH):
        mm_norm = functools.partial(_mm_norm, x3, prm['norm_mix'][layer], mod, layer, 1, 0,
                                    geo=geo)
        if layer % 2 == 0:
            e = layer // 2
            p1 = mm_norm(w=w['in_ab'][e], tn=tn)
            gates = mm_norm(w=w['gates_ab'][e], tn=GATE_W)
            gates_t = gates.reshape(geo.n_seq * nc, geo.chunk, GATE_W).transpose(0, 2, 1)
            ya, cv, new_ssd = _ssd(p1, gates, gates_t, conv_a, ssd, prm, e, geo, new_ssd)
            hb, new_c, nn, mmm = _mlstm(p1, gates, gates_t, mem_c, mem_n, mem_m, prm, e, geo, new_c)
            n_conv.append(cv); n_n.append(nn); n_m.append(mmm)
            x3 = _mm_res([ya, hb], [w['out_a'][e], w['out_b'][e]], x3, mod, layer, 2, geo, tn)
        else:
            o = layer // 2
            p = mm_norm(w=w['in_c'][o], tn=tn)
            oc, new_hgrn = _hgrn(p, hgrn, prm, o, geo, new_hgrn)
            x3 = _mm_res([oc], [w['out_c'][o]], x3, mod, layer, 2, geo, tn)
        act, fb = _ffn_in(x3, prm['norm_ffn'][layer], mod, layer, w['ffn_g'][layer],
                          w['ffn_u'][layer], ffn_buf, prm['conv_w_f'][layer],
                          prm['conv_b_f'][layer], geo, tn)
        n_ffn.append(fb)
        x3 = _mm_res([act], [w['ffn_d'][layer]], x3, mod, layer, 5, geo, tn)
    y = _final_norm(x3, prm['norm_f'], geo)
    cat = lambda xs: jnp.concatenate(xs, axis=0)
    return (y, cat(n_conv), new_ssd.reshape(ssd.shape), new_c, cat(n_n), cat(n_m), new_hgrn,
            cat(n_ffn))


def kernel(x_prompt, x_sample, c_prompt, c_sample, state_ssd_conv, state_ssd, state_mlstm_c, state_mlstm_n, state_mlstm_m, state_hgrn, state_ffn_conv, w_ada, b_ada, norm_mix, norm_ffn, w_in_ab, conv_w_a, conv_b_a, dt_bias, a_log, d_skip, norm_a, i_bias, f_bias, norm_b, w_out_ab, w_in_c, lb_logits, norm_c, w_out_c, w_ffn_g, w_ffn_u, conv_w_f, conv_b_f, w_ffn_d, norm_f):
    prm = dict(norm_mix=norm_mix, norm_ffn=norm_ffn, w_in_ab=w_in_ab, conv_w_a=conv_w_a,
               conv_b_a=conv_b_a, dt_bias=dt_bias, a_log=a_log, d_skip=d_skip, norm_a=norm_a,
               i_bias=i_bias, f_bias=f_bias, norm_b=norm_b, w_out_ab=w_out_ab, w_in_c=w_in_c,
               lb_logits=lb_logits, norm_c=norm_c, w_out_c=w_out_c, w_ffn_g=w_ffn_g,
               w_ffn_u=w_ffn_u, conv_w_f=conv_w_f, conv_b_f=conv_b_f, w_ffn_d=w_ffn_d,
               norm_f=norm_f)
    bp, lp, _ = x_prompt.shape
    bs, ls, _ = x_sample.shape
    n_even, n_odd = state_ssd.shape[0], state_hgrn.shape[0]
    w = _prep_weights(prm)

    n_c = bs + bp
    n_c_pad = -(-n_c // SUBLANES) * SUBLANES
    c_all = jnp.pad(jnp.concatenate([c_sample, c_prompt], axis=0), ((0, n_c_pad - n_c), (0, 0)))
    mod = _ada(c_all, w_ada, b_ada).reshape(DEPTH, n_c_pad, 1, 6 * D_MODEL)

    zeros = lambda *s: jnp.zeros(s, F32)
    st_p = (zeros(n_even, bp, CONV_K_A - 1, CONV_DIM_A), zeros(n_even, bp, H_A, P_A, N_A),
            zeros(n_even, bp, H_B, DK_B, DV_B), zeros(n_even, bp, H_B, DK_B),
            zeros(n_even, bp, H_B), zeros(n_odd, bp, H_C, DK_C, DV_C),
            zeros(DEPTH, bp, CONV_K_F - 1, D_FF))
    lc_p = math.gcd(lp, PROMPT_CHUNK)
    geo_p = Geo(n_seq=bp, seq_len=lp, seq_blk=1, row_blk=min(lp, 1024), chunk=lc_p, valid=lc_p,
                mod_off=bs)
    out_p = _trunk(x_prompt, mod, st_p, prm, w, geo_p)

    geo_s = Geo(n_seq=bs, seq_len=SUBLANES, seq_blk=bs, row_blk=SUBLANES, chunk=SUBLANES,
                valid=ls, mod_off=0)
    xs = jnp.pad(x_sample, ((0, 0), (0, SUBLANES - ls), (0, 0)))
    st_s = (state_ssd_conv, state_ssd, state_mlstm_c, state_mlstm_n, state_mlstm_m, state_hgrn,
            state_ffn_conv)
    out_s = _trunk(xs, mod, st_s, prm, w, geo_s)
    return (out_p[0], out_s[0][:, :ls]) + tuple(out_p[1:]) + tuple(out_s[1:])
```

```python
import collections
import functools
import math

import jax
import jax.numpy as jnp
from jax import lax
from jax.experimental import pallas as pl
from jax.experimental.pallas import tpu as pltpu

F32 = jnp.float32
MXU_DTYPE = jnp.bfloat16
HI = lax.Precision.HIGHEST
NEG_BIG = -1e30
LOG2E = 1.4426950408889634

D_MODEL = 2048
DEPTH = 4
EPS = 1e-6
PROMPT_CHUNK = 64
H_A, P_A, G_A, N_A, CONV_K_A = 32, 64, 4, 128, 4
D_A = H_A * P_A
CONV_DIM_A = D_A + 2 * G_A * N_A
H_B, DK_B, DV_B = 8, 128, 256
D_B = H_B * DV_B
H_C, DK_C, DV_C = 16, 128, 128
D_C = H_C * DV_C
D_FF, CONV_K_F = 5632, 3
FFN_TN = 512
FFN_DOWN_TN = 512
GATE_W = 128
SUBLANES = 8
VMEM_LIMIT = 56 * 1024 * 1024

_NT = (((1,), (1,)), ((), ()))
_TN = (((0,), (0,)), ((), ()))

Geo = collections.namedtuple("Geo", "n_seq seq_len seq_blk row_blk chunk valid mod_off scan_seqs")


def _cparams(sem):
    return pltpu.CompilerParams(dimension_semantics=sem, vmem_limit_bytes=VMEM_LIMIT)


def _sigmoid(x):
    return 1.0 / (1.0 + jnp.exp(-x))


def _softplus(x):
    return jnp.maximum(x, 0.0) + jnp.log1p(jnp.exp(-jnp.abs(x)))


def _log_sigmoid(x):
    return jnp.minimum(x, 0.0) - jnp.log1p(jnp.exp(-jnp.abs(x)))


def _tri(n):
    r = lax.broadcasted_iota(jnp.int32, (n, n), 0)
    c = lax.broadcasted_iota(jnp.int32, (n, n), 1)
    mask = c <= r
    return mask.astype(F32), (r <= c).astype(F32), mask


def _shifted(x, tails, k, axis=0):
    row = lax.broadcasted_iota(jnp.int32, x.shape, axis)
    out = pltpu.roll(x, k, axis)
    for r in range(k):
        out = jnp.where(row == r, tails[len(tails) - k + r], out)
    return out


def _drop_ref(fn, idx):
    def wrapped(*refs):
        return fn(*refs[:idx], *refs[idx + 1:])
    return wrapped


def _at_first_chunk(single_chunk, init):
    if single_chunk:
        init()
    else:
        pl.when(pl.program_id(1) == 0)(init)


def _rmsnorm_rows(x):
    return x * lax.rsqrt(jnp.mean(x * x, axis=-1, keepdims=True) + EPS)


def _ada_kernel(c_ref, w_ref, b_ref, o_ref):
    c = c_ref[...]
    ca = (c * _sigmoid(c)).astype(MXU_DTYPE)
    o_ref[0] = jnp.dot(ca, w_ref[0].astype(MXU_DTYPE), preferred_element_type=F32) + b_ref[0]


def _ada(c_all, w_ada, b_ada, tn=1024):
    rows = c_all.shape[0]
    n = w_ada.shape[2]
    return pl.pallas_call(
        _ada_kernel,
        grid=(DEPTH, n // tn),
        in_specs=[
            pl.BlockSpec((rows, D_MODEL), lambda l, j: (0, 0)),
            pl.BlockSpec((1, D_MODEL, tn), lambda l, j: (l, 0, j)),
            pl.BlockSpec((1, 1, tn), lambda l, j: (l, 0, j)),
        ],
        out_specs=pl.BlockSpec((1, rows, tn), lambda l, j: (l, 0, j)),
        out_shape=jax.ShapeDtypeStruct((DEPTH, rows, n), F32),
        compiler_params=_cparams(("parallel", "parallel")),
        name="ada",
    )(c_all, w_ada, b_ada.reshape(DEPTH, 1, n))


def _tiles(geo):
    nrt = geo.seq_len // geo.row_blk
    return nrt, (geo.n_seq // geo.seq_blk) * nrt, geo.seq_blk * geo.row_blk


def _x_spec(geo, width, col_of):
    nrt = geo.seq_len // geo.row_blk
    return pl.BlockSpec((geo.seq_blk, geo.row_blk, width),
                        lambda i, j: (i // nrt, i % nrt, col_of(j)))


def _mod_spec(geo, layer, width, col_of):
    nrt = geo.seq_len // geo.row_blk
    return pl.BlockSpec((1, geo.seq_blk, 1, width),
                        lambda i, j: (layer, geo.mod_off + i // nrt, 0, col_of(j)))


def _norm_mod_to_scratch(x_ref, nw_ref, sc_ref, sh_ref, h_scr):
    h = (_rmsnorm_rows(x_ref[...]) * nw_ref[...]) * (1.0 + sc_ref[0]) + sh_ref[0]
    h_scr[...] = h.reshape(h_scr.shape).astype(h_scr.dtype)


def _mm_norm_kernel(x_ref, nw_ref, sc_ref, sh_ref, w_ref, o_ref, h_scr):
    @pl.when(pl.program_id(1) == 0)
    def _():
        _norm_mod_to_scratch(x_ref, nw_ref, sc_ref, sh_ref, h_scr)

    o_ref[...] = jnp.dot(h_scr[...], w_ref[...], preferred_element_type=F32).astype(o_ref.dtype)


def _mm_norm(x3, nw, mod, layer, k_sc, k_sh, w, geo, tn):
    _, n_i, tm = _tiles(geo)
    n = w.shape[1]
    return pl.pallas_call(
        _mm_norm_kernel,
        grid=(n_i, n // tn),
        in_specs=[
            _x_spec(geo, D_MODEL, lambda j: 0),
            pl.BlockSpec((1, D_MODEL), lambda i, j: (0, 0)),
            _mod_spec(geo, layer, D_MODEL, lambda j: k_sc),
            _mod_spec(geo, layer, D_MODEL, lambda j: k_sh),
            pl.BlockSpec((D_MODEL, tn), lambda i, j: (0, j)),
        ],
        out_specs=pl.BlockSpec((tm, tn), lambda i, j: (i, j)),
        out_shape=jax.ShapeDtypeStruct((geo.n_seq * geo.seq_len, n), F32),
        scratch_shapes=[pltpu.VMEM((tm, D_MODEL), MXU_DTYPE)],
        compiler_params=_cparams(("parallel", "arbitrary")),
        name="mm_norm",
    )(x3, nw.reshape(1, D_MODEL), mod, mod, w)


def _mm_res_kernel(*refs, n_lhs):
    a_refs, w_refs = refs[:n_lhs], refs[n_lhs:2 * n_lhs]
    xres_ref, gate_ref, o_ref = refs[2 * n_lhs:]
    acc = None
    for a_ref, w_ref in zip(a_refs, w_refs):
        d = jnp.dot(a_ref[...], w_ref[...], preferred_element_type=F32)
        acc = d if acc is None else acc + d
    o_ref[...] = xres_ref[...] + gate_ref[0] * acc.reshape(o_ref.shape)


def _mm_res(a_list, w_list, x3, mod, layer, k_gate, geo, tn):
    _, n_i, tm = _tiles(geo)
    per = D_MODEL // tn
    in_specs = [pl.BlockSpec((tm, a.shape[1]), lambda i, j: (i, 0)) for a in a_list]
    in_specs += [pl.BlockSpec((w.shape[0], tn), lambda i, j: (0, j)) for w in w_list]
    in_specs += [_x_spec(geo, tn, lambda j: j),
                 _mod_spec(geo, layer, tn, lambda j: k_gate * per + j)]
    return pl.pallas_call(
        functools.partial(_mm_res_kernel, n_lhs=len(a_list)),
        grid=(n_i, per),
        in_specs=in_specs,
        out_specs=_x_spec(geo, tn, lambda j: j),
        out_shape=jax.ShapeDtypeStruct(x3.shape, F32),
        compiler_params=_cparams(("parallel", "parallel")),
        name="mm_res",
    )(*a_list, *w_list, x3, mod)


def _ffn_in_kernel(x_ref, nw_ref, sc_ref, sh_ref, wg_ref, wu_ref, cs_ref, cw_ref, cb_ref,
                   a_ref, cso_ref, h_scr, tail_scr, *, nrt, valid):
    i, j = pl.program_id(0), pl.program_id(1)

    @pl.when(j == 0)
    def _():
        _norm_mod_to_scratch(x_ref, nw_ref, sc_ref, sh_ref, h_scr)

    h = h_scr[...]
    seq_blk, row_blk, tn = x_ref.shape[0], x_ref.shape[1], a_ref.shape[1]
    g = jnp.dot(h, wg_ref[...], preferred_element_type=F32).reshape(seq_blk, row_blk, tn)
    u = jnp.dot(h, wu_ref[...], preferred_element_type=F32).reshape(seq_blk, row_blk, tn)
    prev = cs_ref[0]
    if nrt > 1:
        prev = jnp.where(i % nrt == 0, prev, tail_scr[j])
    t2, t1 = prev[:, 0:1, :], prev[:, 1:2, :]
    w = cw_ref[...]
    y = (cb_ref[...] + w[0:1] * _shifted(g, [t2, t1], 2, axis=1)
         + w[1:2] * _shifted(g, [t2, t1], 1, axis=1) + w[2:3] * g)
    a_ref[...] = (y * _sigmoid(y) * u).reshape(a_ref.shape).astype(a_ref.dtype)
    new_tail = g[:, valid - 2:valid, :]
    if nrt > 1:
        tail_scr[j] = new_tail
    cso_ref[0] = new_tail


def _ffn_in(x3, nw, mod, layer, w_g, w_u, conv_state, conv_w, conv_b, geo, tn):
    nrt, n_i, tm = _tiles(geo)
    n_j = D_FF // tn
    valid = geo.row_blk if geo.valid == geo.chunk else geo.valid
    tail_shape = (n_j, geo.seq_blk, CONV_K_F - 1, tn) if nrt > 1 else (1, 1, CONV_K_F - 1, 128)
    cs_spec = pl.BlockSpec((1, geo.seq_blk, CONV_K_F - 1, tn), lambda i, j: (layer, i // nrt, 0, j))
    act, tails = pl.pallas_call(
        functools.partial(_ffn_in_kernel, nrt=nrt, valid=valid),
        grid=(n_i, n_j),
        in_specs=[
            _x_spec(geo, D_MODEL, lambda j: 0),
            pl.BlockSpec((1, D_MODEL), lambda i, j: (0, 0)),
            _mod_spec(geo, layer, D_MODEL, lambda j: 4),
            _mod_spec(geo, layer, D_MODEL, lambda j: 3),
            pl.BlockSpec((D_MODEL, tn), lambda i, j: (0, j)),
            pl.BlockSpec((D_MODEL, tn), lambda i, j: (0, j)),
            cs_spec,
            pl.BlockSpec((CONV_K_F, tn), lambda i, j: (0, j)),
            pl.BlockSpec((1, tn), lambda i, j: (0, j)),
        ],
        out_specs=[
            pl.BlockSpec((tm, tn), lambda i, j: (i, j)),
            pl.BlockSpec((1, geo.seq_blk, CONV_K_F - 1, tn), lambda i, j: (i, 0, 0, j)),
        ],
        out_shape=[
            jax.ShapeDtypeStruct((geo.n_seq * geo.seq_len, D_FF), MXU_DTYPE),
            jax.ShapeDtypeStruct((n_i, geo.seq_blk, CONV_K_F - 1, D_FF), F32),
        ],
        scratch_shapes=[pltpu.VMEM((tm, D_MODEL), MXU_DTYPE),
                        pltpu.VMEM(tail_shape, F32)],
        compiler_params=_cparams(("arbitrary", "arbitrary")),
        name="ffn_in",
    )(x3, nw.reshape(1, D_MODEL), mod, mod, w_g, w_u, conv_state, conv_w, conv_b.reshape(1, D_FF))
    last = tails.reshape(n_i // nrt, nrt, geo.seq_blk, CONV_K_F - 1, D_FF)[:, nrt - 1]
    return act, last.reshape(1, geo.n_seq, CONV_K_F - 1, D_FF)


def _norm_kernel(x_ref, nw_ref, o_ref):
    o_ref[...] = _rmsnorm_rows(x_ref[...]) * nw_ref[...]


def _final_norm(x3, nw, geo):
    _, n_i, _ = _tiles(geo)
    return pl.pallas_call(
        _norm_kernel,
        grid=(n_i, 1),
        in_specs=[_x_spec(geo, D_MODEL, lambda j: 0),
                  pl.BlockSpec((1, D_MODEL), lambda i, j: (0, 0))],
        out_specs=_x_spec(geo, D_MODEL, lambda j: 0),
        out_shape=jax.ShapeDtypeStruct(x3.shape, F32),
        compiler_params=_cparams(("parallel", "arbitrary")),
        name="final_norm",
    )(x3, nw.reshape(1, D_MODEL))


def _state_spec(shape, layer, geo):
    rest = tuple(shape[2:])
    zeros = (0,) * len(rest)
    return pl.BlockSpec((1, geo.scan_seqs) + rest, lambda i, c: (layer, i) + zeros)


def _row_spec(geo, width, col):
    nc = geo.seq_len // geo.chunk
    return pl.BlockSpec((geo.scan_seqs * geo.chunk, width), lambda i, c: (i * nc + c, col))


def _gates_t_spec(geo):
    nc = geo.seq_len // geo.chunk
    return pl.BlockSpec((geo.scan_seqs, GATE_W, geo.chunk), lambda i, c: (i * nc + c, 0, 0))


ROWS, LEAD, STATE, CONST = "rows", "lead", "state", "const"


def _per_sequence(body, geo, kinds):
    if geo.scan_seqs == 1:
        return body

    def view(ref, kind, k):
        if kind == ROWS:
            return ref.at[pl.ds(k * geo.chunk, geo.chunk)]
        if kind == LEAD:
            return ref.at[pl.ds(k, 1)]
        if kind == STATE:
            return ref.at[:, pl.ds(k, 1)]
        return ref

    def wrapped(*refs):
        for k in range(geo.scan_seqs):
            body(*[view(r, kind, k) for r, kind in zip(refs, kinds)])
    return wrapped


def _seq_call(kernel_fn, geo, name, in_specs, args, out_specs, out_shapes, scratch, kinds,
              stacked_prev):
    nc = geo.seq_len // geo.chunk
    assert geo.scan_seqs == 1 or nc == 1
    kernel_fn = _per_sequence(kernel_fn, geo, kinds)
    in_specs, args = list(in_specs), list(args)
    aliases = {}
    assert len(stacked_prev) <= 1
    for out_idx, arr in stacked_prev.items():
        kernel_fn = _drop_ref(kernel_fn, len(in_specs))
        aliases[len(in_specs)] = out_idx
        in_specs.append(pl.BlockSpec(memory_space=pl.ANY))
        args.append(arr)
    return pl.pallas_call(
        kernel_fn,
        grid=(geo.n_seq // geo.scan_seqs, nc),
        in_specs=in_specs,
        out_specs=out_specs,
        out_shape=out_shapes,
        scratch_shapes=scratch,
        input_output_aliases=aliases,
        compiler_params=_cparams(("parallel", "arbitrary")),
        name=name,
    )(*args)


def _ssd_kernel(z_ref, xbc_ref, gc_ref, gr_ref, cs_ref, s0_ref, cw_ref, cb_ref,
                dtb_r_ref, dtb_c_ref, al_r_ref, al_c_ref, dsk_ref, nw_ref, exp_ref,
                y_ref, cso_ref, so_ref, tail_scr, *, lc, lv, single_chunk):
    def init():
        tail_scr[0, 5:8, :] = cs_ref[0, 0]
        so_ref[0, 0] = s0_ref[0, 0]

    _at_first_chunk(single_chunk, init)

    x = xbc_ref[...]
    tails = [tail_scr[0, 5:6, :], tail_scr[0, 6:7, :], tail_scr[0, 7:8, :]]
    w = cw_ref[0]
    xc = (cb_ref[0] + w[0:1] * _shifted(x, tails, 3) + w[1:2] * _shifted(x, tails, 2)
          + w[2:3] * _shifted(x, tails, 1) + w[3:4] * x)
    xc = xc * _sigmoid(xc)
    new_tail = x[lv - 3:lv, :]
    tail_scr[0, 5:8, :] = new_tail
    cso_ref[0, 0] = new_tail
    xa = xc[:, :D_A]
    bm = xc[:, D_A:D_A + G_A * N_A]
    cm = xc[:, D_A + G_A * N_A:]

    dt_c = _softplus(gc_ref[:, 0:H_A] + dtb_r_ref[...])
    dt_r = _softplus(gr_ref[0, 0:H_A, :] + dtb_c_ref[...])
    if lv < lc:
        dt_c = jnp.where(lax.broadcasted_iota(jnp.int32, dt_c.shape, 0) < lv, dt_c, 0.0)
        dt_r = jnp.where(lax.broadcasted_iota(jnp.int32, dt_r.shape, 1) < lv, dt_r, 0.0)
    lower, upper, mask = _tri(lc)
    cum_c = jnp.dot(lower, dt_c * (-jnp.exp(al_r_ref[...])), precision=HI)
    cum_r = jnp.dot(dt_r * (-jnp.exp(al_c_ref[...])), upper, precision=HI)
    cum_last = cum_c[lc - 1:lc, :]
    expand = exp_ref[...]
    ecum_x = jnp.dot(jnp.exp(cum_c), expand, precision=HI)
    tail_x = jnp.dot(jnp.exp(cum_last - cum_c) * dt_c, expand, precision=HI)

    hg = H_A // G_A
    gw = hg * P_A
    ys = []
    for g in range(G_A):
        cg = cm[:, g * N_A:(g + 1) * N_A]
        bg = bm[:, g * N_A:(g + 1) * N_A]
        cb_ts = lax.dot_general(cg, bg, _NT, preferred_element_type=F32)
        s_g = so_ref[0, 0, g]
        y_inter = lax.dot_general(cg, s_g, _NT, preferred_element_type=F32)
        xg = xa[:, g * gw:(g + 1) * gw]
        pieces = []
        for j in range(hg // 2):
            xp = xg[:, j * 128:(j + 1) * 128]
            lane = lax.broadcasted_iota(jnp.int32, xp.shape, 1)
            acc = None
            for half in range(2):
                h = g * hg + 2 * j + half
                dec = jnp.exp(jnp.where(mask, cum_c[:, h:h + 1] - cum_r[h:h + 1, :], -jnp.inf))
                w_ts = cb_ts * dec * dt_r[h:h + 1, :]
                xh = jnp.where((lane >= P_A) if half else (lane < P_A), xp, 0.0)
                d = jnp.dot(w_ts, xh, preferred_element_type=F32)
                acc = d if acc is None else acc + d
            pieces.append(acc)
        y_intra = jnp.concatenate(pieces, axis=1)
        ys.append(y_intra + y_inter * ecum_x[:, g * gw:(g + 1) * gw])
        upd = lax.dot_general(xg * tail_x[:, g * gw:(g + 1) * gw], bg, _TN,
                              preferred_element_type=F32)
        for hh in range(hg):
            h = g * hg + hh
            rows = slice(hh * P_A, (hh + 1) * P_A)
            so_ref[0, 0, g, rows, :] = (s_g[rows, :] * jnp.exp(cum_r[h:h + 1, lc - 1:lc])
                                        + upd[rows, :])

    y = jnp.concatenate(ys, axis=1) + dsk_ref[...] * xa
    z = z_ref[...]
    y = y * (z * _sigmoid(z))
    y = jnp.concatenate([_rmsnorm_rows(y[:, g * gw:(g + 1) * gw]) for g in range(G_A)], axis=1)
    y_ref[...] = (y * nw_ref[...]).astype(y_ref.dtype)


def _ssd(p1, gates, gates_t, conv_state, ssd_state, prm, e, geo, prev_state_out):
    b = geo.n_seq
    const2 = lambda shape: pl.BlockSpec(shape, lambda i, c: (0, 0))
    n_even = ssd_state.shape[0]
    s5 = ssd_state.reshape(n_even, b, G_A, (H_A // G_A) * P_A, N_A)
    expand = (jnp.arange(D_A)[None, :] // P_A == jnp.arange(H_A)[:, None]).astype(F32)
    cso_shape = (1,) + conv_state.shape[1:]
    y, cso, so = _seq_call(
        functools.partial(_ssd_kernel, lc=geo.chunk, lv=geo.valid,
                          single_chunk=geo.seq_len == geo.chunk), geo, "ssd",
        in_specs=[
            _row_spec(geo, D_A, 0),
            _row_spec(geo, CONV_DIM_A, 2),
            _row_spec(geo, GATE_W, 0),
            _gates_t_spec(geo),
            _state_spec(conv_state.shape, e, geo),
            _state_spec(s5.shape, e, geo),
            pl.BlockSpec((1, CONV_K_A, CONV_DIM_A), lambda i, c: (e, 0, 0)),
            pl.BlockSpec((1, 1, CONV_DIM_A), lambda i, c: (e, 0, 0)),
            const2((1, H_A)), const2((H_A, 1)), const2((1, H_A)), const2((H_A, 1)),
            const2((1, D_A)), const2((1, D_A)), const2((H_A, D_A)),
        ],
        args=(p1, p1, gates, gates_t, conv_state, s5,
              prm['conv_w_a'], prm['conv_b_a'].reshape(n_even, 1, CONV_DIM_A),
              prm['dt_bias'][e].reshape(1, H_A), prm['dt_bias'][e].reshape(H_A, 1),
              prm['a_log'][e].reshape(1, H_A), prm['a_log'][e].reshape(H_A, 1),
              jnp.repeat(prm['d_skip'][e], P_A).reshape(1, D_A),
              prm['norm_a'][e].reshape(1, D_A), expand),
        out_specs=[_row_spec(geo, D_A, 0), _state_spec(cso_shape, 0, geo),
                   _state_spec(s5.shape, e, geo)],
        out_shapes=[
            jax.ShapeDtypeStruct((b * geo.seq_len, D_A), MXU_DTYPE),
            jax.ShapeDtypeStruct(cso_shape, F32),
            jax.ShapeDtypeStruct(s5.shape, F32),
        ],
        scratch=[pltpu.VMEM((geo.scan_seqs, SUBLANES, CONV_DIM_A), F32)],
        kinds=[ROWS, ROWS, ROWS, LEAD, STATE, STATE] + [CONST] * 9 + [ROWS, STATE, STATE, LEAD],
        stacked_prev={} if prev_state_out is None else {2: prev_state_out},
    )
    return y, cso, so


def _mlstm_kernel(q_ref, k_ref, v_ref, og_ref, gc_ref, gr_ref, c0_ref, n0_ref, m0_ref,
                  ib_r_ref, ib_c_ref, fb_r_ref, fb_c_ref, nw_ref,
                  h_ref, co_ref, no_ref, mo_ref, *, lc, lv, single_chunk):
    def init():
        co_ref[0, 0] = c0_ref[0, 0]
        no_ref[0, 0] = n0_ref[0, 0]
        mo_ref[0, 0] = m0_ref[0, 0]

    _at_first_chunk(single_chunk, init)

    i0, f0 = H_A, H_A + H_B
    li_c = gc_ref[:, i0:i0 + H_B] + ib_r_ref[...]
    lf_c = _log_sigmoid(gc_ref[:, f0:f0 + H_B] + fb_r_ref[...])
    li_r = gr_ref[0, i0:i0 + H_B, :] + ib_c_ref[...]
    lf_r = _log_sigmoid(gr_ref[0, f0:f0 + H_B, :] + fb_c_ref[...])
    if lv < lc:
        vc = lax.broadcasted_iota(jnp.int32, li_c.shape, 0) < lv
        vr = lax.broadcasted_iota(jnp.int32, li_r.shape, 1) < lv
        li_c, lf_c = jnp.where(vc, li_c, NEG_BIG), jnp.where(vc, lf_c, 0.0)
        li_r, lf_r = jnp.where(vr, li_r, NEG_BIG), jnp.where(vr, lf_r, 0.0)
    lower, upper, mask = _tri(lc)
    bc_c = jnp.dot(lower, lf_c, precision=HI)
    bc_r = jnp.dot(lf_r, upper, precision=HI)

    m_old = mo_ref[0, 0]
    n_old = no_ref[0, 0]
    lane_h = lax.broadcasted_iota(jnp.int32, (1, H_B), 1)
    m_out = jnp.zeros((1, H_B), F32)
    hs = []
    for h in range(H_B):
        qh = q_ref[:, h * DK_B:(h + 1) * DK_B] * (DK_B ** -0.5)
        kh = k_ref[:, h * DK_B:(h + 1) * DK_B]
        vh = v_ref[:, h * DV_B:(h + 1) * DV_B]
        m_prev = m_old[:, h:h + 1]
        bcc = bc_c[:, h:h + 1]
        dmat = jnp.where(mask, bcc - bc_r[h:h + 1, :] + li_r[h:h + 1, :], -jnp.inf)
        inter = bcc + m_prev
        m_t = jnp.maximum(inter, jnp.max(dmat, axis=1, keepdims=True))
        w_in = jnp.exp(inter - m_t)
        qk = lax.dot_general(qh, kh, _NT, preferred_element_type=F32)
        w_ts = jnp.exp(dmat - m_t) * qk
        c_h = co_ref[0, 0, h]
        n_h = n_old[h:h + 1, :]
        num = (jnp.dot(w_ts, vh, preferred_element_type=F32)
               + w_in * jnp.dot(qh, c_h, preferred_element_type=F32))
        den = (jnp.sum(w_ts, axis=1, keepdims=True)
               + w_in * jnp.sum(qh * n_h, axis=1, keepdims=True))
        hs.append(_rmsnorm_rows(num / jnp.maximum(jnp.abs(den), jnp.exp(-m_t))))
        m_new = m_t[lv - 1:lv, :]
        bc_last = bcc[lc - 1:lc, :]
        w_s = jnp.exp(bc_last - bcc + li_c[:, h:h + 1] - m_new)
        w_c = jnp.exp(bc_last + m_prev - m_new)
        ks = kh * w_s
        co_ref[0, 0, h] = w_c * c_h + lax.dot_general(ks, vh, _TN, preferred_element_type=F32)
        no_ref[0, 0, h:h + 1, :] = w_c * n_h + jnp.sum(ks, axis=0, keepdims=True)
        m_out = jnp.where(lane_h == h, m_new, m_out)
    mo_ref[0, 0] = m_out
    hn = jnp.concatenate(hs, axis=1) * nw_ref[...]
    h_ref[...] = (hn * _sigmoid(og_ref[...])).astype(h_ref.dtype)


def _mlstm(p1, gates, gates_t, c_state, n_state, m_state, prm, e, geo, prev_state_out):
    b = geo.n_seq
    const2 = lambda shape: pl.BlockSpec(shape, lambda i, c: (0, 0))
    qk_w = H_B * DK_B
    m4 = m_state.reshape(m_state.shape[0], b, 1, H_B)
    one = lambda shape: (1,) + tuple(shape[1:])
    h, co, no, mo = _seq_call(
        functools.partial(_mlstm_kernel, lc=geo.chunk, lv=geo.valid,
                          single_chunk=geo.seq_len == geo.chunk), geo, "mlstm",
        in_specs=[
            _row_spec(geo, qk_w, 9), _row_spec(geo, qk_w, 10),
            _row_spec(geo, D_B, 1), _row_spec(geo, D_B, 2),
            _row_spec(geo, GATE_W, 0),
            _gates_t_spec(geo),
            _state_spec(c_state.shape, e, geo), _state_spec(n_state.shape, e, geo),
            _state_spec(m4.shape, e, geo),
            const2((1, H_B)), const2((H_B, 1)), const2((1, H_B)), const2((H_B, 1)),
            const2((1, D_B)),
        ],
        args=(p1, p1, p1, p1, gates, gates_t, c_state, n_state, m4,
              prm['i_bias'][e].reshape(1, H_B), prm['i_bias'][e].reshape(H_B, 1),
              prm['f_bias'][e].reshape(1, H_B), prm['f_bias'][e].reshape(H_B, 1),
              prm['norm_b'][e].reshape(1, D_B)),
        out_specs=[_row_spec(geo, D_B, 0), _state_spec(c_state.shape, e, geo),
                   _state_spec(one(n_state.shape), 0, geo), _state_spec(one(m4.shape), 0, geo)],
        out_shapes=[
            jax.ShapeDtypeStruct((b * geo.seq_len, D_B), MXU_DTYPE),
            jax.ShapeDtypeStruct(c_state.shape, F32),
            jax.ShapeDtypeStruct(one(n_state.shape), F32),
            jax.ShapeDtypeStruct(one(m4.shape), F32),
        ],
        scratch=[],
        kinds=[ROWS] * 5 + [LEAD] + [STATE] * 3 + [CONST] * 5 + [ROWS] + [STATE] * 3,
        stacked_prev={} if prev_state_out is None else {1: prev_state_out},
    )
    return h, co, no, mo.reshape(1, b, H_B)


def _hgrn_kernel(q_ref, f_ref, i_ref, g_ref, lbl_ref, s0_ref, nw_ref, o_ref, so_ref,
                 *, lc, lv, layer_o, bs, single_chunk):
    def init():
        so_ref[0, 0] = s0_ref[0, 0]

    _at_first_chunk(single_chunk, init)

    lbl = lbl_ref[...]
    ex = jnp.exp(lbl - jnp.max(lbl, axis=0, keepdims=True))
    sm = ex / jnp.sum(ex, axis=0, keepdims=True)
    lb_all = [sm[0:1, :]]
    for r in range(1, lbl.shape[0]):
        lb_all.append(lb_all[-1] + sm[r:r + 1, :])
    lb = lb_all[layer_o] - lb_all[0]

    fx = f_ref[...]
    e1 = jnp.exp(-jnp.abs(fx))
    log_sig = jnp.minimum(fx, 0.0) - jnp.log1p(e1)
    la = jnp.log(lb)
    lb_ = jnp.log1p(-lb) + log_sig
    logf = jnp.maximum(la, lb_) + jnp.log1p(jnp.exp(-jnp.abs(la - lb_)))
    kk = (1.0 - lb) * (jnp.where(fx >= 0.0, e1, 1.0) / (1.0 + e1))
    if lv < lc:
        valid = lax.broadcasted_iota(jnp.int32, fx.shape, 0) < lv
        logf = jnp.where(valid, logf, 0.0)
        kk = jnp.where(valid, kk, 0.0)

    nb = lc // bs
    r_i = lax.broadcasted_iota(jnp.int32, (lc, lc), 0)
    c_i = lax.broadcasted_iota(jnp.int32, (lc, lc), 1)
    sh = int(math.log2(bs))
    blk_lower = ((c_i <= r_i) & ((c_i >> sh) == (r_i >> sh))).astype(F32)
    gw = jnp.dot(blk_lower, logf, precision=HI)
    q = q_ref[...]
    v = i_ref[...]
    blk = lambda a, i: a[i * bs:(i + 1) * bs, :]
    tots = [gw[(i + 1) * bs - 1:(i + 1) * bs, :] for i in range(nb)]
    before = [jnp.zeros_like(tots[0])]
    for i in range(nb):
        before.append(before[-1] + tots[i])
    g_tot = before[nb]
    qt = q * jnp.exp(gw)
    kt = [blk(kk, j) * jnp.exp(tots[j] - blk(gw, j)) for j in range(nb)]
    q_in = jnp.concatenate([blk(qt, i) * jnp.exp(before[i]) for i in range(nb)], axis=0)
    k_out = jnp.concatenate([kt[j] * jnp.exp(g_tot - before[j + 1]) for j in range(nb)], axis=0)

    hsl = lambda h: slice(h * DK_C, (h + 1) * DK_C)
    gw2 = gw * LOG2E
    ck = jnp.log2(kk) - gw2
    n_t = bs // SUBLANES
    lane_s = lax.broadcasted_iota(jnp.int32, (SUBLANES, lc), 1)
    row_t = lax.broadcasted_iota(jnp.int32, (bs, lc), 0)
    col_s = lax.broadcasted_iota(jnp.int32, (bs, lc), 1)
    o_blocks = []
    for i in range(nb):
        g_i, q_i, ck_i = blk(gw2, i), blk(q, i), blk(ck, i)
        att_d = [[jnp.zeros((SUBLANES, lc), F32) for _ in range(n_t)] for _ in range(H_C)]
        for s in range(bs):
            t0 = s // SUBLANES
            p = q_i[t0 * SUBLANES:, :] * jnp.exp2(g_i[t0 * SUBLANES:, :] + ck_i[s:s + 1, :])
            for h in range(H_C):
                a = jnp.sum(p[:, hsl(h)], axis=1, keepdims=True)
                for tt in range(t0, n_t):
                    a_t = a[(tt - t0) * SUBLANES:(tt - t0 + 1) * SUBLANES, :]
                    att_d[h][tt] = jnp.where(lane_s == i * bs + s, a_t, att_d[h][tt])
        causal = (col_s - i * bs) <= row_t
        if i > 0:
            k_hat = jnp.concatenate(
                [kt[j] if j == i - 1 else kt[j] * jnp.exp(before[i] - before[j + 1])
                 for j in range(i)] + [jnp.zeros((lc - i * bs, D_C), F32)], axis=0)
            q_ti = blk(qt, i)
        acc = []
        for h in range(H_C):
            att = att_d[h][0] if n_t == 1 else jnp.concatenate(att_d[h], axis=0)
            att = jnp.where(causal, att, 0.0)
            if i > 0:
                att = att + lax.dot_general(q_ti[:, hsl(h)], k_hat[:, hsl(h)], _NT,
                                            preferred_element_type=F32)
            acc.append(jnp.dot(att, v[:, hsl(h)], preferred_element_type=F32))
        o_blocks.append(jnp.concatenate(acc, axis=1))
    o_intra = jnp.concatenate(o_blocks, axis=0) if nb > 1 else o_blocks[0]

    outs = []
    for h in range(H_C):
        s_h = so_ref[0, 0, h]
        o_h = o_intra[:, hsl(h)] + jnp.dot(q_in[:, hsl(h)], s_h, preferred_element_type=F32)
        outs.append(_rmsnorm_rows(o_h))
        dec_col = jnp.transpose(jnp.broadcast_to(jnp.exp(g_tot[:, hsl(h)]), (DK_C, DK_C)))
        so_ref[0, 0, h] = dec_col * s_h + lax.dot_general(k_out[:, hsl(h)], v[:, hsl(h)], _TN,
                                                          preferred_element_type=F32)
    gate = g_ref[...]
    o_ref[...] = (jnp.concatenate(outs, axis=1) * nw_ref[...]
                  * (gate * _sigmoid(gate))).astype(o_ref.dtype)


def _hgrn(p, state, prm, o, geo, prev_state_out):
    n_odd = prm['lb_logits'].shape[0]
    return _seq_call(
        functools.partial(_hgrn_kernel, lc=geo.chunk, lv=geo.valid, layer_o=o,
                          bs=min(16, geo.chunk), single_chunk=geo.seq_len == geo.chunk), geo, "hgrn",
        in_specs=[
            _row_spec(geo, D_C, 0), _row_spec(geo, D_C, 1), _row_spec(geo, D_C, 2),
            _row_spec(geo, D_C, 3),
            pl.BlockSpec((n_odd, D_C), lambda i, c: (0, 0)),
            _state_spec(state.shape, o, geo),
            pl.BlockSpec((1, D_C), lambda i, c: (0, 0)),
        ],
        args=(p, p, p, p, prm['lb_logits'], state, prm['norm_c'][o].reshape(1, D_C)),
        out_specs=[_row_spec(geo, D_C, 0), _state_spec(state.shape, o, geo)],
        out_shapes=[
            jax.ShapeDtypeStruct((geo.n_seq * geo.seq_len, D_C), MXU_DTYPE),
            jax.ShapeDtypeStruct(state.shape, F32),
        ],
        scratch=[],
        kinds=[ROWS] * 4 + [CONST, STATE, CONST, ROWS, STATE],
        stacked_prev={} if prev_state_out is None else {1: prev_state_out},
    )


def _prep_weights(prm):
    a0, a1, a2 = D_A, D_A + CONV_DIM_A, D_A + CONV_DIM_A + H_A
    q1 = a2 + H_B * DK_B
    k1 = q1 + H_B * DK_B
    v1 = k1 + D_B
    o1 = v1 + D_B
    n_gate = H_A + 2 * H_B
    cast = lambda a: a.astype(MXU_DTYPE)
    layers = lambda a: [cast(a[i]) for i in range(a.shape[0])]
    w = {'in_ab': [], 'gates_ab': []}
    for e in range(prm['w_in_ab'].shape[0]):
        wab = prm['w_in_ab'][e]
        w['in_ab'].append(cast(jnp.concatenate(
            [wab[:, :a0], wab[:, k1:v1], wab[:, v1:o1], wab[:, a0:a1], wab[:, a2:q1],
             wab[:, q1:k1]], axis=1)))
        w['gates_ab'].append(cast(jnp.pad(
            jnp.concatenate([wab[:, a1:a2], wab[:, o1:]], axis=1), ((0, 0), (0, GATE_W - n_gate)))))
    w['out_a'] = [cast(a[:D_A]) for a in prm['w_out_ab']]
    w['out_b'] = [cast(a[D_A:]) for a in prm['w_out_ab']]
    for name, key in (('in_c', 'w_in_c'), ('out_c', 'w_out_c'), ('ffn_g', 'w_ffn_g'),
                      ('ffn_u', 'w_ffn_u'), ('ffn_d', 'w_ffn_d')):
        w[name] = layers(prm[key])
    return w


def _trunk(x3, mod, states, prm, w, geo, tn_in, tn_res):
    conv_a, ssd, mem_c, mem_n, mem_m, hgrn, ffn_buf = states
    n_conv, n_n, n_m, n_ffn = [], [], [], []
    new_ssd = new_c = new_hgrn = None
    nc = geo.seq_len // geo.chunk
    for layer in range(DEPTH):
        mm_norm = functools.partial(_mm_norm, x3, prm['norm_mix'][layer], mod, layer, 1, 0,
                                    geo=geo)
        if layer % 2 == 0:
            e = layer // 2
            p1 = mm_norm(w=w['in_ab'][e], tn=tn_in[layer])
            gates = mm_norm(w=w['gates_ab'][e], tn=GATE_W)
            gates_t = gates.reshape(geo.n_seq * nc, geo.chunk, GATE_W).transpose(0, 2, 1)
            ya, cv, new_ssd = _ssd(p1, gates, gates_t, conv_a, ssd, prm, e, geo, new_ssd)
            hb, new_c, nn, mmm = _mlstm(p1, gates, gates_t, mem_c, mem_n, mem_m, prm, e, geo, new_c)
            n_conv.append(cv); n_n.append(nn); n_m.append(mmm)
            x3 = _mm_res([ya, hb], [w['out_a'][e], w['out_b'][e]], x3, mod, layer, 2, geo, tn_res[layer])
        else:
            o = layer // 2
            p = mm_norm(w=w['in_c'][o], tn=tn_in[layer])
            oc, new_hgrn = _hgrn(p, hgrn, prm, o, geo, new_hgrn)
            x3 = _mm_res([oc], [w['out_c'][o]], x3, mod, layer, 2, geo, tn_res[layer])
        act, fb = _ffn_in(x3, prm['norm_ffn'][layer], mod, layer, w['ffn_g'][layer],
                          w['ffn_u'][layer], ffn_buf, prm['conv_w_f'][layer],
                          prm['conv_b_f'][layer], geo, FFN_TN)
        n_ffn.append(fb)
        x3 = _mm_res([act], [w['ffn_d'][layer]], x3, mod, layer, 5, geo, FFN_DOWN_TN)
    y = _final_norm(x3, prm['norm_f'], geo)
    cat = lambda xs: jnp.concatenate(xs, axis=0)
    return (y, cat(n_conv), new_ssd.reshape(ssd.shape), new_c, cat(n_n), cat(n_m), new_hgrn,
            cat(n_ffn))


def kernel(x_prompt, x_sample, c_prompt, c_sample, state_ssd_conv, state_ssd, state_mlstm_c, state_mlstm_n, state_mlstm_m, state_hgrn, state_ffn_conv, w_ada, b_ada, norm_mix, norm_ffn, w_in_ab, conv_w_a, conv_b_a, dt_bias, a_log, d_skip, norm_a, i_bias, f_bias, norm_b, w_out_ab, w_in_c, lb_logits, norm_c, w_out_c, w_ffn_g, w_ffn_u, conv_w_f, conv_b_f, w_ffn_d, norm_f):
    prm = dict(norm_mix=norm_mix, norm_ffn=norm_ffn, w_in_ab=w_in_ab, conv_w_a=conv_w_a,
               conv_b_a=conv_b_a, dt_bias=dt_bias, a_log=a_log, d_skip=d_skip, norm_a=norm_a,
               i_bias=i_bias, f_bias=f_bias, norm_b=norm_b, w_out_ab=w_out_ab, w_in_c=w_in_c,
               lb_logits=lb_logits, norm_c=norm_c, w_out_c=w_out_c, w_ffn_g=w_ffn_g,
               w_ffn_u=w_ffn_u, conv_w_f=conv_w_f, conv_b_f=conv_b_f, w_ffn_d=w_ffn_d,
               norm_f=norm_f)
    bp, lp, _ = x_prompt.shape
    bs, ls, _ = x_sample.shape
    n_even, n_odd = state_ssd.shape[0], state_hgrn.shape[0]
    w = _prep_weights(prm)

    n_c = bs + bp
    n_c_pad = -(-n_c // SUBLANES) * SUBLANES
    c_all = jnp.pad(jnp.concatenate([c_sample, c_prompt], axis=0), ((0, n_c_pad - n_c), (0, 0)))
    mod = _ada(c_all, w_ada, b_ada).reshape(DEPTH, n_c_pad, 1, 6 * D_MODEL)

    zeros = lambda *s: jnp.zeros(s, F32)
    st_p = (zeros(n_even, bp, CONV_K_A - 1, CONV_DIM_A), zeros(n_even, bp, H_A, P_A, N_A),
            zeros(n_even, bp, H_B, DK_B, DV_B), zeros(n_even, bp, H_B, DK_B),
            zeros(n_even, bp, H_B), zeros(n_odd, bp, H_C, DK_C, DV_C),
            zeros(DEPTH, bp, CONV_K_F - 1, D_FF))
    lc_p = math.gcd(lp, PROMPT_CHUNK)
    geo_p = Geo(n_seq=bp, seq_len=lp, seq_blk=1, row_blk=min(lp, 1024), chunk=lc_p, valid=lc_p,
                mod_off=bs, scan_seqs=1)
    out_p = _trunk(x_prompt, mod, st_p, prm, w, geo_p, tn_in=(512, 512, 1024, 1024),
                   tn_res=(512, 512, 1024, 1024))

    geo_s = Geo(n_seq=bs, seq_len=SUBLANES, seq_blk=bs, row_blk=SUBLANES, chunk=SUBLANES,
                valid=ls, mod_off=0, scan_seqs=4)
    xs = jnp.pad(x_sample, ((0, 0), (0, SUBLANES - ls), (0, 0)))
    st_s = (state_ssd_conv, state_ssd, state_mlstm_c, state_mlstm_n, state_mlstm_m, state_hgrn,
            state_ffn_conv)
    out_s = _trunk(xs, mod, st_s, prm, w, geo_s, tn_in=(512,) * DEPTH, tn_res=(512,) * DEPTH)
    return (out_p[0], out_s[0][:, :ls]) + tuple(out_p[1:]) + tuple(out_s[1:])
```

```python
import collections
import functools
import math

import jax
import jax.numpy as jnp
from jax import lax
from jax.experimental import pallas as pl
from jax.experimental.pallas import tpu as pltpu

F32 = jnp.float32
MXU_DTYPE = jnp.bfloat16
HI = lax.Precision.HIGHEST
NEG_BIG = -1e30
LOG2E = 1.4426950408889634

D_MODEL = 2048
DEPTH = 4
EPS = 1e-6
PROMPT_CHUNK = 64
H_A, P_A, G_A, N_A, CONV_K_A = 32, 64, 4, 128, 4
D_A = H_A * P_A
CONV_DIM_A = D_A + 2 * G_A * N_A
H_B, DK_B, DV_B = 8, 128, 256
D_B = H_B * DV_B
H_C, DK_C, DV_C = 16, 128, 128
D_C = H_C * DV_C
D_FF, CONV_K_F = 5632, 3
FFN_TN = 512
FFN_DOWN_TN = 512
GATE_W = 128
SUBLANES = 8
VMEM_LIMIT = 56 * 1024 * 1024

_NT = (((1,), (1,)), ((), ()))
_TN = (((0,), (0,)), ((), ()))

Geo = collections.namedtuple("Geo", "n_seq seq_len seq_blk row_blk chunk valid mod_off scan_seqs")


def _cparams(sem, flags=None):
    return pltpu.CompilerParams(dimension_semantics=sem, vmem_limit_bytes=VMEM_LIMIT, flags=flags)


def _sigmoid(x):
    return 1.0 / (1.0 + jnp.exp(-x))


def _softplus(x):
    return jnp.maximum(x, 0.0) + jnp.log1p(jnp.exp(-jnp.abs(x)))


def _log_sigmoid(x):
    return jnp.minimum(x, 0.0) - jnp.log1p(jnp.exp(-jnp.abs(x)))


def _tri(n):
    r = lax.broadcasted_iota(jnp.int32, (n, n), 0)
    c = lax.broadcasted_iota(jnp.int32, (n, n), 1)
    mask = c <= r
    return mask.astype(F32), (r <= c).astype(F32), mask


def _shifted(x, tails, k, axis=0):
    row = lax.broadcasted_iota(jnp.int32, x.shape, axis)
    out = pltpu.roll(x, k, axis)
    for r in range(k):
        out = jnp.where(row == r, tails[len(tails) - k + r], out)
    return out


def _drop_ref(fn, idx):
    def wrapped(*refs):
        return fn(*refs[:idx], *refs[idx + 1:])
    return wrapped


def _rmsnorm_rows(x):
    return x * lax.rsqrt(jnp.mean(x * x, axis=-1, keepdims=True) + EPS)


def _ada_kernel(c_ref, w_ref, b_ref, o_ref):
    c = c_ref[...]
    ca = (c * _sigmoid(c)).astype(MXU_DTYPE)
    o_ref[0] = jnp.dot(ca, w_ref[0].astype(MXU_DTYPE), preferred_element_type=F32) + b_ref[0]


def _ada(c_all, w_ada, b_ada, tn=1024):
    rows = c_all.shape[0]
    n = w_ada.shape[2]
    return pl.pallas_call(
        _ada_kernel,
        grid=(DEPTH, n // tn),
        in_specs=[
            pl.BlockSpec((rows, D_MODEL), lambda l, j: (0, 0)),
            pl.BlockSpec((1, D_MODEL, tn), lambda l, j: (l, 0, j)),
            pl.BlockSpec((1, 1, tn), lambda l, j: (l, 0, j)),
        ],
        out_specs=pl.BlockSpec((1, rows, tn), lambda l, j: (l, 0, j)),
        out_shape=jax.ShapeDtypeStruct((DEPTH, rows, n), F32),
        compiler_params=_cparams(("parallel", "parallel")),
        name="ada",
    )(c_all, w_ada, b_ada.reshape(DEPTH, 1, n))


def _tiles(geo):
    nrt = geo.seq_len // geo.row_blk
    return nrt, (geo.n_seq // geo.seq_blk) * nrt, geo.seq_blk * geo.row_blk


def _x_spec(geo, width, col_of):
    nrt = geo.seq_len // geo.row_blk
    return pl.BlockSpec((geo.seq_blk, geo.row_blk, width),
                        lambda i, j: (i // nrt, i % nrt, col_of(j)))


def _mod_spec(geo, layer, width, col_of):
    nrt = geo.seq_len // geo.row_blk
    return pl.BlockSpec((1, geo.seq_blk, 1, width),
                        lambda i, j: (layer, geo.mod_off + i // nrt, 0, col_of(j)))


def _norm_mod_to_scratch(x_ref, nw_ref, sc_ref, sh_ref, h_scr):
    h = (_rmsnorm_rows(x_ref[...]) * nw_ref[...]) * (1.0 + sc_ref[0]) + sh_ref[0]
    h_scr[...] = h.reshape(h_scr.shape).astype(h_scr.dtype)


def _mm_norm_kernel(x_ref, nw_ref, sc_ref, sh_ref, w_ref, o_ref, h_scr):
    @pl.when(pl.program_id(1) == 0)
    def _():
        _norm_mod_to_scratch(x_ref, nw_ref, sc_ref, sh_ref, h_scr)

    o_ref[...] = jnp.dot(h_scr[...], w_ref[0], preferred_element_type=F32).astype(o_ref.dtype)


def _mm_norm(x3, nw, mod, layer, k_sc, k_sh, w, w_layer, geo, tn):
    _, n_i, tm = _tiles(geo)
    n = w.shape[2]
    return pl.pallas_call(
        _mm_norm_kernel,
        grid=(n_i, n // tn),
        in_specs=[
            _x_spec(geo, D_MODEL, lambda j: 0),
            pl.BlockSpec((1, D_MODEL), lambda i, j: (0, 0)),
            _mod_spec(geo, layer, D_MODEL, lambda j: k_sc),
            _mod_spec(geo, layer, D_MODEL, lambda j: k_sh),
            pl.BlockSpec((1, D_MODEL, tn), lambda i, j: (w_layer, 0, j)),
        ],
        out_specs=pl.BlockSpec((tm, tn), lambda i, j: (i, j)),
        out_shape=jax.ShapeDtypeStruct((geo.n_seq * geo.seq_len, n), F32),
        scratch_shapes=[pltpu.VMEM((tm, D_MODEL), MXU_DTYPE)],
        compiler_params=_cparams(("parallel", "arbitrary")),
        name="mm_norm",
    )(x3, nw.reshape(1, D_MODEL), mod, mod, w)


def _mm_res_kernel(*refs, n_lhs):
    a_refs, w_refs = refs[:n_lhs], refs[n_lhs:2 * n_lhs]
    xres_ref, gate_ref, o_ref = refs[2 * n_lhs:]
    acc = None
    for a_ref, w_ref in zip(a_refs, w_refs):
        d = jnp.dot(a_ref[...], w_ref[0], preferred_element_type=F32)
        acc = d if acc is None else acc + d
    o_ref[...] = xres_ref[...] + gate_ref[0] * acc.reshape(o_ref.shape)


def _mm_res(a_list, w, w_layer, x3, mod, layer, k_gate, geo, tn):
    _, n_i, tm = _tiles(geo)
    per = D_MODEL // tn
    in_specs = [pl.BlockSpec((tm, a.shape[1]), lambda i, j: (i, 0)) for a in a_list]
    assert all(a.shape[1] == a_list[0].shape[1] for a in a_list)
    in_specs += [pl.BlockSpec((1, a.shape[1], tn), lambda i, j, r=r: (w_layer, r, j))
                 for r, a in enumerate(a_list)]
    w_list = [w] * len(a_list)
    in_specs += [_x_spec(geo, tn, lambda j: j),
                 _mod_spec(geo, layer, tn, lambda j: k_gate * per + j)]
    return pl.pallas_call(
        functools.partial(_mm_res_kernel, n_lhs=len(a_list)),
        grid=(n_i, per),
        in_specs=in_specs,
        out_specs=_x_spec(geo, tn, lambda j: j),
        out_shape=jax.ShapeDtypeStruct(x3.shape, F32),
        compiler_params=_cparams(("parallel", "parallel")),
        name="mm_res",
    )(*a_list, *w_list, x3, mod)


def _ffn_in_kernel(x_ref, nw_ref, sc_ref, sh_ref, wg_ref, wu_ref, cs_ref, cw_ref, cb_ref,
                   a_ref, cso_ref, h_scr, tail_scr, *, nrt, valid):
    i, j = pl.program_id(0), pl.program_id(1)

    @pl.when(j == 0)
    def _():
        _norm_mod_to_scratch(x_ref, nw_ref, sc_ref, sh_ref, h_scr)

    h = h_scr[...]
    seq_blk, row_blk, tn = x_ref.shape[0], x_ref.shape[1], a_ref.shape[1]
    g = jnp.dot(h, wg_ref[0], preferred_element_type=F32).reshape(seq_blk, row_blk, tn)
    u = jnp.dot(h, wu_ref[0], preferred_element_type=F32).reshape(seq_blk, row_blk, tn)
    prev = cs_ref[0]
    if nrt > 1:
        prev = jnp.where(i % nrt == 0, prev, tail_scr[j])
    t2, t1 = prev[:, 0:1, :], prev[:, 1:2, :]
    w = cw_ref[...]
    y = (cb_ref[...] + w[0:1] * _shifted(g, [t2, t1], 2, axis=1)
         + w[1:2] * _shifted(g, [t2, t1], 1, axis=1) + w[2:3] * g)
    a_ref[...] = (y * _sigmoid(y) * u).reshape(a_ref.shape).astype(a_ref.dtype)
    new_tail = g[:, valid - 2:valid, :]
    if nrt > 1:
        tail_scr[j] = new_tail
    cso_ref[0] = new_tail


def _ffn_in(x3, nw, mod, layer, w_g, w_u, conv_state, conv_w, conv_b, geo, tn):
    nrt, n_i, tm = _tiles(geo)
    n_j = D_FF // tn
    valid = geo.row_blk if geo.valid == geo.chunk else geo.valid
    tail_shape = (n_j, geo.seq_blk, CONV_K_F - 1, tn) if nrt > 1 else (1, 1, CONV_K_F - 1, 128)
    cs_spec = pl.BlockSpec((1, geo.seq_blk, CONV_K_F - 1, tn), lambda i, j: (layer, i // nrt, 0, j))
    act, tails = pl.pallas_call(
        functools.partial(_ffn_in_kernel, nrt=nrt, valid=valid),
        grid=(n_i, n_j),
        in_specs=[
            _x_spec(geo, D_MODEL, lambda j: 0),
            pl.BlockSpec((1, D_MODEL), lambda i, j: (0, 0)),
            _mod_spec(geo, layer, D_MODEL, lambda j: 4),
            _mod_spec(geo, layer, D_MODEL, lambda j: 3),
            pl.BlockSpec((1, D_MODEL, tn), lambda i, j: (layer, 0, j)),
            pl.BlockSpec((1, D_MODEL, tn), lambda i, j: (layer, 0, j)),
            cs_spec,
            pl.BlockSpec((CONV_K_F, tn), lambda i, j: (0, j)),
            pl.BlockSpec((1, tn), lambda i, j: (0, j)),
        ],
        out_specs=[
            pl.BlockSpec((tm, tn), lambda i, j: (i, j)),
            pl.BlockSpec((1, geo.seq_blk, CONV_K_F - 1, tn), lambda i, j: (i, 0, 0, j)),
        ],
        out_shape=[
            jax.ShapeDtypeStruct((geo.n_seq * geo.seq_len, D_FF), MXU_DTYPE),
            jax.ShapeDtypeStruct((n_i, geo.seq_blk, CONV_K_F - 1, D_FF), F32),
        ],
        scratch_shapes=[pltpu.VMEM((tm, D_MODEL), MXU_DTYPE),
                        pltpu.VMEM(tail_shape, F32)],
        compiler_params=_cparams(("arbitrary", "arbitrary")),
        name="ffn_in",
    )(x3, nw.reshape(1, D_MODEL), mod, mod, w_g, w_u, conv_state, conv_w, conv_b.reshape(1, D_FF))
    last = tails.reshape(n_i // nrt, nrt, geo.seq_blk, CONV_K_F - 1, D_FF)[:, nrt - 1]
    return act, last.reshape(1, geo.n_seq, CONV_K_F - 1, D_FF)


def _norm_kernel(x_ref, nw_ref, o_ref):
    o_ref[...] = _rmsnorm_rows(x_ref[...]) * nw_ref[...]


def _final_norm(x3, nw, geo):
    _, n_i, _ = _tiles(geo)
    return pl.pallas_call(
        _norm_kernel,
        grid=(n_i, 1),
        in_specs=[_x_spec(geo, D_MODEL, lambda j: 0),
                  pl.BlockSpec((1, D_MODEL), lambda i, j: (0, 0))],
        out_specs=_x_spec(geo, D_MODEL, lambda j: 0),
        out_shape=jax.ShapeDtypeStruct(x3.shape, F32),
        compiler_params=_cparams(("parallel", "arbitrary")),
        name="final_norm",
    )(x3, nw.reshape(1, D_MODEL))


def _state_spec(shape, layer, geo):
    rest = tuple(shape[2:])
    zeros = (0,) * len(rest)
    return pl.BlockSpec((1, geo.scan_seqs) + rest, lambda i, c: (layer, i) + zeros)


def _row_spec(geo, width, col):
    nc = geo.seq_len // geo.chunk
    return pl.BlockSpec((geo.scan_seqs * geo.chunk, width), lambda i, c: (i * nc + c, col))


def _gates_t_spec(geo):
    nc = geo.seq_len // geo.chunk
    return pl.BlockSpec((geo.scan_seqs, GATE_W, geo.chunk), lambda i, c: (i * nc + c, 0, 0))


ROWS, LEAD, STATE, CONST = "rows", "lead", "state", "const"


def _per_sequence(body, geo, kinds, interleave):
    def view(ref, kind, k):
        if geo.scan_seqs == 1:
            return ref
        if kind == ROWS:
            return ref.at[pl.ds(k * geo.chunk, geo.chunk)]
        if kind == LEAD:
            return ref.at[pl.ds(k, 1)]
        if kind == STATE:
            return ref.at[:, pl.ds(k, 1)]
        return ref

    def wrapped(*refs):
        running = [body(*[view(r, kind, k) for r, kind in zip(refs, kinds)])
                   for k in range(geo.scan_seqs)]
        if not interleave:
            for gen in running:
                for _ in gen:
                    pass
            return
        while running:
            still = []
            for gen in running:
                if next(gen, StopIteration) is not StopIteration:
                    still.append(gen)
            running = still
    return wrapped


def _seq_call(kernel_fn, geo, name, in_specs, args, out_specs, out_shapes, scratch, kinds,
              stacked_prev, interleave=True):
    nc = geo.seq_len // geo.chunk
    assert geo.scan_seqs == 1 or nc == 1
    kernel_fn = _per_sequence(kernel_fn, geo, kinds, interleave)
    in_specs, args = list(in_specs), list(args)
    aliases = {}
    assert len(stacked_prev) <= 1
    for out_idx, arr in stacked_prev.items():
        kernel_fn = _drop_ref(kernel_fn, len(in_specs))
        aliases[len(in_specs)] = out_idx
        in_specs.append(pl.BlockSpec(memory_space=pl.ANY))
        args.append(arr)
    return pl.pallas_call(
        kernel_fn,
        grid=(geo.n_seq // geo.scan_seqs, nc),
        in_specs=in_specs,
        out_specs=out_specs,
        out_shape=out_shapes,
        scratch_shapes=scratch,
        input_output_aliases=aliases,
        compiler_params=_cparams(("parallel", "arbitrary")),
        name=name,
    )(*args)


def _ssd_kernel(z_ref, xbc_ref, gc_ref, gr_ref, cs_ref, s0_ref, cw_ref, cb_ref,
                dtb_r_ref, dtb_c_ref, al_r_ref, al_c_ref, dsk_ref, nw_ref, exp_ref,
                y_ref, cso_ref, so_ref, tail_scr, *, lc, lv, single_chunk):
    if single_chunk:
        s_ref = s0_ref
        cs = cs_ref[0, 0]
        tails = [cs[0:1, :], cs[1:2, :], cs[2:3, :]]
    else:
        @pl.when(pl.program_id(1) == 0)
        def _():
            tail_scr[0, 5:8, :] = cs_ref[0, 0]
            so_ref[0, 0] = s0_ref[0, 0]

        s_ref = so_ref
        tails = [tail_scr[0, 5:6, :], tail_scr[0, 6:7, :], tail_scr[0, 7:8, :]]

    x = xbc_ref[...]
    w = cw_ref[0]
    xc = (cb_ref[0] + w[0:1] * _shifted(x, tails, 3) + w[1:2] * _shifted(x, tails, 2)
          + w[2:3] * _shifted(x, tails, 1) + w[3:4] * x)
    xc = xc * _sigmoid(xc)
    new_tail = x[lv - 3:lv, :]
    if not single_chunk:
        tail_scr[0, 5:8, :] = new_tail
    cso_ref[0, 0] = new_tail
    xa = xc[:, :D_A]
    bm = xc[:, D_A:D_A + G_A * N_A]
    cm = xc[:, D_A + G_A * N_A:]

    dt_c = _softplus(gc_ref[:, 0:H_A] + dtb_r_ref[...])
    dt_r = _softplus(gr_ref[0, 0:H_A, :] + dtb_c_ref[...])
    if lv < lc:
        dt_c = jnp.where(lax.broadcasted_iota(jnp.int32, dt_c.shape, 0) < lv, dt_c, 0.0)
        dt_r = jnp.where(lax.broadcasted_iota(jnp.int32, dt_r.shape, 1) < lv, dt_r, 0.0)
    lower, upper, mask = _tri(lc)
    cum_c = jnp.dot(lower, dt_c * (-jnp.exp(al_r_ref[...])), precision=HI)
    cum_r = jnp.dot(dt_r * (-jnp.exp(al_c_ref[...])), upper, precision=HI)
    cum_last = cum_c[lc - 1:lc, :]
    expand = exp_ref[...]
    ecum_x = jnp.dot(jnp.exp(cum_c), expand, precision=HI)
    tail_x = jnp.dot(jnp.exp(cum_last - cum_c) * dt_c, expand, precision=HI)

    hg = H_A // G_A
    gw = hg * P_A
    groups = range(G_A)
    gsl = lambda g: slice(g * gw, (g + 1) * gw)
    cg = [cm[:, g * N_A:(g + 1) * N_A] for g in groups]
    bg = [bm[:, g * N_A:(g + 1) * N_A] for g in groups]
    yield
    cb_ts = [lax.dot_general(cg[g], bg[g], _NT, preferred_element_type=F32) for g in groups]
    y_inter = [lax.dot_general(cg[g], s_ref[0, 0, g], _NT, preferred_element_type=F32)
               for g in groups]
    upd = [lax.dot_general(xa[:, gsl(g)] * tail_x[:, gsl(g)], bg[g], _TN,
                           preferred_element_type=F32) for g in groups]
    yield
    w_ts = [cb_ts[h // hg] * dt_r[h:h + 1, :]
            * jnp.exp(jnp.where(mask, cum_c[:, h:h + 1] - cum_r[h:h + 1, :], -jnp.inf))
            for h in range(H_A)]
    yield
    pieces = []
    for j in range(H_A // 2):
        xp = xa[:, j * 128:(j + 1) * 128]
        lane = lax.broadcasted_iota(jnp.int32, xp.shape, 1)
        pieces.append(
            jnp.dot(w_ts[2 * j], jnp.where(lane < P_A, xp, 0.0), preferred_element_type=F32)
            + jnp.dot(w_ts[2 * j + 1], jnp.where(lane >= P_A, xp, 0.0),
                      preferred_element_type=F32))
    yield
    y = (jnp.concatenate(pieces, axis=1) + jnp.concatenate(y_inter, axis=1) * ecum_x
         + dsk_ref[...] * xa)
    for g in groups:
        for hh in range(hg):
            h = g * hg + hh
            rows = slice(hh * P_A, (hh + 1) * P_A)
            so_ref[0, 0, g, rows, :] = (s_ref[0, 0, g, rows, :] * jnp.exp(cum_r[h:h + 1, lc - 1:lc])
                                        + upd[g][rows, :])

    z = z_ref[...]
    y = y * (z * _sigmoid(z))
    y = jnp.concatenate([_rmsnorm_rows(y[:, g * gw:(g + 1) * gw]) for g in range(G_A)], axis=1)
    y_ref[...] = (y * nw_ref[...]).astype(y_ref.dtype)


def _ssd(p1, gates, gates_t, conv_state, ssd_state, prm, e, geo, prev_state_out):
    b = geo.n_seq
    const2 = lambda shape: pl.BlockSpec(shape, lambda i, c: (0, 0))
    n_even = ssd_state.shape[0]
    s5 = ssd_state.reshape(n_even, b, G_A, (H_A // G_A) * P_A, N_A)
    expand = (jnp.arange(D_A)[None, :] // P_A == jnp.arange(H_A)[:, None]).astype(F32)
    cso_shape = (1,) + conv_state.shape[1:]
    y, cso, so = _seq_call(
        functools.partial(_ssd_kernel, lc=geo.chunk, lv=geo.valid,
                          single_chunk=geo.seq_len == geo.chunk), geo, "ssd",
        in_specs=[
            _row_spec(geo, D_A, 0),
            _row_spec(geo, CONV_DIM_A, 2),
            _row_spec(geo, GATE_W, 0),
            _gates_t_spec(geo),
            _state_spec(conv_state.shape, e, geo),
            _state_spec(s5.shape, e, geo),
            pl.BlockSpec((1, CONV_K_A, CONV_DIM_A), lambda i, c: (e, 0, 0)),
            pl.BlockSpec((1, 1, CONV_DIM_A), lambda i, c: (e, 0, 0)),
            const2((1, H_A)), const2((H_A, 1)), const2((1, H_A)), const2((H_A, 1)),
            const2((1, D_A)), const2((1, D_A)), const2((H_A, D_A)),
        ],
        args=(p1, p1, gates, gates_t, conv_state, s5,
              prm['conv_w_a'], prm['conv_b_a'].reshape(n_even, 1, CONV_DIM_A),
              prm['dt_bias'][e].reshape(1, H_A), prm['dt_bias'][e].reshape(H_A, 1),
              prm['a_log'][e].reshape(1, H_A), prm['a_log'][e].reshape(H_A, 1),
              jnp.repeat(prm['d_skip'][e], P_A).reshape(1, D_A),
              prm['norm_a'][e].reshape(1, D_A), expand),
        out_specs=[_row_spec(geo, D_A, 0), _state_spec(cso_shape, 0, geo),
                   _state_spec(s5.shape, e, geo)],
        out_shapes=[
            jax.ShapeDtypeStruct((b * geo.seq_len, D_A), MXU_DTYPE),
            jax.ShapeDtypeStruct(cso_shape, F32),
            jax.ShapeDtypeStruct(s5.shape, F32),
        ],
        scratch=[pltpu.VMEM((geo.scan_seqs, SUBLANES, CONV_DIM_A), F32)],
        kinds=[ROWS, ROWS, ROWS, LEAD, STATE, STATE] + [CONST] * 9 + [ROWS, STATE, STATE, LEAD],
        stacked_prev={} if prev_state_out is None else {2: prev_state_out},
    )
    return y, cso, so


def _mlstm_kernel(q_ref, k_ref, v_ref, og_ref, gc_ref, gr_ref, c0_ref, n0_ref, m0_ref,
                  ib_r_ref, ib_c_ref, fb_r_ref, fb_c_ref, nw_ref,
                  h_ref, co_ref, no_ref, mo_ref, *, lc, lv, single_chunk):
    if single_chunk:
        c_ref, n_ref, m_ref = c0_ref, n0_ref, m0_ref
    else:
        @pl.when(pl.program_id(1) == 0)
        def _():
            co_ref[0, 0] = c0_ref[0, 0]
            no_ref[0, 0] = n0_ref[0, 0]
            mo_ref[0, 0] = m0_ref[0, 0]

        c_ref, n_ref, m_ref = co_ref, no_ref, mo_ref
    m_old, n_old = m_ref[0, 0], n_ref[0, 0]

    i0, f0 = H_A, H_A + H_B
    li_c = gc_ref[:, i0:i0 + H_B] + ib_r_ref[...]
    lf_c = _log_sigmoid(gc_ref[:, f0:f0 + H_B] + fb_r_ref[...])
    li_r = gr_ref[0, i0:i0 + H_B, :] + ib_c_ref[...]
    lf_r = _log_sigmoid(gr_ref[0, f0:f0 + H_B, :] + fb_c_ref[...])
    if lv < lc:
        vc = lax.broadcasted_iota(jnp.int32, li_c.shape, 0) < lv
        vr = lax.broadcasted_iota(jnp.int32, li_r.shape, 1) < lv
        li_c, lf_c = jnp.where(vc, li_c, NEG_BIG), jnp.where(vc, lf_c, 0.0)
        li_r, lf_r = jnp.where(vr, li_r, NEG_BIG), jnp.where(vr, lf_r, 0.0)
    lower, upper, mask = _tri(lc)
    bc_c = jnp.dot(lower, lf_c, precision=HI)
    bc_r = jnp.dot(lf_r, upper, precision=HI)

    heads = range(H_B)
    q = [q_ref[:, h * DK_B:(h + 1) * DK_B] * (DK_B ** -0.5) for h in heads]
    k = [k_ref[:, h * DK_B:(h + 1) * DK_B] for h in heads]
    v = [v_ref[:, h * DV_B:(h + 1) * DV_B] for h in heads]
    yield
    qk = [lax.dot_general(q[h], k[h], _NT, preferred_element_type=F32) for h in heads]
    q_c = [jnp.dot(q[h], c_ref[0, 0, h], preferred_element_type=F32) for h in heads]
    yield
    m_t, w_in, w_ts = [], [], []
    for h in heads:
        bcc = bc_c[:, h:h + 1]
        dmat = jnp.where(mask, bcc - bc_r[h:h + 1, :] + li_r[h:h + 1, :], -jnp.inf)
        inter = bcc + m_old[:, h:h + 1]
        m_t.append(jnp.maximum(inter, jnp.max(dmat, axis=1, keepdims=True)))
        w_in.append(jnp.exp(inter - m_t[h]))
        w_ts.append(jnp.exp(dmat - m_t[h]) * qk[h])
    yield
    wv = [jnp.dot(w_ts[h], v[h], preferred_element_type=F32) for h in heads]
    yield
    hs = []
    for h in heads:
        num = wv[h] + w_in[h] * q_c[h]
        den = (jnp.sum(w_ts[h], axis=1, keepdims=True)
               + w_in[h] * jnp.sum(q[h] * n_old[h:h + 1, :], axis=1, keepdims=True))
        hs.append(_rmsnorm_rows(num / jnp.maximum(jnp.abs(den), jnp.exp(-m_t[h]))))
    lane_h = lax.broadcasted_iota(jnp.int32, (1, H_B), 1)
    m_out = jnp.zeros((1, H_B), F32)
    ks, w_c = [], []
    for h in heads:
        bcc = bc_c[:, h:h + 1]
        m_new = m_t[h][lv - 1:lv, :]
        bc_last = bcc[lc - 1:lc, :]
        ks.append(k[h] * jnp.exp(bc_last - bcc + li_c[:, h:h + 1] - m_new))
        w_c.append(jnp.exp(bc_last + m_old[:, h:h + 1] - m_new))
        m_out = jnp.where(lane_h == h, m_new, m_out)
    yield
    kv = [lax.dot_general(ks[h], v[h], _TN, preferred_element_type=F32) for h in heads]
    yield
    for h in heads:
        co_ref[0, 0, h] = w_c[h] * c_ref[0, 0, h] + kv[h]
        no_ref[0, 0, h:h + 1, :] = w_c[h] * n_old[h:h + 1, :] + jnp.sum(ks[h], axis=0, keepdims=True)
    mo_ref[0, 0] = m_out
    hn = jnp.concatenate(hs, axis=1) * nw_ref[...]
    h_ref[...] = (hn * _sigmoid(og_ref[...])).astype(h_ref.dtype)


def _mlstm(p1, gates, gates_t, c_state, n_state, m_state, prm, e, geo, prev_state_out):
    b = geo.n_seq
    const2 = lambda shape: pl.BlockSpec(shape, lambda i, c: (0, 0))
    qk_w = H_B * DK_B
    m4 = m_state.reshape(m_state.shape[0], b, 1, H_B)
    one = lambda shape: (1,) + tuple(shape[1:])
    h, co, no, mo = _seq_call(
        functools.partial(_mlstm_kernel, lc=geo.chunk, lv=geo.valid,
                          single_chunk=geo.seq_len == geo.chunk), geo, "mlstm",
        in_specs=[
            _row_spec(geo, qk_w, 9), _row_spec(geo, qk_w, 10),
            _row_spec(geo, D_B, 1), _row_spec(geo, D_B, 2),
            _row_spec(geo, GATE_W, 0),
            _gates_t_spec(geo),
            _state_spec(c_state.shape, e, geo), _state_spec(n_state.shape, e, geo),
            _state_spec(m4.shape, e, geo),
            const2((1, H_B)), const2((H_B, 1)), const2((1, H_B)), const2((H_B, 1)),
            const2((1, D_B)),
        ],
        args=(p1, p1, p1, p1, gates, gates_t, c_state, n_state, m4,
              prm['i_bias'][e].reshape(1, H_B), prm['i_bias'][e].reshape(H_B, 1),
              prm['f_bias'][e].reshape(1, H_B), prm['f_bias'][e].reshape(H_B, 1),
              prm['norm_b'][e].reshape(1, D_B)),
        out_specs=[_row_spec(geo, D_B, 0), _state_spec(c_state.shape, e, geo),
                   _state_spec(one(n_state.shape), 0, geo), _state_spec(one(m4.shape), 0, geo)],
        out_shapes=[
            jax.ShapeDtypeStruct((b * geo.seq_len, D_B), MXU_DTYPE),
            jax.ShapeDtypeStruct(c_state.shape, F32),
            jax.ShapeDtypeStruct(one(n_state.shape), F32),
            jax.ShapeDtypeStruct(one(m4.shape), F32),
        ],
        scratch=[],
        kinds=[ROWS] * 5 + [LEAD] + [STATE] * 3 + [CONST] * 5 + [ROWS] + [STATE] * 3,
        stacked_prev={} if prev_state_out is None else {1: prev_state_out},
        interleave=False,
    )
    return h, co, no, mo.reshape(1, b, H_B)


def _hgrn_kernel(q_ref, f_ref, i_ref, g_ref, lbl_ref, s0_ref, nw_ref, o_ref, so_ref,
                 *, lc, lv, layer_o, bs, single_chunk):
    if single_chunk:
        s_ref = s0_ref
    else:
        @pl.when(pl.program_id(1) == 0)
        def _():
            so_ref[0, 0] = s0_ref[0, 0]

        s_ref = so_ref

    lbl = lbl_ref[...]
    ex = jnp.exp(lbl - jnp.max(lbl, axis=0, keepdims=True))
    sm = ex / jnp.sum(ex, axis=0, keepdims=True)
    lb_all = [sm[0:1, :]]
    for r in range(1, lbl.shape[0]):
        lb_all.append(lb_all[-1] + sm[r:r + 1, :])
    lb = lb_all[layer_o] - lb_all[0]

    fx = f_ref[...]
    e1 = jnp.exp(-jnp.abs(fx))
    log_sig = jnp.minimum(fx, 0.0) - jnp.log1p(e1)
    la = jnp.log(lb)
    lb_ = jnp.log1p(-lb) + log_sig
    logf = jnp.maximum(la, lb_) + jnp.log1p(jnp.exp(-jnp.abs(la - lb_)))
    kk = (1.0 - lb) * (jnp.where(fx >= 0.0, e1, 1.0) / (1.0 + e1))
    if lv < lc:
        valid = lax.broadcasted_iota(jnp.int32, fx.shape, 0) < lv
        logf = jnp.where(valid, logf, 0.0)
        kk = jnp.where(valid, kk, 0.0)

    nb = lc // bs
    r_i = lax.broadcasted_iota(jnp.int32, (lc, lc), 0)
    c_i = lax.broadcasted_iota(jnp.int32, (lc, lc), 1)
    sh = int(math.log2(bs))
    blk_lower = ((c_i <= r_i) & ((c_i >> sh) == (r_i >> sh))).astype(F32)
    gw = jnp.dot(blk_lower, logf, precision=HI)
    q = q_ref[...]
    v = i_ref[...]
    blk = lambda a, i: a[i * bs:(i + 1) * bs, :]
    tots = [gw[(i + 1) * bs - 1:(i + 1) * bs, :] for i in range(nb)]
    before = [jnp.zeros_like(tots[0])]
    for i in range(nb):
        before.append(before[-1] + tots[i])
    g_tot = before[nb]
    qt = q * jnp.exp(gw)
    kt = [blk(kk, j) * jnp.exp(tots[j] - blk(gw, j)) for j in range(nb)]
    q_in = jnp.concatenate([blk(qt, i) * jnp.exp(before[i]) for i in range(nb)], axis=0)
    k_out = jnp.concatenate([kt[j] * jnp.exp(g_tot - before[j + 1]) for j in range(nb)], axis=0)

    heads = range(H_C)
    hsl = lambda h: slice(h * DK_C, (h + 1) * DK_C)
    yield
    o_inter = [jnp.dot(q_in[:, hsl(h)], s_ref[0, 0, h], preferred_element_type=F32)
               for h in heads]
    kv = [lax.dot_general(k_out[:, hsl(h)], v[:, hsl(h)], _TN, preferred_element_type=F32)
          for h in heads]
    att_off = [None]
    for i in range(1, nb):
        k_hat = jnp.concatenate(
            [kt[j] if j == i - 1 else kt[j] * jnp.exp(before[i] - before[j + 1])
             for j in range(i)] + [jnp.zeros((lc - i * bs, D_C), F32)], axis=0)
        q_ti = blk(qt, i)
        att_off.append([lax.dot_general(q_ti[:, hsl(h)], k_hat[:, hsl(h)], _NT,
                                        preferred_element_type=F32) for h in heads])
    yield
    gw2 = gw * LOG2E
    ck = jnp.log2(kk) - gw2
    n_t = bs // SUBLANES
    lane_s = lax.broadcasted_iota(jnp.int32, (SUBLANES, lc), 1)
    row_t = lax.broadcasted_iota(jnp.int32, (bs, lc), 0)
    col_s = lax.broadcasted_iota(jnp.int32, (bs, lc), 1)
    att = []
    for i in range(nb):
        g_i, q_i, ck_i = blk(gw2, i), blk(q, i), blk(ck, i)
        att_d = [[jnp.zeros((SUBLANES, lc), F32) for _ in range(n_t)] for _ in heads]
        for s in range(bs):
            t0 = s // SUBLANES
            p = q_i[t0 * SUBLANES:, :] * jnp.exp2(g_i[t0 * SUBLANES:, :] + ck_i[s:s + 1, :])
            for h in heads:
                a = jnp.sum(p[:, hsl(h)], axis=1, keepdims=True)
                for tt in range(t0, n_t):
                    a_t = a[(tt - t0) * SUBLANES:(tt - t0 + 1) * SUBLANES, :]
                    att_d[h][tt] = jnp.where(lane_s == i * bs + s, a_t, att_d[h][tt])
        causal = (col_s - i * bs) <= row_t
        att_i = []
        for h in heads:
            a = att_d[h][0] if n_t == 1 else jnp.concatenate(att_d[h], axis=0)
            a = jnp.where(causal, a, 0.0)
            att_i.append(a if i == 0 else a + att_off[i][h])
        att.append(att_i)
    yield
    o_intra = [[jnp.dot(att[i][h], v[:, hsl(h)], preferred_element_type=F32) for h in heads]
               for i in range(nb)]
    yield
    outs = []
    for h in heads:
        o_h = o_intra[0][h] if nb == 1 else jnp.concatenate([o_intra[i][h] for i in range(nb)],
                                                             axis=0)
        outs.append(_rmsnorm_rows(o_h + o_inter[h]))
        dec_col = jnp.transpose(jnp.broadcast_to(jnp.exp(g_tot[:, hsl(h)]), (DK_C, DK_C)))
        so_ref[0, 0, h] = dec_col * s_ref[0, 0, h] + kv[h]
    gate = g_ref[...]
    o_ref[...] = (jnp.concatenate(outs, axis=1) * nw_ref[...]
                  * (gate * _sigmoid(gate))).astype(o_ref.dtype)


def _hgrn(p, state, prm, o, geo, prev_state_out):
    n_odd = prm['lb_logits'].shape[0]
    return _seq_call(
        functools.partial(_hgrn_kernel, lc=geo.chunk, lv=geo.valid, layer_o=o,
                          bs=min(16, geo.chunk), single_chunk=geo.seq_len == geo.chunk), geo, "hgrn",
        in_specs=[
            _row_spec(geo, D_C, 0), _row_spec(geo, D_C, 1), _row_spec(geo, D_C, 2),
            _row_spec(geo, D_C, 3),
            pl.BlockSpec((n_odd, D_C), lambda i, c: (0, 0)),
            _state_spec(state.shape, o, geo),
            pl.BlockSpec((1, D_C), lambda i, c: (0, 0)),
        ],
        args=(p, p, p, p, prm['lb_logits'], state, prm['norm_c'][o].reshape(1, D_C)),
        out_specs=[_row_spec(geo, D_C, 0), _state_spec(state.shape, o, geo)],
        out_shapes=[
            jax.ShapeDtypeStruct((geo.n_seq * geo.seq_len, D_C), MXU_DTYPE),
            jax.ShapeDtypeStruct(state.shape, F32),
        ],
        scratch=[],
        kinds=[ROWS] * 4 + [CONST, STATE, CONST, ROWS, STATE],
        stacked_prev={} if prev_state_out is None else {1: prev_state_out},
    )


def _prep_weights(prm):
    a0, a1, a2 = D_A, D_A + CONV_DIM_A, D_A + CONV_DIM_A + H_A
    q1 = a2 + H_B * DK_B
    k1 = q1 + H_B * DK_B
    v1 = k1 + D_B
    o1 = v1 + D_B
    n_gate = H_A + 2 * H_B
    cast = lambda a: a.astype(MXU_DTYPE)
    wab = prm['w_in_ab']
    w = {
        'in_ab': cast(jnp.concatenate(
            [wab[:, :, :a0], wab[:, :, k1:v1], wab[:, :, v1:o1], wab[:, :, a0:a1],
             wab[:, :, a2:q1], wab[:, :, q1:k1]], axis=2)),
        'gates_ab': cast(jnp.pad(jnp.concatenate([wab[:, :, a1:a2], wab[:, :, o1:]], axis=2),
                                 ((0, 0), (0, 0), (0, GATE_W - n_gate)))),
    }
    for name, key in (('out_ab', 'w_out_ab'), ('in_c', 'w_in_c'), ('out_c', 'w_out_c'),
                      ('ffn_g', 'w_ffn_g'), ('ffn_u', 'w_ffn_u'), ('ffn_d', 'w_ffn_d')):
        w[name] = cast(prm[key])
    return w


def _trunk(x3, mod, states, prm, w, geo, tn_in, tn_res):
    conv_a, ssd, mem_c, mem_n, mem_m, hgrn, ffn_buf = states
    n_conv, n_n, n_m, n_ffn = [], [], [], []
    new_ssd = new_c = new_hgrn = None
    nc = geo.seq_len // geo.chunk
    for layer in range(DEPTH):
        mm_norm = functools.partial(_mm_norm, x3, prm['norm_mix'][layer], mod, layer, 1, 0,
                                    geo=geo)
        if layer % 2 == 0:
            e = layer // 2
            p1 = mm_norm(w=w['in_ab'], w_layer=e, tn=tn_in[layer])
            gates = mm_norm(w=w['gates_ab'], w_layer=e, tn=GATE_W)
            gates_t = gates.reshape(geo.n_seq * nc, geo.chunk, GATE_W).transpose(0, 2, 1)
            ya, cv, new_ssd = _ssd(p1, gates, gates_t, conv_a, ssd, prm, e, geo, new_ssd)
            hb, new_c, nn, mmm = _mlstm(p1, gates, gates_t, mem_c, mem_n, mem_m, prm, e, geo, new_c)
            n_conv.append(cv); n_n.append(nn); n_m.append(mmm)
            x3 = _mm_res([ya, hb], w['out_ab'], e, x3, mod, layer, 2, geo, tn_res[layer])
        else:
            o = layer // 2
            p = mm_norm(w=w['in_c'], w_layer=o, tn=tn_in[layer])
            oc, new_hgrn = _hgrn(p, hgrn, prm, o, geo, new_hgrn)
            x3 = _mm_res([oc], w['out_c'], o, x3, mod, layer, 2, geo, tn_res[layer])
        act, fb = _ffn_in(x3, prm['norm_ffn'][layer], mod, layer, w['ffn_g'], w['ffn_u'],
                          ffn_buf, prm['conv_w_f'][layer], prm['conv_b_f'][layer], geo, FFN_TN)
        n_ffn.append(fb)
        x3 = _mm_res([act], w['ffn_d'], layer, x3, mod, layer, 5, geo, FFN_DOWN_TN)
    y = _final_norm(x3, prm['norm_f'], geo)
    cat = lambda xs: jnp.concatenate(xs, axis=0)
    return (y, cat(n_conv), new_ssd.reshape(ssd.shape), new_c, cat(n_n), cat(n_m), new_hgrn,
            cat(n_ffn))


def kernel(x_prompt, x_sample, c_prompt, c_sample, state_ssd_conv, state_ssd, state_mlstm_c, state_mlstm_n, state_mlstm_m, state_hgrn, state_ffn_conv, w_ada, b_ada, norm_mix, norm_ffn, w_in_ab, conv_w_a, conv_b_a, dt_bias, a_log, d_skip, norm_a, i_bias, f_bias, norm_b, w_out_ab, w_in_c, lb_logits, norm_c, w_out_c, w_ffn_g, w_ffn_u, conv_w_f, conv_b_f, w_ffn_d, norm_f):
    prm = dict(norm_mix=norm_mix, norm_ffn=norm_ffn, w_in_ab=w_in_ab, conv_w_a=conv_w_a,
               conv_b_a=conv_b_a, dt_bias=dt_bias, a_log=a_log, d_skip=d_skip, norm_a=norm_a,
               i_bias=i_bias, f_bias=f_bias, norm_b=norm_b, w_out_ab=w_out_ab, w_in_c=w_in_c,
               lb_logits=lb_logits, norm_c=norm_c, w_out_c=w_out_c, w_ffn_g=w_ffn_g,
               w_ffn_u=w_ffn_u, conv_w_f=conv_w_f, conv_b_f=conv_b_f, w_ffn_d=w_ffn_d,
               norm_f=norm_f)
    bp, lp, _ = x_prompt.shape
    bs, ls, _ = x_sample.shape
    n_even, n_odd = state_ssd.shape[0], state_hgrn.shape[0]
    w = _prep_weights(prm)

    n_c = bs + bp
    n_c_pad = -(-n_c // SUBLANES) * SUBLANES
    c_all = jnp.pad(jnp.concatenate([c_sample, c_prompt], axis=0), ((0, n_c_pad - n_c), (0, 0)))
    mod = _ada(c_all, w_ada, b_ada).reshape(DEPTH, n_c_pad, 1, 6 * D_MODEL)

    zeros = lambda *s: jnp.zeros(s, F32)
    st_p = (zeros(n_even, bp, CONV_K_A - 1, CONV_DIM_A), zeros(n_even, bp, H_A, P_A, N_A),
            zeros(n_even, bp, H_B, DK_B, DV_B), zeros(n_even, bp, H_B, DK_B),
            zeros(n_even, bp, H_B), zeros(n_odd, bp, H_C, DK_C, DV_C),
            zeros(DEPTH, bp, CONV_K_F - 1, D_FF))
    lc_p = math.gcd(lp, PROMPT_CHUNK)
    geo_p = Geo(n_seq=bp, seq_len=lp, seq_blk=1, row_blk=min(lp, 1024), chunk=lc_p, valid=lc_p,
                mod_off=bs, scan_seqs=1)
    out_p = _trunk(x_prompt, mod, st_p, prm, w, geo_p, tn_in=(1024,) * DEPTH,
                   tn_res=(1024,) * DEPTH)

    geo_s = Geo(n_seq=bs, seq_len=SUBLANES, seq_blk=bs, row_blk=SUBLANES, chunk=SUBLANES,
                valid=ls, mod_off=0, scan_seqs=4)
    xs = jnp.pad(x_sample, ((0, 0), (0, SUBLANES - ls), (0, 0)))
    st_s = (state_ssd_conv, state_ssd, state_mlstm_c, state_mlstm_n, state_mlstm_m, state_hgrn,
            state_ffn_conv)
    out_s = _trunk(xs, mod, st_s, prm, w, geo_s, tn_in=(512,) * DEPTH, tn_res=(512,) * DEPTH)
    return (out_p[0], out_s[0][:, :ls]) + tuple(out_p[1:]) + tuple(out_s[1:])
```

```python
import collections
import functools
import math

import jax
import jax.numpy as jnp
from jax import lax
from jax.experimental import pallas as pl
from jax.experimental.pallas import tpu as pltpu

F32 = jnp.float32
MXU_DTYPE = jnp.bfloat16
HI = lax.Precision.HIGHEST
NEG_BIG = -1e30
LOG2E = 1.4426950408889634

D_MODEL = 2048
DEPTH = 4
EPS = 1e-6
PROMPT_CHUNK = 64
H_A, P_A, G_A, N_A, CONV_K_A = 32, 64, 4, 128, 4
D_A = H_A * P_A
CONV_DIM_A = D_A + 2 * G_A * N_A
H_B, DK_B, DV_B = 8, 128, 256
D_B = H_B * DV_B
H_C, DK_C, DV_C = 16, 128, 128
D_C = H_C * DV_C
D_FF, CONV_K_F = 5632, 3
FFN_TN = 512
FFN_DOWN_TN = 512
GATE_W = 128
SUBLANES = 8
VMEM_LIMIT = 56 * 1024 * 1024

_NT = (((1,), (1,)), ((), ()))
_TN = (((0,), (0,)), ((), ()))

Geo = collections.namedtuple("Geo", "n_seq seq_len seq_blk row_blk chunk valid mod_off scan_seqs")


def _cparams(sem, flags=None):
    return pltpu.CompilerParams(dimension_semantics=sem, vmem_limit_bytes=VMEM_LIMIT, flags=flags)


def _sigmoid(x):
    return 1.0 / (1.0 + jnp.exp(-x))


def _softplus(x):
    return jnp.maximum(x, 0.0) + jnp.log1p(jnp.exp(-jnp.abs(x)))


def _log_sigmoid(x):
    return jnp.minimum(x, 0.0) - jnp.log1p(jnp.exp(-jnp.abs(x)))


def _tri(n):
    r = lax.broadcasted_iota(jnp.int32, (n, n), 0)
    c = lax.broadcasted_iota(jnp.int32, (n, n), 1)
    mask = c <= r
    return mask.astype(F32), (r <= c).astype(F32), mask


def _shifted(x, tails, k, axis=0):
    row = lax.broadcasted_iota(jnp.int32, x.shape, axis)
    out = pltpu.roll(x, k, axis)
    for r in range(k):
        out = jnp.where(row == r, tails[len(tails) - k + r], out)
    return out


def _drop_ref(fn, idx):
    def wrapped(*refs):
        return fn(*refs[:idx], *refs[idx + 1:])
    return wrapped


def _rmsnorm_rows(x):
    return x * lax.rsqrt(jnp.mean(x * x, axis=-1, keepdims=True) + EPS)


def _ada_kernel(c_ref, w_ref, b_ref, o_ref):
    c = c_ref[...]
    ca = (c * _sigmoid(c)).astype(MXU_DTYPE)
    o_ref[0] = jnp.dot(ca, w_ref[0].astype(MXU_DTYPE), preferred_element_type=F32) + b_ref[0]


def _ada(c_all, w_ada, b_ada, tn=1024):
    rows = c_all.shape[0]
    n = w_ada.shape[2]
    return pl.pallas_call(
        _ada_kernel,
        grid=(DEPTH, n // tn),
        in_specs=[
            pl.BlockSpec((rows, D_MODEL), lambda l, j: (0, 0)),
            pl.BlockSpec((1, D_MODEL, tn), lambda l, j: (l, 0, j)),
            pl.BlockSpec((1, 1, tn), lambda l, j: (l, 0, j)),
        ],
        out_specs=pl.BlockSpec((1, rows, tn), lambda l, j: (l, 0, j)),
        out_shape=jax.ShapeDtypeStruct((DEPTH, rows, n), F32),
        compiler_params=_cparams(("parallel", "parallel")),
        name="ada",
    )(c_all, w_ada, b_ada.reshape(DEPTH, 1, n))


def _tiles(geo):
    nrt = geo.seq_len // geo.row_blk
    return nrt, (geo.n_seq // geo.seq_blk) * nrt, geo.seq_blk * geo.row_blk


def _x_spec(geo, width, col_of):
    nrt = geo.seq_len // geo.row_blk
    return pl.BlockSpec((geo.seq_blk, geo.row_blk, width),
                        lambda i, j: (i // nrt, i % nrt, col_of(j)))


def _mod_spec(geo, layer, width, col_of):
    nrt = geo.seq_len // geo.row_blk
    return pl.BlockSpec((1, geo.seq_blk, 1, width),
                        lambda i, j: (layer, geo.mod_off + i // nrt, 0, col_of(j)))


def _norm_mod_to_scratch(x_ref, nw_ref, sc_ref, sh_ref, h_scr):
    h = (_rmsnorm_rows(x_ref[...]) * nw_ref[...]) * (1.0 + sc_ref[0]) + sh_ref[0]
    h_scr[...] = h.reshape(h_scr.shape).astype(h_scr.dtype)


def _weight_spec(wop, k, tn):
    _, layer, row_block = wop
    return pl.BlockSpec((1, k, tn), lambda i, j: (layer, row_block, j))


def _emit_spec_shape(k, n, tn):
    return (pl.BlockSpec((1, k, tn), lambda i, j: (0, 0, j)),
            jax.ShapeDtypeStruct((1, k, n), MXU_DTYPE))


def _weight_tile(w_ref, wq_ref):
    w = w_ref[0].astype(MXU_DTYPE)
    if wq_ref is not None:
        wq_ref[0] = w
    return w


def _mm_norm_kernel(x_ref, nw_ref, sc_ref, sh_ref, w_ref, o_ref, *rest, emit):
    wq_ref, h_scr = rest if emit else (None,) + rest

    @pl.when(pl.program_id(1) == 0)
    def _():
        _norm_mod_to_scratch(x_ref, nw_ref, sc_ref, sh_ref, h_scr)

    o_ref[...] = jnp.dot(h_scr[...], _weight_tile(w_ref, wq_ref),
                         preferred_element_type=F32).astype(o_ref.dtype)


def _mm_norm(x3, nw, mod, layer, k_sc, k_sh, wop, geo, tn, emit=False):
    _, n_i, tm = _tiles(geo)
    n = wop[0].shape[2]
    out_specs = [pl.BlockSpec((tm, tn), lambda i, j: (i, j))]
    out_shape = [jax.ShapeDtypeStruct((geo.n_seq * geo.seq_len, n), F32)]
    if emit:
        assert n_i == 1
        spec, shape = _emit_spec_shape(D_MODEL, n, tn)
        out_specs.append(spec)
        out_shape.append(shape)
    outs = pl.pallas_call(
        functools.partial(_mm_norm_kernel, emit=emit),
        grid=(n_i, n // tn),
        in_specs=[
            _x_spec(geo, D_MODEL, lambda j: 0),
            pl.BlockSpec((1, D_MODEL), lambda i, j: (0, 0)),
            _mod_spec(geo, layer, D_MODEL, lambda j: k_sc),
            _mod_spec(geo, layer, D_MODEL, lambda j: k_sh),
            _weight_spec(wop, D_MODEL, tn),
        ],
        out_specs=out_specs,
        out_shape=out_shape,
        scratch_shapes=[pltpu.VMEM((tm, D_MODEL), MXU_DTYPE)],
        compiler_params=_cparams(("parallel", "arbitrary")),
        name="mm_norm",
    )(x3, nw.reshape(1, D_MODEL), mod, mod, wop[0])
    return outs[0], (outs[1] if emit else None)


def _mm_res_kernel(*refs, n_lhs, emit):
    a_refs, w_refs = refs[:n_lhs], refs[n_lhs:2 * n_lhs]
    xres_ref, gate_ref, o_ref = refs[2 * n_lhs:2 * n_lhs + 3]
    wq_refs = refs[2 * n_lhs + 3:] if emit else (None,) * n_lhs
    acc = None
    for a_ref, w_ref, wq_ref in zip(a_refs, w_refs, wq_refs):
        d = jnp.dot(a_ref[...], _weight_tile(w_ref, wq_ref), preferred_element_type=F32)
        acc = d if acc is None else acc + d
    o_ref[...] = xres_ref[...] + gate_ref[0] * acc.reshape(o_ref.shape)


def _mm_res(a_list, wops, x3, mod, layer, k_gate, geo, tn, emit=False):
    _, n_i, tm = _tiles(geo)
    per = D_MODEL // tn
    in_specs = [pl.BlockSpec((tm, a.shape[1]), lambda i, j: (i, 0)) for a in a_list]
    in_specs += [_weight_spec(wop, a.shape[1], tn) for wop, a in zip(wops, a_list)]
    in_specs += [_x_spec(geo, tn, lambda j: j),
                 _mod_spec(geo, layer, tn, lambda j: k_gate * per + j)]
    out_specs = [_x_spec(geo, tn, lambda j: j)]
    out_shape = [jax.ShapeDtypeStruct(x3.shape, F32)]
    if emit:
        assert n_i == 1
        for a in a_list:
            spec, shape = _emit_spec_shape(a.shape[1], D_MODEL, tn)
            out_specs.append(spec)
            out_shape.append(shape)
    outs = pl.pallas_call(
        functools.partial(_mm_res_kernel, n_lhs=len(a_list), emit=emit),
        grid=(n_i, per),
        in_specs=in_specs,
        out_specs=out_specs,
        out_shape=out_shape,
        compiler_params=_cparams(("parallel", "parallel")),
        name="mm_res",
    )(*a_list, *[wop[0] for wop in wops], x3, mod)
    return outs[0], (list(outs[1:]) if emit else None)


def _ffn_in_kernel(x_ref, nw_ref, sc_ref, sh_ref, wg_ref, wu_ref, cs_ref, cw_ref, cb_ref,
                   a_ref, cso_ref, *rest, nrt, valid, emit):
    wgq_ref, wuq_ref, h_scr, tail_scr = rest if emit else (None, None) + rest
    i, j = pl.program_id(0), pl.program_id(1)

    @pl.when(j == 0)
    def _():
        _norm_mod_to_scratch(x_ref, nw_ref, sc_ref, sh_ref, h_scr)

    h = h_scr[...]
    seq_blk, row_blk, tn = x_ref.shape[0], x_ref.shape[1], a_ref.shape[1]
    g = jnp.dot(h, _weight_tile(wg_ref, wgq_ref),
                preferred_element_type=F32).reshape(seq_blk, row_blk, tn)
    u = jnp.dot(h, _weight_tile(wu_ref, wuq_ref),
                preferred_element_type=F32).reshape(seq_blk, row_blk, tn)
    prev = cs_ref[0]
    if nrt > 1:
        prev = jnp.where(i % nrt == 0, prev, tail_scr[j])
    t2, t1 = prev[:, 0:1, :], prev[:, 1:2, :]
    w = cw_ref[...]
    y = (cb_ref[...] + w[0:1] * _shifted(g, [t2, t1], 2, axis=1)
         + w[1:2] * _shifted(g, [t2, t1], 1, axis=1) + w[2:3] * g)
    a_ref[...] = (y * _sigmoid(y) * u).reshape(a_ref.shape).astype(a_ref.dtype)
    new_tail = g[:, valid - 2:valid, :]
    if nrt > 1:
        tail_scr[j] = new_tail
    cso_ref[0] = new_tail


def _ffn_in(x3, nw, mod, layer, wop_g, wop_u, conv_state, conv_w, conv_b, geo, tn, emit=False):
    nrt, n_i, tm = _tiles(geo)
    n_j = D_FF // tn
    valid = geo.row_blk if geo.valid == geo.chunk else geo.valid
    tail_shape = (n_j, geo.seq_blk, CONV_K_F - 1, tn) if nrt > 1 else (1, 1, CONV_K_F - 1, 128)
    cs_spec = pl.BlockSpec((1, geo.seq_blk, CONV_K_F - 1, tn), lambda i, j: (layer, i // nrt, 0, j))
    emit_specs, emit_shapes = [], []
    if emit:
        assert n_i == 1
        for _ in range(2):
            spec, shape = _emit_spec_shape(D_MODEL, D_FF, tn)
            emit_specs.append(spec)
            emit_shapes.append(shape)
    act, tails, *emitted = pl.pallas_call(
        functools.partial(_ffn_in_kernel, nrt=nrt, valid=valid, emit=emit),
        grid=(n_i, n_j),
        in_specs=[
            _x_spec(geo, D_MODEL, lambda j: 0),
            pl.BlockSpec((1, D_MODEL), lambda i, j: (0, 0)),
            _mod_spec(geo, layer, D_MODEL, lambda j: 4),
            _mod_spec(geo, layer, D_MODEL, lambda j: 3),
            _weight_spec(wop_g, D_MODEL, tn),
            _weight_spec(wop_u, D_MODEL, tn),
            cs_spec,
            pl.BlockSpec((CONV_K_F, tn), lambda i, j: (0, j)),
            pl.BlockSpec((1, tn), lambda i, j: (0, j)),
        ],
        out_specs=[
            pl.BlockSpec((tm, tn), lambda i, j: (i, j)),
            pl.BlockSpec((1, geo.seq_blk, CONV_K_F - 1, tn), lambda i, j: (i, 0, 0, j)),
        ] + emit_specs,
        out_shape=[
            jax.ShapeDtypeStruct((geo.n_seq * geo.seq_len, D_FF), MXU_DTYPE),
            jax.ShapeDtypeStruct((n_i, geo.seq_blk, CONV_K_F - 1, D_FF), F32),
        ] + emit_shapes,
        scratch_shapes=[pltpu.VMEM((tm, D_MODEL), MXU_DTYPE),
                        pltpu.VMEM(tail_shape, F32)],
        compiler_params=_cparams(("arbitrary", "arbitrary")),
        name="ffn_in",
    )(x3, nw.reshape(1, D_MODEL), mod, mod, wop_g[0], wop_u[0], conv_state, conv_w,
      conv_b.reshape(1, D_FF))
    last = tails.reshape(n_i // nrt, nrt, geo.seq_blk, CONV_K_F - 1, D_FF)[:, nrt - 1]
    return act, last.reshape(1, geo.n_seq, CONV_K_F - 1, D_FF), (emitted if emit else None)


def _norm_kernel(x_ref, nw_ref, o_ref):
    o_ref[...] = _rmsnorm_rows(x_ref[...]) * nw_ref[...]


def _final_norm(x3, nw, geo):
    _, n_i, _ = _tiles(geo)
    return pl.pallas_call(
        _norm_kernel,
        grid=(n_i, 1),
        in_specs=[_x_spec(geo, D_MODEL, lambda j: 0),
                  pl.BlockSpec((1, D_MODEL), lambda i, j: (0, 0))],
        out_specs=_x_spec(geo, D_MODEL, lambda j: 0),
        out_shape=jax.ShapeDtypeStruct(x3.shape, F32),
        compiler_params=_cparams(("parallel", "arbitrary")),
        name="final_norm",
    )(x3, nw.reshape(1, D_MODEL))


def _state_spec(shape, layer, geo):
    rest = tuple(shape[2:])
    zeros = (0,) * len(rest)
    return pl.BlockSpec((1, geo.scan_seqs) + rest, lambda i, c: (layer, i) + zeros)


def _row_spec(geo, width, col):
    nc = geo.seq_len // geo.chunk
    return pl.BlockSpec((geo.scan_seqs * geo.chunk, width), lambda i, c: (i * nc + c, col))


def _gates_t_spec(geo):
    nc = geo.seq_len // geo.chunk
    return pl.BlockSpec((geo.scan_seqs, GATE_W, geo.chunk), lambda i, c: (i * nc + c, 0, 0))


ROWS, LEAD, STATE, CONST = "rows", "lead", "state", "const"


def _per_sequence(body, geo, kinds, interleave):
    def view(ref, kind, k):
        if geo.scan_seqs == 1:
            return ref
        if kind == ROWS:
            return ref.at[pl.ds(k * geo.chunk, geo.chunk)]
        if kind == LEAD:
            return ref.at[pl.ds(k, 1)]
        if kind == STATE:
            return ref.at[:, pl.ds(k, 1)]
        return ref

    def wrapped(*refs):
        running = [body(*[view(r, kind, k) for r, kind in zip(refs, kinds)])
                   for k in range(geo.scan_seqs)]
        if not interleave:
            for gen in running:
                for _ in gen:
                    pass
            return
        while running:
            still = []
            for gen in running:
                if next(gen, StopIteration) is not StopIteration:
                    still.append(gen)
            running = still
    return wrapped


def _seq_call(kernel_fn, geo, name, in_specs, args, out_specs, out_shapes, scratch, kinds,
              stacked_prev, interleave=True):
    nc = geo.seq_len // geo.chunk
    assert geo.scan_seqs == 1 or nc == 1
    kernel_fn = _per_sequence(kernel_fn, geo, kinds, interleave)
    in_specs, args = list(in_specs), list(args)
    aliases = {}
    assert len(stacked_prev) <= 1
    for out_idx, arr in stacked_prev.items():
        kernel_fn = _drop_ref(kernel_fn, len(in_specs))
        aliases[len(in_specs)] = out_idx
        in_specs.append(pl.BlockSpec(memory_space=pl.ANY))
        args.append(arr)
    return pl.pallas_call(
        kernel_fn,
        grid=(geo.n_seq // geo.scan_seqs, nc),
        in_specs=in_specs,
        out_specs=out_specs,
        out_shape=out_shapes,
        scratch_shapes=scratch,
        input_output_aliases=aliases,
        compiler_params=_cparams(("parallel", "arbitrary")),
        name=name,
    )(*args)


def _ssd_kernel(z_ref, xbc_ref, gc_ref, gr_ref, cs_ref, s0_ref, cw_ref, cb_ref,
                dtb_r_ref, dtb_c_ref, al_r_ref, al_c_ref, dsk_ref, nw_ref, exp_ref,
                y_ref, cso_ref, so_ref, tail_scr, *, lc, lv, single_chunk):
    if single_chunk:
        s_ref = s0_ref
        cs = cs_ref[0, 0]
        tails = [cs[0:1, :], cs[1:2, :], cs[2:3, :]]
    else:
        @pl.when(pl.program_id(1) == 0)
        def _():
            tail_scr[0, 5:8, :] = cs_ref[0, 0]
            so_ref[0, 0] = s0_ref[0, 0]

        s_ref = so_ref
        tails = [tail_scr[0, 5:6, :], tail_scr[0, 6:7, :], tail_scr[0, 7:8, :]]

    x = xbc_ref[...]
    w = cw_ref[0]
    xc = (cb_ref[0] + w[0:1] * _shifted(x, tails, 3) + w[1:2] * _shifted(x, tails, 2)
          + w[2:3] * _shifted(x, tails, 1) + w[3:4] * x)
    xc = xc * _sigmoid(xc)
    new_tail = x[lv - 3:lv, :]
    if not single_chunk:
        tail_scr[0, 5:8, :] = new_tail
    cso_ref[0, 0] = new_tail
    xa = xc[:, :D_A]
    bm = xc[:, D_A:D_A + G_A * N_A]
    cm = xc[:, D_A + G_A * N_A:]

    dt_c = _softplus(gc_ref[:, 0:H_A] + dtb_r_ref[...])
    dt_r = _softplus(gr_ref[0, 0:H_A, :] + dtb_c_ref[...])
    if lv < lc:
        dt_c = jnp.where(lax.broadcasted_iota(jnp.int32, dt_c.shape, 0) < lv, dt_c, 0.0)
        dt_r = jnp.where(lax.broadcasted_iota(jnp.int32, dt_r.shape, 1) < lv, dt_r, 0.0)
    lower, upper, mask = _tri(lc)
    cum_c = jnp.dot(lower, dt_c * (-jnp.exp(al_r_ref[...])), precision=HI)
    cum_r = jnp.dot(dt_r * (-jnp.exp(al_c_ref[...])), upper, precision=HI)
    cum_last = cum_c[lc - 1:lc, :]
    expand = exp_ref[...]
    ecum_x = jnp.dot(jnp.exp(cum_c), expand, precision=HI)
    tail_x = jnp.dot(jnp.exp(cum_last - cum_c) * dt_c, expand, precision=HI)

    hg = H_A // G_A
    gw = hg * P_A
    groups = range(G_A)
    gsl = lambda g: slice(g * gw, (g + 1) * gw)
    cg = [cm[:, g * N_A:(g + 1) * N_A] for g in groups]
    bg = [bm[:, g * N_A:(g + 1) * N_A] for g in groups]
    yield
    cb_ts = [lax.dot_general(cg[g], bg[g], _NT, preferred_element_type=F32) for g in groups]
    y_inter = [lax.dot_general(cg[g], s_ref[0, 0, g], _NT, preferred_element_type=F32)
               for g in groups]
    upd = [lax.dot_general(xa[:, gsl(g)] * tail_x[:, gsl(g)], bg[g], _TN,
                           preferred_element_type=F32) for g in groups]
    yield
    w_ts = [cb_ts[h // hg] * dt_r[h:h + 1, :]
            * jnp.exp(jnp.where(mask, cum_c[:, h:h + 1] - cum_r[h:h + 1, :], -jnp.inf))
            for h in range(H_A)]
    yield
    pieces = []
    for j in range(H_A // 2):
        xp = xa[:, j * 128:(j + 1) * 128]
        lane = lax.broadcasted_iota(jnp.int32, xp.shape, 1)
        pieces.append(
            jnp.dot(w_ts[2 * j], jnp.where(lane < P_A, xp, 0.0), preferred_element_type=F32)
            + jnp.dot(w_ts[2 * j + 1], jnp.where(lane >= P_A, xp, 0.0),
                      preferred_element_type=F32))
    yield
    y = (jnp.concatenate(pieces, axis=1) + jnp.concatenate(y_inter, axis=1) * ecum_x
         + dsk_ref[...] * xa)
    for g in groups:
        for hh in range(hg):
            h = g * hg + hh
            rows = slice(hh * P_A, (hh + 1) * P_A)
            so_ref[0, 0, g, rows, :] = (s_ref[0, 0, g, rows, :] * jnp.exp(cum_r[h:h + 1, lc - 1:lc])
                                        + upd[g][rows, :])

    z = z_ref[...]
    y = y * (z * _sigmoid(z))
    y = jnp.concatenate([_rmsnorm_rows(y[:, g * gw:(g + 1) * gw]) for g in range(G_A)], axis=1)
    y_ref[...] = (y * nw_ref[...]).astype(y_ref.dtype)


def _ssd(p1, gates, gates_t, conv_state, ssd_state, prm, e, geo, prev_state_out):
    b = geo.n_seq
    const2 = lambda shape: pl.BlockSpec(shape, lambda i, c: (0, 0))
    n_even = ssd_state.shape[0]
    s5 = ssd_state.reshape(n_even, b, G_A, (H_A // G_A) * P_A, N_A)
    expand = (jnp.arange(D_A)[None, :] // P_A == jnp.arange(H_A)[:, None]).astype(F32)
    cso_shape = (1,) + conv_state.shape[1:]
    y, cso, so = _seq_call(
        functools.partial(_ssd_kernel, lc=geo.chunk, lv=geo.valid,
                          single_chunk=geo.seq_len == geo.chunk), geo, "ssd",
        in_specs=[
            _row_spec(geo, D_A, 0),
            _row_spec(geo, CONV_DIM_A, 2),
            _row_spec(geo, GATE_W, 0),
            _gates_t_spec(geo),
            _state_spec(conv_state.shape, e, geo),
            _state_spec(s5.shape, e, geo),
            pl.BlockSpec((1, CONV_K_A, CONV_DIM_A), lambda i, c: (e, 0, 0)),
            pl.BlockSpec((1, 1, CONV_DIM_A), lambda i, c: (e, 0, 0)),
            const2((1, H_A)), const2((H_A, 1)), const2((1, H_A)), const2((H_A, 1)),
            const2((1, D_A)), const2((1, D_A)), const2((H_A, D_A)),
        ],
        args=(p1, p1, gates, gates_t, conv_state, s5,
              prm['conv_w_a'], prm['conv_b_a'].reshape(n_even, 1, CONV_DIM_A),
              prm['dt_bias'][e].reshape(1, H_A), prm['dt_bias'][e].reshape(H_A, 1),
              prm['a_log'][e].reshape(1, H_A), prm['a_log'][e].reshape(H_A, 1),
              jnp.repeat(prm['d_skip'][e], P_A).reshape(1, D_A),
              prm['norm_a'][e].reshape(1, D_A), expand),
        out_specs=[_row_spec(geo, D_A, 0), _state_spec(cso_shape, 0, geo),
                   _state_spec(s5.shape, e, geo)],
        out_shapes=[
            jax.ShapeDtypeStruct((b * geo.seq_len, D_A), MXU_DTYPE),
            jax.ShapeDtypeStruct(cso_shape, F32),
            jax.ShapeDtypeStruct(s5.shape, F32),
        ],
        scratch=[pltpu.VMEM((geo.scan_seqs, SUBLANES, CONV_DIM_A), F32)],
        kinds=[ROWS, ROWS, ROWS, LEAD, STATE, STATE] + [CONST] * 9 + [ROWS, STATE, STATE, LEAD],
        stacked_prev={} if prev_state_out is None else {2: prev_state_out},
    )
    return y, cso, so


def _mlstm_kernel(q_ref, k_ref, v_ref, og_ref, gc_ref, gr_ref, c0_ref, n0_ref, m0_ref,
                  ib_r_ref, ib_c_ref, fb_r_ref, fb_c_ref, nw_ref,
                  h_ref, co_ref, no_ref, mo_ref, *, lc, lv, single_chunk):
    if single_chunk:
        c_ref, n_ref, m_ref = c0_ref, n0_ref, m0_ref
    else:
        @pl.when(pl.program_id(1) == 0)
        def _():
            co_ref[0, 0] = c0_ref[0, 0]
            no_ref[0, 0] = n0_ref[0, 0]
            mo_ref[0, 0] = m0_ref[0, 0]

        c_ref, n_ref, m_ref = co_ref, no_ref, mo_ref
    m_old, n_old = m_ref[0, 0], n_ref[0, 0]

    i0, f0 = H_A, H_A + H_B
    li_c = gc_ref[:, i0:i0 + H_B] + ib_r_ref[...]
    lf_c = _log_sigmoid(gc_ref[:, f0:f0 + H_B] + fb_r_ref[...])
    li_r = gr_ref[0, i0:i0 + H_B, :] + ib_c_ref[...]
    lf_r = _log_sigmoid(gr_ref[0, f0:f0 + H_B, :] + fb_c_ref[...])
    if lv < lc:
        vc = lax.broadcasted_iota(jnp.int32, li_c.shape, 0) < lv
        vr = lax.broadcasted_iota(jnp.int32, li_r.shape, 1) < lv
        li_c, lf_c = jnp.where(vc, li_c, NEG_BIG), jnp.where(vc, lf_c, 0.0)
        li_r, lf_r = jnp.where(vr, li_r, NEG_BIG), jnp.where(vr, lf_r, 0.0)
    lower, upper, mask = _tri(lc)
    bc_c = jnp.dot(lower, lf_c, precision=HI)
    bc_r = jnp.dot(lf_r, upper, precision=HI)

    heads = range(H_B)
    q = [q_ref[:, h * DK_B:(h + 1) * DK_B] * (DK_B ** -0.5) for h in heads]
    k = [k_ref[:, h * DK_B:(h + 1) * DK_B] for h in heads]
    v = [v_ref[:, h * DV_B:(h + 1) * DV_B] for h in heads]
    yield
    qk = [lax.dot_general(q[h], k[h], _NT, preferred_element_type=F32) for h in heads]
    q_c = [jnp.dot(q[h], c_ref[0, 0, h], preferred_element_type=F32) for h in heads]
    yield
    m_t, w_in, w_ts = [], [], []
    for h in heads:
        bcc = bc_c[:, h:h + 1]
        dmat = jnp.where(mask, bcc - bc_r[h:h + 1, :] + li_r[h:h + 1, :], -jnp.inf)
        inter = bcc + m_old[:, h:h + 1]
        m_t.append(jnp.maximum(inter, jnp.max(dmat, axis=1, keepdims=True)))
        w_in.append(jnp.exp(inter - m_t[h]))
        w_ts.append(jnp.exp(dmat - m_t[h]) * qk[h])
    yield
    wv = [jnp.dot(w_ts[h], v[h], preferred_element_type=F32) for h in heads]
    yield
    hs = []
    for h in heads:
        num = wv[h] + w_in[h] * q_c[h]
        den = (jnp.sum(w_ts[h], axis=1, keepdims=True)
               + w_in[h] * jnp.sum(q[h] * n_old[h:h + 1, :], axis=1, keepdims=True))
        hs.append(_rmsnorm_rows(num / jnp.maximum(jnp.abs(den), jnp.exp(-m_t[h]))))
    lane_h = lax.broadcasted_iota(jnp.int32, (1, H_B), 1)
    m_out = jnp.zeros((1, H_B), F32)
    ks, w_c = [], []
    for h in heads:
        bcc = bc_c[:, h:h + 1]
        m_new = m_t[h][lv - 1:lv, :]
        bc_last = bcc[lc - 1:lc, :]
        ks.append(k[h] * jnp.exp(bc_last - bcc + li_c[:, h:h + 1] - m_new))
        w_c.append(jnp.exp(bc_last + m_old[:, h:h + 1] - m_new))
        m_out = jnp.where(lane_h == h, m_new, m_out)
    yield
    kv = [lax.dot_general(ks[h], v[h], _TN, preferred_element_type=F32) for h in heads]
    yield
    for h in heads:
        co_ref[0, 0, h] = w_c[h] * c_ref[0, 0, h] + kv[h]
        no_ref[0, 0, h:h + 1, :] = w_c[h] * n_old[h:h + 1, :] + jnp.sum(ks[h], axis=0, keepdims=True)
    mo_ref[0, 0] = m_out
    hn = jnp.concatenate(hs, axis=1) * nw_ref[...]
    h_ref[...] = (hn * _sigmoid(og_ref[...])).astype(h_ref.dtype)


def _mlstm(p1, gates, gates_t, c_state, n_state, m_state, prm, e, geo, prev_state_out):
    b = geo.n_seq
    const2 = lambda shape: pl.BlockSpec(shape, lambda i, c: (0, 0))
    qk_w = H_B * DK_B
    m4 = m_state.reshape(m_state.shape[0], b, 1, H_B)
    one = lambda shape: (1,) + tuple(shape[1:])
    h, co, no, mo = _seq_call(
        functools.partial(_mlstm_kernel, lc=geo.chunk, lv=geo.valid,
                          single_chunk=geo.seq_len == geo.chunk), geo, "mlstm",
        in_specs=[
            _row_spec(geo, qk_w, 9), _row_spec(geo, qk_w, 10),
            _row_spec(geo, D_B, 1), _row_spec(geo, D_B, 2),
            _row_spec(geo, GATE_W, 0),
            _gates_t_spec(geo),
            _state_spec(c_state.shape, e, geo), _state_spec(n_state.shape, e, geo),
            _state_spec(m4.shape, e, geo),
            const2((1, H_B)), const2((H_B, 1)), const2((1, H_B)), const2((H_B, 1)),
            const2((1, D_B)),
        ],
        args=(p1, p1, p1, p1, gates, gates_t, c_state, n_state, m4,
              prm['i_bias'][e].reshape(1, H_B), prm['i_bias'][e].reshape(H_B, 1),
              prm['f_bias'][e].reshape(1, H_B), prm['f_bias'][e].reshape(H_B, 1),
              prm['norm_b'][e].reshape(1, D_B)),
        out_specs=[_row_spec(geo, D_B, 0), _state_spec(c_state.shape, e, geo),
                   _state_spec(one(n_state.shape), 0, geo), _state_spec(one(m4.shape), 0, geo)],
        out_shapes=[
            jax.ShapeDtypeStruct((b * geo.seq_len, D_B), MXU_DTYPE),
            jax.ShapeDtypeStruct(c_state.shape, F32),
            jax.ShapeDtypeStruct(one(n_state.shape), F32),
            jax.ShapeDtypeStruct(one(m4.shape), F32),
        ],
        scratch=[],
        kinds=[ROWS] * 5 + [LEAD] + [STATE] * 3 + [CONST] * 5 + [ROWS] + [STATE] * 3,
        stacked_prev={} if prev_state_out is None else {1: prev_state_out},
        interleave=False,
    )
    return h, co, no, mo.reshape(1, b, H_B)


def _hgrn_kernel(q_ref, f_ref, i_ref, g_ref, lbl_ref, s0_ref, nw_ref, o_ref, so_ref,
                 *, lc, lv, layer_o, bs, single_chunk):
    if single_chunk:
        s_ref = s0_ref
    else:
        @pl.when(pl.program_id(1) == 0)
        def _():
            so_ref[0, 0] = s0_ref[0, 0]

        s_ref = so_ref

    lbl = lbl_ref[...]
    ex = jnp.exp(lbl - jnp.max(lbl, axis=0, keepdims=True))
    sm = ex / jnp.sum(ex, axis=0, keepdims=True)
    lb_all = [sm[0:1, :]]
    for r in range(1, lbl.shape[0]):
        lb_all.append(lb_all[-1] + sm[r:r + 1, :])
    lb = lb_all[layer_o] - lb_all[0]

    fx = f_ref[...]
    e1 = jnp.exp(-jnp.abs(fx))
    log_sig = jnp.minimum(fx, 0.0) - jnp.log1p(e1)
    la = jnp.log(lb)
    lb_ = jnp.log1p(-lb) + log_sig
    logf = jnp.maximum(la, lb_) + jnp.log1p(jnp.exp(-jnp.abs(la - lb_)))
    kk = (1.0 - lb) * (jnp.where(fx >= 0.0, e1, 1.0) / (1.0 + e1))
    if lv < lc:
        valid = lax.broadcasted_iota(jnp.int32, fx.shape, 0) < lv
        logf = jnp.where(valid, logf, 0.0)
        kk = jnp.where(valid, kk, 0.0)

    nb = lc // bs
    r_i = lax.broadcasted_iota(jnp.int32, (lc, lc), 0)
    c_i = lax.broadcasted_iota(jnp.int32, (lc, lc), 1)
    sh = int(math.log2(bs))
    blk_lower = ((c_i <= r_i) & ((c_i >> sh) == (r_i >> sh))).astype(F32)
    gw = jnp.dot(blk_lower, logf, precision=HI)
    q = q_ref[...]
    v = i_ref[...]
    blk = lambda a, i: a[i * bs:(i + 1) * bs, :]
    tots = [gw[(i + 1) * bs - 1:(i + 1) * bs, :] for i in range(nb)]
    before = [jnp.zeros_like(tots[0])]
    for i in range(nb):
        before.append(before[-1] + tots[i])
    g_tot = before[nb]
    qt = q * jnp.exp(gw)
    kt = [blk(kk, j) * jnp.exp(tots[j] - blk(gw, j)) for j in range(nb)]
    q_in = jnp.concatenate([blk(qt, i) * jnp.exp(before[i]) for i in range(nb)], axis=0)
    k_out = jnp.concatenate([kt[j] * jnp.exp(g_tot - before[j + 1]) for j in range(nb)], axis=0)

    heads = range(H_C)
    hsl = lambda h: slice(h * DK_C, (h + 1) * DK_C)
    yield
    o_inter = [jnp.dot(q_in[:, hsl(h)], s_ref[0, 0, h], preferred_element_type=F32)
               for h in heads]
    kv = [lax.dot_general(k_out[:, hsl(h)], v[:, hsl(h)], _TN, preferred_element_type=F32)
          for h in heads]
    att_off = [None]
    for i in range(1, nb):
        k_hat = jnp.concatenate(
            [kt[j] if j == i - 1 else kt[j] * jnp.exp(before[i] - before[j + 1])
             for j in range(i)] + [jnp.zeros((lc - i * bs, D_C), F32)], axis=0)
        q_ti = blk(qt, i)
        att_off.append([lax.dot_general(q_ti[:, hsl(h)], k_hat[:, hsl(h)], _NT,
                                        preferred_element_type=F32) for h in heads])
    yield
    gw2 = gw * LOG2E
    ck = jnp.log2(kk) - gw2
    n_t = bs // SUBLANES
    lane_s = lax.broadcasted_iota(jnp.int32, (SUBLANES, lc), 1)
    row_t = lax.broadcasted_iota(jnp.int32, (bs, lc), 0)
    col_s = lax.broadcasted_iota(jnp.int32, (bs, lc), 1)
    att = []
    for i in range(nb):
        g_i, q_i, ck_i = blk(gw2, i), blk(q, i), blk(ck, i)
        att_d = [[jnp.zeros((SUBLANES, lc), F32) for _ in range(n_t)] for _ in heads]
        for s in range(bs):
            t0 = s // SUBLANES
            p = q_i[t0 * SUBLANES:, :] * jnp.exp2(g_i[t0 * SUBLANES:, :] + ck_i[s:s + 1, :])
            for h in heads:
                a = jnp.sum(p[:, hsl(h)], axis=1, keepdims=True)
                for tt in range(t0, n_t):
                    a_t = a[(tt - t0) * SUBLANES:(tt - t0 + 1) * SUBLANES, :]
                    att_d[h][tt] = jnp.where(lane_s == i * bs + s, a_t, att_d[h][tt])
        causal = (col_s - i * bs) <= row_t
        att_i = []
        for h in heads:
            a = att_d[h][0] if n_t == 1 else jnp.concatenate(att_d[h], axis=0)
            a = jnp.where(causal, a, 0.0)
            att_i.append(a if i == 0 else a + att_off[i][h])
        att.append(att_i)
    yield
    o_intra = [[jnp.dot(att[i][h], v[:, hsl(h)], preferred_element_type=F32) for h in heads]
               for i in range(nb)]
    yield
    outs = []
    for h in heads:
        o_h = o_intra[0][h] if nb == 1 else jnp.concatenate([o_intra[i][h] for i in range(nb)],
                                                             axis=0)
        outs.append(_rmsnorm_rows(o_h + o_inter[h]))
        dec_col = jnp.transpose(jnp.broadcast_to(jnp.exp(g_tot[:, hsl(h)]), (DK_C, DK_C)))
        so_ref[0, 0, h] = dec_col * s_ref[0, 0, h] + kv[h]
    gate = g_ref[...]
    o_ref[...] = (jnp.concatenate(outs, axis=1) * nw_ref[...]
                  * (gate * _sigmoid(gate))).astype(o_ref.dtype)


def _hgrn(p, state, prm, o, geo, prev_state_out):
    n_odd = prm['lb_logits'].shape[0]
    return _seq_call(
        functools.partial(_hgrn_kernel, lc=geo.chunk, lv=geo.valid, layer_o=o,
                          bs=min(16, geo.chunk), single_chunk=geo.seq_len == geo.chunk), geo, "hgrn",
        in_specs=[
            _row_spec(geo, D_C, 0), _row_spec(geo, D_C, 1), _row_spec(geo, D_C, 2),
            _row_spec(geo, D_C, 3),
            pl.BlockSpec((n_odd, D_C), lambda i, c: (0, 0)),
            _state_spec(state.shape, o, geo),
            pl.BlockSpec((1, D_C), lambda i, c: (0, 0)),
        ],
        args=(p, p, p, p, prm['lb_logits'], state, prm['norm_c'][o].reshape(1, D_C)),
        out_specs=[_row_spec(geo, D_C, 0), _state_spec(state.shape, o, geo)],
        out_shapes=[
            jax.ShapeDtypeStruct((geo.n_seq * geo.seq_len, D_C), MXU_DTYPE),
            jax.ShapeDtypeStruct(state.shape, F32),
        ],
        scratch=[],
        kinds=[ROWS] * 4 + [CONST, STATE, CONST, ROWS, STATE],
        stacked_prev={} if prev_state_out is None else {1: prev_state_out},
    )


def _prep_weights(prm):
    a0, a1, a2 = D_A, D_A + CONV_DIM_A, D_A + CONV_DIM_A + H_A
    q1 = a2 + H_B * DK_B
    k1 = q1 + H_B * DK_B
    v1 = k1 + D_B
    o1 = v1 + D_B
    n_gate = H_A + 2 * H_B
    wab = prm['w_in_ab']
    in_ab = jnp.concatenate(
        [wab[:, :, :a0], wab[:, :, k1:v1], wab[:, :, v1:o1], wab[:, :, a0:a1],
         wab[:, :, a2:q1], wab[:, :, q1:k1]], axis=2).astype(MXU_DTYPE)
    gates_ab = jnp.pad(jnp.concatenate([wab[:, :, a1:a2], wab[:, :, o1:]], axis=2),
                       ((0, 0), (0, 0), (0, GATE_W - n_gate))).astype(MXU_DTYPE)
    n_even, n_odd = wab.shape[0], prm['w_in_c'].shape[0]
    return {
        'in_ab': [[(in_ab, e, 0)] for e in range(n_even)],
        'gates_ab': [[(gates_ab, e, 0)] for e in range(n_even)],
        'out_ab': [[(prm['w_out_ab'], e, 0), (prm['w_out_ab'], e, 1)] for e in range(n_even)],
        'in_c': [[(prm['w_in_c'], o, 0)] for o in range(n_odd)],
        'out_c': [[(prm['w_out_c'], o, 0)] for o in range(n_odd)],
        'ffn_g': [[(prm['w_ffn_g'], l, 0)] for l in range(DEPTH)],
        'ffn_u': [[(prm['w_ffn_u'], l, 0)] for l in range(DEPTH)],
        'ffn_d': [[(prm['w_ffn_d'], l, 0)] for l in range(DEPTH)],
    }


def _trunk(x3, mod, states, prm, w, geo, tn_in, tn_res, tn_ffn, tn_down, emit):
    conv_a, ssd, mem_c, mem_n, mem_m, hgrn, ffn_buf = states
    n_conv, n_n, n_m, n_ffn = [], [], [], []
    new_ssd = new_c = new_hgrn = None
    nc = geo.seq_len // geo.chunk
    wq = {name: list(per_layer) for name, per_layer in w.items()}

    def record(name, idx, emitted):
        if emitted is not None:
            wq[name][idx] = [(arr, 0, 0) for arr in emitted]

    casts = lambda wops: emit and any(wop[0].dtype != MXU_DTYPE for wop in wops)
    for layer in range(DEPTH):
        mm_norm = functools.partial(_mm_norm, x3, prm['norm_mix'][layer], mod, layer, 1, 0,
                                    geo=geo)
        if layer % 2 == 0:
            e = layer // 2
            p1, _ = mm_norm(w['in_ab'][e][0], tn=tn_in[layer])
            gates, _ = mm_norm(w['gates_ab'][e][0], tn=GATE_W)
            gates_t = gates.reshape(geo.n_seq * nc, geo.chunk, GATE_W).transpose(0, 2, 1)
            ya, cv, new_ssd = _ssd(p1, gates, gates_t, conv_a, ssd, prm, e, geo, new_ssd)
            hb, new_c, nn, mmm = _mlstm(p1, gates, gates_t, mem_c, mem_n, mem_m, prm, e, geo, new_c)
            n_conv.append(cv); n_n.append(nn); n_m.append(mmm)
            x3, em = _mm_res([ya, hb], w['out_ab'][e], x3, mod, layer, 2, geo, tn_res[layer],
                             emit=casts(w['out_ab'][e]))
            record('out_ab', e, em)
        else:
            o = layer // 2
            p, em = mm_norm(w['in_c'][o][0], tn=tn_in[layer], emit=casts(w['in_c'][o]))
            record('in_c', o, None if em is None else [em])
            oc, new_hgrn = _hgrn(p, hgrn, prm, o, geo, new_hgrn)
            x3, em = _mm_res([oc], w['out_c'][o], x3, mod, layer, 2, geo, tn_res[layer],
                             emit=casts(w['out_c'][o]))
            record('out_c', o, em)
        act, fb, em = _ffn_in(x3, prm['norm_ffn'][layer], mod, layer, w['ffn_g'][layer][0],
                              w['ffn_u'][layer][0], ffn_buf, prm['conv_w_f'][layer],
                              prm['conv_b_f'][layer], geo, tn_ffn,
                              emit=casts(w['ffn_g'][layer] + w['ffn_u'][layer]))
        if em is not None:
            record('ffn_g', layer, em[:1])
            record('ffn_u', layer, em[1:])
        n_ffn.append(fb)
        x3, em = _mm_res([act], w['ffn_d'][layer], x3, mod, layer, 5, geo, tn_down,
                         emit=casts(w['ffn_d'][layer]))
        record('ffn_d', layer, em)
    y = _final_norm(x3, prm['norm_f'], geo)
    cat = lambda xs: jnp.concatenate(xs, axis=0)
    return (y, cat(n_conv), new_ssd.reshape(ssd.shape), new_c, cat(n_n), cat(n_m), new_hgrn,
            cat(n_ffn)), wq


def kernel(x_prompt, x_sample, c_prompt, c_sample, state_ssd_conv, state_ssd, state_mlstm_c, state_mlstm_n, state_mlstm_m, state_hgrn, state_ffn_conv, w_ada, b_ada, norm_mix, norm_ffn, w_in_ab, conv_w_a, conv_b_a, dt_bias, a_log, d_skip, norm_a, i_bias, f_bias, norm_b, w_out_ab, w_in_c, lb_logits, norm_c, w_out_c, w_ffn_g, w_ffn_u, conv_w_f, conv_b_f, w_ffn_d, norm_f):
    prm = dict(norm_mix=norm_mix, norm_ffn=norm_ffn, w_in_ab=w_in_ab, conv_w_a=conv_w_a,
               conv_b_a=conv_b_a, dt_bias=dt_bias, a_log=a_log, d_skip=d_skip, norm_a=norm_a,
               i_bias=i_bias, f_bias=f_bias, norm_b=norm_b, w_out_ab=w_out_ab, w_in_c=w_in_c,
               lb_logits=lb_logits, norm_c=norm_c, w_out_c=w_out_c, w_ffn_g=w_ffn_g,
               w_ffn_u=w_ffn_u, conv_w_f=conv_w_f, conv_b_f=conv_b_f, w_ffn_d=w_ffn_d,
               norm_f=norm_f)
    bp, lp, _ = x_prompt.shape
    bs, ls, _ = x_sample.shape
    n_even, n_odd = state_ssd.shape[0], state_hgrn.shape[0]
    w = _prep_weights(prm)

    n_c = bs + bp
    n_c_pad = -(-n_c // SUBLANES) * SUBLANES
    c_all = jnp.pad(jnp.concatenate([c_sample, c_prompt], axis=0), ((0, n_c_pad - n_c), (0, 0)))
    mod = _ada(c_all, w_ada, b_ada).reshape(DEPTH, n_c_pad, 1, 6 * D_MODEL)

    geo_s = Geo(n_seq=bs, seq_len=SUBLANES, seq_blk=bs, row_blk=SUBLANES, chunk=SUBLANES,
                valid=ls, mod_off=0, scan_seqs=4)
    xs = jnp.pad(x_sample, ((0, 0), (0, SUBLANES - ls), (0, 0)))
    st_s = (state_ssd_conv, state_ssd, state_mlstm_c, state_mlstm_n, state_mlstm_m, state_hgrn,
            state_ffn_conv)
    out_s, w_bf16 = _trunk(xs, mod, st_s, prm, w, geo_s, tn_in=(512,) * DEPTH,
                           tn_res=(512,) * DEPTH, tn_ffn=256, tn_down=256, emit=True)

    zeros = lambda *s: jnp.zeros(s, F32)
    st_p = (zeros(n_even, bp, CONV_K_A - 1, CONV_DIM_A), zeros(n_even, bp, H_A, P_A, N_A),
            zeros(n_even, bp, H_B, DK_B, DV_B), zeros(n_even, bp, H_B, DK_B),
            zeros(n_even, bp, H_B), zeros(n_odd, bp, H_C, DK_C, DV_C),
            zeros(DEPTH, bp, CONV_K_F - 1, D_FF))
    lc_p = math.gcd(lp, PROMPT_CHUNK)
    geo_p = Geo(n_seq=bp, seq_len=lp, seq_blk=1, row_blk=min(lp, 1024), chunk=lc_p, valid=lc_p,
                mod_off=bs, scan_seqs=1)
    out_p, _ = _trunk(x_prompt, mod, st_p, prm, w_bf16, geo_p, tn_in=(1024,) * DEPTH,
                      tn_res=(1024,) * DEPTH, tn_ffn=FFN_TN, tn_down=FFN_DOWN_TN, emit=False)
    return (out_p[0], out_s[0][:, :ls]) + tuple(out_p[1:]) + tuple(out_s[1:])
```

```python
import collections
import functools
import math

import jax
import jax.numpy as jnp
from jax import lax
from jax.experimental import pallas as pl
from jax.experimental.pallas import tpu as pltpu

F32 = jnp.float32
MXU_DTYPE = jnp.bfloat16
HI = lax.Precision.HIGHEST
NEG_BIG = -1e30
LOG2E = 1.4426950408889634

D_MODEL = 2048
DEPTH = 4
EPS = 1e-6
PROMPT_CHUNK = 64
H_A, P_A, G_A, N_A, CONV_K_A = 32, 64, 4, 128, 4
D_A = H_A * P_A
CONV_DIM_A = D_A + 2 * G_A * N_A
H_B, DK_B, DV_B = 8, 128, 256
D_B = H_B * DV_B
H_C, DK_C, DV_C = 16, 128, 128
D_C = H_C * DV_C
D_FF, CONV_K_F = 5632, 3
FFN_TN = 512
FFN_DOWN_TN = 512
PROLOGUE_SPLIT = 4
GATE_W = 128
SUBLANES = 8
VMEM_LIMIT = 56 * 1024 * 1024

_NT = (((1,), (1,)), ((), ()))
_TN = (((0,), (0,)), ((), ()))

Geo = collections.namedtuple(
    "Geo", "n_seq seq_len seq_blk row_blk chunk valid mod_off scan_seqs hgrn_seqs")


def _cparams(sem, flags=None):
    return pltpu.CompilerParams(dimension_semantics=sem, vmem_limit_bytes=VMEM_LIMIT, flags=flags)


def _sigmoid(x):
    return 1.0 / (1.0 + jnp.exp(-x))


def _softplus(x):
    return jnp.maximum(x, 0.0) + jnp.log1p(jnp.exp(-jnp.abs(x)))


def _log_sigmoid(x):
    return jnp.minimum(x, 0.0) - jnp.log1p(jnp.exp(-jnp.abs(x)))


def _tri(n):
    r = lax.broadcasted_iota(jnp.int32, (n, n), 0)
    c = lax.broadcasted_iota(jnp.int32, (n, n), 1)
    mask = c <= r
    return mask.astype(F32), (r <= c).astype(F32), mask


def _shifted(x, tails, k, axis=0):
    row = lax.broadcasted_iota(jnp.int32, x.shape, axis)
    out = pltpu.roll(x, k, axis)
    for r in range(k):
        out = jnp.where(row == r, tails[len(tails) - k + r], out)
    return out


def _drop_ref(fn, idx):
    def wrapped(*refs):
        return fn(*refs[:idx], *refs[idx + 1:])
    return wrapped


def _rmsnorm_rows(x):
    return x * lax.rsqrt(jnp.mean(x * x, axis=-1, keepdims=True) + EPS)


def _ada_kernel(c_ref, w_ref, b_ref, o_ref):
    c = c_ref[...]
    ca = (c * _sigmoid(c)).astype(MXU_DTYPE)
    o_ref[0] = jnp.dot(ca, w_ref[0].astype(MXU_DTYPE), preferred_element_type=F32) + b_ref[0]


def _ada(c_all, w_ada, b_ada, tn=1024):
    rows = c_all.shape[0]
    n = w_ada.shape[2]
    return pl.pallas_call(
        _ada_kernel,
        grid=(DEPTH, n // tn),
        in_specs=[
            pl.BlockSpec((rows, D_MODEL), lambda l, j: (0, 0)),
            pl.BlockSpec((1, D_MODEL, tn), lambda l, j: (l, 0, j)),
            pl.BlockSpec((1, 1, tn), lambda l, j: (l, 0, j)),
        ],
        out_specs=pl.BlockSpec((1, rows, tn), lambda l, j: (l, 0, j)),
        out_shape=jax.ShapeDtypeStruct((DEPTH, rows, n), F32),
        compiler_params=_cparams(("parallel", "parallel")),
        name="ada",
    )(c_all, w_ada, b_ada.reshape(DEPTH, 1, n))


def _tiles(geo):
    nrt = geo.seq_len // geo.row_blk
    return nrt, (geo.n_seq // geo.seq_blk) * nrt, geo.seq_blk * geo.row_blk


def _x_spec(geo, width, col_of):
    nrt = geo.seq_len // geo.row_blk
    return pl.BlockSpec((geo.seq_blk, geo.row_blk, width),
                        lambda i, j: (i // nrt, i % nrt, col_of(j)))


def _mod_spec(geo, layer, width, col_of):
    nrt = geo.seq_len // geo.row_blk
    return pl.BlockSpec((1, geo.seq_blk, 1, width),
                        lambda i, j: (layer, geo.mod_off + i // nrt, 0, col_of(j)))


def _norm_mod_chunks(x_ref, nw_ref, sc_ref, sh_ref):
    seq_blk, row_blk, d = x_ref.shape
    for r in range(PROLOGUE_SPLIT):
        if seq_blk == 1:
            n = row_blk // PROLOGUE_SPLIT
            x, sc, sh = x_ref[:, r * n:(r + 1) * n], sc_ref[0], sh_ref[0]
        else:
            n = seq_blk // PROLOGUE_SPLIT
            x, sc, sh = x_ref[r * n:(r + 1) * n], sc_ref[0, r * n:(r + 1) * n], sh_ref[0, r * n:(r + 1) * n]
        h = (_rmsnorm_rows(x) * nw_ref[...]) * (1.0 + sc) + sh
        rows = h.shape[0] * h.shape[1]
        yield slice(r * rows, (r + 1) * rows), h.reshape(rows, d).astype(MXU_DTYPE)


def _weight_spec(wop, k, tn):
    _, layer, row_block = wop
    return pl.BlockSpec((1, k, tn), lambda i, j: (layer, row_block, j))


def _emit_spec_shape(k, n, tn):
    return (pl.BlockSpec((1, k, tn), lambda i, j: (0, 0, j)),
            jax.ShapeDtypeStruct((1, k, n), MXU_DTYPE))


def _weight_tile(w_ref, wq_ref):
    w = w_ref[0].astype(MXU_DTYPE)
    if wq_ref is not None:
        wq_ref[0] = w
    return w


def _mm_norm_kernel(*refs, emit, narrow):
    x_ref, nw_ref, sc_ref, sh_ref, w_ref = refs[:5]
    rest = list(refs[5:])
    wn_ref = rest.pop(0) if narrow else None
    o_ref = rest.pop(0)
    on_ref = rest.pop(0) if narrow else None
    wq_ref = rest.pop(0) if emit else None
    (h_scr,) = rest
    w = _weight_tile(w_ref, wq_ref)

    @pl.when(pl.program_id(1) == 0)
    def _():
        for rows, h in _norm_mod_chunks(x_ref, nw_ref, sc_ref, sh_ref):
            h_scr[rows, :] = h
            o_ref[rows, :] = jnp.dot(h, w, preferred_element_type=F32)
            if narrow:
                on_ref[rows, :] = jnp.dot(h, wn_ref[0], preferred_element_type=F32)

    @pl.when(pl.program_id(1) > 0)
    def _():
        o_ref[...] = jnp.dot(h_scr[...], w, preferred_element_type=F32)


def _mm_norm(x3, nw, mod, layer, k_sc, k_sh, wop, geo, tn, emit=False, narrow=None):
    _, n_i, tm = _tiles(geo)
    n = wop[0].shape[2]
    in_specs = [
        _x_spec(geo, D_MODEL, lambda j: 0),
        pl.BlockSpec((1, D_MODEL), lambda i, j: (0, 0)),
        _mod_spec(geo, layer, D_MODEL, lambda j: k_sc),
        _mod_spec(geo, layer, D_MODEL, lambda j: k_sh),
        _weight_spec(wop, D_MODEL, tn),
    ]
    args = [x3, nw.reshape(1, D_MODEL), mod, mod, wop[0]]
    out_specs = [pl.BlockSpec((tm, tn), lambda i, j: (i, j))]
    out_shape = [jax.ShapeDtypeStruct((geo.n_seq * geo.seq_len, n), F32)]
    if narrow is not None:
        n_narrow = narrow[0].shape[2]
        in_specs.append(pl.BlockSpec((1, D_MODEL, n_narrow), lambda i, j: (narrow[1], 0, 0)))
        args.append(narrow[0])
        out_specs.append(pl.BlockSpec((tm, n_narrow), lambda i, j: (i, 0)))
        out_shape.append(jax.ShapeDtypeStruct((geo.n_seq * geo.seq_len, n_narrow), F32))
    if emit:
        assert n_i == 1
        spec, shape = _emit_spec_shape(D_MODEL, n, tn)
        out_specs.append(spec)
        out_shape.append(shape)
    outs = list(pl.pallas_call(
        functools.partial(_mm_norm_kernel, emit=emit, narrow=narrow is not None),
        grid=(n_i, n // tn),
        in_specs=in_specs,
        out_specs=out_specs,
        out_shape=out_shape,
        scratch_shapes=[pltpu.VMEM((tm, D_MODEL), MXU_DTYPE)],
        compiler_params=_cparams(("parallel", "arbitrary")),
        name="mm_norm",
    )(*args))
    out = outs.pop(0)
    out_narrow = outs.pop(0) if narrow is not None else None
    return out, out_narrow, (outs.pop(0) if emit else None)


def _mm_res_kernel(*refs, n_lhs, emit):
    a_refs, w_refs = refs[:n_lhs], refs[n_lhs:2 * n_lhs]
    xres_ref, gate_ref, o_ref = refs[2 * n_lhs:2 * n_lhs + 3]
    wq_refs = refs[2 * n_lhs + 3:] if emit else (None,) * n_lhs
    acc = None
    for a_ref, w_ref, wq_ref in zip(a_refs, w_refs, wq_refs):
        d = jnp.dot(a_ref[...], _weight_tile(w_ref, wq_ref), preferred_element_type=F32)
        acc = d if acc is None else acc + d
    o_ref[...] = xres_ref[...] + gate_ref[0] * acc.reshape(o_ref.shape)


def _mm_res(a_list, wops, x3, mod, layer, k_gate, geo, tn, emit=False):
    _, n_i, tm = _tiles(geo)
    per = D_MODEL // tn
    in_specs = [pl.BlockSpec((tm, a.shape[1]), lambda i, j: (i, 0)) for a in a_list]
    in_specs += [_weight_spec(wop, a.shape[1], tn) for wop, a in zip(wops, a_list)]
    in_specs += [_x_spec(geo, tn, lambda j: j),
                 _mod_spec(geo, layer, tn, lambda j: k_gate * per + j)]
    out_specs = [_x_spec(geo, tn, lambda j: j)]
    out_shape = [jax.ShapeDtypeStruct(x3.shape, F32)]
    if emit:
        assert n_i == 1
        for a in a_list:
            spec, shape = _emit_spec_shape(a.shape[1], D_MODEL, tn)
            out_specs.append(spec)
            out_shape.append(shape)
    outs = pl.pallas_call(
        functools.partial(_mm_res_kernel, n_lhs=len(a_list), emit=emit),
        grid=(n_i, per),
        in_specs=in_specs,
        out_specs=out_specs,
        out_shape=out_shape,
        compiler_params=_cparams(("parallel", "parallel")),
        name="mm_res",
    )(*a_list, *[wop[0] for wop in wops], x3, mod)
    return outs[0], (list(outs[1:]) if emit else None)


def _ffn_in_kernel(x_ref, nw_ref, sc_ref, sh_ref, wg_ref, wu_ref, cs_ref, cw_ref, cb_ref,
                   a_ref, cso_ref, *rest, nrt, valid, emit):
    wgq_ref, wuq_ref, h_scr, tail_scr = rest if emit else (None, None) + rest
    i, j = pl.program_id(0), pl.program_id(1)

    seq_blk, row_blk, tn = x_ref.shape[0], x_ref.shape[1], a_ref.shape[1]
    w_g, w_u = _weight_tile(wg_ref, wgq_ref), _weight_tile(wu_ref, wuq_ref)

    def conv_gate(g, u):
        g, u = g.reshape(seq_blk, row_blk, tn), u.reshape(seq_blk, row_blk, tn)
        prev = cs_ref[0]
        if nrt > 1:
            prev = jnp.where(i % nrt == 0, prev, tail_scr[j])
        t2, t1 = prev[:, 0:1, :], prev[:, 1:2, :]
        w = cw_ref[...]
        y = (cb_ref[...] + w[0:1] * _shifted(g, [t2, t1], 2, axis=1)
             + w[1:2] * _shifted(g, [t2, t1], 1, axis=1) + w[2:3] * g)
        a_ref[...] = (y * _sigmoid(y) * u).reshape(a_ref.shape).astype(a_ref.dtype)
        new_tail = g[:, valid - 2:valid, :]
        if nrt > 1:
            tail_scr[j] = new_tail
        cso_ref[0] = new_tail

    @pl.when(j == 0)
    def _():
        gs, us = [], []
        for rows, h in _norm_mod_chunks(x_ref, nw_ref, sc_ref, sh_ref):
            h_scr[rows, :] = h
            gs.append(jnp.dot(h, w_g, preferred_element_type=F32))
            us.append(jnp.dot(h, w_u, preferred_element_type=F32))
        conv_gate(jnp.concatenate(gs, axis=0), jnp.concatenate(us, axis=0))

    @pl.when(j > 0)
    def _():
        h = h_scr[...]
        conv_gate(jnp.dot(h, w_g, preferred_element_type=F32),
                  jnp.dot(h, w_u, preferred_element_type=F32))


def _ffn_in(x3, nw, mod, layer, wop_g, wop_u, conv_state, conv_w, conv_b, geo, tn, emit=False):
    nrt, n_i, tm = _tiles(geo)
    n_j = D_FF // tn
    valid = geo.row_blk if geo.valid == geo.chunk else geo.valid
    tail_shape = (n_j, geo.seq_blk, CONV_K_F - 1, tn) if nrt > 1 else (1, 1, CONV_K_F - 1, 128)
    cs_spec = pl.BlockSpec((1, geo.seq_blk, CONV_K_F - 1, tn), lambda i, j: (layer, i // nrt, 0, j))
    emit_specs, emit_shapes = [], []
    if emit:
        assert n_i == 1
        for _ in range(2):
            spec, shape = _emit_spec_shape(D_MODEL, D_FF, tn)
            emit_specs.append(spec)
            emit_shapes.append(shape)
    act, tails, *emitted = pl.pallas_call(
        functools.partial(_ffn_in_kernel, nrt=nrt, valid=valid, emit=emit),
        grid=(n_i, n_j),
        in_specs=[
            _x_spec(geo, D_MODEL, lambda j: 0),
            pl.BlockSpec((1, D_MODEL), lambda i, j: (0, 0)),
            _mod_spec(geo, layer, D_MODEL, lambda j: 4),
            _mod_spec(geo, layer, D_MODEL, lambda j: 3),
            _weight_spec(wop_g, D_MODEL, tn),
            _weight_spec(wop_u, D_MODEL, tn),
            cs_spec,
            pl.BlockSpec((CONV_K_F, tn), lambda i, j: (0, j)),
            pl.BlockSpec((1, tn), lambda i, j: (0, j)),
        ],
        out_specs=[
            pl.BlockSpec((tm, tn), lambda i, j: (i, j)),
            pl.BlockSpec((1, geo.seq_blk, CONV_K_F - 1, tn), lambda i, j: (i, 0, 0, j)),
        ] + emit_specs,
        out_shape=[
            jax.ShapeDtypeStruct((geo.n_seq * geo.seq_len, D_FF), MXU_DTYPE),
            jax.ShapeDtypeStruct((n_i, geo.seq_blk, CONV_K_F - 1, D_FF), F32),
        ] + emit_shapes,
        scratch_shapes=[pltpu.VMEM((tm, D_MODEL), MXU_DTYPE),
                        pltpu.VMEM(tail_shape, F32)],
        compiler_params=_cparams(("arbitrary", "arbitrary")),
        name="ffn_in",
    )(x3, nw.reshape(1, D_MODEL), mod, mod, wop_g[0], wop_u[0], conv_state, conv_w,
      conv_b.reshape(1, D_FF))
    last = tails.reshape(n_i // nrt, nrt, geo.seq_blk, CONV_K_F - 1, D_FF)[:, nrt - 1]
    return act, last.reshape(1, geo.n_seq, CONV_K_F - 1, D_FF), (emitted if emit else None)


def _norm_kernel(x_ref, nw_ref, o_ref):
    o_ref[...] = _rmsnorm_rows(x_ref[...]) * nw_ref[...]


def _final_norm(x3, nw, geo):
    _, n_i, _ = _tiles(geo)
    return pl.pallas_call(
        _norm_kernel,
        grid=(n_i, 1),
        in_specs=[_x_spec(geo, D_MODEL, lambda j: 0),
                  pl.BlockSpec((1, D_MODEL), lambda i, j: (0, 0))],
        out_specs=_x_spec(geo, D_MODEL, lambda j: 0),
        out_shape=jax.ShapeDtypeStruct(x3.shape, F32),
        compiler_params=_cparams(("parallel", "arbitrary")),
        name="final_norm",
    )(x3, nw.reshape(1, D_MODEL))


def _state_spec(shape, layer, geo):
    rest = tuple(shape[2:])
    zeros = (0,) * len(rest)
    return pl.BlockSpec((1, geo.scan_seqs) + rest, lambda i, c: (layer, i) + zeros)


def _row_spec(geo, width, col):
    return pl.BlockSpec((geo.scan_seqs, geo.chunk, width), lambda i, c: (i, c, col))


def _rows3(a, geo):
    return a.reshape(geo.n_seq, geo.seq_len, a.shape[-1])


def _gates_t_spec(geo):
    return pl.BlockSpec((geo.scan_seqs, 1, GATE_W, geo.chunk), lambda i, c: (i, c, 0, 0))


ROWS, LEAD, STATE, CONST = "rows", "lead", "state", "const"


def _per_sequence(body, geo, kinds, interleave):
    def view(ref, kind, k):
        if kind == ROWS:
            return ref.at[k]
        if kind == LEAD:
            return ref.at[pl.ds(k, 1)]
        if kind == STATE:
            return ref.at[:, pl.ds(k, 1)]
        return ref

    def wrapped(*refs):
        running = [body(*[view(r, kind, k) for r, kind in zip(refs, kinds)])
                   for k in range(geo.scan_seqs)]
        if not interleave:
            for gen in running:
                for _ in gen:
                    pass
            return
        while running:
            still = []
            for gen in running:
                if next(gen, StopIteration) is not StopIteration:
                    still.append(gen)
            running = still
    return wrapped


def _seq_call(kernel_fn, geo, name, in_specs, args, out_specs, out_shapes, scratch, kinds,
              stacked_prev, interleave=True):
    nc = geo.seq_len // geo.chunk
    kernel_fn = _per_sequence(kernel_fn, geo, kinds, interleave)
    in_specs, args = list(in_specs), list(args)
    aliases = {}
    assert len(stacked_prev) <= 1
    for out_idx, arr in stacked_prev.items():
        kernel_fn = _drop_ref(kernel_fn, len(in_specs))
        aliases[len(in_specs)] = out_idx
        in_specs.append(pl.BlockSpec(memory_space=pl.ANY))
        args.append(arr)
    return pl.pallas_call(
        kernel_fn,
        grid=(geo.n_seq // geo.scan_seqs, nc),
        in_specs=in_specs,
        out_specs=out_specs,
        out_shape=out_shapes,
        scratch_shapes=scratch,
        input_output_aliases=aliases,
        compiler_params=_cparams(("parallel", "arbitrary")),
        name=name,
    )(*args)


def _ssd_kernel(z_ref, xbc_ref, gc_ref, gr_ref, cs_ref, s0_ref, cw_ref, cb_ref,
                dtb_r_ref, dtb_c_ref, al_r_ref, al_c_ref, dsk_ref, nw_ref, exp_ref,
                y_ref, cso_ref, so_ref, tail_scr, *, lc, lv, single_chunk):
    if single_chunk:
        s_ref = s0_ref
        cs = cs_ref[0, 0]
        tails = [cs[0:1, :], cs[1:2, :], cs[2:3, :]]
    else:
        @pl.when(pl.program_id(1) == 0)
        def _():
            tail_scr[0, 5:8, :] = cs_ref[0, 0]
            so_ref[0, 0] = s0_ref[0, 0]

        s_ref = so_ref
        tails = [tail_scr[0, 5:6, :], tail_scr[0, 6:7, :], tail_scr[0, 7:8, :]]

    x = xbc_ref[...]
    w = cw_ref[0]
    xc = (cb_ref[0] + w[0:1] * _shifted(x, tails, 3) + w[1:2] * _shifted(x, tails, 2)
          + w[2:3] * _shifted(x, tails, 1) + w[3:4] * x)
    xc = xc * _sigmoid(xc)
    new_tail = x[lv - 3:lv, :]
    if not single_chunk:
        tail_scr[0, 5:8, :] = new_tail
    cso_ref[0, 0] = new_tail
    xa = xc[:, :D_A]
    bm = xc[:, D_A:D_A + G_A * N_A]
    cm = xc[:, D_A + G_A * N_A:]

    dt_c = _softplus(gc_ref[:, 0:H_A] + dtb_r_ref[...])
    dt_r = _softplus(gr_ref[0, 0:H_A, :] + dtb_c_ref[...])
    if lv < lc:
        dt_c = jnp.where(lax.broadcasted_iota(jnp.int32, dt_c.shape, 0) < lv, dt_c, 0.0)
        dt_r = jnp.where(lax.broadcasted_iota(jnp.int32, dt_r.shape, 1) < lv, dt_r, 0.0)
    lower, upper, mask = _tri(lc)
    cum_c = jnp.dot(lower, dt_c * (-jnp.exp(al_r_ref[...])), precision=HI)
    cum_r = jnp.dot(dt_r * (-jnp.exp(al_c_ref[...])), upper, precision=HI)
    cum_last = cum_c[lc - 1:lc, :]
    expand = exp_ref[...]
    ecum_x = jnp.dot(jnp.exp(cum_c), expand, precision=HI)
    tail_x = jnp.dot(jnp.exp(cum_last - cum_c) * dt_c, expand, precision=HI)

    hg = H_A // G_A
    gw = hg * P_A
    groups = range(G_A)
    gsl = lambda g: slice(g * gw, (g + 1) * gw)
    cg = [cm[:, g * N_A:(g + 1) * N_A] for g in groups]
    bg = [bm[:, g * N_A:(g + 1) * N_A] for g in groups]
    yield
    cb_ts = [lax.dot_general(cg[g], bg[g], _NT, preferred_element_type=F32) for g in groups]
    y_inter = [lax.dot_general(cg[g], s_ref[0, 0, g], _NT, preferred_element_type=F32)
               for g in groups]
    upd = [lax.dot_general(xa[:, gsl(g)] * tail_x[:, gsl(g)], bg[g], _TN,
                           preferred_element_type=F32) for g in groups]
    yield
    w_ts = [cb_ts[h // hg] * dt_r[h:h + 1, :]
            * jnp.exp(jnp.where(mask, cum_c[:, h:h + 1] - cum_r[h:h + 1, :], -jnp.inf))
            for h in range(H_A)]
    yield
    pieces = []
    for j in range(H_A // 2):
        xp = xa[:, j * 128:(j + 1) * 128]
        lane = lax.broadcasted_iota(jnp.int32, xp.shape, 1)
        pieces.append(
            jnp.dot(w_ts[2 * j], jnp.where(lane < P_A, xp, 0.0), preferred_element_type=F32)
            + jnp.dot(w_ts[2 * j + 1], jnp.where(lane >= P_A, xp, 0.0),
                      preferred_element_type=F32))
    yield
    y = (jnp.concatenate(pieces, axis=1) + jnp.concatenate(y_inter, axis=1) * ecum_x
         + dsk_ref[...] * xa)
    for g in groups:
        for hh in range(hg):
            h = g * hg + hh
            rows = slice(hh * P_A, (hh + 1) * P_A)
            so_ref[0, 0, g, rows, :] = (s_ref[0, 0, g, rows, :] * jnp.exp(cum_r[h:h + 1, lc - 1:lc])
                                        + upd[g][rows, :])

    z = z_ref[...]
    y = y * (z * _sigmoid(z))
    y = jnp.concatenate([_rmsnorm_rows(y[:, g * gw:(g + 1) * gw]) for g in range(G_A)], axis=1)
    y_ref[...] = (y * nw_ref[...]).astype(y_ref.dtype)


def _ssd(p1, gates, gates_t, conv_state, ssd_state, prm, e, geo, prev_state_out):
    b = geo.n_seq
    const2 = lambda shape: pl.BlockSpec(shape, lambda i, c: (0, 0))
    n_even = ssd_state.shape[0]
    s5 = ssd_state.reshape(n_even, b, G_A, (H_A // G_A) * P_A, N_A)
    expand = (jnp.arange(D_A)[None, :] // P_A == jnp.arange(H_A)[:, None]).astype(F32)
    cso_shape = (1,) + conv_state.shape[1:]
    y, cso, so = _seq_call(
        functools.partial(_ssd_kernel, lc=geo.chunk, lv=geo.valid,
                          single_chunk=geo.seq_len == geo.chunk), geo, "ssd",
        in_specs=[
            _row_spec(geo, D_A, 0),
            _row_spec(geo, CONV_DIM_A, 2),
            _row_spec(geo, GATE_W, 0),
            _gates_t_spec(geo),
            _state_spec(conv_state.shape, e, geo),
            _state_spec(s5.shape, e, geo),
            pl.BlockSpec((1, CONV_K_A, CONV_DIM_A), lambda i, c: (e, 0, 0)),
            pl.BlockSpec((1, 1, CONV_DIM_A), lambda i, c: (e, 0, 0)),
            const2((1, H_A)), const2((H_A, 1)), const2((1, H_A)), const2((H_A, 1)),
            const2((1, D_A)), const2((1, D_A)), const2((H_A, D_A)),
        ],
        args=(_rows3(p1, geo), _rows3(p1, geo), _rows3(gates, geo), gates_t, conv_state, s5,
              prm['conv_w_a'], prm['conv_b_a'].reshape(n_even, 1, CONV_DIM_A),
              prm['dt_bias'][e].reshape(1, H_A), prm['dt_bias'][e].reshape(H_A, 1),
              prm['a_log'][e].reshape(1, H_A), prm['a_log'][e].reshape(H_A, 1),
              jnp.repeat(prm['d_skip'][e], P_A).reshape(1, D_A),
              prm['norm_a'][e].reshape(1, D_A), expand),
        out_specs=[_row_spec(geo, D_A, 0), _state_spec(cso_shape, 0, geo),
                   _state_spec(s5.shape, e, geo)],
        out_shapes=[
            jax.ShapeDtypeStruct((b, geo.seq_len, D_A), MXU_DTYPE),
            jax.ShapeDtypeStruct(cso_shape, F32),
            jax.ShapeDtypeStruct(s5.shape, F32),
        ],
        scratch=[pltpu.VMEM((geo.scan_seqs, SUBLANES, CONV_DIM_A), F32)],
        kinds=[ROWS] * 4 + [STATE, STATE] + [CONST] * 9 + [ROWS, STATE, STATE, LEAD],
        stacked_prev={} if prev_state_out is None else {2: prev_state_out},
    )
    return y.reshape(b * geo.seq_len, D_A), cso, so


def _mlstm_kernel(q_ref, k_ref, v_ref, og_ref, gc_ref, gr_ref, c0_ref, n0_ref, m0_ref,
                  ib_r_ref, ib_c_ref, fb_r_ref, fb_c_ref, nw_ref,
                  h_ref, co_ref, no_ref, mo_ref, *, lc, lv, single_chunk):
    if single_chunk:
        c_ref, n_ref, m_ref = c0_ref, n0_ref, m0_ref
    else:
        @pl.when(pl.program_id(1) == 0)
        def _():
            co_ref[0, 0] = c0_ref[0, 0]
            no_ref[0, 0] = n0_ref[0, 0]
            mo_ref[0, 0] = m0_ref[0, 0]

        c_ref, n_ref, m_ref = co_ref, no_ref, mo_ref
    m_old, n_old = m_ref[0, 0], n_ref[0, 0]

    i0, f0 = H_A, H_A + H_B
    li_c = gc_ref[:, i0:i0 + H_B] + ib_r_ref[...]
    lf_c = _log_sigmoid(gc_ref[:, f0:f0 + H_B] + fb_r_ref[...])
    li_r = gr_ref[0, i0:i0 + H_B, :] + ib_c_ref[...]
    lf_r = _log_sigmoid(gr_ref[0, f0:f0 + H_B, :] + fb_c_ref[...])
    if lv < lc:
        vc = lax.broadcasted_iota(jnp.int32, li_c.shape, 0) < lv
        vr = lax.broadcasted_iota(jnp.int32, li_r.shape, 1) < lv
        li_c, lf_c = jnp.where(vc, li_c, NEG_BIG), jnp.where(vc, lf_c, 0.0)
        li_r, lf_r = jnp.where(vr, li_r, NEG_BIG), jnp.where(vr, lf_r, 0.0)
    lower, upper, mask = _tri(lc)
    bc_c = jnp.dot(lower, lf_c, precision=HI)
    bc_r = jnp.dot(lf_r, upper, precision=HI)

    heads = range(H_B)
    q = [q_ref[:, h * DK_B:(h + 1) * DK_B] * (DK_B ** -0.5) for h in heads]
    k = [k_ref[:, h * DK_B:(h + 1) * DK_B] for h in heads]
    v = [v_ref[:, h * DV_B:(h + 1) * DV_B] for h in heads]
    yield
    qk = [lax.dot_general(q[h], k[h], _NT, preferred_element_type=F32) for h in heads]
    q_c = [jnp.dot(q[h], c_ref[0, 0, h], preferred_element_type=F32) for h in heads]
    yield
    m_t, w_in, w_ts = [], [], []
    for h in heads:
        bcc = bc_c[:, h:h + 1]
        dmat = jnp.where(mask, bcc - bc_r[h:h + 1, :] + li_r[h:h + 1, :], -jnp.inf)
        inter = bcc + m_old[:, h:h + 1]
        m_t.append(jnp.maximum(inter, jnp.max(dmat, axis=1, keepdims=True)))
        w_in.append(jnp.exp(inter - m_t[h]))
        w_ts.append(jnp.exp(dmat - m_t[h]) * qk[h])
    yield
    wv = [jnp.dot(w_ts[h], v[h], preferred_element_type=F32) for h in heads]
    yield
    hs = []
    for h in heads:
        num = wv[h] + w_in[h] * q_c[h]
        den = (jnp.sum(w_ts[h], axis=1, keepdims=True)
               + w_in[h] * jnp.sum(q[h] * n_old[h:h + 1, :], axis=1, keepdims=True))
        hs.append(_rmsnorm_rows(num / jnp.maximum(jnp.abs(den), jnp.exp(-m_t[h]))))
    lane_h = lax.broadcasted_iota(jnp.int32, (1, H_B), 1)
    m_out = jnp.zeros((1, H_B), F32)
    ks, w_c = [], []
    for h in heads:
        bcc = bc_c[:, h:h + 1]
        m_new = m_t[h][lv - 1:lv, :]
        bc_last = bcc[lc - 1:lc, :]
        ks.append(k[h] * jnp.exp(bc_last - bcc + li_c[:, h:h + 1] - m_new))
        w_c.append(jnp.exp(bc_last + m_old[:, h:h + 1] - m_new))
        m_out = jnp.where(lane_h == h, m_new, m_out)
    yield
    kv = [lax.dot_general(ks[h], v[h], _TN, preferred_element_type=F32) for h in heads]
    yield
    for h in heads:
        co_ref[0, 0, h] = w_c[h] * c_ref[0, 0, h] + kv[h]
        no_ref[0, 0, h:h + 1, :] = w_c[h] * n_old[h:h + 1, :] + jnp.sum(ks[h], axis=0, keepdims=True)
    mo_ref[0, 0] = m_out
    hn = jnp.concatenate(hs, axis=1) * nw_ref[...]
    h_ref[...] = (hn * _sigmoid(og_ref[...])).astype(h_ref.dtype)


def _mlstm(p1, gates, gates_t, c_state, n_state, m_state, prm, e, geo, prev_state_out):
    b = geo.n_seq
    const2 = lambda shape: pl.BlockSpec(shape, lambda i, c: (0, 0))
    qk_w = H_B * DK_B
    m4 = m_state.reshape(m_state.shape[0], b, 1, H_B)
    one = lambda shape: (1,) + tuple(shape[1:])
    h, co, no, mo = _seq_call(
        functools.partial(_mlstm_kernel, lc=geo.chunk, lv=geo.valid,
                          single_chunk=geo.seq_len == geo.chunk), geo, "mlstm",
        in_specs=[
            _row_spec(geo, qk_w, 9), _row_spec(geo, qk_w, 10),
            _row_spec(geo, D_B, 1), _row_spec(geo, D_B, 2),
            _row_spec(geo, GATE_W, 0),
            _gates_t_spec(geo),
            _state_spec(c_state.shape, e, geo), _state_spec(n_state.shape, e, geo),
            _state_spec(m4.shape, e, geo),
            const2((1, H_B)), const2((H_B, 1)), const2((1, H_B)), const2((H_B, 1)),
            const2((1, D_B)),
        ],
        args=(_rows3(p1, geo),) * 4 + (_rows3(gates, geo), gates_t, c_state, n_state, m4,
              prm['i_bias'][e].reshape(1, H_B), prm['i_bias'][e].reshape(H_B, 1),
              prm['f_bias'][e].reshape(1, H_B), prm['f_bias'][e].reshape(H_B, 1),
              prm['norm_b'][e].reshape(1, D_B)),
        out_specs=[_row_spec(geo, D_B, 0), _state_spec(c_state.shape, e, geo),
                   _state_spec(one(n_state.shape), 0, geo), _state_spec(one(m4.shape), 0, geo)],
        out_shapes=[
            jax.ShapeDtypeStruct((b, geo.seq_len, D_B), MXU_DTYPE),
            jax.ShapeDtypeStruct(c_state.shape, F32),
            jax.ShapeDtypeStruct(one(n_state.shape), F32),
            jax.ShapeDtypeStruct(one(m4.shape), F32),
        ],
        scratch=[],
        kinds=[ROWS] * 6 + [STATE] * 3 + [CONST] * 5 + [ROWS] + [STATE] * 3,
        stacked_prev={} if prev_state_out is None else {1: prev_state_out},
        interleave=geo.seq_len != geo.chunk,
    )
    return h.reshape(b * geo.seq_len, D_B), co, no, mo.reshape(1, b, H_B)


def _hgrn_kernel(q_ref, f_ref, i_ref, g_ref, lbl_ref, s0_ref, nw_ref, o_ref, so_ref,
                 *, lc, lv, layer_o, bs, single_chunk):
    if single_chunk:
        s_ref = s0_ref
    else:
        @pl.when(pl.program_id(1) == 0)
        def _():
            so_ref[0, 0] = s0_ref[0, 0]

        s_ref = so_ref

    lbl = lbl_ref[...]
    ex = jnp.exp(lbl - jnp.max(lbl, axis=0, keepdims=True))
    sm = ex / jnp.sum(ex, axis=0, keepdims=True)
    lb_all = [sm[0:1, :]]
    for r in range(1, lbl.shape[0]):
        lb_all.append(lb_all[-1] + sm[r:r + 1, :])
    lb = lb_all[layer_o] - lb_all[0]

    fx = f_ref[...]
    e1 = jnp.exp(-jnp.abs(fx))
    log_sig = jnp.minimum(fx, 0.0) - jnp.log1p(e1)
    la = jnp.log(lb)
    lb_ = jnp.log1p(-lb) + log_sig
    logf = jnp.maximum(la, lb_) + jnp.log1p(jnp.exp(-jnp.abs(la - lb_)))
    kk = (1.0 - lb) * (jnp.where(fx >= 0.0, e1, 1.0) / (1.0 + e1))
    if lv < lc:
        valid = lax.broadcasted_iota(jnp.int32, fx.shape, 0) < lv
        logf = jnp.where(valid, logf, 0.0)
        kk = jnp.where(valid, kk, 0.0)

    nb = lc // bs
    r_i = lax.broadcasted_iota(jnp.int32, (lc, lc), 0)
    c_i = lax.broadcasted_iota(jnp.int32, (lc, lc), 1)
    sh = int(math.log2(bs))
    blk_lower = ((c_i <= r_i) & ((c_i >> sh) == (r_i >> sh))).astype(F32)
    gw = jnp.dot(blk_lower, logf, precision=HI)
    q = q_ref[...]
    v = i_ref[...]
    blk = lambda a, i: a[i * bs:(i + 1) * bs, :]
    tots = [gw[(i + 1) * bs - 1:(i + 1) * bs, :] for i in range(nb)]
    before = [jnp.zeros_like(tots[0])]
    for i in range(nb):
        before.append(before[-1] + tots[i])
    g_tot = before[nb]
    qt = q * jnp.exp(gw)
    kt = [blk(kk, j) * jnp.exp(tots[j] - blk(gw, j)) for j in range(nb)]
    q_in = jnp.concatenate([blk(qt, i) * jnp.exp(before[i]) for i in range(nb)], axis=0)
    k_out = jnp.concatenate([kt[j] * jnp.exp(g_tot - before[j + 1]) for j in range(nb)], axis=0)

    heads = range(H_C)
    hsl = lambda h: slice(h * DK_C, (h + 1) * DK_C)
    yield
    o_inter = [jnp.dot(q_in[:, hsl(h)], s_ref[0, 0, h], preferred_element_type=F32)
               for h in heads]
    kv = [lax.dot_general(k_out[:, hsl(h)], v[:, hsl(h)], _TN, preferred_element_type=F32)
          for h in heads]
    att_off = [None]
    for i in range(1, nb):
        k_hat = jnp.concatenate(
            [kt[j] if j == i - 1 else kt[j] * jnp.exp(before[i] - before[j + 1])
             for j in range(i)] + [jnp.zeros((lc - i * bs, D_C), F32)], axis=0)
        q_ti = blk(qt, i)
        att_off.append([lax.dot_general(q_ti[:, hsl(h)], k_hat[:, hsl(h)], _NT,
                                        preferred_element_type=F32) for h in heads])
    yield
    gw2 = gw * LOG2E
    ck = jnp.log2(kk) - gw2
    n_t = bs // SUBLANES
    lane_s = lax.broadcasted_iota(jnp.int32, (SUBLANES, lc), 1)
    row_t = lax.broadcasted_iota(jnp.int32, (bs, lc), 0)
    col_s = lax.broadcasted_iota(jnp.int32, (bs, lc), 1)
    att = []
    for i in range(nb):
        g_i, q_i, ck_i = blk(gw2, i), blk(q, i), blk(ck, i)
        att_d = [[jnp.zeros((SUBLANES, lc), F32) for _ in range(n_t)] for _ in heads]
        for s in range(bs):
            t0 = s // SUBLANES
            p = q_i[t0 * SUBLANES:, :] * jnp.exp2(g_i[t0 * SUBLANES:, :] + ck_i[s:s + 1, :])
            for h in heads:
                a = jnp.sum(p[:, hsl(h)], axis=1, keepdims=True)
                for tt in range(t0, n_t):
                    a_t = a[(tt - t0) * SUBLANES:(tt - t0 + 1) * SUBLANES, :]
                    att_d[h][tt] = jnp.where(lane_s == i * bs + s, a_t, att_d[h][tt])
        causal = (col_s - i * bs) <= row_t
        att_i = []
        for h in heads:
            a = att_d[h][0] if n_t == 1 else jnp.concatenate(att_d[h], axis=0)
            a = jnp.where(causal, a, 0.0)
            att_i.append(a if i == 0 else a + att_off[i][h])
        att.append(att_i)
    yield
    o_intra = [[jnp.dot(att[i][h], v[:, hsl(h)], preferred_element_type=F32) for h in heads]
               for i in range(nb)]
    yield
    outs = []
    for h in heads:
        o_h = o_intra[0][h] if nb == 1 else jnp.concatenate([o_intra[i][h] for i in range(nb)],
                                                             axis=0)
        outs.append(_rmsnorm_rows(o_h + o_inter[h]))
        dec_col = jnp.transpose(jnp.broadcast_to(jnp.exp(g_tot[:, hsl(h)]), (DK_C, DK_C)))
        so_ref[0, 0, h] = dec_col * s_ref[0, 0, h] + kv[h]
    gate = g_ref[...]
    o_ref[...] = (jnp.concatenate(outs, axis=1) * nw_ref[...]
                  * (gate * _sigmoid(gate))).astype(o_ref.dtype)


def _hgrn(p, state, prm, o, geo, prev_state_out):
    geo = geo._replace(scan_seqs=geo.hgrn_seqs)
    n_odd = prm['lb_logits'].shape[0]
    out, new_state = _seq_call(
        functools.partial(_hgrn_kernel, lc=geo.chunk, lv=geo.valid, layer_o=o,
                          bs=min(16, geo.chunk), single_chunk=geo.seq_len == geo.chunk), geo, "hgrn",
        in_specs=[
            _row_spec(geo, D_C, 0), _row_spec(geo, D_C, 1), _row_spec(geo, D_C, 2),
            _row_spec(geo, D_C, 3),
            pl.BlockSpec((n_odd, D_C), lambda i, c: (0, 0)),
            _state_spec(state.shape, o, geo),
            pl.BlockSpec((1, D_C), lambda i, c: (0, 0)),
        ],
        args=(_rows3(p, geo),) * 4 + (prm['lb_logits'], state, prm['norm_c'][o].reshape(1, D_C)),
        out_specs=[_row_spec(geo, D_C, 0), _state_spec(state.shape, o, geo)],
        out_shapes=[
            jax.ShapeDtypeStruct((geo.n_seq, geo.seq_len, D_C), MXU_DTYPE),
            jax.ShapeDtypeStruct(state.shape, F32),
        ],
        scratch=[],
        kinds=[ROWS] * 4 + [CONST, STATE, CONST, ROWS, STATE],
        stacked_prev={} if prev_state_out is None else {1: prev_state_out},
    )
    return out.reshape(geo.n_seq * geo.seq_len, D_C), new_state


def _prep_weights(prm):
    a0, a1, a2 = D_A, D_A + CONV_DIM_A, D_A + CONV_DIM_A + H_A
    q1 = a2 + H_B * DK_B
    k1 = q1 + H_B * DK_B
    v1 = k1 + D_B
    o1 = v1 + D_B
    n_gate = H_A + 2 * H_B
    wab = prm['w_in_ab']
    in_ab = jnp.concatenate(
        [wab[:, :, :a0], wab[:, :, k1:v1], wab[:, :, v1:o1], wab[:, :, a0:a1],
         wab[:, :, a2:q1], wab[:, :, q1:k1]], axis=2).astype(MXU_DTYPE)
    gates_ab = jnp.pad(jnp.concatenate([wab[:, :, a1:a2], wab[:, :, o1:]], axis=2),
                       ((0, 0), (0, 0), (0, GATE_W - n_gate))).astype(MXU_DTYPE)
    n_even, n_odd = wab.shape[0], prm['w_in_c'].shape[0]
    return {
        'in_ab': [[(in_ab, e, 0)] for e in range(n_even)],
        'gates_ab': [[(gates_ab, e, 0)] for e in range(n_even)],
        'out_ab': [[(prm['w_out_ab'], e, 0), (prm['w_out_ab'], e, 1)] for e in range(n_even)],
        'in_c': [[(prm['w_in_c'], o, 0)] for o in range(n_odd)],
        'out_c': [[(prm['w_out_c'], o, 0)] for o in range(n_odd)],
        'ffn_g': [[(prm['w_ffn_g'], l, 0)] for l in range(DEPTH)],
        'ffn_u': [[(prm['w_ffn_u'], l, 0)] for l in range(DEPTH)],
        'ffn_d': [[(prm['w_ffn_d'], l, 0)] for l in range(DEPTH)],
    }


def _trunk(x3, mod, states, prm, w, geo, tn_in, tn_res, tn_ffn, tn_down, emit):
    conv_a, ssd, mem_c, mem_n, mem_m, hgrn, ffn_buf = states
    n_conv, n_n, n_m, n_ffn = [], [], [], []
    new_ssd = new_c = new_hgrn = None
    nc = geo.seq_len // geo.chunk
    wq = {name: list(per_layer) for name, per_layer in w.items()}

    def record(name, idx, emitted):
        if emitted is not None:
            wq[name][idx] = [(arr, 0, 0) for arr in emitted]

    casts = lambda wops: emit and any(wop[0].dtype != MXU_DTYPE for wop in wops)
    for layer in range(DEPTH):
        mm_norm = functools.partial(_mm_norm, x3, prm['norm_mix'][layer], mod, layer, 1, 0,
                                    geo=geo)
        if layer % 2 == 0:
            e = layer // 2
            p1, gates, _ = mm_norm(w['in_ab'][e][0], tn=tn_in[layer], narrow=w['gates_ab'][e][0])
            gates_t = gates.reshape(geo.n_seq, nc, geo.chunk, GATE_W).transpose(0, 1, 3, 2)
            ya, cv, new_ssd = _ssd(p1, gates, gates_t, conv_a, ssd, prm, e, geo, new_ssd)
            hb, new_c, nn, mmm = _mlstm(p1, gates, gates_t, mem_c, mem_n, mem_m, prm, e, geo, new_c)
            n_conv.append(cv); n_n.append(nn); n_m.append(mmm)
            x3, em = _mm_res([ya, hb], w['out_ab'][e], x3, mod, layer, 2, geo, tn_res[layer],
                             emit=casts(w['out_ab'][e]))
            record('out_ab', e, em)
        else:
            o = layer // 2
            p, _, em = mm_norm(w['in_c'][o][0], tn=tn_in[layer], emit=casts(w['in_c'][o]))
            record('in_c', o, None if em is None else [em])
            oc, new_hgrn = _hgrn(p, hgrn, prm, o, geo, new_hgrn)
            x3, em = _mm_res([oc], w['out_c'][o], x3, mod, layer, 2, geo, tn_res[layer],
                             emit=casts(w['out_c'][o]))
            record('out_c', o, em)
        act, fb, em = _ffn_in(x3, prm['norm_ffn'][layer], mod, layer, w['ffn_g'][layer][0],
                              w['ffn_u'][layer][0], ffn_buf, prm['conv_w_f'][layer],
                              prm['conv_b_f'][layer], geo, tn_ffn,
                              emit=casts(w['ffn_g'][layer] + w['ffn_u'][layer]))
        if em is not None:
            record('ffn_g', layer, em[:1])
            record('ffn_u', layer, em[1:])
        n_ffn.append(fb)
        x3, em = _mm_res([act], w['ffn_d'][layer], x3, mod, layer, 5, geo, tn_down,
                         emit=casts(w['ffn_d'][layer]))
        record('ffn_d', layer, em)
    y = _final_norm(x3, prm['norm_f'], geo)
    cat = lambda xs: jnp.concatenate(xs, axis=0)
    return (y, cat(n_conv), new_ssd.reshape(ssd.shape), new_c, cat(n_n), cat(n_m), new_hgrn,
            cat(n_ffn)), wq


def kernel(x_prompt, x_sample, c_prompt, c_sample, state_ssd_conv, state_ssd, state_mlstm_c, state_mlstm_n, state_mlstm_m, state_hgrn, state_ffn_conv, w_ada, b_ada, norm_mix, norm_ffn, w_in_ab, conv_w_a, conv_b_a, dt_bias, a_log, d_skip, norm_a, i_bias, f_bias, norm_b, w_out_ab, w_in_c, lb_logits, norm_c, w_out_c, w_ffn_g, w_ffn_u, conv_w_f, conv_b_f, w_ffn_d, norm_f):
    prm = dict(norm_mix=norm_mix, norm_ffn=norm_ffn, w_in_ab=w_in_ab, conv_w_a=conv_w_a,
               conv_b_a=conv_b_a, dt_bias=dt_bias, a_log=a_log, d_skip=d_skip, norm_a=norm_a,
               i_bias=i_bias, f_bias=f_bias, norm_b=norm_b, w_out_ab=w_out_ab, w_in_c=w_in_c,
               lb_logits=lb_logits, norm_c=norm_c, w_out_c=w_out_c, w_ffn_g=w_ffn_g,
               w_ffn_u=w_ffn_u, conv_w_f=conv_w_f, conv_b_f=conv_b_f, w_ffn_d=w_ffn_d,
               norm_f=norm_f)
    bp, lp, _ = x_prompt.shape
    bs, ls, _ = x_sample.shape
    n_even, n_odd = state_ssd.shape[0], state_hgrn.shape[0]
    w = _prep_weights(prm)

    n_c = bs + bp
    n_c_pad = -(-n_c // SUBLANES) * SUBLANES
    c_all = jnp.pad(jnp.concatenate([c_sample, c_prompt], axis=0), ((0, n_c_pad - n_c), (0, 0)))
    mod = _ada(c_all, w_ada, b_ada).reshape(DEPTH, n_c_pad, 1, 6 * D_MODEL)

    geo_s = Geo(n_seq=bs, seq_len=SUBLANES, seq_blk=bs, row_blk=SUBLANES, chunk=SUBLANES,
                valid=ls, mod_off=0, scan_seqs=4, hgrn_seqs=4)
    xs = jnp.pad(x_sample, ((0, 0), (0, SUBLANES - ls), (0, 0)))
    st_s = (state_ssd_conv, state_ssd, state_mlstm_c, state_mlstm_n, state_mlstm_m, state_hgrn,
            state_ffn_conv)
    out_s, w_bf16 = _trunk(xs, mod, st_s, prm, w, geo_s, tn_in=(512,) * DEPTH,
                           tn_res=(512,) * DEPTH, tn_ffn=256, tn_down=256, emit=True)

    zeros = lambda *s: jnp.zeros(s, F32)
    st_p = (zeros(n_even, bp, CONV_K_A - 1, CONV_DIM_A), zeros(n_even, bp, H_A, P_A, N_A),
            zeros(n_even, bp, H_B, DK_B, DV_B), zeros(n_even, bp, H_B, DK_B),
            zeros(n_even, bp, H_B), zeros(n_odd, bp, H_C, DK_C, DV_C),
            zeros(DEPTH, bp, CONV_K_F - 1, D_FF))
    lc_p = math.gcd(lp, PROMPT_CHUNK)
    geo_p = Geo(n_seq=bp, seq_len=lp, seq_blk=1, row_blk=min(lp, 1024), chunk=lc_p, valid=lc_p,
                mod_off=bs, scan_seqs=1, hgrn_seqs=2 if bp % 2 == 0 else 1)
    out_p, _ = _trunk(x_prompt, mod, st_p, prm, w_bf16, geo_p, tn_in=(1024,) * DEPTH,
                      tn_res=(1024,) * DEPTH, tn_ffn=FFN_TN, tn_down=FFN_DOWN_TN, emit=False)
    return (out_p[0], out_s[0][:, :ls]) + tuple(out_p[1:]) + tuple(out_s[1:])
```

```python
import collections
import functools
import math

import jax
import jax.numpy as jnp
from jax import lax
from jax.experimental import pallas as pl
from jax.experimental.pallas import tpu as pltpu

F32 = jnp.float32
MXU_DTYPE = jnp.bfloat16
HI = lax.Precision.HIGHEST
NEG_BIG = -1e30
LOG2E = 1.4426950408889634

D_MODEL = 2048
DEPTH = 4
EPS = 1e-6
PROMPT_CHUNK = 64
H_A, P_A, G_A, N_A, CONV_K_A = 32, 64, 4, 128, 4
D_A = H_A * P_A
CONV_DIM_A = D_A + 2 * G_A * N_A
H_B, DK_B, DV_B = 8, 128, 256
D_B = H_B * DV_B
H_C, DK_C, DV_C = 16, 128, 128
D_C = H_C * DV_C
D_FF, CONV_K_F = 5632, 3
FFN_TN = 512
FFN_DOWN_TN = 512
PROLOGUE_SPLIT = 4
GATE_W = 128
SUBLANES = 8
VMEM_LIMIT = 56 * 1024 * 1024

_NT = (((1,), (1,)), ((), ()))
_TN = (((0,), (0,)), ((), ()))

Geo = collections.namedtuple(
    "Geo", "n_seq seq_len seq_blk row_blk chunk valid mod_off scan_seqs mlstm_seqs hgrn_seqs")


def _cparams(sem, flags=None):
    return pltpu.CompilerParams(dimension_semantics=sem, vmem_limit_bytes=VMEM_LIMIT, flags=flags)


def _sigmoid(x):
    return 1.0 / (1.0 + jnp.exp(-x))


def _softplus(x):
    return jnp.maximum(x, 0.0) + jnp.log1p(jnp.exp(-jnp.abs(x)))


def _log_sigmoid(x):
    return jnp.minimum(x, 0.0) - jnp.log1p(jnp.exp(-jnp.abs(x)))


def _tri(n):
    r = lax.broadcasted_iota(jnp.int32, (n, n), 0)
    c = lax.broadcasted_iota(jnp.int32, (n, n), 1)
    mask = c <= r
    return mask.astype(F32), (r <= c).astype(F32), mask


def _shifted(x, tails, k, axis=0):
    rolled = pltpu.roll(x, k, axis)
    head = lax.slice_in_dim(rolled, 0, SUBLANES, axis=axis)
    row = lax.broadcasted_iota(jnp.int32, head.shape, axis)
    for r in range(k):
        head = jnp.where(row == r, tails[len(tails) - k + r], head)
    if x.shape[axis] == SUBLANES:
        return head
    rest = lax.slice_in_dim(rolled, SUBLANES, x.shape[axis], axis=axis)
    return jnp.concatenate([head, rest], axis=axis)


def _drop_ref(fn, idx):
    def wrapped(*refs):
        return fn(*refs[:idx], *refs[idx + 1:])
    return wrapped


def _rmsnorm_rows(x):
    return x * lax.rsqrt(jnp.mean(x * x, axis=-1, keepdims=True) + EPS)


def _ada_kernel(c_ref, w_ref, b_ref, o_ref):
    c = c_ref[...]
    ca = (c * _sigmoid(c)).astype(MXU_DTYPE)
    o_ref[0] = jnp.dot(ca, w_ref[0].astype(MXU_DTYPE), preferred_element_type=F32) + b_ref[0]


def _ada(c_all, w_ada, b_ada, tn=1024):
    rows = c_all.shape[0]
    n = w_ada.shape[2]
    return pl.pallas_call(
        _ada_kernel,
        grid=(DEPTH, n // tn),
        in_specs=[
            pl.BlockSpec((rows, D_MODEL), lambda l, j: (0, 0)),
            pl.BlockSpec((1, D_MODEL, tn), lambda l, j: (l, 0, j)),
            pl.BlockSpec((1, 1, tn), lambda l, j: (l, 0, j)),
        ],
        out_specs=pl.BlockSpec((1, rows, tn), lambda l, j: (l, 0, j)),
        out_shape=jax.ShapeDtypeStruct((DEPTH, rows, n), F32),
        compiler_params=_cparams(("parallel", "parallel")),
        name="ada",
    )(c_all, w_ada, b_ada.reshape(DEPTH, 1, n))


def _tiles(geo):
    nrt = geo.seq_len // geo.row_blk
    return nrt, (geo.n_seq // geo.seq_blk) * nrt, geo.seq_blk * geo.row_blk


def _x_spec(geo, width, col_of):
    nrt = geo.seq_len // geo.row_blk
    return pl.BlockSpec((geo.seq_blk, geo.row_blk, width),
                        lambda i, j: (i // nrt, i % nrt, col_of(j)))


def _mod_spec(geo, layer, width, col_of):
    nrt = geo.seq_len // geo.row_blk
    return pl.BlockSpec((1, geo.seq_blk, 1, width),
                        lambda i, j: (layer, geo.mod_off + i // nrt, 0, col_of(j)))


def _norm_mod_chunks(x_ref, nw_ref, sc_ref, sh_ref):
    seq_blk, row_blk, d = x_ref.shape
    for r in range(PROLOGUE_SPLIT):
        if seq_blk == 1:
            n = row_blk // PROLOGUE_SPLIT
            x, sc, sh = x_ref[:, r * n:(r + 1) * n], sc_ref[0], sh_ref[0]
        else:
            n = seq_blk // PROLOGUE_SPLIT
            x, sc, sh = x_ref[r * n:(r + 1) * n], sc_ref[0, r * n:(r + 1) * n], sh_ref[0, r * n:(r + 1) * n]
        h = (_rmsnorm_rows(x) * nw_ref[...]) * (1.0 + sc) + sh
        rows = h.shape[0] * h.shape[1]
        yield slice(r * rows, (r + 1) * rows), h.reshape(rows, d).astype(MXU_DTYPE)


def _weight_spec(wop, k, tn):
    _, layer, row_block = wop
    return pl.BlockSpec((1, k, tn), lambda i, j: (layer, row_block, j))


def _emit_spec_shape(k, n, tn):
    return (pl.BlockSpec((1, k, tn), lambda i, j: (0, 0, j)),
            jax.ShapeDtypeStruct((1, k, n), MXU_DTYPE))


def _weight_tile(w_ref, wq_ref):
    w = w_ref[0].astype(MXU_DTYPE)
    if wq_ref is not None:
        wq_ref[0] = w
    return w


def _mm_norm_kernel(*refs, emit, narrow):
    x_ref, nw_ref, sc_ref, sh_ref, w_ref = refs[:5]
    rest = list(refs[5:])
    wn_ref = rest.pop(0) if narrow else None
    o_ref = rest.pop(0)
    on_ref = rest.pop(0) if narrow else None
    wq_ref = rest.pop(0) if emit else None
    (h_scr,) = rest
    w = _weight_tile(w_ref, wq_ref)

    @pl.when(pl.program_id(1) == 0)
    def _():
        for rows, h in _norm_mod_chunks(x_ref, nw_ref, sc_ref, sh_ref):
            h_scr[rows, :] = h
            o_ref[rows, :] = jnp.dot(h, w, preferred_element_type=F32)
            if narrow:
                on_ref[rows, :] = jnp.dot(h, wn_ref[0], preferred_element_type=F32)

    @pl.when(pl.program_id(1) > 0)
    def _():
        o_ref[...] = jnp.dot(h_scr[...], w, preferred_element_type=F32)


def _mm_norm(x3, nw, mod, layer, k_sc, k_sh, wop, geo, tn, emit=False, narrow=None):
    _, n_i, tm = _tiles(geo)
    n = wop[0].shape[2]
    in_specs = [
        _x_spec(geo, D_MODEL, lambda j: 0),
        pl.BlockSpec((1, D_MODEL), lambda i, j: (0, 0)),
        _mod_spec(geo, layer, D_MODEL, lambda j: k_sc),
        _mod_spec(geo, layer, D_MODEL, lambda j: k_sh),
        _weight_spec(wop, D_MODEL, tn),
    ]
    args = [x3, nw.reshape(1, D_MODEL), mod, mod, wop[0]]
    out_specs = [pl.BlockSpec((tm, tn), lambda i, j: (i, j))]
    out_shape = [jax.ShapeDtypeStruct((geo.n_seq * geo.seq_len, n), F32)]
    if narrow is not None:
        n_narrow = narrow[0].shape[2]
        in_specs.append(pl.BlockSpec((1, D_MODEL, n_narrow), lambda i, j: (narrow[1], 0, 0)))
        args.append(narrow[0])
        out_specs.append(pl.BlockSpec((tm, n_narrow), lambda i, j: (i, 0)))
        out_shape.append(jax.ShapeDtypeStruct((geo.n_seq * geo.seq_len, n_narrow), F32))
    if emit:
        assert n_i == 1
        spec, shape = _emit_spec_shape(D_MODEL, n, tn)
        out_specs.append(spec)
        out_shape.append(shape)
    outs = list(pl.pallas_call(
        functools.partial(_mm_norm_kernel, emit=emit, narrow=narrow is not None),
        grid=(n_i, n // tn),
        in_specs=in_specs,
        out_specs=out_specs,
        out_shape=out_shape,
        scratch_shapes=[pltpu.VMEM((tm, D_MODEL), MXU_DTYPE)],
        compiler_params=_cparams(("parallel", "arbitrary")),
        name="mm_norm",
    )(*args))
    out = outs.pop(0)
    out_narrow = outs.pop(0) if narrow is not None else None
    return out, out_narrow, (outs.pop(0) if emit else None)


def _mm_res_kernel(*refs, n_lhs, emit):
    a_refs, w_refs = refs[:n_lhs], refs[n_lhs:2 * n_lhs]
    xres_ref, gate_ref, o_ref = refs[2 * n_lhs:2 * n_lhs + 3]
    wq_refs = refs[2 * n_lhs + 3:] if emit else (None,) * n_lhs
    acc = None
    for a_ref, w_ref, wq_ref in zip(a_refs, w_refs, wq_refs):
        d = jnp.dot(a_ref[...], _weight_tile(w_ref, wq_ref), preferred_element_type=F32)
        acc = d if acc is None else acc + d
    o_ref[...] = xres_ref[...] + gate_ref[0] * acc.reshape(o_ref.shape)


def _mm_res(a_list, wops, x3, mod, layer, k_gate, geo, tn, emit=False):
    _, n_i, tm = _tiles(geo)
    per = D_MODEL // tn
    in_specs = [pl.BlockSpec((tm, a.shape[1]), lambda i, j: (i, 0)) for a in a_list]
    in_specs += [_weight_spec(wop, a.shape[1], tn) for wop, a in zip(wops, a_list)]
    in_specs += [_x_spec(geo, tn, lambda j: j),
                 _mod_spec(geo, layer, tn, lambda j: k_gate * per + j)]
    out_specs = [_x_spec(geo, tn, lambda j: j)]
    out_shape = [jax.ShapeDtypeStruct(x3.shape, F32)]
    if emit:
        assert n_i == 1
        for a in a_list:
            spec, shape = _emit_spec_shape(a.shape[1], D_MODEL, tn)
            out_specs.append(spec)
            out_shape.append(shape)
    outs = pl.pallas_call(
        functools.partial(_mm_res_kernel, n_lhs=len(a_list), emit=emit),
        grid=(n_i, per),
        in_specs=in_specs,
        out_specs=out_specs,
        out_shape=out_shape,
        compiler_params=_cparams(("parallel", "parallel")),
        name="mm_res",
    )(*a_list, *[wop[0] for wop in wops], x3, mod)
    return outs[0], (list(outs[1:]) if emit else None)


def _ffn_in_kernel(x_ref, nw_ref, sc_ref, sh_ref, wg_ref, wu_ref, cs_ref, cw_ref, cb_ref,
                   a_ref, cso_ref, *rest, nrt, valid, emit):
    wgq_ref, wuq_ref, h_scr, tail_scr = rest if emit else (None, None) + rest
    i, j = pl.program_id(0), pl.program_id(1)

    seq_blk, row_blk, tn = x_ref.shape[0], x_ref.shape[1], a_ref.shape[1]
    w_g, w_u = _weight_tile(wg_ref, wgq_ref), _weight_tile(wu_ref, wuq_ref)

    def conv_gate(g, u):
        g, u = g.reshape(seq_blk, row_blk, tn), u.reshape(seq_blk, row_blk, tn)
        prev = cs_ref[0]
        if nrt > 1:
            prev = jnp.where(i % nrt == 0, prev, tail_scr[j])
        t2, t1 = prev[:, 0:1, :], prev[:, 1:2, :]
        w = cw_ref[...]
        y = (cb_ref[...] + w[0:1] * _shifted(g, [t2, t1], 2, axis=1)
             + w[1:2] * _shifted(g, [t2, t1], 1, axis=1) + w[2:3] * g)
        a_ref[...] = (y * _sigmoid(y) * u).reshape(a_ref.shape).astype(a_ref.dtype)
        new_tail = g[:, valid - 2:valid, :]
        if nrt > 1:
            tail_scr[j] = new_tail
        cso_ref[0] = new_tail

    @pl.when(j == 0)
    def _():
        gs, us = [], []
        for rows, h in _norm_mod_chunks(x_ref, nw_ref, sc_ref, sh_ref):
            h_scr[rows, :] = h
            gs.append(jnp.dot(h, w_g, preferred_element_type=F32))
            us.append(jnp.dot(h, w_u, preferred_element_type=F32))
        conv_gate(jnp.concatenate(gs, axis=0), jnp.concatenate(us, axis=0))

    @pl.when(j > 0)
    def _():
        h = h_scr[...]
        conv_gate(jnp.dot(h, w_g, preferred_element_type=F32),
                  jnp.dot(h, w_u, preferred_element_type=F32))


def _ffn_in(x3, nw, mod, layer, wop_g, wop_u, conv_state, conv_w, conv_b, geo, tn, emit=False):
    nrt, n_i, tm = _tiles(geo)
    n_j = D_FF // tn
    valid = geo.row_blk if geo.valid == geo.chunk else geo.valid
    tail_shape = (n_j, geo.seq_blk, CONV_K_F - 1, tn) if nrt > 1 else (1, 1, CONV_K_F - 1, 128)
    cs_spec = pl.BlockSpec((1, geo.seq_blk, CONV_K_F - 1, tn), lambda i, j: (layer, i // nrt, 0, j))
    emit_specs, emit_shapes = [], []
    if emit:
        assert n_i == 1
        for _ in range(2):
            spec, shape = _emit_spec_shape(D_MODEL, D_FF, tn)
            emit_specs.append(spec)
            emit_shapes.append(shape)
    act, tails, *emitted = pl.pallas_call(
        functools.partial(_ffn_in_kernel, nrt=nrt, valid=valid, emit=emit),
        grid=(n_i, n_j),
        in_specs=[
            _x_spec(geo, D_MODEL, lambda j: 0),
            pl.BlockSpec((1, D_MODEL), lambda i, j: (0, 0)),
            _mod_spec(geo, layer, D_MODEL, lambda j: 4),
            _mod_spec(geo, layer, D_MODEL, lambda j: 3),
            _weight_spec(wop_g, D_MODEL, tn),
            _weight_spec(wop_u, D_MODEL, tn),
            cs_spec,
            pl.BlockSpec((CONV_K_F, tn), lambda i, j: (0, j)),
            pl.BlockSpec((1, tn), lambda i, j: (0, j)),
        ],
        out_specs=[
            pl.BlockSpec((tm, tn), lambda i, j: (i, j)),
            pl.BlockSpec((1, geo.seq_blk, CONV_K_F - 1, tn), lambda i, j: (i, 0, 0, j)),
        ] + emit_specs,
        out_shape=[
            jax.ShapeDtypeStruct((geo.n_seq * geo.seq_len, D_FF), MXU_DTYPE),
            jax.ShapeDtypeStruct((n_i, geo.seq_blk, CONV_K_F - 1, D_FF), F32),
        ] + emit_shapes,
        scratch_shapes=[pltpu.VMEM((tm, D_MODEL), MXU_DTYPE),
                        pltpu.VMEM(tail_shape, F32)],
        compiler_params=_cparams(("arbitrary", "arbitrary")),
        name="ffn_in",
    )(x3, nw.reshape(1, D_MODEL), mod, mod, wop_g[0], wop_u[0], conv_state, conv_w,
      conv_b.reshape(1, D_FF))
    last = tails.reshape(n_i // nrt, nrt, geo.seq_blk, CONV_K_F - 1, D_FF)[:, nrt - 1]
    return act, last.reshape(1, geo.n_seq, CONV_K_F - 1, D_FF), (emitted if emit else None)


def _norm_kernel(x_ref, nw_ref, o_ref):
    o_ref[...] = _rmsnorm_rows(x_ref[...]) * nw_ref[...]


def _final_norm(x3, nw, geo):
    _, n_i, _ = _tiles(geo)
    return pl.pallas_call(
        _norm_kernel,
        grid=(n_i, 1),
        in_specs=[_x_spec(geo, D_MODEL, lambda j: 0),
                  pl.BlockSpec((1, D_MODEL), lambda i, j: (0, 0))],
        out_specs=_x_spec(geo, D_MODEL, lambda j: 0),
        out_shape=jax.ShapeDtypeStruct(x3.shape, F32),
        compiler_params=_cparams(("parallel", "arbitrary")),
        name="final_norm",
    )(x3, nw.reshape(1, D_MODEL))


def _state_spec(shape, layer, geo):
    rest = tuple(shape[2:])
    zeros = (0,) * len(rest)
    return pl.BlockSpec((1, geo.scan_seqs) + rest, lambda i, c: (layer, i) + zeros)


def _row_spec(geo, width, col):
    return pl.BlockSpec((geo.scan_seqs, geo.chunk, width), lambda i, c: (i, c, col))


def _rows3(a, geo):
    return a.reshape(geo.n_seq, geo.seq_len, a.shape[-1])


def _gates_t_spec(geo):
    return pl.BlockSpec((geo.scan_seqs, 1, GATE_W, geo.chunk), lambda i, c: (i, c, 0, 0))


ROWS, LEAD, STATE, CONST = "rows", "lead", "state", "const"


def _per_sequence(body, geo, kinds, interleave):
    def view(ref, kind, k):
        if kind == ROWS:
            return ref.at[k]
        if kind == LEAD:
            return ref.at[pl.ds(k, 1)]
        if kind == STATE:
            return ref.at[:, pl.ds(k, 1)]
        return ref

    def wrapped(*refs):
        running = [body(*[view(r, kind, k) for r, kind in zip(refs, kinds)])
                   for k in range(geo.scan_seqs)]
        if not interleave:
            for gen in running:
                for _ in gen:
                    pass
            return
        while running:
            still = []
            for gen in running:
                if next(gen, StopIteration) is not StopIteration:
                    still.append(gen)
            running = still
    return wrapped


def _seq_call(kernel_fn, geo, name, in_specs, args, out_specs, out_shapes, scratch, kinds,
              stacked_prev, interleave=True):
    nc = geo.seq_len // geo.chunk
    kernel_fn = _per_sequence(kernel_fn, geo, kinds, interleave)
    in_specs, args = list(in_specs), list(args)
    aliases = {}
    assert len(stacked_prev) <= 1
    for out_idx, arr in stacked_prev.items():
        kernel_fn = _drop_ref(kernel_fn, len(in_specs))
        aliases[len(in_specs)] = out_idx
        in_specs.append(pl.BlockSpec(memory_space=pl.ANY))
        args.append(arr)
    return pl.pallas_call(
        kernel_fn,
        grid=(geo.n_seq // geo.scan_seqs, nc),
        in_specs=in_specs,
        out_specs=out_specs,
        out_shape=out_shapes,
        scratch_shapes=scratch,
        input_output_aliases=aliases,
        compiler_params=_cparams(("parallel", "arbitrary")),
        name=name,
    )(*args)


def _ssd_kernel(z_ref, xbc_ref, gc_ref, gr_ref, cs_ref, s0_ref, cw_ref, cb_ref,
                dtb_r_ref, dtb_c_ref, al_r_ref, al_c_ref, dsk_ref, nw_ref, exp_ref,
                y_ref, cso_ref, so_ref, tail_scr, *, lc, lv, single_chunk):
    if single_chunk:
        s_ref = s0_ref
        cs = cs_ref[0, 0]
        tails = [cs[0:1, :], cs[1:2, :], cs[2:3, :]]
    else:
        @pl.when(pl.program_id(1) == 0)
        def _():
            tail_scr[0, 5:8, :] = cs_ref[0, 0]
            so_ref[0, 0] = s0_ref[0, 0]

        s_ref = so_ref
        tails = [tail_scr[0, 5:6, :], tail_scr[0, 6:7, :], tail_scr[0, 7:8, :]]

    x = xbc_ref[...]
    w = cw_ref[0]
    xc = (cb_ref[0] + w[0:1] * _shifted(x, tails, 3) + w[1:2] * _shifted(x, tails, 2)
          + w[2:3] * _shifted(x, tails, 1) + w[3:4] * x)
    xc = xc * _sigmoid(xc)
    new_tail = x[lv - 3:lv, :]
    if not single_chunk:
        tail_scr[0, 5:8, :] = new_tail
    cso_ref[0, 0] = new_tail
    xa = xc[:, :D_A]
    bm = xc[:, D_A:D_A + G_A * N_A]
    cm = xc[:, D_A + G_A * N_A:]

    dt_c = _softplus(gc_ref[:, 0:H_A] + dtb_r_ref[...])
    dt_r = _softplus(gr_ref[0, 0:H_A, :] + dtb_c_ref[...])
    if lv < lc:
        dt_c = jnp.where(lax.broadcasted_iota(jnp.int32, dt_c.shape, 0) < lv, dt_c, 0.0)
        dt_r = jnp.where(lax.broadcasted_iota(jnp.int32, dt_r.shape, 1) < lv, dt_r, 0.0)
    lower, upper, mask = _tri(lc)
    cum_c = jnp.dot(lower, dt_c * (-jnp.exp(al_r_ref[...])), precision=HI)
    cum_r = jnp.dot(dt_r * (-jnp.exp(al_c_ref[...])), upper, precision=HI)
    cum_last = cum_c[lc - 1:lc, :]
    expand = exp_ref[...]
    ecum_x = jnp.dot(jnp.exp(cum_c), expand, precision=HI)
    tail_x = jnp.dot(jnp.exp(cum_last - cum_c) * dt_c, expand, precision=HI)

    hg = H_A // G_A
    gw = hg * P_A
    groups = range(G_A)
    gsl = lambda g: slice(g * gw, (g + 1) * gw)
    cg = [cm[:, g * N_A:(g + 1) * N_A] for g in groups]
    bg = [bm[:, g * N_A:(g + 1) * N_A] for g in groups]
    yield
    cb_ts = [lax.dot_general(cg[g], bg[g], _NT, preferred_element_type=F32) for g in groups]
    y_inter = [lax.dot_general(cg[g], s_ref[0, 0, g], _NT, preferred_element_type=F32)
               for g in groups]
    upd = [lax.dot_general(xa[:, gsl(g)] * tail_x[:, gsl(g)], bg[g], _TN,
                           preferred_element_type=F32) for g in groups]
    yield
    w_ts = [cb_ts[h // hg] * dt_r[h:h + 1, :]
            * jnp.exp(jnp.where(mask, cum_c[:, h:h + 1] - cum_r[h:h + 1, :], -jnp.inf))
            for h in range(H_A)]
    yield
    pieces = []
    for j in range(H_A // 2):
        xp = xa[:, j * 128:(j + 1) * 128]
        lane = lax.broadcasted_iota(jnp.int32, xp.shape, 1)
        pieces.append(
            jnp.dot(w_ts[2 * j], jnp.where(lane < P_A, xp, 0.0), preferred_element_type=F32)
            + jnp.dot(w_ts[2 * j + 1], jnp.where(lane >= P_A, xp, 0.0),
                      preferred_element_type=F32))
    yield
    y = (jnp.concatenate(pieces, axis=1) + jnp.concatenate(y_inter, axis=1) * ecum_x
         + dsk_ref[...] * xa)
    for g in groups:
        for hh in range(hg):
            h = g * hg + hh
            rows = slice(hh * P_A, (hh + 1) * P_A)
            so_ref[0, 0, g, rows, :] = (s_ref[0, 0, g, rows, :] * jnp.exp(cum_r[h:h + 1, lc - 1:lc])
                                        + upd[g][rows, :])

    z = z_ref[...]
    y = y * (z * _sigmoid(z))
    y = jnp.concatenate([_rmsnorm_rows(y[:, g * gw:(g + 1) * gw]) for g in range(G_A)], axis=1)
    y_ref[...] = (y * nw_ref[...]).astype(y_ref.dtype)


def _ssd(p1, gates, gates_t, conv_state, ssd_state, prm, e, geo, prev_state_out):
    b = geo.n_seq
    const2 = lambda shape: pl.BlockSpec(shape, lambda i, c: (0, 0))
    n_even = ssd_state.shape[0]
    s5 = ssd_state.reshape(n_even, b, G_A, (H_A // G_A) * P_A, N_A)
    expand = (jnp.arange(D_A)[None, :] // P_A == jnp.arange(H_A)[:, None]).astype(F32)
    cso_shape = (1,) + conv_state.shape[1:]
    y, cso, so = _seq_call(
        functools.partial(_ssd_kernel, lc=geo.chunk, lv=geo.valid,
                          single_chunk=geo.seq_len == geo.chunk), geo, "ssd",
        in_specs=[
            _row_spec(geo, D_A, 0),
            _row_spec(geo, CONV_DIM_A, 2),
            _row_spec(geo, GATE_W, 0),
            _gates_t_spec(geo),
            _state_spec(conv_state.shape, e, geo),
            _state_spec(s5.shape, e, geo),
            pl.BlockSpec((1, CONV_K_A, CONV_DIM_A), lambda i, c: (e, 0, 0)),
            pl.BlockSpec((1, 1, CONV_DIM_A), lambda i, c: (e, 0, 0)),
            const2((1, H_A)), const2((H_A, 1)), const2((1, H_A)), const2((H_A, 1)),
            const2((1, D_A)), const2((1, D_A)), const2((H_A, D_A)),
        ],
        args=(_rows3(p1, geo), _rows3(p1, geo), _rows3(gates, geo), gates_t, conv_state, s5,
              prm['conv_w_a'], prm['conv_b_a'].reshape(n_even, 1, CONV_DIM_A),
              prm['dt_bias'][e].reshape(1, H_A), prm['dt_bias'][e].reshape(H_A, 1),
              prm['a_log'][e].reshape(1, H_A), prm['a_log'][e].reshape(H_A, 1),
              jnp.repeat(prm['d_skip'][e], P_A).reshape(1, D_A),
              prm['norm_a'][e].reshape(1, D_A), expand),
        out_specs=[_row_spec(geo, D_A, 0), _state_spec(cso_shape, 0, geo),
                   _state_spec(s5.shape, e, geo)],
        out_shapes=[
            jax.ShapeDtypeStruct((b, geo.seq_len, D_A), MXU_DTYPE),
            jax.ShapeDtypeStruct(cso_shape, F32),
            jax.ShapeDtypeStruct(s5.shape, F32),
        ],
        scratch=[pltpu.VMEM((geo.scan_seqs, SUBLANES, CONV_DIM_A), F32)],
        kinds=[ROWS] * 4 + [STATE, STATE] + [CONST] * 9 + [ROWS, STATE, STATE, LEAD],
        stacked_prev={} if prev_state_out is None else {2: prev_state_out},
    )
    return y.reshape(b * geo.seq_len, D_A), cso, so


def _mlstm_kernel(q_ref, k_ref, v_ref, og_ref, gc_ref, gr_ref, c0_ref, n0_ref, m0_ref,
                  ib_r_ref, ib_c_ref, fb_r_ref, fb_c_ref, nw_ref,
                  h_ref, co_ref, no_ref, mo_ref, *, lc, lv, single_chunk):
    if single_chunk:
        c_ref, n_ref, m_ref = c0_ref, n0_ref, m0_ref
    else:
        @pl.when(pl.program_id(1) == 0)
        def _():
            co_ref[0, 0] = c0_ref[0, 0]
            no_ref[0, 0] = n0_ref[0, 0]
            mo_ref[0, 0] = m0_ref[0, 0]

        c_ref, n_ref, m_ref = co_ref, no_ref, mo_ref
    m_old, n_old = m_ref[0, 0], n_ref[0, 0]

    i0, f0 = H_A, H_A + H_B
    li_c = gc_ref[:, i0:i0 + H_B] + ib_r_ref[...]
    lf_c = _log_sigmoid(gc_ref[:, f0:f0 + H_B] + fb_r_ref[...])
    li_r = gr_ref[0, i0:i0 + H_B, :] + ib_c_ref[...]
    lf_r = _log_sigmoid(gr_ref[0, f0:f0 + H_B, :] + fb_c_ref[...])
    if lv < lc:
        vc = lax.broadcasted_iota(jnp.int32, li_c.shape, 0) < lv
        vr = lax.broadcasted_iota(jnp.int32, li_r.shape, 1) < lv
        li_c, lf_c = jnp.where(vc, li_c, NEG_BIG), jnp.where(vc, lf_c, 0.0)
        li_r, lf_r = jnp.where(vr, li_r, NEG_BIG), jnp.where(vr, lf_r, 0.0)
    lower, upper, mask = _tri(lc)
    bc_c = jnp.dot(lower, lf_c, precision=HI)
    bc_r = jnp.dot(lf_r, upper, precision=HI)

    heads = range(H_B)
    q = [q_ref[:, h * DK_B:(h + 1) * DK_B] * (DK_B ** -0.5) for h in heads]
    k = [k_ref[:, h * DK_B:(h + 1) * DK_B] for h in heads]
    v = [v_ref[:, h * DV_B:(h + 1) * DV_B] for h in heads]
    yield
    qk = [lax.dot_general(q[h], k[h], _NT, preferred_element_type=F32) for h in heads]
    q_c = [jnp.dot(q[h], c_ref[0, 0, h], preferred_element_type=F32) for h in heads]
    yield
    m_t, w_in, w_ts = [], [], []
    for h in heads:
        bcc = bc_c[:, h:h + 1]
        dmat = jnp.where(mask, bcc - bc_r[h:h + 1, :] + li_r[h:h + 1, :], -jnp.inf)
        inter = bcc + m_old[:, h:h + 1]
        m_t.append(jnp.maximum(inter, jnp.max(dmat, axis=1, keepdims=True)))
        w_in.append(jnp.exp(inter - m_t[h]))
        w_ts.append(jnp.exp(dmat - m_t[h]) * qk[h])
    yield
    wv = [jnp.dot(w_ts[h], v[h], preferred_element_type=F32) for h in heads]
    yield
    hs = []
    for h in heads:
        num = wv[h] + w_in[h] * q_c[h]
        den = (jnp.sum(w_ts[h], axis=1, keepdims=True)
               + w_in[h] * jnp.sum(q[h] * n_old[h:h + 1, :], axis=1, keepdims=True))
        hs.append(_rmsnorm_rows(num / jnp.maximum(jnp.abs(den), jnp.exp(-m_t[h]))))
    lane_h = lax.broadcasted_iota(jnp.int32, (1, H_B), 1)
    m_out = jnp.zeros((1, H_B), F32)
    ks, w_c = [], []
    for h in heads:
        bcc = bc_c[:, h:h + 1]
        m_new = m_t[h][lv - 1:lv, :]
        bc_last = bcc[lc - 1:lc, :]
        ks.append(k[h] * jnp.exp(bc_last - bcc + li_c[:, h:h + 1] - m_new))
        w_c.append(jnp.exp(bc_last + m_old[:, h:h + 1] - m_new))
        m_out = jnp.where(lane_h == h, m_new, m_out)
    yield
    kv = [lax.dot_general(ks[h], v[h], _TN, preferred_element_type=F32) for h in heads]
    yield
    for h in heads:
        co_ref[0, 0, h] = w_c[h] * c_ref[0, 0, h] + kv[h]
        no_ref[0, 0, h:h + 1, :] = w_c[h] * n_old[h:h + 1, :] + jnp.sum(ks[h], axis=0, keepdims=True)
    mo_ref[0, 0] = m_out
    hn = jnp.concatenate(hs, axis=1) * nw_ref[...]
    h_ref[...] = (hn * _sigmoid(og_ref[...])).astype(h_ref.dtype)


def _mlstm(p1, gates, gates_t, c_state, n_state, m_state, prm, e, geo, prev_state_out):
    geo = geo._replace(scan_seqs=geo.mlstm_seqs)
    b = geo.n_seq
    const2 = lambda shape: pl.BlockSpec(shape, lambda i, c: (0, 0))
    qk_w = H_B * DK_B
    m4 = m_state.reshape(m_state.shape[0], b, 1, H_B)
    one = lambda shape: (1,) + tuple(shape[1:])
    h, co, no, mo = _seq_call(
        functools.partial(_mlstm_kernel, lc=geo.chunk, lv=geo.valid,
                          single_chunk=geo.seq_len == geo.chunk), geo, "mlstm",
        in_specs=[
            _row_spec(geo, qk_w, 9), _row_spec(geo, qk_w, 10),
            _row_spec(geo, D_B, 1), _row_spec(geo, D_B, 2),
            _row_spec(geo, GATE_W, 0),
            _gates_t_spec(geo),
            _state_spec(c_state.shape, e, geo), _state_spec(n_state.shape, e, geo),
            _state_spec(m4.shape, e, geo),
            const2((1, H_B)), const2((H_B, 1)), const2((1, H_B)), const2((H_B, 1)),
            const2((1, D_B)),
        ],
        args=(_rows3(p1, geo),) * 4 + (_rows3(gates, geo), gates_t, c_state, n_state, m4,
              prm['i_bias'][e].reshape(1, H_B), prm['i_bias'][e].reshape(H_B, 1),
              prm['f_bias'][e].reshape(1, H_B), prm['f_bias'][e].reshape(H_B, 1),
              prm['norm_b'][e].reshape(1, D_B)),
        out_specs=[_row_spec(geo, D_B, 0), _state_spec(c_state.shape, e, geo),
                   _state_spec(one(n_state.shape), 0, geo), _state_spec(one(m4.shape), 0, geo)],
        out_shapes=[
            jax.ShapeDtypeStruct((b, geo.seq_len, D_B), MXU_DTYPE),
            jax.ShapeDtypeStruct(c_state.shape, F32),
            jax.ShapeDtypeStruct(one(n_state.shape), F32),
            jax.ShapeDtypeStruct(one(m4.shape), F32),
        ],
        scratch=[],
        kinds=[ROWS] * 6 + [STATE] * 3 + [CONST] * 5 + [ROWS] + [STATE] * 3,
        stacked_prev={} if prev_state_out is None else {1: prev_state_out},
        interleave=geo.seq_len != geo.chunk,
    )
    return h.reshape(b * geo.seq_len, D_B), co, no, mo.reshape(1, b, H_B)


def _hgrn_kernel(q_ref, f_ref, i_ref, g_ref, lbl_ref, s0_ref, nw_ref, o_ref, so_ref,
                 *, lc, lv, layer_o, bs, single_chunk):
    if single_chunk:
        s_ref = s0_ref
    else:
        @pl.when(pl.program_id(1) == 0)
        def _():
            so_ref[0, 0] = s0_ref[0, 0]

        s_ref = so_ref

    lbl = lbl_ref[...]
    ex = jnp.exp(lbl - jnp.max(lbl, axis=0, keepdims=True))
    sm = ex / jnp.sum(ex, axis=0, keepdims=True)
    lb_all = [sm[0:1, :]]
    for r in range(1, lbl.shape[0]):
        lb_all.append(lb_all[-1] + sm[r:r + 1, :])
    lb = lb_all[layer_o] - lb_all[0]

    fx = f_ref[...]
    e1 = jnp.exp(-jnp.abs(fx))
    log_sig = jnp.minimum(fx, 0.0) - jnp.log1p(e1)
    la = jnp.log(lb)
    lb_ = jnp.log1p(-lb) + log_sig
    logf = jnp.maximum(la, lb_) + jnp.log1p(jnp.exp(-jnp.abs(la - lb_)))
    kk = (1.0 - lb) * (jnp.where(fx >= 0.0, e1, 1.0) / (1.0 + e1))
    if lv < lc:
        valid = lax.broadcasted_iota(jnp.int32, fx.shape, 0) < lv
        logf = jnp.where(valid, logf, 0.0)
        kk = jnp.where(valid, kk, 0.0)

    nb = lc // bs
    r_i = lax.broadcasted_iota(jnp.int32, (lc, lc), 0)
    c_i = lax.broadcasted_iota(jnp.int32, (lc, lc), 1)
    sh = int(math.log2(bs))
    blk_lower = ((c_i <= r_i) & ((c_i >> sh) == (r_i >> sh))).astype(F32)
    gw = jnp.dot(blk_lower, logf, precision=HI)
    q = q_ref[...]
    v = i_ref[...]
    blk = lambda a, i: a[i * bs:(i + 1) * bs, :]
    tots = [gw[(i + 1) * bs - 1:(i + 1) * bs, :] for i in range(nb)]
    before = [jnp.zeros_like(tots[0])]
    for i in range(nb):
        before.append(before[-1] + tots[i])
    g_tot = before[nb]
    qt = q * jnp.exp(gw)
    kt = [blk(kk, j) * jnp.exp(tots[j] - blk(gw, j)) for j in range(nb)]
    q_in = jnp.concatenate([blk(qt, i) * jnp.exp(before[i]) for i in range(nb)], axis=0)
    k_out = jnp.concatenate([kt[j] * jnp.exp(g_tot - before[j + 1]) for j in range(nb)], axis=0)

    heads = range(H_C)
    hsl = lambda h: slice(h * DK_C, (h + 1) * DK_C)
    yield
    o_inter = [jnp.dot(q_in[:, hsl(h)], s_ref[0, 0, h], preferred_element_type=F32)
               for h in heads]
    kv = [lax.dot_general(k_out[:, hsl(h)], v[:, hsl(h)], _TN, preferred_element_type=F32)
          for h in heads]
    att_off = [None]
    for i in range(1, nb):
        k_hat = jnp.concatenate(
            [kt[j] if j == i - 1 else kt[j] * jnp.exp(before[i] - before[j + 1])
             for j in range(i)] + [jnp.zeros((lc - i * bs, D_C), F32)], axis=0)
        q_ti = blk(qt, i)
        att_off.append([lax.dot_general(q_ti[:, hsl(h)], k_hat[:, hsl(h)], _NT,
                                        preferred_element_type=F32) for h in heads])
    yield
    gw2 = gw * LOG2E
    ck = jnp.log2(kk) - gw2
    n_t = bs // SUBLANES
    lane_s = lax.broadcasted_iota(jnp.int32, (SUBLANES, lc), 1)
    row_t = lax.broadcasted_iota(jnp.int32, (bs, lc), 0)
    col_s = lax.broadcasted_iota(jnp.int32, (bs, lc), 1)
    att = []
    for i in range(nb):
        g_i, q_i, ck_i = blk(gw2, i), blk(q, i), blk(ck, i)
        att_d = [[jnp.zeros((SUBLANES, lc), F32) for _ in range(n_t)] for _ in heads]
        for s in range(bs):
            t0 = s // SUBLANES
            p = q_i[t0 * SUBLANES:, :] * jnp.exp2(g_i[t0 * SUBLANES:, :] + ck_i[s:s + 1, :])
            for h in heads:
                a = jnp.sum(p[:, hsl(h)], axis=1, keepdims=True)
                for tt in range(t0, n_t):
                    a_t = a[(tt - t0) * SUBLANES:(tt - t0 + 1) * SUBLANES, :]
                    att_d[h][tt] = jnp.where(lane_s == i * bs + s, a_t, att_d[h][tt])
        causal = (col_s - i * bs) <= row_t
        att_i = []
        for h in heads:
            a = att_d[h][0] if n_t == 1 else jnp.concatenate(att_d[h], axis=0)
            a = jnp.where(causal, a, 0.0)
            att_i.append(a if i == 0 else a + att_off[i][h])
        att.append(att_i)
    yield
    o_intra = [[jnp.dot(att[i][h], v[:, hsl(h)], preferred_element_type=F32) for h in heads]
               for i in range(nb)]
    yield
    outs = []
    for h in heads:
        o_h = o_intra[0][h] if nb == 1 else jnp.concatenate([o_intra[i][h] for i in range(nb)],
                                                             axis=0)
        outs.append(_rmsnorm_rows(o_h + o_inter[h]))
        dec_col = jnp.transpose(jnp.broadcast_to(jnp.exp(g_tot[:, hsl(h)]), (DK_C, DK_C)))
        so_ref[0, 0, h] = dec_col * s_ref[0, 0, h] + kv[h]
    gate = g_ref[...]
    o_ref[...] = (jnp.concatenate(outs, axis=1) * nw_ref[...]
                  * (gate * _sigmoid(gate))).astype(o_ref.dtype)


def _hgrn(p, state, prm, o, geo, prev_state_out):
    geo = geo._replace(scan_seqs=geo.hgrn_seqs)
    n_odd = prm['lb_logits'].shape[0]
    out, new_state = _seq_call(
        functools.partial(_hgrn_kernel, lc=geo.chunk, lv=geo.valid, layer_o=o,
                          bs=min(16, geo.chunk), single_chunk=geo.seq_len == geo.chunk), geo, "hgrn",
        in_specs=[
            _row_spec(geo, D_C, 0), _row_spec(geo, D_C, 1), _row_spec(geo, D_C, 2),
            _row_spec(geo, D_C, 3),
            pl.BlockSpec((n_odd, D_C), lambda i, c: (0, 0)),
            _state_spec(state.shape, o, geo),
            pl.BlockSpec((1, D_C), lambda i, c: (0, 0)),
        ],
        args=(_rows3(p, geo),) * 4 + (prm['lb_logits'], state, prm['norm_c'][o].reshape(1, D_C)),
        out_specs=[_row_spec(geo, D_C, 0), _state_spec(state.shape, o, geo)],
        out_shapes=[
            jax.ShapeDtypeStruct((geo.n_seq, geo.seq_len, D_C), MXU_DTYPE),
            jax.ShapeDtypeStruct(state.shape, F32),
        ],
        scratch=[],
        kinds=[ROWS] * 4 + [CONST, STATE, CONST, ROWS, STATE],
        stacked_prev={} if prev_state_out is None else {1: prev_state_out},
    )
    return out.reshape(geo.n_seq * geo.seq_len, D_C), new_state


def _in_ab_column_groups():
    a0, a1, a2 = D_A, D_A + CONV_DIM_A, D_A + CONV_DIM_A + H_A
    q1 = a2 + H_B * DK_B
    k1 = q1 + H_B * DK_B
    v1 = k1 + D_B
    o1 = v1 + D_B
    end = o1 + 2 * H_B
    return [(0, a0), (k1, v1), (v1, o1), (a0, a1), (a2, q1), (q1, k1)], [(a1, a2), (o1, end)]


def _regroup_kernel(w_ref, o_ref, g_ref):
    wide, narrow = _in_ab_column_groups()
    w = w_ref[0]
    o_ref[0] = jnp.concatenate([w[:, a:b] for a, b in wide], axis=1).astype(o_ref.dtype)
    n_gate = sum(b - a for a, b in narrow)
    pad = jnp.zeros((w.shape[0], GATE_W - n_gate), w.dtype)
    g_ref[0] = jnp.concatenate([w[:, a:b] for a, b in narrow] + [pad], axis=1).astype(g_ref.dtype)


def _regroup_in_ab(wab, rows=256):
    n_even, k, n_in = wab.shape
    wide, _ = _in_ab_column_groups()
    n_wide = sum(b - a for a, b in wide)
    return pl.pallas_call(
        _regroup_kernel,
        grid=(n_even, k // rows),
        in_specs=[pl.BlockSpec((1, rows, n_in), lambda e, r: (e, r, 0))],
        out_specs=[pl.BlockSpec((1, rows, n_wide), lambda e, r: (e, r, 0)),
                   pl.BlockSpec((1, rows, GATE_W), lambda e, r: (e, r, 0))],
        out_shape=[jax.ShapeDtypeStruct((n_even, k, n_wide), MXU_DTYPE),
                   jax.ShapeDtypeStruct((n_even, k, GATE_W), MXU_DTYPE)],
        compiler_params=_cparams(("parallel", "parallel")),
        name="regroup_in_ab",
    )(wab)


def _prep_weights(prm):
    wab = prm['w_in_ab']
    in_ab, gates_ab = _regroup_in_ab(wab)
    n_even, n_odd = wab.shape[0], prm['w_in_c'].shape[0]
    return {
        'in_ab': [[(in_ab, e, 0)] for e in range(n_even)],
        'gates_ab': [[(gates_ab, e, 0)] for e in range(n_even)],
        'out_ab': [[(prm['w_out_ab'], e, 0), (prm['w_out_ab'], e, 1)] for e in range(n_even)],
        'in_c': [[(prm['w_in_c'], o, 0)] for o in range(n_odd)],
        'out_c': [[(prm['w_out_c'], o, 0)] for o in range(n_odd)],
        'ffn_g': [[(prm['w_ffn_g'], l, 0)] for l in range(DEPTH)],
        'ffn_u': [[(prm['w_ffn_u'], l, 0)] for l in range(DEPTH)],
        'ffn_d': [[(prm['w_ffn_d'], l, 0)] for l in range(DEPTH)],
    }


def _trunk(x3, mod, states, prm, w, geo, tn_in, tn_res, tn_ffn, tn_down, emit):
    conv_a, ssd, mem_c, mem_n, mem_m, hgrn, ffn_buf = states
    n_conv, n_n, n_m, n_ffn = [], [], [], []
    new_ssd = new_c = new_hgrn = None
    nc = geo.seq_len // geo.chunk
    wq = {name: list(per_layer) for name, per_layer in w.items()}

    def record(name, idx, emitted):
        if emitted is not None:
            wq[name][idx] = [(arr, 0, 0) for arr in emitted]

    casts = lambda wops: emit and any(wop[0].dtype != MXU_DTYPE for wop in wops)
    for layer in range(DEPTH):
        mm_norm = functools.partial(_mm_norm, x3, prm['norm_mix'][layer], mod, layer, 1, 0,
                                    geo=geo)
        if layer % 2 == 0:
            e = layer // 2
            p1, gates, _ = mm_norm(w['in_ab'][e][0], tn=tn_in[layer], narrow=w['gates_ab'][e][0])
            gates_t = gates.reshape(geo.n_seq, nc, geo.chunk, GATE_W).transpose(0, 1, 3, 2)
            ya, cv, new_ssd = _ssd(p1, gates, gates_t, conv_a, ssd, prm, e, geo, new_ssd)
            hb, new_c, nn, mmm = _mlstm(p1, gates, gates_t, mem_c, mem_n, mem_m, prm, e, geo, new_c)
            n_conv.append(cv); n_n.append(nn); n_m.append(mmm)
            x3, em = _mm_res([ya, hb], w['out_ab'][e], x3, mod, layer, 2, geo, tn_res[layer],
                             emit=casts(w['out_ab'][e]))
            record('out_ab', e, em)
        else:
            o = layer // 2
            p, _, em = mm_norm(w['in_c'][o][0], tn=tn_in[layer], emit=casts(w['in_c'][o]))
            record('in_c', o, None if em is None else [em])
            oc, new_hgrn = _hgrn(p, hgrn, prm, o, geo, new_hgrn)
            x3, em = _mm_res([oc], w['out_c'][o], x3, mod, layer, 2, geo, tn_res[layer],
                             emit=casts(w['out_c'][o]))
            record('out_c', o, em)
        act, fb, em = _ffn_in(x3, prm['norm_ffn'][layer], mod, layer, w['ffn_g'][layer][0],
                              w['ffn_u'][layer][0], ffn_buf, prm['conv_w_f'][layer],
                              prm['conv_b_f'][layer], geo, tn_ffn,
                              emit=casts(w['ffn_g'][layer] + w['ffn_u'][layer]))
        if em is not None:
            record('ffn_g', layer, em[:1])
            record('ffn_u', layer, em[1:])
        n_ffn.append(fb)
        x3, em = _mm_res([act], w['ffn_d'][layer], x3, mod, layer, 5, geo, tn_down,
                         emit=casts(w['ffn_d'][layer]))
        record('ffn_d', layer, em)
    y = _final_norm(x3, prm['norm_f'], geo)
    cat = lambda xs: jnp.concatenate(xs, axis=0)
    return (y, cat(n_conv), new_ssd.reshape(ssd.shape), new_c, cat(n_n), cat(n_m), new_hgrn,
            cat(n_ffn)), wq


def kernel(x_prompt, x_sample, c_prompt, c_sample, state_ssd_conv, state_ssd, state_mlstm_c, state_mlstm_n, state_mlstm_m, state_hgrn, state_ffn_conv, w_ada, b_ada, norm_mix, norm_ffn, w_in_ab, conv_w_a, conv_b_a, dt_bias, a_log, d_skip, norm_a, i_bias, f_bias, norm_b, w_out_ab, w_in_c, lb_logits, norm_c, w_out_c, w_ffn_g, w_ffn_u, conv_w_f, conv_b_f, w_ffn_d, norm_f):
    prm = dict(norm_mix=norm_mix, norm_ffn=norm_ffn, w_in_ab=w_in_ab, conv_w_a=conv_w_a,
               conv_b_a=conv_b_a, dt_bias=dt_bias, a_log=a_log, d_skip=d_skip, norm_a=norm_a,
               i_bias=i_bias, f_bias=f_bias, norm_b=norm_b, w_out_ab=w_out_ab, w_in_c=w_in_c,
               lb_logits=lb_logits, norm_c=norm_c, w_out_c=w_out_c, w_ffn_g=w_ffn_g,
               w_ffn_u=w_ffn_u, conv_w_f=conv_w_f, conv_b_f=conv_b_f, w_ffn_d=w_ffn_d,
               norm_f=norm_f)
    bp, lp, _ = x_prompt.shape
    bs, ls, _ = x_sample.shape
    n_even, n_odd = state_ssd.shape[0], state_hgrn.shape[0]
    w = _prep_weights(prm)

    n_c = bs + bp
    n_c_pad = -(-n_c // SUBLANES) * SUBLANES
    c_all = jnp.pad(jnp.concatenate([c_sample, c_prompt], axis=0), ((0, n_c_pad - n_c), (0, 0)))
    mod = _ada(c_all, w_ada, b_ada).reshape(DEPTH, n_c_pad, 1, 6 * D_MODEL)

    geo_s = Geo(n_seq=bs, seq_len=SUBLANES, seq_blk=bs, row_blk=SUBLANES, chunk=SUBLANES,
                valid=ls, mod_off=0, scan_seqs=8, mlstm_seqs=4, hgrn_seqs=8)
    xs = jnp.pad(x_sample, ((0, 0), (0, SUBLANES - ls), (0, 0)))
    st_s = (state_ssd_conv, state_ssd, state_mlstm_c, state_mlstm_n, state_mlstm_m, state_hgrn,
            state_ffn_conv)
    out_s, w_bf16 = _trunk(xs, mod, st_s, prm, w, geo_s, tn_in=(512,) * DEPTH,
                           tn_res=(512,) * DEPTH, tn_ffn=256, tn_down=256, emit=True)

    zeros = lambda *s: jnp.zeros(s, F32)
    st_p = (zeros(n_even, bp, CONV_K_A - 1, CONV_DIM_A), zeros(n_even, bp, H_A, P_A, N_A),
            zeros(n_even, bp, H_B, DK_B, DV_B), zeros(n_even, bp, H_B, DK_B),
            zeros(n_even, bp, H_B), zeros(n_odd, bp, H_C, DK_C, DV_C),
            zeros(DEPTH, bp, CONV_K_F - 1, D_FF))
    lc_p = math.gcd(lp, PROMPT_CHUNK)
    geo_p = Geo(n_seq=bp, seq_len=lp, seq_blk=1, row_blk=min(lp, 1024), chunk=lc_p, valid=lc_p,
                mod_off=bs, scan_seqs=1, mlstm_seqs=1, hgrn_seqs=2 if bp % 2 == 0 else 1)
    out_p, _ = _trunk(x_prompt, mod, st_p, prm, w_bf16, geo_p, tn_in=(1024,) * DEPTH,
                      tn_res=(1024,) * DEPTH, tn_ffn=FFN_TN, tn_down=FFN_DOWN_TN, emit=False)
    return (out_p[0], out_s[0][:, :ls]) + tuple(out_p[1:]) + tuple(out_s[1:])
```

```python
import collections
import functools
import math

import jax
import jax.numpy as jnp
from jax import lax
from jax.experimental import pallas as pl
from jax.experimental.pallas import tpu as pltpu

F32 = jnp.float32
MXU_DTYPE = jnp.bfloat16
HI = lax.Precision.HIGHEST
NEG_BIG = -1e30
LOG2E = 1.4426950408889634

D_MODEL = 2048
DEPTH = 4
EPS = 1e-6
PROMPT_CHUNK = 64
H_A, P_A, G_A, N_A, CONV_K_A = 32, 64, 4, 128, 4
D_A = H_A * P_A
CONV_DIM_A = D_A + 2 * G_A * N_A
H_B, DK_B, DV_B = 8, 128, 256
D_B = H_B * DV_B
H_C, DK_C, DV_C = 16, 128, 128
D_C = H_C * DV_C
D_FF, CONV_K_F = 5632, 3
FFN_TN = 512
FFN_DOWN_TN = 512
PROLOGUE_SPLIT = 4
GATE_W = 128
SUBLANES = 8
VMEM_LIMIT = 56 * 1024 * 1024

_NT = (((1,), (1,)), ((), ()))
_TN = (((0,), (0,)), ((), ()))

Geo = collections.namedtuple(
    "Geo", "n_seq seq_len seq_blk row_blk chunk valid mod_off scan_seqs mlstm_seqs hgrn_seqs")


def _cparams(sem):
    return pltpu.CompilerParams(dimension_semantics=sem, vmem_limit_bytes=VMEM_LIMIT)


def _sigmoid(x):
    return 1.0 / (1.0 + jnp.exp(-x))


def _softplus(x):
    return jnp.maximum(x, 0.0) + jnp.log1p(jnp.exp(-jnp.abs(x)))


def _log_sigmoid(x):
    return jnp.minimum(x, 0.0) - jnp.log1p(jnp.exp(-jnp.abs(x)))


def _tri(n):
    r = lax.broadcasted_iota(jnp.int32, (n, n), 0)
    c = lax.broadcasted_iota(jnp.int32, (n, n), 1)
    mask = c <= r
    return mask.astype(F32), (r <= c).astype(F32), mask


def _shifted(x, tails, k, axis=0):
    rolled = pltpu.roll(x, k, axis)
    head = lax.slice_in_dim(rolled, 0, SUBLANES, axis=axis)
    row = lax.broadcasted_iota(jnp.int32, head.shape, axis)
    for r in range(k):
        head = jnp.where(row == r, tails[len(tails) - k + r], head)
    if x.shape[axis] == SUBLANES:
        return head
    rest = lax.slice_in_dim(rolled, SUBLANES, x.shape[axis], axis=axis)
    return jnp.concatenate([head, rest], axis=axis)


def _drop_ref(fn, idx):
    def wrapped(*refs):
        return fn(*refs[:idx], *refs[idx + 1:])
    return wrapped


def _rmsnorm_rows(x):
    return x * lax.rsqrt(jnp.mean(x * x, axis=-1, keepdims=True) + EPS)


def _ada_kernel(c_ref, w_ref, b_ref, o_ref):
    c = c_ref[...]
    ca = (c * _sigmoid(c)).astype(MXU_DTYPE)
    o_ref[0] = jnp.dot(ca, w_ref[0].astype(MXU_DTYPE), preferred_element_type=F32) + b_ref[0]


def _ada(c_all, w_ada, b_ada, tn=1024):
    rows = c_all.shape[0]
    n = w_ada.shape[2]
    return pl.pallas_call(
        _ada_kernel,
        grid=(DEPTH, n // tn),
        in_specs=[
            pl.BlockSpec((rows, D_MODEL), lambda l, j: (0, 0)),
            pl.BlockSpec((1, D_MODEL, tn), lambda l, j: (l, 0, j)),
            pl.BlockSpec((1, 1, tn), lambda l, j: (l, 0, j)),
        ],
        out_specs=pl.BlockSpec((1, rows, tn), lambda l, j: (l, 0, j)),
        out_shape=jax.ShapeDtypeStruct((DEPTH, rows, n), F32),
        compiler_params=_cparams(("parallel", "parallel")),
        name="ada",
    )(c_all, w_ada, b_ada.reshape(DEPTH, 1, n))


def _tiles(geo):
    nrt = geo.seq_len // geo.row_blk
    return nrt, (geo.n_seq // geo.seq_blk) * nrt, geo.seq_blk * geo.row_blk


def _x_spec(geo, width, col_of):
    nrt = geo.seq_len // geo.row_blk
    return pl.BlockSpec((geo.seq_blk, geo.row_blk, width),
                        lambda i, j: (i // nrt, i % nrt, col_of(j)))


def _mod_spec(geo, layer, width, col_of):
    nrt = geo.seq_len // geo.row_blk
    return pl.BlockSpec((1, geo.seq_blk, 1, width),
                        lambda i, j: (layer, geo.mod_off + i // nrt, 0, col_of(j)))


def _norm_mod_chunks(x_ref, nw_ref, sc_ref, sh_ref):
    seq_blk, row_blk, d = x_ref.shape
    for r in range(PROLOGUE_SPLIT):
        if seq_blk == 1:
            n = row_blk // PROLOGUE_SPLIT
            x, sc, sh = x_ref[:, r * n:(r + 1) * n], sc_ref[0], sh_ref[0]
        else:
            n = seq_blk // PROLOGUE_SPLIT
            x, sc, sh = x_ref[r * n:(r + 1) * n], sc_ref[0, r * n:(r + 1) * n], sh_ref[0, r * n:(r + 1) * n]
        h = (_rmsnorm_rows(x) * nw_ref[...]) * (1.0 + sc) + sh
        rows = h.shape[0] * h.shape[1]
        yield slice(r * rows, (r + 1) * rows), h.reshape(rows, d).astype(MXU_DTYPE)


def _weight_spec(wop, k, tn):
    _, layer, row_block = wop
    return pl.BlockSpec((1, k, tn), lambda i, j: (layer, row_block, j))


def _emit_spec_shape(k, n, tn):
    return (pl.BlockSpec((1, k, tn), lambda i, j: (0, 0, j)),
            jax.ShapeDtypeStruct((1, k, n), MXU_DTYPE))


def _weight_tile(w_ref, wq_ref):
    w = w_ref[0].astype(MXU_DTYPE)
    if wq_ref is not None:
        wq_ref[0] = w
    return w


def _mm_norm_kernel(*refs, emit, narrow):
    x_ref, nw_ref, sc_ref, sh_ref, w_ref = refs[:5]
    rest = list(refs[5:])
    wn_ref = rest.pop(0) if narrow else None
    o_ref = rest.pop(0)
    on_ref = rest.pop(0) if narrow else None
    wq_ref = rest.pop(0) if emit else None
    (h_scr,) = rest
    w = _weight_tile(w_ref, wq_ref)

    @pl.when(pl.program_id(1) == 0)
    def _():
        for rows, h in _norm_mod_chunks(x_ref, nw_ref, sc_ref, sh_ref):
            h_scr[rows, :] = h
            o_ref[rows, :] = jnp.dot(h, w, preferred_element_type=F32)
            if narrow:
                on_ref[rows, :] = jnp.dot(h, wn_ref[0], preferred_element_type=F32)

    @pl.when(pl.program_id(1) > 0)
    def _():
        o_ref[...] = jnp.dot(h_scr[...], w, preferred_element_type=F32)


def _mm_norm(x3, nw, mod, layer, k_sc, k_sh, wop, geo, tn, emit=False, narrow=None):
    _, n_i, tm = _tiles(geo)
    n = wop[0].shape[2]
    in_specs = [
        _x_spec(geo, D_MODEL, lambda j: 0),
        pl.BlockSpec((1, D_MODEL), lambda i, j: (0, 0)),
        _mod_spec(geo, layer, D_MODEL, lambda j: k_sc),
        _mod_spec(geo, layer, D_MODEL, lambda j: k_sh),
        _weight_spec(wop, D_MODEL, tn),
    ]
    args = [x3, nw.reshape(1, D_MODEL), mod, mod, wop[0]]
    out_specs = [pl.BlockSpec((tm, tn), lambda i, j: (i, j))]
    out_shape = [jax.ShapeDtypeStruct((geo.n_seq * geo.seq_len, n), F32)]
    if narrow is not None:
        n_narrow = narrow[0].shape[2]
        in_specs.append(pl.BlockSpec((1, D_MODEL, n_narrow), lambda i, j: (narrow[1], 0, 0)))
        args.append(narrow[0])
        out_specs.append(pl.BlockSpec((tm, n_narrow), lambda i, j: (i, 0)))
        out_shape.append(jax.ShapeDtypeStruct((geo.n_seq * geo.seq_len, n_narrow), F32))
    if emit:
        assert n_i == 1
        spec, shape = _emit_spec_shape(D_MODEL, n, tn)
        out_specs.append(spec)
        out_shape.append(shape)
    outs = list(pl.pallas_call(
        functools.partial(_mm_norm_kernel, emit=emit, narrow=narrow is not None),
        grid=(n_i, n // tn),
        in_specs=in_specs,
        out_specs=out_specs,
        out_shape=out_shape,
        scratch_shapes=[pltpu.VMEM((tm, D_MODEL), MXU_DTYPE)],
        compiler_params=_cparams(("parallel", "arbitrary")),
        name="mm_norm",
    )(*args))
    out = outs.pop(0)
    out_narrow = outs.pop(0) if narrow is not None else None
    return out, out_narrow, (outs.pop(0) if emit else None)


def _mm_res_kernel(*refs, n_lhs, emit):
    a_refs, w_refs = refs[:n_lhs], refs[n_lhs:2 * n_lhs]
    xres_ref, gate_ref, o_ref = refs[2 * n_lhs:2 * n_lhs + 3]
    wq_refs = refs[2 * n_lhs + 3:] if emit else (None,) * n_lhs
    acc = None
    for a_ref, w_ref, wq_ref in zip(a_refs, w_refs, wq_refs):
        d = jnp.dot(a_ref[...], _weight_tile(w_ref, wq_ref), preferred_element_type=F32)
        acc = d if acc is None else acc + d
    o_ref[...] = xres_ref[...] + gate_ref[0] * acc.reshape(o_ref.shape)


def _mm_res(a_list, wops, x3, mod, layer, k_gate, geo, tn, emit=False):
    _, n_i, tm = _tiles(geo)
    per = D_MODEL // tn
    in_specs = [pl.BlockSpec((tm, a.shape[1]), lambda i, j: (i, 0)) for a in a_list]
    in_specs += [_weight_spec(wop, a.shape[1], tn) for wop, a in zip(wops, a_list)]
    in_specs += [_x_spec(geo, tn, lambda j: j),
                 _mod_spec(geo, layer, tn, lambda j: k_gate * per + j)]
    out_specs = [_x_spec(geo, tn, lambda j: j)]
    out_shape = [jax.ShapeDtypeStruct(x3.shape, F32)]
    if emit:
        assert n_i == 1
        for a in a_list:
            spec, shape = _emit_spec_shape(a.shape[1], D_MODEL, tn)
            out_specs.append(spec)
            out_shape.append(shape)
    outs = pl.pallas_call(
        functools.partial(_mm_res_kernel, n_lhs=len(a_list), emit=emit),
        grid=(n_i, per),
        in_specs=in_specs,
        out_specs=out_specs,
        out_shape=out_shape,
        compiler_params=_cparams(("parallel", "parallel")),
        name="mm_res",
    )(*a_list, *[wop[0] for wop in wops], x3, mod)
    return outs[0], (list(outs[1:]) if emit else None)


def _ffn_in_kernel(x_ref, nw_ref, sc_ref, sh_ref, wg_ref, wu_ref, cs_ref, cw_ref, cb_ref,
                   a_ref, cso_ref, *rest, nrt, valid, emit):
    wgq_ref, wuq_ref, h_scr, tail_scr = rest if emit else (None, None) + rest
    i, j = pl.program_id(0), pl.program_id(1)

    seq_blk, row_blk, tn = x_ref.shape[0], x_ref.shape[1], a_ref.shape[1]
    w_g, w_u = _weight_tile(wg_ref, wgq_ref), _weight_tile(wu_ref, wuq_ref)

    def conv_gate(g, u):
        g, u = g.reshape(seq_blk, row_blk, tn), u.reshape(seq_blk, row_blk, tn)
        prev = cs_ref[0]
        if nrt > 1:
            prev = jnp.where(i % nrt == 0, prev, tail_scr[j])
        t2, t1 = prev[:, 0:1, :], prev[:, 1:2, :]
        w = cw_ref[...]
        y = (cb_ref[...] + w[0:1] * _shifted(g, [t2, t1], 2, axis=1)
             + w[1:2] * _shifted(g, [t2, t1], 1, axis=1) + w[2:3] * g)
        a_ref[...] = (y * _sigmoid(y) * u).reshape(a_ref.shape).astype(a_ref.dtype)
        new_tail = g[:, valid - 2:valid, :]
        if nrt > 1:
            tail_scr[j] = new_tail
        cso_ref[0] = new_tail

    @pl.when(j == 0)
    def _():
        gs, us = [], []
        for rows, h in _norm_mod_chunks(x_ref, nw_ref, sc_ref, sh_ref):
            h_scr[rows, :] = h
            gs.append(jnp.dot(h, w_g, preferred_element_type=F32))
            us.append(jnp.dot(h, w_u, preferred_element_type=F32))
        conv_gate(jnp.concatenate(gs, axis=0), jnp.concatenate(us, axis=0))

    @pl.when(j > 0)
    def _():
        h = h_scr[...]
        conv_gate(jnp.dot(h, w_g, preferred_element_type=F32),
                  jnp.dot(h, w_u, preferred_element_type=F32))


def _ffn_in(x3, nw, mod, layer, wop_g, wop_u, conv_state, conv_w, conv_b, geo, tn, emit=False):
    nrt, n_i, tm = _tiles(geo)
    n_j = D_FF // tn
    valid = geo.row_blk if geo.valid == geo.chunk else geo.valid
    tail_shape = (n_j, geo.seq_blk, CONV_K_F - 1, tn) if nrt > 1 else (1, 1, CONV_K_F - 1, 128)
    cs_spec = pl.BlockSpec((1, geo.seq_blk, CONV_K_F - 1, tn), lambda i, j: (layer, i // nrt, 0, j))
    emit_specs, emit_shapes = [], []
    if emit:
        assert n_i == 1
        for _ in range(2):
            spec, shape = _emit_spec_shape(D_MODEL, D_FF, tn)
            emit_specs.append(spec)
            emit_shapes.append(shape)
    act, tails, *emitted = pl.pallas_call(
        functools.partial(_ffn_in_kernel, nrt=nrt, valid=valid, emit=emit),
        grid=(n_i, n_j),
        in_specs=[
            _x_spec(geo, D_MODEL, lambda j: 0),
            pl.BlockSpec((1, D_MODEL), lambda i, j: (0, 0)),
            _mod_spec(geo, layer, D_MODEL, lambda j: 4),
            _mod_spec(geo, layer, D_MODEL, lambda j: 3),
            _weight_spec(wop_g, D_MODEL, tn),
            _weight_spec(wop_u, D_MODEL, tn),
            cs_spec,
            pl.BlockSpec((CONV_K_F, tn), lambda i, j: (0, j)),
            pl.BlockSpec((1, tn), lambda i, j: (0, j)),
        ],
        out_specs=[
            pl.BlockSpec((tm, tn), lambda i, j: (i, j)),
            pl.BlockSpec((1, geo.seq_blk, CONV_K_F - 1, tn), lambda i, j: (i, 0, 0, j)),
        ] + emit_specs,
        out_shape=[
            jax.ShapeDtypeStruct((geo.n_seq * geo.seq_len, D_FF), MXU_DTYPE),
            jax.ShapeDtypeStruct((n_i, geo.seq_blk, CONV_K_F - 1, D_FF), F32),
        ] + emit_shapes,
        scratch_shapes=[pltpu.VMEM((tm, D_MODEL), MXU_DTYPE),
                        pltpu.VMEM(tail_shape, F32)],
        compiler_params=_cparams(("arbitrary", "arbitrary")),
        name="ffn_in",
    )(x3, nw.reshape(1, D_MODEL), mod, mod, wop_g[0], wop_u[0], conv_state, conv_w,
      conv_b.reshape(1, D_FF))
    last = tails.reshape(n_i // nrt, nrt, geo.seq_blk, CONV_K_F - 1, D_FF)[:, nrt - 1]
    return act, last.reshape(1, geo.n_seq, CONV_K_F - 1, D_FF), (emitted if emit else None)


def _norm_kernel(x_ref, nw_ref, o_ref):
    o_ref[...] = _rmsnorm_rows(x_ref[...]) * nw_ref[...]


def _final_norm(x3, nw, geo):
    _, n_i, _ = _tiles(geo)
    return pl.pallas_call(
        _norm_kernel,
        grid=(n_i, 1),
        in_specs=[_x_spec(geo, D_MODEL, lambda j: 0),
                  pl.BlockSpec((1, D_MODEL), lambda i, j: (0, 0))],
        out_specs=_x_spec(geo, D_MODEL, lambda j: 0),
        out_shape=jax.ShapeDtypeStruct(x3.shape, F32),
        compiler_params=_cparams(("parallel", "arbitrary")),
        name="final_norm",
    )(x3, nw.reshape(1, D_MODEL))


def _state_spec(shape, layer, geo):
    rest = tuple(shape[2:])
    zeros = (0,) * len(rest)
    return pl.BlockSpec((1, geo.scan_seqs) + rest, lambda i, c: (layer, i) + zeros)


def _row_spec(geo, width, col):
    return pl.BlockSpec((geo.scan_seqs, geo.chunk, width), lambda i, c: (i, c, col))


def _rows3(a, geo):
    return a.reshape(geo.n_seq, geo.seq_len, a.shape[-1])


def _gates_t_spec(geo):
    return pl.BlockSpec((geo.scan_seqs, 1, GATE_W, geo.chunk), lambda i, c: (i, c, 0, 0))


ROWS, LEAD, STATE, CONST = "rows", "lead", "state", "const"


def _per_sequence(body, geo, kinds, interleave):
    def view(ref, kind, k):
        if kind == ROWS:
            return ref.at[k]
        if kind == LEAD:
            return ref.at[pl.ds(k, 1)]
        if kind == STATE:
            return ref.at[:, pl.ds(k, 1)]
        return ref

    def wrapped(*refs):
        running = [body(*[view(r, kind, k) for r, kind in zip(refs, kinds)])
                   for k in range(geo.scan_seqs)]
        if not interleave:
            for gen in running:
                for _ in gen:
                    pass
            return
        while running:
            still = []
            for gen in running:
                if next(gen, StopIteration) is not StopIteration:
                    still.append(gen)
            running = still
    return wrapped


def _seq_call(kernel_fn, geo, name, in_specs, args, out_specs, out_shapes, scratch, kinds,
              stacked_prev, interleave=True):
    nc = geo.seq_len // geo.chunk
    kernel_fn = _per_sequence(kernel_fn, geo, kinds, interleave)
    in_specs, args = list(in_specs), list(args)
    aliases = {}
    assert len(stacked_prev) <= 1
    for out_idx, arr in stacked_prev.items():
        kernel_fn = _drop_ref(kernel_fn, len(in_specs))
        aliases[len(in_specs)] = out_idx
        in_specs.append(pl.BlockSpec(memory_space=pl.ANY))
        args.append(arr)
    return pl.pallas_call(
        kernel_fn,
        grid=(geo.n_seq // geo.scan_seqs, nc),
        in_specs=in_specs,
        out_specs=out_specs,
        out_shape=out_shapes,
        scratch_shapes=scratch,
        input_output_aliases=aliases,
        compiler_params=_cparams(("parallel", "arbitrary")),
        name=name,
    )(*args)


def _ssd_kernel(z_ref, xbc_ref, gc_ref, gr_ref, cs_ref, s0_ref, cw_ref, cb_ref,
                dtb_r_ref, dtb_c_ref, al_r_ref, al_c_ref, dsk_ref, nw_ref, exp_ref,
                y_ref, cso_ref, so_ref, tail_scr, *, lc, lv, single_chunk):
    if single_chunk:
        s_ref = s0_ref
        cs = cs_ref[0, 0]
        tails = [cs[0:1, :], cs[1:2, :], cs[2:3, :]]
    else:
        @pl.when(pl.program_id(1) == 0)
        def _():
            tail_scr[0, 5:8, :] = cs_ref[0, 0]
            so_ref[0, 0] = s0_ref[0, 0]

        s_ref = so_ref
        tails = [tail_scr[0, 5:6, :], tail_scr[0, 6:7, :], tail_scr[0, 7:8, :]]

    x = xbc_ref[...]
    w = cw_ref[0]
    xc = (cb_ref[0] + w[0:1] * _shifted(x, tails, 3) + w[1:2] * _shifted(x, tails, 2)
          + w[2:3] * _shifted(x, tails, 1) + w[3:4] * x)
    xc = xc * _sigmoid(xc)
    new_tail = x[lv - 3:lv, :]
    if not single_chunk:
        tail_scr[0, 5:8, :] = new_tail
    cso_ref[0, 0] = new_tail
    xa = xc[:, :D_A]
    bm = xc[:, D_A:D_A + G_A * N_A]
    cm = xc[:, D_A + G_A * N_A:]

    dt_c = _softplus(gc_ref[:, 0:H_A] + dtb_r_ref[...])
    dt_r = _softplus(gr_ref[0, 0:H_A, :] + dtb_c_ref[...])
    if lv < lc:
        dt_c = jnp.where(lax.broadcasted_iota(jnp.int32, dt_c.shape, 0) < lv, dt_c, 0.0)
        dt_r = jnp.where(lax.broadcasted_iota(jnp.int32, dt_r.shape, 1) < lv, dt_r, 0.0)
    lower, upper, mask = _tri(lc)
    cum_c = jnp.dot(lower, dt_c * (-jnp.exp(al_r_ref[...])), precision=HI)
    cum_r = jnp.dot(dt_r * (-jnp.exp(al_c_ref[...])), upper, precision=HI)
    cum_last = cum_c[lc - 1:lc, :]
    expand = exp_ref[...]
    ecum_x = jnp.dot(jnp.exp(cum_c), expand, precision=HI)
    tail_x = jnp.dot(jnp.exp(cum_last - cum_c) * dt_c, expand, precision=HI)

    hg = H_A // G_A
    gw = hg * P_A
    groups = range(G_A)
    gsl = lambda g: slice(g * gw, (g + 1) * gw)
    cg = [cm[:, g * N_A:(g + 1) * N_A] for g in groups]
    bg = [bm[:, g * N_A:(g + 1) * N_A] for g in groups]
    yield
    cb_ts = [lax.dot_general(cg[g], bg[g], _NT, preferred_element_type=F32) for g in groups]
    y_inter = [lax.dot_general(cg[g], s_ref[0, 0, g], _NT, preferred_element_type=F32)
               for g in groups]
    upd = [lax.dot_general(xa[:, gsl(g)] * tail_x[:, gsl(g)], bg[g], _TN,
                           preferred_element_type=F32) for g in groups]
    yield
    w_ts = [cb_ts[h // hg] * dt_r[h:h + 1, :]
            * jnp.exp(jnp.where(mask, cum_c[:, h:h + 1] - cum_r[h:h + 1, :], -jnp.inf))
            for h in range(H_A)]
    yield
    pieces = []
    for j in range(H_A // 2):
        xp = xa[:, j * 128:(j + 1) * 128]
        lane = lax.broadcasted_iota(jnp.int32, xp.shape, 1)
        pieces.append(
            jnp.dot(w_ts[2 * j], jnp.where(lane < P_A, xp, 0.0), preferred_element_type=F32)
            + jnp.dot(w_ts[2 * j + 1], jnp.where(lane >= P_A, xp, 0.0),
                      preferred_element_type=F32))
    yield
    y = (jnp.concatenate(pieces, axis=1) + jnp.concatenate(y_inter, axis=1) * ecum_x
         + dsk_ref[...] * xa)
    for g in groups:
        for hh in range(hg):
            h = g * hg + hh
            rows = slice(hh * P_A, (hh + 1) * P_A)
            so_ref[0, 0, g, rows, :] = (s_ref[0, 0, g, rows, :] * jnp.exp(cum_r[h:h + 1, lc - 1:lc])
                                        + upd[g][rows, :])

    z = z_ref[...]
    y = y * (z * _sigmoid(z))
    y = jnp.concatenate([_rmsnorm_rows(y[:, g * gw:(g + 1) * gw]) for g in range(G_A)], axis=1)
    y_ref[...] = (y * nw_ref[...]).astype(y_ref.dtype)


def _ssd(p1, gates, gates_t, conv_state, ssd_state, prm, e, geo, prev_state_out):
    b = geo.n_seq
    const2 = lambda shape: pl.BlockSpec(shape, lambda i, c: (0, 0))
    n_even = ssd_state.shape[0]
    s5 = ssd_state.reshape(n_even, b, G_A, (H_A // G_A) * P_A, N_A)
    expand = (jnp.arange(D_A)[None, :] // P_A == jnp.arange(H_A)[:, None]).astype(F32)
    cso_shape = (1,) + conv_state.shape[1:]
    y, cso, so = _seq_call(
        functools.partial(_ssd_kernel, lc=geo.chunk, lv=geo.valid,
                          single_chunk=geo.seq_len == geo.chunk), geo, "ssd",
        in_specs=[
            _row_spec(geo, D_A, 0),
            _row_spec(geo, CONV_DIM_A, 2),
            _row_spec(geo, GATE_W, 0),
            _gates_t_spec(geo),
            _state_spec(conv_state.shape, e, geo),
            _state_spec(s5.shape, e, geo),
            pl.BlockSpec((1, CONV_K_A, CONV_DIM_A), lambda i, c: (e, 0, 0)),
            pl.BlockSpec((1, 1, CONV_DIM_A), lambda i, c: (e, 0, 0)),
            const2((1, H_A)), const2((H_A, 1)), const2((1, H_A)), const2((H_A, 1)),
            const2((1, D_A)), const2((1, D_A)), const2((H_A, D_A)),
        ],
        args=(_rows3(p1, geo), _rows3(p1, geo), _rows3(gates, geo), gates_t, conv_state, s5,
              prm['conv_w_a'], prm['conv_b_a'].reshape(n_even, 1, CONV_DIM_A),
              prm['dt_bias'][e].reshape(1, H_A), prm['dt_bias'][e].reshape(H_A, 1),
              prm['a_log'][e].reshape(1, H_A), prm['a_log'][e].reshape(H_A, 1),
              jnp.repeat(prm['d_skip'][e], P_A).reshape(1, D_A),
              prm['norm_a'][e].reshape(1, D_A), expand),
        out_specs=[_row_spec(geo, D_A, 0), _state_spec(cso_shape, 0, geo),
                   _state_spec(s5.shape, e, geo)],
        out_shapes=[
            jax.ShapeDtypeStruct((b, geo.seq_len, D_A), MXU_DTYPE),
            jax.ShapeDtypeStruct(cso_shape, F32),
            jax.ShapeDtypeStruct(s5.shape, F32),
        ],
        scratch=[pltpu.VMEM((geo.scan_seqs, SUBLANES, CONV_DIM_A), F32)],
        kinds=[ROWS] * 4 + [STATE, STATE] + [CONST] * 9 + [ROWS, STATE, STATE, LEAD],
        stacked_prev={} if prev_state_out is None else {2: prev_state_out},
    )
    return y.reshape(b * geo.seq_len, D_A), cso, so


def _mlstm_kernel(q_ref, k_ref, v_ref, og_ref, gc_ref, gr_ref, c0_ref, n0_ref, m0_ref,
                  ib_r_ref, ib_c_ref, fb_r_ref, fb_c_ref, nw_ref,
                  h_ref, co_ref, no_ref, mo_ref, *, lc, lv, single_chunk):
    if single_chunk:
        c_ref, n_ref, m_ref = c0_ref, n0_ref, m0_ref
    else:
        @pl.when(pl.program_id(1) == 0)
        def _():
            co_ref[0, 0] = c0_ref[0, 0]
            no_ref[0, 0] = n0_ref[0, 0]
            mo_ref[0, 0] = m0_ref[0, 0]

        c_ref, n_ref, m_ref = co_ref, no_ref, mo_ref
    m_old, n_old = m_ref[0, 0], n_ref[0, 0]

    i0, f0 = H_A, H_A + H_B
    li_c = gc_ref[:, i0:i0 + H_B] + ib_r_ref[...]
    lf_c = _log_sigmoid(gc_ref[:, f0:f0 + H_B] + fb_r_ref[...])
    li_r = gr_ref[0, i0:i0 + H_B, :] + ib_c_ref[...]
    lf_r = _log_sigmoid(gr_ref[0, f0:f0 + H_B, :] + fb_c_ref[...])
    if lv < lc:
        vc = lax.broadcasted_iota(jnp.int32, li_c.shape, 0) < lv
        vr = lax.broadcasted_iota(jnp.int32, li_r.shape, 1) < lv
        li_c, lf_c = jnp.where(vc, li_c, NEG_BIG), jnp.where(vc, lf_c, 0.0)
        li_r, lf_r = jnp.where(vr, li_r, NEG_BIG), jnp.where(vr, lf_r, 0.0)
    lower, upper, mask = _tri(lc)
    bc_c = jnp.dot(lower, lf_c, precision=HI)
    bc_r = jnp.dot(lf_r, upper, precision=HI)

    heads = range(H_B)
    q = [q_ref[:, h * DK_B:(h + 1) * DK_B] * (DK_B ** -0.5) for h in heads]
    k = [k_ref[:, h * DK_B:(h + 1) * DK_B] for h in heads]
    v = [v_ref[:, h * DV_B:(h + 1) * DV_B] for h in heads]
    yield
    qk = [lax.dot_general(q[h], k[h], _NT, preferred_element_type=F32) for h in heads]
    yield
    m_t, w_in, w_ts = [], [], []
    for h in heads:
        bcc = bc_c[:, h:h + 1]
        dmat = jnp.where(mask, bcc - bc_r[h:h + 1, :] + li_r[h:h + 1, :], -jnp.inf)
        inter = bcc + m_old[:, h:h + 1]
        m_t.append(jnp.maximum(inter, jnp.max(dmat, axis=1, keepdims=True)))
        w_in.append(jnp.exp(inter - m_t[h]))
        w_ts.append(jnp.exp(dmat - m_t[h]) * qk[h])
    yield
    q_c = [jnp.dot(q[h], c_ref[0, 0, h], preferred_element_type=F32) for h in heads]
    wv = [jnp.dot(w_ts[h], v[h], preferred_element_type=F32) for h in heads]
    yield
    hs = []
    for h in heads:
        num = wv[h] + w_in[h] * q_c[h]
        den = (jnp.sum(w_ts[h], axis=1, keepdims=True)
               + w_in[h] * jnp.sum(q[h] * n_old[h:h + 1, :], axis=1, keepdims=True))
        hs.append(_rmsnorm_rows(num / jnp.maximum(jnp.abs(den), jnp.exp(-m_t[h]))))
    lane_h = lax.broadcasted_iota(jnp.int32, (1, H_B), 1)
    m_out = jnp.zeros((1, H_B), F32)
    ks, w_c = [], []
    for h in heads:
        bcc = bc_c[:, h:h + 1]
        m_new = m_t[h][lv - 1:lv, :]
        bc_last = bcc[lc - 1:lc, :]
        ks.append(k[h] * jnp.exp(bc_last - bcc + li_c[:, h:h + 1] - m_new))
        w_c.append(jnp.exp(bc_last + m_old[:, h:h + 1] - m_new))
        m_out = jnp.where(lane_h == h, m_new, m_out)
    yield
    kv = [lax.dot_general(ks[h], v[h], _TN, preferred_element_type=F32) for h in heads]
    yield
    for h in heads:
        co_ref[0, 0, h] = w_c[h] * c_ref[0, 0, h] + kv[h]
        no_ref[0, 0, h:h + 1, :] = w_c[h] * n_old[h:h + 1, :] + jnp.sum(ks[h], axis=0, keepdims=True)
    mo_ref[0, 0] = m_out
    hn = jnp.concatenate(hs, axis=1) * nw_ref[...]
    h_ref[...] = (hn * _sigmoid(og_ref[...])).astype(h_ref.dtype)


def _mlstm(p1, gates, gates_t, c_state, n_state, m_state, prm, e, geo, prev_state_out):
    geo = geo._replace(scan_seqs=geo.mlstm_seqs)
    b = geo.n_seq
    const2 = lambda shape: pl.BlockSpec(shape, lambda i, c: (0, 0))
    qk_w = H_B * DK_B
    m4 = m_state.reshape(m_state.shape[0], b, 1, H_B)
    one = lambda shape: (1,) + tuple(shape[1:])
    h, co, no, mo = _seq_call(
        functools.partial(_mlstm_kernel, lc=geo.chunk, lv=geo.valid,
                          single_chunk=geo.seq_len == geo.chunk), geo, "mlstm",
        in_specs=[
            _row_spec(geo, qk_w, 9), _row_spec(geo, qk_w, 10),
            _row_spec(geo, D_B, 1), _row_spec(geo, D_B, 2),
            _row_spec(geo, GATE_W, 0),
            _gates_t_spec(geo),
            _state_spec(c_state.shape, e, geo), _state_spec(n_state.shape, e, geo),
            _state_spec(m4.shape, e, geo),
            const2((1, H_B)), const2((H_B, 1)), const2((1, H_B)), const2((H_B, 1)),
            const2((1, D_B)),
        ],
        args=(_rows3(p1, geo),) * 4 + (_rows3(gates, geo), gates_t, c_state, n_state, m4,
              prm['i_bias'][e].reshape(1, H_B), prm['i_bias'][e].reshape(H_B, 1),
              prm['f_bias'][e].reshape(1, H_B), prm['f_bias'][e].reshape(H_B, 1),
              prm['norm_b'][e].reshape(1, D_B)),
        out_specs=[_row_spec(geo, D_B, 0), _state_spec(c_state.shape, e, geo),
                   _state_spec(one(n_state.shape), 0, geo), _state_spec(one(m4.shape), 0, geo)],
        out_shapes=[
            jax.ShapeDtypeStruct((b, geo.seq_len, D_B), MXU_DTYPE),
            jax.ShapeDtypeStruct(c_state.shape, F32),
            jax.ShapeDtypeStruct(one(n_state.shape), F32),
            jax.ShapeDtypeStruct(one(m4.shape), F32),
        ],
        scratch=[],
        kinds=[ROWS] * 6 + [STATE] * 3 + [CONST] * 5 + [ROWS] + [STATE] * 3,
        stacked_prev={} if prev_state_out is None else {1: prev_state_out},
        interleave=geo.seq_len != geo.chunk,
    )
    return h.reshape(b * geo.seq_len, D_B), co, no, mo.reshape(1, b, H_B)


def _hgrn_kernel(q_ref, f_ref, i_ref, g_ref, lbl_ref, s0_ref, nw_ref, o_ref, so_ref,
                 *, lc, lv, layer_o, bs, single_chunk):
    if single_chunk:
        s_ref = s0_ref
    else:
        @pl.when(pl.program_id(1) == 0)
        def _():
            so_ref[0, 0] = s0_ref[0, 0]

        s_ref = so_ref

    lbl = lbl_ref[...]
    ex = jnp.exp(lbl - jnp.max(lbl, axis=0, keepdims=True))
    sm = ex / jnp.sum(ex, axis=0, keepdims=True)
    lb_all = [sm[0:1, :]]
    for r in range(1, lbl.shape[0]):
        lb_all.append(lb_all[-1] + sm[r:r + 1, :])
    lb = lb_all[layer_o] - lb_all[0]

    fx = f_ref[...]
    e1 = jnp.exp(-jnp.abs(fx))
    log_sig = jnp.minimum(fx, 0.0) - jnp.log1p(e1)
    la = jnp.log(lb)
    lb_ = jnp.log1p(-lb) + log_sig
    logf = jnp.maximum(la, lb_) + jnp.log1p(jnp.exp(-jnp.abs(la - lb_)))
    kk = (1.0 - lb) * (jnp.where(fx >= 0.0, e1, 1.0) / (1.0 + e1))
    if lv < lc:
        valid = lax.broadcasted_iota(jnp.int32, fx.shape, 0) < lv
        logf = jnp.where(valid, logf, 0.0)
        kk = jnp.where(valid, kk, 0.0)

    nb = lc // bs
    r_i = lax.broadcasted_iota(jnp.int32, (lc, lc), 0)
    c_i = lax.broadcasted_iota(jnp.int32, (lc, lc), 1)
    sh = int(math.log2(bs))
    blk_lower = ((c_i <= r_i) & ((c_i >> sh) == (r_i >> sh))).astype(F32)
    gw = jnp.dot(blk_lower, logf, precision=HI)
    q = q_ref[...]
    v = i_ref[...]
    blk = lambda a, i: a[i * bs:(i + 1) * bs, :]
    tots = [gw[(i + 1) * bs - 1:(i + 1) * bs, :] for i in range(nb)]
    before = [jnp.zeros_like(tots[0])]
    for i in range(nb):
        before.append(before[-1] + tots[i])
    g_tot = before[nb]
    qt = q * jnp.exp(gw)
    kt = [blk(kk, j) * jnp.exp(tots[j] - blk(gw, j)) for j in range(nb)]
    q_in = jnp.concatenate([blk(qt, i) * jnp.exp(before[i]) for i in range(nb)], axis=0)
    k_out = jnp.concatenate([kt[j] * jnp.exp(g_tot - before[j + 1]) for j in range(nb)], axis=0)

    heads = range(H_C)
    hsl = lambda h: slice(h * DK_C, (h + 1) * DK_C)
    yield
    o_inter = [jnp.dot(q_in[:, hsl(h)], s_ref[0, 0, h], preferred_element_type=F32)
               for h in heads]
    kv = [lax.dot_general(k_out[:, hsl(h)], v[:, hsl(h)], _TN, preferred_element_type=F32)
          for h in heads]
    att_off = [None]
    for i in range(1, nb):
        k_hat = jnp.concatenate(
            [kt[j] if j == i - 1 else kt[j] * jnp.exp(before[i] - before[j + 1])
             for j in range(i)] + [jnp.zeros((lc - i * bs, D_C), F32)], axis=0)
        q_ti = blk(qt, i)
        att_off.append([lax.dot_general(q_ti[:, hsl(h)], k_hat[:, hsl(h)], _NT,
                                        preferred_element_type=F32) for h in heads])
    yield
    gw2 = gw * LOG2E
    ck = jnp.log2(kk) - gw2
    n_t = bs // SUBLANES
    lane_s = lax.broadcasted_iota(jnp.int32, (SUBLANES, lc), 1)
    row_t = lax.broadcasted_iota(jnp.int32, (bs, lc), 0)
    col_s = lax.broadcasted_iota(jnp.int32, (bs, lc), 1)
    att = []
    for i in range(nb):
        g_i, q_i, ck_i = blk(gw2, i), blk(q, i), blk(ck, i)
        att_d = [[jnp.zeros((SUBLANES, lc), F32) for _ in range(n_t)] for _ in heads]
        for s in range(bs):
            t0 = s // SUBLANES
            p = q_i[t0 * SUBLANES:, :] * jnp.exp2(g_i[t0 * SUBLANES:, :] + ck_i[s:s + 1, :])
            for h in heads:
                a = jnp.sum(p[:, hsl(h)], axis=1, keepdims=True)
                for tt in range(t0, n_t):
                    a_t = a[(tt - t0) * SUBLANES:(tt - t0 + 1) * SUBLANES, :]
                    att_d[h][tt] = jnp.where(lane_s == i * bs + s, a_t, att_d[h][tt])
        causal = (col_s - i * bs) <= row_t
        att_i = []
        for h in heads:
            a = att_d[h][0] if n_t == 1 else jnp.concatenate(att_d[h], axis=0)
            a = jnp.where(causal, a, 0.0)
            att_i.append(a if i == 0 else a + att_off[i][h])
        att.append(att_i)
    yield
    o_intra = [[jnp.dot(att[i][h], v[:, hsl(h)], preferred_element_type=F32) for h in heads]
               for i in range(nb)]
    yield
    outs = []
    for h in heads:
        o_h = o_intra[0][h] if nb == 1 else jnp.concatenate([o_intra[i][h] for i in range(nb)],
                                                             axis=0)
        outs.append(_rmsnorm_rows(o_h + o_inter[h]))
        dec_col = jnp.transpose(jnp.broadcast_to(jnp.exp(g_tot[:, hsl(h)]), (DK_C, DK_C)))
        so_ref[0, 0, h] = dec_col * s_ref[0, 0, h] + kv[h]
    gate = g_ref[...]
    o_ref[...] = (jnp.concatenate(outs, axis=1) * nw_ref[...]
                  * (gate * _sigmoid(gate))).astype(o_ref.dtype)


def _hgrn(p, state, prm, o, geo, prev_state_out):
    geo = geo._replace(scan_seqs=geo.hgrn_seqs)
    n_odd = prm['lb_logits'].shape[0]
    out, new_state = _seq_call(
        functools.partial(_hgrn_kernel, lc=geo.chunk, lv=geo.valid, layer_o=o,
                          bs=min(16, geo.chunk), single_chunk=geo.seq_len == geo.chunk), geo, "hgrn",
        in_specs=[
            _row_spec(geo, D_C, 0), _row_spec(geo, D_C, 1), _row_spec(geo, D_C, 2),
            _row_spec(geo, D_C, 3),
            pl.BlockSpec((n_odd, D_C), lambda i, c: (0, 0)),
            _state_spec(state.shape, o, geo),
            pl.BlockSpec((1, D_C), lambda i, c: (0, 0)),
        ],
        args=(_rows3(p, geo),) * 4 + (prm['lb_logits'], state, prm['norm_c'][o].reshape(1, D_C)),
        out_specs=[_row_spec(geo, D_C, 0), _state_spec(state.shape, o, geo)],
        out_shapes=[
            jax.ShapeDtypeStruct((geo.n_seq, geo.seq_len, D_C), MXU_DTYPE),
            jax.ShapeDtypeStruct(state.shape, F32),
        ],
        scratch=[],
        kinds=[ROWS] * 4 + [CONST, STATE, CONST, ROWS, STATE],
        stacked_prev={} if prev_state_out is None else {1: prev_state_out},
    )
    return out.reshape(geo.n_seq * geo.seq_len, D_C), new_state


def _in_ab_column_groups():
    a0, a1, a2 = D_A, D_A + CONV_DIM_A, D_A + CONV_DIM_A + H_A
    q1 = a2 + H_B * DK_B
    k1 = q1 + H_B * DK_B
    v1 = k1 + D_B
    o1 = v1 + D_B
    end = o1 + 2 * H_B
    return [(0, a0), (k1, v1), (v1, o1), (a0, a1), (a2, q1), (q1, k1)], [(a1, a2), (o1, end)]


def _regroup_in_ab(wab):
    wide, narrow = _in_ab_column_groups()
    n_gate = sum(b - a for a, b in narrow)
    in_ab = jnp.concatenate([wab[:, :, a:b] for a, b in wide], axis=2).astype(MXU_DTYPE)
    gates = jnp.pad(jnp.concatenate([wab[:, :, a:b] for a, b in narrow], axis=2),
                    ((0, 0), (0, 0), (0, GATE_W - n_gate))).astype(MXU_DTYPE)
    return in_ab, gates


def _prep_weights(prm):
    wab = prm['w_in_ab']
    in_ab, gates_ab = _regroup_in_ab(wab)
    n_even, n_odd = wab.shape[0], prm['w_in_c'].shape[0]
    return {
        'in_ab': [[(in_ab, e, 0)] for e in range(n_even)],
        'gates_ab': [[(gates_ab, e, 0)] for e in range(n_even)],
        'out_ab': [[(prm['w_out_ab'], e, 0), (prm['w_out_ab'], e, 1)] for e in range(n_even)],
        'in_c': [[(prm['w_in_c'], o, 0)] for o in range(n_odd)],
        'out_c': [[(prm['w_out_c'], o, 0)] for o in range(n_odd)],
        'ffn_g': [[(prm['w_ffn_g'], l, 0)] for l in range(DEPTH)],
        'ffn_u': [[(prm['w_ffn_u'], l, 0)] for l in range(DEPTH)],
        'ffn_d': [[(prm['w_ffn_d'], l, 0)] for l in range(DEPTH)],
    }


def _trunk(x3, mod, states, prm, w, geo, tn_in, tn_res, tn_ffn, tn_down, emit):
    conv_a, ssd, mem_c, mem_n, mem_m, hgrn, ffn_buf = states
    n_conv, n_n, n_m, n_ffn = [], [], [], []
    new_ssd = new_c = new_hgrn = None
    nc = geo.seq_len // geo.chunk
    wq = {name: list(per_layer) for name, per_layer in w.items()}

    def record(name, idx, emitted):
        if emitted is not None:
            wq[name][idx] = [(arr, 0, 0) for arr in emitted]

    casts = lambda wops: emit and any(wop[0].dtype != MXU_DTYPE for wop in wops)
    for layer in range(DEPTH):
        mm_norm = functools.partial(_mm_norm, x3, prm['norm_mix'][layer], mod, layer, 1, 0,
                                    geo=geo)
        if layer % 2 == 0:
            e = layer // 2
            p1, gates, _ = mm_norm(w['in_ab'][e][0], tn=tn_in[layer], narrow=w['gates_ab'][e][0])
            gates_t = gates.reshape(geo.n_seq, nc, geo.chunk, GATE_W).transpose(0, 1, 3, 2)
            ya, cv, new_ssd = _ssd(p1, gates, gates_t, conv_a, ssd, prm, e, geo, new_ssd)
            hb, new_c, nn, mmm = _mlstm(p1, gates, gates_t, mem_c, mem_n, mem_m, prm, e, geo, new_c)
            n_conv.append(cv); n_n.append(nn); n_m.append(mmm)
            x3, em = _mm_res([ya, hb], w['out_ab'][e], x3, mod, layer, 2, geo, tn_res[layer],
                             emit=casts(w['out_ab'][e]))
            record('out_ab', e, em)
        else:
            o = layer // 2
            p, _, em = mm_norm(w['in_c'][o][0], tn=tn_in[layer], emit=casts(w['in_c'][o]))
            record('in_c', o, None if em is None else [em])
            oc, new_hgrn = _hgrn(p, hgrn, prm, o, geo, new_hgrn)
            x3, em = _mm_res([oc], w['out_c'][o], x3, mod, layer, 2, geo, tn_res[layer],
                             emit=casts(w['out_c'][o]))
            record('out_c', o, em)
        act, fb, em = _ffn_in(x3, prm['norm_ffn'][layer], mod, layer, w['ffn_g'][layer][0],
                              w['ffn_u'][layer][0], ffn_buf, prm['conv_w_f'][layer],
                              prm['conv_b_f'][layer], geo, tn_ffn,
                              emit=casts(w['ffn_g'][layer] + w['ffn_u'][layer]))
        if em is not None:
            record('ffn_g', layer, em[:1])
            record('ffn_u', layer, em[1:])
        n_ffn.append(fb)
        x3, em = _mm_res([act], w['ffn_d'][layer], x3, mod, layer, 5, geo, tn_down,
                         emit=casts(w['ffn_d'][layer]))
        record('ffn_d', layer, em)
    y = _final_norm(x3, prm['norm_f'], geo)
    cat = lambda xs: jnp.concatenate(xs, axis=0)
    return (y, cat(n_conv), new_ssd.reshape(ssd.shape), new_c, cat(n_n), cat(n_m), new_hgrn,
            cat(n_ffn)), wq


def kernel(x_prompt, x_sample, c_prompt, c_sample, state_ssd_conv, state_ssd, state_mlstm_c, state_mlstm_n, state_mlstm_m, state_hgrn, state_ffn_conv, w_ada, b_ada, norm_mix, norm_ffn, w_in_ab, conv_w_a, conv_b_a, dt_bias, a_log, d_skip, norm_a, i_bias, f_bias, norm_b, w_out_ab, w_in_c, lb_logits, norm_c, w_out_c, w_ffn_g, w_ffn_u, conv_w_f, conv_b_f, w_ffn_d, norm_f):
    prm = dict(norm_mix=norm_mix, norm_ffn=norm_ffn, w_in_ab=w_in_ab, conv_w_a=conv_w_a,
               conv_b_a=conv_b_a, dt_bias=dt_bias, a_log=a_log, d_skip=d_skip, norm_a=norm_a,
               i_bias=i_bias, f_bias=f_bias, norm_b=norm_b, w_out_ab=w_out_ab, w_in_c=w_in_c,
               lb_logits=lb_logits, norm_c=norm_c, w_out_c=w_out_c, w_ffn_g=w_ffn_g,
               w_ffn_u=w_ffn_u, conv_w_f=conv_w_f, conv_b_f=conv_b_f, w_ffn_d=w_ffn_d,
               norm_f=norm_f)
    bp, lp, _ = x_prompt.shape
    bs, ls, _ = x_sample.shape
    n_even, n_odd = state_ssd.shape[0], state_hgrn.shape[0]
    w = _prep_weights(prm)

    n_c = bs + bp
    n_c_pad = -(-n_c // SUBLANES) * SUBLANES
    c_all = jnp.pad(jnp.concatenate([c_sample, c_prompt], axis=0), ((0, n_c_pad - n_c), (0, 0)))
    mod = _ada(c_all, w_ada, b_ada).reshape(DEPTH, n_c_pad, 1, 6 * D_MODEL)

    geo_s = Geo(n_seq=bs, seq_len=SUBLANES, seq_blk=bs, row_blk=SUBLANES, chunk=SUBLANES,
                valid=ls, mod_off=0, scan_seqs=8, mlstm_seqs=4, hgrn_seqs=8)
    xs = jnp.pad(x_sample, ((0, 0), (0, SUBLANES - ls), (0, 0)))
    st_s = (state_ssd_conv, state_ssd, state_mlstm_c, state_mlstm_n, state_mlstm_m, state_hgrn,
            state_ffn_conv)
    out_s, w_bf16 = _trunk(xs, mod, st_s, prm, w, geo_s, tn_in=(512,) * DEPTH,
                           tn_res=(512,) * DEPTH, tn_ffn=256, tn_down=256, emit=True)

    zeros = lambda *s: jnp.zeros(s, F32)
    st_p = (zeros(n_even, bp, CONV_K_A - 1, CONV_DIM_A), zeros(n_even, bp, H_A, P_A, N_A),
            zeros(n_even, bp, H_B, DK_B, DV_B), zeros(n_even, bp, H_B, DK_B),
            zeros(n_even, bp, H_B), zeros(n_odd, bp, H_C, DK_C, DV_C),
            zeros(DEPTH, bp, CONV_K_F - 1, D_FF))
    lc_p = math.gcd(lp, PROMPT_CHUNK)
    geo_p = Geo(n_seq=bp, seq_len=lp, seq_blk=1, row_blk=min(lp, 1024), chunk=lc_p, valid=lc_p,
                mod_off=bs, scan_seqs=1, mlstm_seqs=1, hgrn_seqs=2 if bp % 2 == 0 else 1)
    out_p, _ = _trunk(x_prompt, mod, st_p, prm, w_bf16, geo_p, tn_in=(1024,) * DEPTH,
                      tn_res=(1024,) * DEPTH, tn_ffn=FFN_TN, tn_down=FFN_DOWN_TN, emit=False)
    return (out_p[0], out_s[0][:, :ls]) + tuple(out_p[1:]) + tuple(out_s[1:])
```

```python
import collections
import functools
import math

import jax
import jax.numpy as jnp
from jax import lax
from jax.experimental import pallas as pl
from jax.experimental.pallas import tpu as pltpu

F32 = jnp.float32
MXU_DTYPE = jnp.bfloat16
HI = lax.Precision.HIGHEST
NEG_BIG = -1e30
LOG2E = 1.4426950408889634

D_MODEL = 2048
DEPTH = 4
EPS = 1e-6
PROMPT_CHUNK = 64
H_A, P_A, G_A, N_A, CONV_K_A = 32, 64, 4, 128, 4
D_A = H_A * P_A
CONV_DIM_A = D_A + 2 * G_A * N_A
H_B, DK_B, DV_B = 8, 128, 256
D_B = H_B * DV_B
H_C, DK_C, DV_C = 16, 128, 128
D_C = H_C * DV_C
D_FF, CONV_K_F = 5632, 3
FFN_TN = 512
FFN_DOWN_TN = 512
PROLOGUE_SPLIT = 4
GATE_W = 128
SUBLANES = 8
VMEM_LIMIT = 56 * 1024 * 1024

_NT = (((1,), (1,)), ((), ()))
_TN = (((0,), (0,)), ((), ()))

Geo = collections.namedtuple(
    "Geo", "n_seq seq_len seq_blk row_blk chunk valid mod_off scan_seqs mlstm_seqs hgrn_seqs")


def _cparams(sem):
    return pltpu.CompilerParams(dimension_semantics=sem, vmem_limit_bytes=VMEM_LIMIT)


def _sigmoid(x):
    return 1.0 / (1.0 + jnp.exp(-x))


def _softplus(x):
    return jnp.maximum(x, 0.0) + jnp.log1p(jnp.exp(-jnp.abs(x)))


def _log_sigmoid(x):
    return jnp.minimum(x, 0.0) - jnp.log1p(jnp.exp(-jnp.abs(x)))


def _tri(n):
    r = lax.broadcasted_iota(jnp.int32, (n, n), 0)
    c = lax.broadcasted_iota(jnp.int32, (n, n), 1)
    mask = c <= r
    return mask.astype(F32), (r <= c).astype(F32), mask


def _split3(a):
    hi = a.astype(MXU_DTYPE)
    rest = a - hi.astype(F32)
    mid = rest.astype(MXU_DTYPE)
    lo = (rest - mid.astype(F32)).astype(MXU_DTYPE)
    return hi, mid, lo


def _select_dot(a, b, selector):
    if selector == 'a':
        s = a.astype(MXU_DTYPE)
        parts = [jnp.dot(s, t, preferred_element_type=F32) for t in _split3(b)]
    else:
        s = b.astype(MXU_DTYPE)
        parts = [jnp.dot(t, s, preferred_element_type=F32) for t in _split3(a)]
    return parts[0] + parts[1] + parts[2]


def _shifted(x, tails, k, axis=0):
    rolled = pltpu.roll(x, k, axis)
    head = lax.slice_in_dim(rolled, 0, SUBLANES, axis=axis)
    row = lax.broadcasted_iota(jnp.int32, head.shape, axis)
    for r in range(k):
        head = jnp.where(row == r, tails[len(tails) - k + r], head)
    if x.shape[axis] == SUBLANES:
        return head
    rest = lax.slice_in_dim(rolled, SUBLANES, x.shape[axis], axis=axis)
    return jnp.concatenate([head, rest], axis=axis)


def _drop_ref(fn, idx):
    def wrapped(*refs):
        return fn(*refs[:idx], *refs[idx + 1:])
    return wrapped


def _rmsnorm_rows(x):
    return x * lax.rsqrt(jnp.mean(x * x, axis=-1, keepdims=True) + EPS)


def _ada_kernel(c_ref, w_ref, b_ref, o_ref):
    c = c_ref[...]
    ca = (c * _sigmoid(c)).astype(MXU_DTYPE)
    o_ref[0] = jnp.dot(ca, w_ref[0].astype(MXU_DTYPE), preferred_element_type=F32) + b_ref[0]


def _ada(c_all, w_ada, b_ada, tn=1024):
    rows = c_all.shape[0]
    n = w_ada.shape[2]
    return pl.pallas_call(
        _ada_kernel,
        grid=(DEPTH, n // tn),
        in_specs=[
            pl.BlockSpec((rows, D_MODEL), lambda l, j: (0, 0)),
            pl.BlockSpec((1, D_MODEL, tn), lambda l, j: (l, 0, j)),
            pl.BlockSpec((1, 1, tn), lambda l, j: (l, 0, j)),
        ],
        out_specs=pl.BlockSpec((1, rows, tn), lambda l, j: (l, 0, j)),
        out_shape=jax.ShapeDtypeStruct((DEPTH, rows, n), F32),
        compiler_params=_cparams(("parallel", "parallel")),
        name="ada",
    )(c_all, w_ada, b_ada.reshape(DEPTH, 1, n))


def _tiles(geo):
    nrt = geo.seq_len // geo.row_blk
    return nrt, (geo.n_seq // geo.seq_blk) * nrt, geo.seq_blk * geo.row_blk


def _x_spec(geo, width, col_of):
    nrt = geo.seq_len // geo.row_blk
    return pl.BlockSpec((geo.seq_blk, geo.row_blk, width),
                        lambda i, j: (i // nrt, i % nrt, col_of(j)))


def _mod_spec(geo, layer, width, col_of):
    nrt = geo.seq_len // geo.row_blk
    return pl.BlockSpec((1, geo.seq_blk, 1, width),
                        lambda i, j: (layer, geo.mod_off + i // nrt, 0, col_of(j)))


def _norm_mod_chunks(x_ref, nw_ref, sc_ref, sh_ref):
    seq_blk, row_blk, d = x_ref.shape
    for r in range(PROLOGUE_SPLIT):
        if seq_blk == 1:
            n = row_blk // PROLOGUE_SPLIT
            x, sc, sh = x_ref[:, r * n:(r + 1) * n], sc_ref[0], sh_ref[0]
        else:
            n = seq_blk // PROLOGUE_SPLIT
            x, sc, sh = x_ref[r * n:(r + 1) * n], sc_ref[0, r * n:(r + 1) * n], sh_ref[0, r * n:(r + 1) * n]
        h = (_rmsnorm_rows(x) * nw_ref[...]) * (1.0 + sc) + sh
        rows = h.shape[0] * h.shape[1]
        yield slice(r * rows, (r + 1) * rows), h.reshape(rows, d).astype(MXU_DTYPE)


def _weight_spec(wop, k, tn):
    _, layer, row_block = wop
    return pl.BlockSpec((1, k, tn), lambda i, j: (layer, row_block, j))


def _emit_spec_shape(k, n, tn):
    return (pl.BlockSpec((1, k, tn), lambda i, j: (0, 0, j)),
            jax.ShapeDtypeStruct((1, k, n), MXU_DTYPE))


def _weight_tile(w_ref, wq_ref):
    w = w_ref[0].astype(MXU_DTYPE)
    if wq_ref is not None:
        wq_ref[0] = w
    return w


def _mm_norm_kernel(*refs, emit, narrow):
    x_ref, nw_ref, sc_ref, sh_ref, w_ref = refs[:5]
    rest = list(refs[5:])
    wn_ref = rest.pop(0) if narrow else None
    o_ref = rest.pop(0)
    on_ref = rest.pop(0) if narrow else None
    wq_ref = rest.pop(0) if emit else None
    (h_scr,) = rest
    w = _weight_tile(w_ref, wq_ref)

    @pl.when(pl.program_id(1) == 0)
    def _():
        for rows, h in _norm_mod_chunks(x_ref, nw_ref, sc_ref, sh_ref):
            h_scr[rows, :] = h
            o_ref[rows, :] = jnp.dot(h, w, preferred_element_type=F32)
            if narrow:
                on_ref[rows, :] = jnp.dot(h, wn_ref[0], preferred_element_type=F32)

    @pl.when(pl.program_id(1) > 0)
    def _():
        o_ref[...] = jnp.dot(h_scr[...], w, preferred_element_type=F32)


def _mm_norm(x3, nw, mod, layer, k_sc, k_sh, wop, geo, tn, emit=False, narrow=None):
    _, n_i, tm = _tiles(geo)
    n = wop[0].shape[2]
    in_specs = [
        _x_spec(geo, D_MODEL, lambda j: 0),
        pl.BlockSpec((1, D_MODEL), lambda i, j: (0, 0)),
        _mod_spec(geo, layer, D_MODEL, lambda j: k_sc),
        _mod_spec(geo, layer, D_MODEL, lambda j: k_sh),
        _weight_spec(wop, D_MODEL, tn),
    ]
    args = [x3, nw.reshape(1, D_MODEL), mod, mod, wop[0]]
    out_specs = [pl.BlockSpec((tm, tn), lambda i, j: (i, j))]
    out_shape = [jax.ShapeDtypeStruct((geo.n_seq * geo.seq_len, n), F32)]
    if narrow is not None:
        n_narrow = narrow[0].shape[2]
        in_specs.append(pl.BlockSpec((1, D_MODEL, n_narrow), lambda i, j: (narrow[1], 0, 0)))
        args.append(narrow[0])
        out_specs.append(pl.BlockSpec((tm, n_narrow), lambda i, j: (i, 0)))
        out_shape.append(jax.ShapeDtypeStruct((geo.n_seq * geo.seq_len, n_narrow), F32))
    if emit:
        assert n_i == 1
        spec, shape = _emit_spec_shape(D_MODEL, n, tn)
        out_specs.append(spec)
        out_shape.append(shape)
    outs = list(pl.pallas_call(
        functools.partial(_mm_norm_kernel, emit=emit, narrow=narrow is not None),
        grid=(n_i, n // tn),
        in_specs=in_specs,
        out_specs=out_specs,
        out_shape=out_shape,
        scratch_shapes=[pltpu.VMEM((tm, D_MODEL), MXU_DTYPE)],
        compiler_params=_cparams(("parallel", "arbitrary")),
        name="mm_norm",
    )(*args))
    out = outs.pop(0)
    out_narrow = outs.pop(0) if narrow is not None else None
    return out, out_narrow, (outs.pop(0) if emit else None)


def _mm_res_kernel(*refs, n_lhs, emit):
    a_refs, w_refs = refs[:n_lhs], refs[n_lhs:2 * n_lhs]
    xres_ref, gate_ref, o_ref = refs[2 * n_lhs:2 * n_lhs + 3]
    wq_refs = refs[2 * n_lhs + 3:] if emit else (None,) * n_lhs
    acc = None
    for a_ref, w_ref, wq_ref in zip(a_refs, w_refs, wq_refs):
        d = jnp.dot(a_ref[...], _weight_tile(w_ref, wq_ref), preferred_element_type=F32)
        acc = d if acc is None else acc + d
    o_ref[...] = xres_ref[...] + gate_ref[0] * acc.reshape(o_ref.shape)


def _mm_res(a_list, wops, x3, mod, layer, k_gate, geo, tn, emit=False):
    _, n_i, tm = _tiles(geo)
    per = D_MODEL // tn
    in_specs = [pl.BlockSpec((tm, a.shape[1]), lambda i, j: (i, 0)) for a in a_list]
    in_specs += [_weight_spec(wop, a.shape[1], tn) for wop, a in zip(wops, a_list)]
    in_specs += [_x_spec(geo, tn, lambda j: j),
                 _mod_spec(geo, layer, tn, lambda j: k_gate * per + j)]
    out_specs = [_x_spec(geo, tn, lambda j: j)]
    out_shape = [jax.ShapeDtypeStruct(x3.shape, F32)]
    if emit:
        assert n_i == 1
        for a in a_list:
            spec, shape = _emit_spec_shape(a.shape[1], D_MODEL, tn)
            out_specs.append(spec)
            out_shape.append(shape)
    outs = pl.pallas_call(
        functools.partial(_mm_res_kernel, n_lhs=len(a_list), emit=emit),
        grid=(n_i, per),
        in_specs=in_specs,
        out_specs=out_specs,
        out_shape=out_shape,
        compiler_params=_cparams(("parallel", "parallel")),
        name="mm_res",
    )(*a_list, *[wop[0] for wop in wops], x3, mod)
    return outs[0], (list(outs[1:]) if emit else None)


def _ffn_in_kernel(x_ref, nw_ref, sc_ref, sh_ref, wg_ref, wu_ref, cs_ref, cw_ref, cb_ref,
                   a_ref, cso_ref, *rest, nrt, valid, emit):
    wgq_ref, wuq_ref, h_scr, tail_scr = rest if emit else (None, None) + rest
    i, j = pl.program_id(0), pl.program_id(1)

    seq_blk, row_blk, tn = x_ref.shape[0], x_ref.shape[1], a_ref.shape[1]
    w_g, w_u = _weight_tile(wg_ref, wgq_ref), _weight_tile(wu_ref, wuq_ref)

    def conv_gate(g, u):
        g, u = g.reshape(seq_blk, row_blk, tn), u.reshape(seq_blk, row_blk, tn)
        prev = cs_ref[0]
        if nrt > 1:
            prev = jnp.where(i % nrt == 0, prev, tail_scr[j])
        t2, t1 = prev[:, 0:1, :], prev[:, 1:2, :]
        w = cw_ref[...]
        y = (cb_ref[...] + w[0:1] * _shifted(g, [t2, t1], 2, axis=1)
             + w[1:2] * _shifted(g, [t2, t1], 1, axis=1) + w[2:3] * g)
        a_ref[...] = (y * _sigmoid(y) * u).reshape(a_ref.shape).astype(a_ref.dtype)
        new_tail = g[:, valid - 2:valid, :]
        if nrt > 1:
            tail_scr[j] = new_tail
        cso_ref[0] = new_tail

    @pl.when(j == 0)
    def _():
        gs, us = [], []
        for rows, h in _norm_mod_chunks(x_ref, nw_ref, sc_ref, sh_ref):
            h_scr[rows, :] = h
            gs.append(jnp.dot(h, w_g, preferred_element_type=F32))
            us.append(jnp.dot(h, w_u, preferred_element_type=F32))
        conv_gate(jnp.concatenate(gs, axis=0), jnp.concatenate(us, axis=0))

    @pl.when(j > 0)
    def _():
        h = h_scr[...]
        conv_gate(jnp.dot(h, w_g, preferred_element_type=F32),
                  jnp.dot(h, w_u, preferred_element_type=F32))


def _ffn_in(x3, nw, mod, layer, wop_g, wop_u, conv_state, conv_w, conv_b, geo, tn, emit=False):
    nrt, n_i, tm = _tiles(geo)
    n_j = D_FF // tn
    valid = geo.row_blk if geo.valid == geo.chunk else geo.valid
    tail_shape = (n_j, geo.seq_blk, CONV_K_F - 1, tn) if nrt > 1 else (1, 1, CONV_K_F - 1, 128)
    cs_spec = pl.BlockSpec((1, geo.seq_blk, CONV_K_F - 1, tn), lambda i, j: (layer, i // nrt, 0, j))
    emit_specs, emit_shapes = [], []
    if emit:
        assert n_i == 1
        for _ in range(2):
            spec, shape = _emit_spec_shape(D_MODEL, D_FF, tn)
            emit_specs.append(spec)
            emit_shapes.append(shape)
    act, tails, *emitted = pl.pallas_call(
        functools.partial(_ffn_in_kernel, nrt=nrt, valid=valid, emit=emit),
        grid=(n_i, n_j),
        in_specs=[
            _x_spec(geo, D_MODEL, lambda j: 0),
            pl.BlockSpec((1, D_MODEL), lambda i, j: (0, 0)),
            _mod_spec(geo, layer, D_MODEL, lambda j: 4),
            _mod_spec(geo, layer, D_MODEL, lambda j: 3),
            _weight_spec(wop_g, D_MODEL, tn),
            _weight_spec(wop_u, D_MODEL, tn),
            cs_spec,
            pl.BlockSpec((CONV_K_F, tn), lambda i, j: (0, j)),
            pl.BlockSpec((1, tn), lambda i, j: (0, j)),
        ],
        out_specs=[
            pl.BlockSpec((tm, tn), lambda i, j: (i, j)),
            pl.BlockSpec((1, geo.seq_blk, CONV_K_F - 1, tn), lambda i, j: (i, 0, 0, j)),
        ] + emit_specs,
        out_shape=[
            jax.ShapeDtypeStruct((geo.n_seq * geo.seq_len, D_FF), MXU_DTYPE),
            jax.ShapeDtypeStruct((n_i, geo.seq_blk, CONV_K_F - 1, D_FF), F32),
        ] + emit_shapes,
        scratch_shapes=[pltpu.VMEM((tm, D_MODEL), MXU_DTYPE),
                        pltpu.VMEM(tail_shape, F32)],
        compiler_params=_cparams(("arbitrary", "arbitrary")),
        name="ffn_in",
    )(x3, nw.reshape(1, D_MODEL), mod, mod, wop_g[0], wop_u[0], conv_state, conv_w,
      conv_b.reshape(1, D_FF))
    last = tails.reshape(n_i // nrt, nrt, geo.seq_blk, CONV_K_F - 1, D_FF)[:, nrt - 1]
    return act, last.reshape(1, geo.n_seq, CONV_K_F - 1, D_FF), (emitted if emit else None)


def _norm_kernel(x_ref, nw_ref, o_ref):
    o_ref[...] = _rmsnorm_rows(x_ref[...]) * nw_ref[...]


def _final_norm(x3, nw, geo):
    _, n_i, _ = _tiles(geo)
    return pl.pallas_call(
        _norm_kernel,
        grid=(n_i, 1),
        in_specs=[_x_spec(geo, D_MODEL, lambda j: 0),
                  pl.BlockSpec((1, D_MODEL), lambda i, j: (0, 0))],
        out_specs=_x_spec(geo, D_MODEL, lambda j: 0),
        out_shape=jax.ShapeDtypeStruct(x3.shape, F32),
        compiler_params=_cparams(("parallel", "arbitrary")),
        name="final_norm",
    )(x3, nw.reshape(1, D_MODEL))


def _state_spec(shape, layer, geo):
    rest = tuple(shape[2:])
    zeros = (0,) * len(rest)
    return pl.BlockSpec((1, geo.scan_seqs) + rest, lambda i, c: (layer, i) + zeros)


def _row_spec(geo, width, col):
    return pl.BlockSpec((geo.scan_seqs, geo.chunk, width), lambda i, c: (i, c, col))


def _rows3(a, geo):
    return a.reshape(geo.n_seq, geo.seq_len, a.shape[-1])


def _gates_t_spec(geo):
    return pl.BlockSpec((geo.scan_seqs, 1, GATE_W, geo.chunk), lambda i, c: (i, c, 0, 0))


ROWS, LEAD, STATE, CONST = "rows", "lead", "state", "const"


def _per_sequence(body, geo, kinds, interleave):
    def view(ref, kind, k):
        if kind == ROWS:
            return ref.at[k]
        if kind == LEAD:
            return ref.at[pl.ds(k, 1)]
        if kind == STATE:
            return ref.at[:, pl.ds(k, 1)]
        return ref

    def wrapped(*refs):
        running = [body(*[view(r, kind, k) for r, kind in zip(refs, kinds)])
                   for k in range(geo.scan_seqs)]
        if not interleave:
            for gen in running:
                for _ in gen:
                    pass
            return
        while running:
            still = []
            for gen in running:
                if next(gen, StopIteration) is not StopIteration:
                    still.append(gen)
            running = still
    return wrapped


def _seq_call(kernel_fn, geo, name, in_specs, args, out_specs, out_shapes, scratch, kinds,
              stacked_prev, interleave=True):
    nc = geo.seq_len // geo.chunk
    kernel_fn = _per_sequence(kernel_fn, geo, kinds, interleave)
    in_specs, args = list(in_specs), list(args)
    aliases = {}
    assert len(stacked_prev) <= 1
    for out_idx, arr in stacked_prev.items():
        kernel_fn = _drop_ref(kernel_fn, len(in_specs))
        aliases[len(in_specs)] = out_idx
        in_specs.append(pl.BlockSpec(memory_space=pl.ANY))
        args.append(arr)
    return pl.pallas_call(
        kernel_fn,
        grid=(geo.n_seq // geo.scan_seqs, nc),
        in_specs=in_specs,
        out_specs=out_specs,
        out_shape=out_shapes,
        scratch_shapes=scratch,
        input_output_aliases=aliases,
        compiler_params=_cparams(("parallel", "arbitrary")),
        name=name,
    )(*args)


def _ssd_kernel(z_ref, xbc_ref, gc_ref, gr_ref, cs_ref, s0_ref, cw_ref, cb_ref,
                dtb_r_ref, dtb_c_ref, al_r_ref, al_c_ref, dsk_ref, nw_ref, exp_ref,
                y_ref, cso_ref, so_ref, tail_scr, *, lc, lv, single_chunk):
    if single_chunk:
        s_ref = s0_ref
        cs = cs_ref[0, 0]
        tails = [cs[0:1, :], cs[1:2, :], cs[2:3, :]]
    else:
        @pl.when(pl.program_id(1) == 0)
        def _():
            tail_scr[0, 5:8, :] = cs_ref[0, 0]
            so_ref[0, 0] = s0_ref[0, 0]

        s_ref = so_ref
        tails = [tail_scr[0, 5:6, :], tail_scr[0, 6:7, :], tail_scr[0, 7:8, :]]

    x = xbc_ref[...]
    w = cw_ref[0]
    xc = (cb_ref[0] + w[0:1] * _shifted(x, tails, 3) + w[1:2] * _shifted(x, tails, 2)
          + w[2:3] * _shifted(x, tails, 1) + w[3:4] * x)
    xc = xc * _sigmoid(xc)
    new_tail = x[lv - 3:lv, :]
    if not single_chunk:
        tail_scr[0, 5:8, :] = new_tail
    cso_ref[0, 0] = new_tail
    xa = xc[:, :D_A]
    bm = xc[:, D_A:D_A + G_A * N_A]
    cm = xc[:, D_A + G_A * N_A:]

    dt_c = _softplus(gc_ref[:, 0:H_A] + dtb_r_ref[...])
    dt_r = _softplus(gr_ref[0, 0:H_A, :] + dtb_c_ref[...])
    if lv < lc:
        dt_c = jnp.where(lax.broadcasted_iota(jnp.int32, dt_c.shape, 0) < lv, dt_c, 0.0)
        dt_r = jnp.where(lax.broadcasted_iota(jnp.int32, dt_r.shape, 1) < lv, dt_r, 0.0)
    lower, upper, mask = _tri(lc)
    cum_c = _select_dot(lower, dt_c * (-jnp.exp(al_r_ref[...])), 'a')
    cum_r = _select_dot(dt_r * (-jnp.exp(al_c_ref[...])), upper, 'b')
    cum_last = cum_c[lc - 1:lc, :]
    expand = exp_ref[...]
    ecum_x = _select_dot(jnp.exp(cum_c), expand, 'b')
    tail_x = _select_dot(jnp.exp(cum_last - cum_c) * dt_c, expand, 'b')

    hg = H_A // G_A
    gw = hg * P_A
    groups = range(G_A)
    gsl = lambda g: slice(g * gw, (g + 1) * gw)
    cg = [cm[:, g * N_A:(g + 1) * N_A] for g in groups]
    bg = [bm[:, g * N_A:(g + 1) * N_A] for g in groups]
    yield
    cb_ts = [lax.dot_general(cg[g], bg[g], _NT, preferred_element_type=F32) for g in groups]
    y_inter = [lax.dot_general(cg[g], s_ref[0, 0, g], _NT, preferred_element_type=F32)
               for g in groups]
    upd = [lax.dot_general(xa[:, gsl(g)] * tail_x[:, gsl(g)], bg[g], _TN,
                           preferred_element_type=F32) for g in groups]
    yield
    w_ts = [cb_ts[h // hg] * dt_r[h:h + 1, :]
            * jnp.exp(jnp.where(mask, cum_c[:, h:h + 1] - cum_r[h:h + 1, :], -jnp.inf))
            for h in range(H_A)]
    yield
    pieces = []
    for j in range(H_A // 2):
        xp = xa[:, j * 128:(j + 1) * 128]
        lane = lax.broadcasted_iota(jnp.int32, xp.shape, 1)
        pieces.append(
            jnp.dot(w_ts[2 * j], jnp.where(lane < P_A, xp, 0.0), preferred_element_type=F32)
            + jnp.dot(w_ts[2 * j + 1], jnp.where(lane >= P_A, xp, 0.0),
                      preferred_element_type=F32))
    yield
    y = (jnp.concatenate(pieces, axis=1) + jnp.concatenate(y_inter, axis=1) * ecum_x
         + dsk_ref[...] * xa)
    for g in groups:
        for hh in range(hg):
            h = g * hg + hh
            rows = slice(hh * P_A, (hh + 1) * P_A)
            so_ref[0, 0, g, rows, :] = (s_ref[0, 0, g, rows, :] * jnp.exp(cum_r[h:h + 1, lc - 1:lc])
                                        + upd[g][rows, :])

    z = z_ref[...]
    y = y * (z * _sigmoid(z))
    y = jnp.concatenate([_rmsnorm_rows(y[:, g * gw:(g + 1) * gw]) for g in range(G_A)], axis=1)
    y_ref[...] = (y * nw_ref[...]).astype(y_ref.dtype)


def _ssd(p1, gates, gates_t, conv_state, ssd_state, prm, e, geo, prev_state_out):
    b = geo.n_seq
    const2 = lambda shape: pl.BlockSpec(shape, lambda i, c: (0, 0))
    n_even = ssd_state.shape[0]
    s5 = ssd_state.reshape(n_even, b, G_A, (H_A // G_A) * P_A, N_A)
    expand = (jnp.arange(D_A)[None, :] // P_A == jnp.arange(H_A)[:, None]).astype(F32)
    cso_shape = (1,) + conv_state.shape[1:]
    y, cso, so = _seq_call(
        functools.partial(_ssd_kernel, lc=geo.chunk, lv=geo.valid,
                          single_chunk=geo.seq_len == geo.chunk), geo, "ssd",
        in_specs=[
            _row_spec(geo, D_A, 0),
            _row_spec(geo, CONV_DIM_A, 2),
            _row_spec(geo, GATE_W, 0),
            _gates_t_spec(geo),
            _state_spec(conv_state.shape, e, geo),
            _state_spec(s5.shape, e, geo),
            pl.BlockSpec((1, CONV_K_A, CONV_DIM_A), lambda i, c: (e, 0, 0)),
            pl.BlockSpec((1, 1, CONV_DIM_A), lambda i, c: (e, 0, 0)),
            const2((1, H_A)), const2((H_A, 1)), const2((1, H_A)), const2((H_A, 1)),
            const2((1, D_A)), const2((1, D_A)), const2((H_A, D_A)),
        ],
        args=(_rows3(p1, geo), _rows3(p1, geo), _rows3(gates, geo), gates_t, conv_state, s5,
              prm['conv_w_a'], prm['conv_b_a'].reshape(n_even, 1, CONV_DIM_A),
              prm['dt_bias'][e].reshape(1, H_A), prm['dt_bias'][e].reshape(H_A, 1),
              prm['a_log'][e].reshape(1, H_A), prm['a_log'][e].reshape(H_A, 1),
              jnp.repeat(prm['d_skip'][e], P_A).reshape(1, D_A),
              prm['norm_a'][e].reshape(1, D_A), expand),
        out_specs=[_row_spec(geo, D_A, 0), _state_spec(cso_shape, 0, geo),
                   _state_spec(s5.shape, e, geo)],
        out_shapes=[
            jax.ShapeDtypeStruct((b, geo.seq_len, D_A), MXU_DTYPE),
            jax.ShapeDtypeStruct(cso_shape, F32),
            jax.ShapeDtypeStruct(s5.shape, F32),
        ],
        scratch=[pltpu.VMEM((geo.scan_seqs, SUBLANES, CONV_DIM_A), F32)],
        kinds=[ROWS] * 4 + [STATE, STATE] + [CONST] * 9 + [ROWS, STATE, STATE, LEAD],
        stacked_prev={} if prev_state_out is None else {2: prev_state_out},
    )
    return y.reshape(b * geo.seq_len, D_A), cso, so


def _mlstm_kernel(q_ref, k_ref, v_ref, og_ref, gc_ref, gr_ref, c0_ref, n0_ref, m0_ref,
                  ib_r_ref, ib_c_ref, fb_r_ref, fb_c_ref, nw_ref,
                  h_ref, co_ref, no_ref, mo_ref, *, lc, lv, single_chunk):
    if single_chunk:
        c_ref, n_ref, m_ref = c0_ref, n0_ref, m0_ref
    else:
        @pl.when(pl.program_id(1) == 0)
        def _():
            co_ref[0, 0] = c0_ref[0, 0]
            no_ref[0, 0] = n0_ref[0, 0]
            mo_ref[0, 0] = m0_ref[0, 0]

        c_ref, n_ref, m_ref = co_ref, no_ref, mo_ref
    m_old, n_old = m_ref[0, 0], n_ref[0, 0]

    i0, f0 = H_A, H_A + H_B
    li_c = gc_ref[:, i0:i0 + H_B] + ib_r_ref[...]
    lf_c = _log_sigmoid(gc_ref[:, f0:f0 + H_B] + fb_r_ref[...])
    li_r = gr_ref[0, i0:i0 + H_B, :] + ib_c_ref[...]
    lf_r = _log_sigmoid(gr_ref[0, f0:f0 + H_B, :] + fb_c_ref[...])
    if lv < lc:
        vc = lax.broadcasted_iota(jnp.int32, li_c.shape, 0) < lv
        vr = lax.broadcasted_iota(jnp.int32, li_r.shape, 1) < lv
        li_c, lf_c = jnp.where(vc, li_c, NEG_BIG), jnp.where(vc, lf_c, 0.0)
        li_r, lf_r = jnp.where(vr, li_r, NEG_BIG), jnp.where(vr, lf_r, 0.0)
    lower, upper, mask = _tri(lc)
    bc_c = jnp.dot(lower, lf_c, precision=HI)
    bc_r = jnp.dot(lf_r, upper, precision=HI)

    heads = range(H_B)
    q = [q_ref[:, h * DK_B:(h + 1) * DK_B] * (DK_B ** -0.5) for h in heads]
    k = [k_ref[:, h * DK_B:(h + 1) * DK_B] for h in heads]
    v = [v_ref[:, h * DV_B:(h + 1) * DV_B] for h in heads]
    yield
    qk = [lax.dot_general(q[h], k[h], _NT, preferred_element_type=F32) for h in heads]
    yield
    m_t, w_in, w_ts = [], [], []
    for h in heads:
        bcc = bc_c[:, h:h + 1]
        dmat = jnp.where(mask, bcc - bc_r[h:h + 1, :] + li_r[h:h + 1, :], -jnp.inf)
        inter = bcc + m_old[:, h:h + 1]
        m_t.append(jnp.maximum(inter, jnp.max(dmat, axis=1, keepdims=True)))
        w_in.append(jnp.exp(inter - m_t[h]))
        w_ts.append(jnp.exp(dmat - m_t[h]) * qk[h])
    yield
    q_c = [jnp.dot(q[h], c_ref[0, 0, h], preferred_element_type=F32) for h in heads]
    wv = [jnp.dot(w_ts[h], v[h], preferred_element_type=F32) for h in heads]
    yield
    hs = []
    for h in heads:
        num = wv[h] + w_in[h] * q_c[h]
        den = (jnp.sum(w_ts[h], axis=1, keepdims=True)
               + w_in[h] * jnp.sum(q[h] * n_old[h:h + 1, :], axis=1, keepdims=True))
        hs.append(_rmsnorm_rows(num / jnp.maximum(jnp.abs(den), jnp.exp(-m_t[h]))))
    lane_h = lax.broadcasted_iota(jnp.int32, (1, H_B), 1)
    m_out = jnp.zeros((1, H_B), F32)
    ks, w_c = [], []
    for h in heads:
        bcc = bc_c[:, h:h + 1]
        m_new = m_t[h][lv - 1:lv, :]
        bc_last = bcc[lc - 1:lc, :]
        ks.append(k[h] * jnp.exp(bc_last - bcc + li_c[:, h:h + 1] - m_new))
        w_c.append(jnp.exp(bc_last + m_old[:, h:h + 1] - m_new))
        m_out = jnp.where(lane_h == h, m_new, m_out)
    yield
    kv = [lax.dot_general(ks[h], v[h], _TN, preferred_element_type=F32) for h in heads]
    yield
    for h in heads:
        co_ref[0, 0, h] = w_c[h] * c_ref[0, 0, h] + kv[h]
        no_ref[0, 0, h:h + 1, :] = w_c[h] * n_old[h:h + 1, :] + jnp.sum(ks[h], axis=0, keepdims=True)
    mo_ref[0, 0] = m_out
    hn = jnp.concatenate(hs, axis=1) * nw_ref[...]
    h_ref[...] = (hn * _sigmoid(og_ref[...])).astype(h_ref.dtype)


def _mlstm(p1, gates, gates_t, c_state, n_state, m_state, prm, e, geo, prev_state_out):
    geo = geo._replace(scan_seqs=geo.mlstm_seqs)
    b = geo.n_seq
    const2 = lambda shape: pl.BlockSpec(shape, lambda i, c: (0, 0))
    qk_w = H_B * DK_B
    m4 = m_state.reshape(m_state.shape[0], b, 1, H_B)
    one = lambda shape: (1,) + tuple(shape[1:])
    h, co, no, mo = _seq_call(
        functools.partial(_mlstm_kernel, lc=geo.chunk, lv=geo.valid,
                          single_chunk=geo.seq_len == geo.chunk), geo, "mlstm",
        in_specs=[
            _row_spec(geo, qk_w, 9), _row_spec(geo, qk_w, 10),
            _row_spec(geo, D_B, 1), _row_spec(geo, D_B, 2),
            _row_spec(geo, GATE_W, 0),
            _gates_t_spec(geo),
            _state_spec(c_state.shape, e, geo), _state_spec(n_state.shape, e, geo),
            _state_spec(m4.shape, e, geo),
            const2((1, H_B)), const2((H_B, 1)), const2((1, H_B)), const2((H_B, 1)),
            const2((1, D_B)),
        ],
        args=(_rows3(p1, geo),) * 4 + (_rows3(gates, geo), gates_t, c_state, n_state, m4,
              prm['i_bias'][e].reshape(1, H_B), prm['i_bias'][e].reshape(H_B, 1),
              prm['f_bias'][e].reshape(1, H_B), prm['f_bias'][e].reshape(H_B, 1),
              prm['norm_b'][e].reshape(1, D_B)),
        out_specs=[_row_spec(geo, D_B, 0), _state_spec(c_state.shape, e, geo),
                   _state_spec(one(n_state.shape), 0, geo), _state_spec(one(m4.shape), 0, geo)],
        out_shapes=[
            jax.ShapeDtypeStruct((b, geo.seq_len, D_B), MXU_DTYPE),
            jax.ShapeDtypeStruct(c_state.shape, F32),
            jax.ShapeDtypeStruct(one(n_state.shape), F32),
            jax.ShapeDtypeStruct(one(m4.shape), F32),
        ],
        scratch=[],
        kinds=[ROWS] * 6 + [STATE] * 3 + [CONST] * 5 + [ROWS] + [STATE] * 3,
        stacked_prev={} if prev_state_out is None else {1: prev_state_out},
        interleave=geo.seq_len != geo.chunk,
    )
    return h.reshape(b * geo.seq_len, D_B), co, no, mo.reshape(1, b, H_B)


def _hgrn_kernel(q_ref, f_ref, i_ref, g_ref, lbl_ref, s0_ref, nw_ref, o_ref, so_ref,
                 *, lc, lv, layer_o, bs, single_chunk):
    if single_chunk:
        s_ref = s0_ref
    else:
        @pl.when(pl.program_id(1) == 0)
        def _():
            so_ref[0, 0] = s0_ref[0, 0]

        s_ref = so_ref

    lbl = lbl_ref[...]
    ex = jnp.exp(lbl - jnp.max(lbl, axis=0, keepdims=True))
    sm = ex / jnp.sum(ex, axis=0, keepdims=True)
    lb_all = [sm[0:1, :]]
    for r in range(1, lbl.shape[0]):
        lb_all.append(lb_all[-1] + sm[r:r + 1, :])
    lb = lb_all[layer_o] - lb_all[0]

    fx = f_ref[...]
    e1 = jnp.exp(-jnp.abs(fx))
    log_sig = jnp.minimum(fx, 0.0) - jnp.log1p(e1)
    la = jnp.log(lb)
    lb_ = jnp.log1p(-lb) + log_sig
    logf = jnp.maximum(la, lb_) + jnp.log1p(jnp.exp(-jnp.abs(la - lb_)))
    kk = (1.0 - lb) * (jnp.where(fx >= 0.0, e1, 1.0) / (1.0 + e1))
    if lv < lc:
        valid = lax.broadcasted_iota(jnp.int32, fx.shape, 0) < lv
        logf = jnp.where(valid, logf, 0.0)
        kk = jnp.where(valid, kk, 0.0)

    nb = lc // bs
    r_i = lax.broadcasted_iota(jnp.int32, (lc, lc), 0)
    c_i = lax.broadcasted_iota(jnp.int32, (lc, lc), 1)
    sh = int(math.log2(bs))
    blk_lower = ((c_i <= r_i) & ((c_i >> sh) == (r_i >> sh))).astype(F32)
    gw = jnp.dot(blk_lower, logf, precision=HI)
    q = q_ref[...]
    v = i_ref[...]
    blk = lambda a, i: a[i * bs:(i + 1) * bs, :]
    tots = [gw[(i + 1) * bs - 1:(i + 1) * bs, :] for i in range(nb)]
    before = [jnp.zeros_like(tots[0])]
    for i in range(nb):
        before.append(before[-1] + tots[i])
    g_tot = before[nb]
    qt = q * jnp.exp(gw)
    kt = [blk(kk, j) * jnp.exp(tots[j] - blk(gw, j)) for j in range(nb)]
    q_in = jnp.concatenate([blk(qt, i) * jnp.exp(before[i]) for i in range(nb)], axis=0)
    k_out = jnp.concatenate([kt[j] * jnp.exp(g_tot - before[j + 1]) for j in range(nb)], axis=0)

    heads = range(H_C)
    hsl = lambda h: slice(h * DK_C, (h + 1) * DK_C)
    yield
    o_inter = [jnp.dot(q_in[:, hsl(h)], s_ref[0, 0, h], preferred_element_type=F32)
               for h in heads]
    kv = [lax.dot_general(k_out[:, hsl(h)], v[:, hsl(h)], _TN, preferred_element_type=F32)
          for h in heads]
    att_off = [None]
    for i in range(1, nb):
        k_hat = jnp.concatenate(
            [kt[j] if j == i - 1 else kt[j] * jnp.exp(before[i] - before[j + 1])
             for j in range(i)] + [jnp.zeros((lc - i * bs, D_C), F32)], axis=0)
        q_ti = blk(qt, i)
        att_off.append([lax.dot_general(q_ti[:, hsl(h)], k_hat[:, hsl(h)], _NT,
                                        preferred_element_type=F32) for h in heads])
    yield
    gw2 = gw * LOG2E
    ck = jnp.log2(kk) - gw2
    n_t = bs // SUBLANES
    lane_s = lax.broadcasted_iota(jnp.int32, (SUBLANES, lc), 1)
    row_t = lax.broadcasted_iota(jnp.int32, (bs, lc), 0)
    col_s = lax.broadcasted_iota(jnp.int32, (bs, lc), 1)
    att = []
    for i in range(nb):
        g_i, q_i, ck_i = blk(gw2, i), blk(q, i), blk(ck, i)
        att_d = [[jnp.zeros((SUBLANES, lc), F32) for _ in range(n_t)] for _ in heads]
        for s in range(bs):
            t0 = s // SUBLANES
            p = q_i[t0 * SUBLANES:, :] * jnp.exp2(g_i[t0 * SUBLANES:, :] + ck_i[s:s + 1, :])
            for h in heads:
                a = jnp.sum(p[:, hsl(h)], axis=1, keepdims=True)
                for tt in range(t0, n_t):
                    a_t = a[(tt - t0) * SUBLANES:(tt - t0 + 1) * SUBLANES, :]
                    att_d[h][tt] = jnp.where(lane_s == i * bs + s, a_t, att_d[h][tt])
        causal = (col_s - i * bs) <= row_t
        att_i = []
        for h in heads:
            a = att_d[h][0] if n_t == 1 else jnp.concatenate(att_d[h], axis=0)
            a = jnp.where(causal, a, 0.0)
            att_i.append(a if i == 0 else a + att_off[i][h])
        att.append(att_i)
    yield
    o_intra = [[jnp.dot(att[i][h], v[:, hsl(h)], preferred_element_type=F32) for h in heads]
               for i in range(nb)]
    yield
    outs = []
    for h in heads:
        o_h = o_intra[0][h] if nb == 1 else jnp.concatenate([o_intra[i][h] for i in range(nb)],
                                                             axis=0)
        outs.append(_rmsnorm_rows(o_h + o_inter[h]))
        dec_col = jnp.transpose(jnp.broadcast_to(jnp.exp(g_tot[:, hsl(h)]), (DK_C, DK_C)))
        so_ref[0, 0, h] = dec_col * s_ref[0, 0, h] + kv[h]
    gate = g_ref[...]
    o_ref[...] = (jnp.concatenate(outs, axis=1) * nw_ref[...]
                  * (gate * _sigmoid(gate))).astype(o_ref.dtype)


def _hgrn(p, state, prm, o, geo, prev_state_out):
    geo = geo._replace(scan_seqs=geo.hgrn_seqs)
    n_odd = prm['lb_logits'].shape[0]
    out, new_state = _seq_call(
        functools.partial(_hgrn_kernel, lc=geo.chunk, lv=geo.valid, layer_o=o,
                          bs=min(16, geo.chunk), single_chunk=geo.seq_len == geo.chunk), geo, "hgrn",
        in_specs=[
            _row_spec(geo, D_C, 0), _row_spec(geo, D_C, 1), _row_spec(geo, D_C, 2),
            _row_spec(geo, D_C, 3),
            pl.BlockSpec((n_odd, D_C), lambda i, c: (0, 0)),
            _state_spec(state.shape, o, geo),
            pl.BlockSpec((1, D_C), lambda i, c: (0, 0)),
        ],
        args=(_rows3(p, geo),) * 4 + (prm['lb_logits'], state, prm['norm_c'][o].reshape(1, D_C)),
        out_specs=[_row_spec(geo, D_C, 0), _state_spec(state.shape, o, geo)],
        out_shapes=[
            jax.ShapeDtypeStruct((geo.n_seq, geo.seq_len, D_C), MXU_DTYPE),
            jax.ShapeDtypeStruct(state.shape, F32),
        ],
        scratch=[],
        kinds=[ROWS] * 4 + [CONST, STATE, CONST, ROWS, STATE],
        stacked_prev={} if prev_state_out is None else {1: prev_state_out},
    )
    return out.reshape(geo.n_seq * geo.seq_len, D_C), new_state


def _in_ab_column_groups():
    a0, a1, a2 = D_A, D_A + CONV_DIM_A, D_A + CONV_DIM_A + H_A
    q1 = a2 + H_B * DK_B
    k1 = q1 + H_B * DK_B
    v1 = k1 + D_B
    o1 = v1 + D_B
    end = o1 + 2 * H_B
    return [(0, a0), (k1, v1), (v1, o1), (a0, a1), (a2, q1), (q1, k1)], [(a1, a2), (o1, end)]


def _regroup_in_ab(wab):
    wide, narrow = _in_ab_column_groups()
    n_gate = sum(b - a for a, b in narrow)
    in_ab = jnp.concatenate([wab[:, :, a:b] for a, b in wide], axis=2).astype(MXU_DTYPE)
    gates = jnp.pad(jnp.concatenate([wab[:, :, a:b] for a, b in narrow], axis=2),
                    ((0, 0), (0, 0), (0, GATE_W - n_gate))).astype(MXU_DTYPE)
    return in_ab, gates


def _prep_weights(prm):
    wab = prm['w_in_ab']
    in_ab, gates_ab = _regroup_in_ab(wab)
    n_even, n_odd = wab.shape[0], prm['w_in_c'].shape[0]
    return {
        'in_ab': [[(in_ab, e, 0)] for e in range(n_even)],
        'gates_ab': [[(gates_ab, e, 0)] for e in range(n_even)],
        'out_ab': [[(prm['w_out_ab'], e, 0), (prm['w_out_ab'], e, 1)] for e in range(n_even)],
        'in_c': [[(prm['w_in_c'], o, 0)] for o in range(n_odd)],
        'out_c': [[(prm['w_out_c'], o, 0)] for o in range(n_odd)],
        'ffn_g': [[(prm['w_ffn_g'], l, 0)] for l in range(DEPTH)],
        'ffn_u': [[(prm['w_ffn_u'], l, 0)] for l in range(DEPTH)],
        'ffn_d': [[(prm['w_ffn_d'], l, 0)] for l in range(DEPTH)],
    }


def _trunk(x3, mod, states, prm, w, geo, tn_in, tn_res, tn_ffn, tn_down, emit, rows_in=None):
    conv_a, ssd, mem_c, mem_n, mem_m, hgrn, ffn_buf = states
    n_conv, n_n, n_m, n_ffn = [], [], [], []
    new_ssd = new_c = new_hgrn = None
    nc = geo.seq_len // geo.chunk
    wq = {name: list(per_layer) for name, per_layer in w.items()}

    def record(name, idx, emitted):
        if emitted is not None:
            wq[name][idx] = [(arr, 0, 0) for arr in emitted]

    casts = lambda wops: emit and any(wop[0].dtype != MXU_DTYPE for wop in wops)
    for layer in range(DEPTH):
        geo_in = geo if rows_in is None else geo._replace(row_blk=rows_in[layer])
        mm_norm = functools.partial(_mm_norm, x3, prm['norm_mix'][layer], mod, layer, 1, 0,
                                    geo=geo_in)
        if layer % 2 == 0:
            e = layer // 2
            p1, gates, _ = mm_norm(w['in_ab'][e][0], tn=tn_in[layer], narrow=w['gates_ab'][e][0])
            gates_t = gates.reshape(geo.n_seq, nc, geo.chunk, GATE_W).transpose(0, 1, 3, 2)
            ya, cv, new_ssd = _ssd(p1, gates, gates_t, conv_a, ssd, prm, e, geo, new_ssd)
            hb, new_c, nn, mmm = _mlstm(p1, gates, gates_t, mem_c, mem_n, mem_m, prm, e, geo, new_c)
            n_conv.append(cv); n_n.append(nn); n_m.append(mmm)
            x3, em = _mm_res([ya, hb], w['out_ab'][e], x3, mod, layer, 2, geo, tn_res[layer],
                             emit=casts(w['out_ab'][e]))
            record('out_ab', e, em)
        else:
            o = layer // 2
            p, _, em = mm_norm(w['in_c'][o][0], tn=tn_in[layer], emit=casts(w['in_c'][o]))
            record('in_c', o, None if em is None else [em])
            oc, new_hgrn = _hgrn(p, hgrn, prm, o, geo, new_hgrn)
            x3, em = _mm_res([oc], w['out_c'][o], x3, mod, layer, 2, geo, tn_res[layer],
                             emit=casts(w['out_c'][o]))
            record('out_c', o, em)
        act, fb, em = _ffn_in(x3, prm['norm_ffn'][layer], mod, layer, w['ffn_g'][layer][0],
                              w['ffn_u'][layer][0], ffn_buf, prm['conv_w_f'][layer],
                              prm['conv_b_f'][layer], geo_in, tn_ffn,
                              emit=casts(w['ffn_g'][layer] + w['ffn_u'][layer]))
        if em is not None:
            record('ffn_g', layer, em[:1])
            record('ffn_u', layer, em[1:])
        n_ffn.append(fb)
        x3, em = _mm_res([act], w['ffn_d'][layer], x3, mod, layer, 5, geo, tn_down,
                         emit=casts(w['ffn_d'][layer]))
        record('ffn_d', layer, em)
    y = _final_norm(x3, prm['norm_f'], geo)
    cat = lambda xs: jnp.concatenate(xs, axis=0)
    return (y, cat(n_conv), new_ssd.reshape(ssd.shape), new_c, cat(n_n), cat(n_m), new_hgrn,
            cat(n_ffn)), wq


def kernel(x_prompt, x_sample, c_prompt, c_sample, state_ssd_conv, state_ssd, state_mlstm_c, state_mlstm_n, state_mlstm_m, state_hgrn, state_ffn_conv, w_ada, b_ada, norm_mix, norm_ffn, w_in_ab, conv_w_a, conv_b_a, dt_bias, a_log, d_skip, norm_a, i_bias, f_bias, norm_b, w_out_ab, w_in_c, lb_logits, norm_c, w_out_c, w_ffn_g, w_ffn_u, conv_w_f, conv_b_f, w_ffn_d, norm_f):
    prm = dict(norm_mix=norm_mix, norm_ffn=norm_ffn, w_in_ab=w_in_ab, conv_w_a=conv_w_a,
               conv_b_a=conv_b_a, dt_bias=dt_bias, a_log=a_log, d_skip=d_skip, norm_a=norm_a,
               i_bias=i_bias, f_bias=f_bias, norm_b=norm_b, w_out_ab=w_out_ab, w_in_c=w_in_c,
               lb_logits=lb_logits, norm_c=norm_c, w_out_c=w_out_c, w_ffn_g=w_ffn_g,
               w_ffn_u=w_ffn_u, conv_w_f=conv_w_f, conv_b_f=conv_b_f, w_ffn_d=w_ffn_d,
               norm_f=norm_f)
    bp, lp, _ = x_prompt.shape
    bs, ls, _ = x_sample.shape
    n_even, n_odd = state_ssd.shape[0], state_hgrn.shape[0]
    w = _prep_weights(prm)

    n_c = bs + bp
    n_c_pad = -(-n_c // SUBLANES) * SUBLANES
    c_all = jnp.pad(jnp.concatenate([c_sample, c_prompt], axis=0), ((0, n_c_pad - n_c), (0, 0)))
    mod = _ada(c_all, w_ada, b_ada).reshape(DEPTH, n_c_pad, 1, 6 * D_MODEL)

    geo_s = Geo(n_seq=bs, seq_len=SUBLANES, seq_blk=bs, row_blk=SUBLANES, chunk=SUBLANES,
                valid=ls, mod_off=0, scan_seqs=8, mlstm_seqs=4, hgrn_seqs=8)
    xs = jnp.pad(x_sample, ((0, 0), (0, SUBLANES - ls), (0, 0)))
    st_s = (state_ssd_conv, state_ssd, state_mlstm_c, state_mlstm_n, state_mlstm_m, state_hgrn,
            state_ffn_conv)
    out_s, w_bf16 = _trunk(xs, mod, st_s, prm, w, geo_s, tn_in=(512,) * DEPTH,
                           tn_res=(512,) * DEPTH, tn_ffn=256, tn_down=256, emit=True)

    zeros = lambda *s: jnp.zeros(s, F32)
    st_p = (zeros(n_even, bp, CONV_K_A - 1, CONV_DIM_A), zeros(n_even, bp, H_A, P_A, N_A),
            zeros(n_even, bp, H_B, DK_B, DV_B), zeros(n_even, bp, H_B, DK_B),
            zeros(n_even, bp, H_B), zeros(n_odd, bp, H_C, DK_C, DV_C),
            zeros(DEPTH, bp, CONV_K_F - 1, D_FF))
    lc_p = math.gcd(lp, PROMPT_CHUNK)
    geo_p = Geo(n_seq=bp, seq_len=lp, seq_blk=1, row_blk=min(lp, 1024), chunk=lc_p, valid=lc_p,
                mod_off=bs, scan_seqs=1, mlstm_seqs=1, hgrn_seqs=2 if bp % 2 == 0 else 1)
    out_p, _ = _trunk(x_prompt, mod, st_p, prm, w_bf16, geo_p, tn_in=(1024,) * DEPTH,
                      tn_res=(1024,) * DEPTH, tn_ffn=FFN_TN, tn_down=FFN_DOWN_TN, emit=False)
    return (out_p[0], out_s[0][:, :ls]) + tuple(out_p[1:]) + tuple(out_s[1:])
```

```python
import collections
import functools
import math

import jax
import jax.numpy as jnp
from jax import lax
from jax.experimental import pallas as pl
from jax.experimental.pallas import tpu as pltpu

F32 = jnp.float32
MXU_DTYPE = jnp.bfloat16
HI = lax.Precision.HIGHEST
NEG_BIG = -1e30
LOG2E = 1.4426950408889634

D_MODEL = 2048
DEPTH = 4
EPS = 1e-6
PROMPT_CHUNK = 128
H_A, P_A, G_A, N_A, CONV_K_A = 32, 64, 4, 128, 4
D_A = H_A * P_A
CONV_DIM_A = D_A + 2 * G_A * N_A
H_B, DK_B, DV_B = 8, 128, 256
D_B = H_B * DV_B
H_C, DK_C, DV_C = 16, 128, 128
D_C = H_C * DV_C
D_FF, CONV_K_F = 5632, 3
FFN_TN = 512
FFN_DOWN_TN = 512
PROLOGUE_SPLIT = 4
GATE_W = 128
SUBLANES = 8
VMEM_LIMIT = 56 * 1024 * 1024

_NT = (((1,), (1,)), ((), ()))
_TN = (((0,), (0,)), ((), ()))

Geo = collections.namedtuple(
    "Geo", "n_seq seq_len seq_blk row_blk chunk valid mod_off scan_seqs mlstm_seqs hgrn_seqs")


def _cparams(sem):
    return pltpu.CompilerParams(dimension_semantics=sem, vmem_limit_bytes=VMEM_LIMIT)


def _sigmoid(x):
    return 1.0 / (1.0 + jnp.exp(-x))


def _softplus(x):
    return jnp.maximum(x, 0.0) + jnp.log1p(jnp.exp(-jnp.abs(x)))


def _log_sigmoid(x):
    return jnp.minimum(x, 0.0) - jnp.log1p(jnp.exp(-jnp.abs(x)))


def _tri(n):
    r = lax.broadcasted_iota(jnp.int32, (n, n), 0)
    c = lax.broadcasted_iota(jnp.int32, (n, n), 1)
    mask = c <= r
    return mask.astype(F32), (r <= c).astype(F32), mask


def _split3(a):
    hi = a.astype(MXU_DTYPE)
    rest = a - hi.astype(F32)
    mid = rest.astype(MXU_DTYPE)
    lo = (rest - mid.astype(F32)).astype(MXU_DTYPE)
    return hi, mid, lo


def _select_dot(a, b, selector):
    if selector == 'a':
        s = a.astype(MXU_DTYPE)
        parts = [jnp.dot(s, t, preferred_element_type=F32) for t in _split3(b)]
    else:
        s = b.astype(MXU_DTYPE)
        parts = [jnp.dot(t, s, preferred_element_type=F32) for t in _split3(a)]
    return parts[0] + parts[1] + parts[2]


def _shifted(x, tails, k, axis=0):
    rolled = pltpu.roll(x, k, axis)
    head = lax.slice_in_dim(rolled, 0, SUBLANES, axis=axis)
    row = lax.broadcasted_iota(jnp.int32, head.shape, axis)
    for r in range(k):
        head = jnp.where(row == r, tails[len(tails) - k + r], head)
    if x.shape[axis] == SUBLANES:
        return head
    rest = lax.slice_in_dim(rolled, SUBLANES, x.shape[axis], axis=axis)
    return jnp.concatenate([head, rest], axis=axis)


def _drop_ref(fn, idx):
    def wrapped(*refs):
        return fn(*refs[:idx], *refs[idx + 1:])
    return wrapped


def _rmsnorm_rows(x):
    return x * lax.rsqrt(jnp.mean(x * x, axis=-1, keepdims=True) + EPS)


def _ada_kernel(c_ref, w_ref, b_ref, o_ref):
    c = c_ref[...]
    ca = (c * _sigmoid(c)).astype(MXU_DTYPE)
    o_ref[0] = jnp.dot(ca, w_ref[0].astype(MXU_DTYPE), preferred_element_type=F32) + b_ref[0]


def _ada(c_all, w_ada, b_ada, tn=1024):
    rows = c_all.shape[0]
    n = w_ada.shape[2]
    return pl.pallas_call(
        _ada_kernel,
        grid=(DEPTH, n // tn),
        in_specs=[
            pl.BlockSpec((rows, D_MODEL), lambda l, j: (0, 0)),
            pl.BlockSpec((1, D_MODEL, tn), lambda l, j: (l, 0, j)),
            pl.BlockSpec((1, 1, tn), lambda l, j: (l, 0, j)),
        ],
        out_specs=pl.BlockSpec((1, rows, tn), lambda l, j: (l, 0, j)),
        out_shape=jax.ShapeDtypeStruct((DEPTH, rows, n), F32),
        compiler_params=_cparams(("parallel", "parallel")),
        name="ada",
    )(c_all, w_ada, b_ada.reshape(DEPTH, 1, n))


def _tiles(geo):
    nrt = geo.seq_len // geo.row_blk
    return nrt, (geo.n_seq // geo.seq_blk) * nrt, geo.seq_blk * geo.row_blk


def _x_spec(geo, width, col_of):
    nrt = geo.seq_len // geo.row_blk
    return pl.BlockSpec((geo.seq_blk, geo.row_blk, width),
                        lambda i, j: (i // nrt, i % nrt, col_of(j)))


def _mod_spec(geo, layer, width, col_of):
    nrt = geo.seq_len // geo.row_blk
    return pl.BlockSpec((1, geo.seq_blk, 1, width),
                        lambda i, j: (layer, geo.mod_off + i // nrt, 0, col_of(j)))


def _norm_mod_chunks(x_ref, nw_ref, sc_ref, sh_ref):
    seq_blk, row_blk, d = x_ref.shape
    for r in range(PROLOGUE_SPLIT):
        if seq_blk == 1:
            n = row_blk // PROLOGUE_SPLIT
            x, sc, sh = x_ref[:, r * n:(r + 1) * n], sc_ref[0], sh_ref[0]
        else:
            n = seq_blk // PROLOGUE_SPLIT
            x, sc, sh = x_ref[r * n:(r + 1) * n], sc_ref[0, r * n:(r + 1) * n], sh_ref[0, r * n:(r + 1) * n]
        h = (_rmsnorm_rows(x) * nw_ref[...]) * (1.0 + sc) + sh
        rows = h.shape[0] * h.shape[1]
        yield slice(r * rows, (r + 1) * rows), h.reshape(rows, d).astype(MXU_DTYPE)


def _weight_spec(wop, k, tn):
    _, layer, row_block = wop
    return pl.BlockSpec((1, k, tn), lambda i, j: (layer, row_block, j))


def _emit_spec_shape(k, n, tn):
    return (pl.BlockSpec((1, k, tn), lambda i, j: (0, 0, j)),
            jax.ShapeDtypeStruct((1, k, n), MXU_DTYPE))


def _weight_tile(w_ref, wq_ref):
    w = w_ref[0].astype(MXU_DTYPE)
    if wq_ref is not None:
        wq_ref[0] = w
    return w


def _mm_norm_kernel(*refs, emit, narrow):
    x_ref, nw_ref, sc_ref, sh_ref, w_ref = refs[:5]
    rest = list(refs[5:])
    wn_ref = rest.pop(0) if narrow else None
    o_ref = rest.pop(0)
    on_ref = rest.pop(0) if narrow else None
    wq_ref = rest.pop(0) if emit else None
    (h_scr,) = rest
    w = _weight_tile(w_ref, wq_ref)

    @pl.when(pl.program_id(1) == 0)
    def _():
        for rows, h in _norm_mod_chunks(x_ref, nw_ref, sc_ref, sh_ref):
            h_scr[rows, :] = h
            o_ref[rows, :] = jnp.dot(h, w, preferred_element_type=F32)
            if narrow:
                on_ref[rows, :] = jnp.dot(h, wn_ref[0], preferred_element_type=F32)

    @pl.when(pl.program_id(1) > 0)
    def _():
        o_ref[...] = jnp.dot(h_scr[...], w, preferred_element_type=F32)


def _mm_norm(x3, nw, mod, layer, k_sc, k_sh, wop, geo, tn, emit=False, narrow=None):
    _, n_i, tm = _tiles(geo)
    n = wop[0].shape[2]
    in_specs = [
        _x_spec(geo, D_MODEL, lambda j: 0),
        pl.BlockSpec((1, D_MODEL), lambda i, j: (0, 0)),
        _mod_spec(geo, layer, D_MODEL, lambda j: k_sc),
        _mod_spec(geo, layer, D_MODEL, lambda j: k_sh),
        _weight_spec(wop, D_MODEL, tn),
    ]
    args = [x3, nw.reshape(1, D_MODEL), mod, mod, wop[0]]
    out_specs = [pl.BlockSpec((tm, tn), lambda i, j: (i, j))]
    out_shape = [jax.ShapeDtypeStruct((geo.n_seq * geo.seq_len, n), F32)]
    if narrow is not None:
        n_narrow = narrow[0].shape[2]
        in_specs.append(pl.BlockSpec((1, D_MODEL, n_narrow), lambda i, j: (narrow[1], 0, 0)))
        args.append(narrow[0])
        out_specs.append(pl.BlockSpec((tm, n_narrow), lambda i, j: (i, 0)))
        out_shape.append(jax.ShapeDtypeStruct((geo.n_seq * geo.seq_len, n_narrow), F32))
    if emit:
        assert n_i == 1
        spec, shape = _emit_spec_shape(D_MODEL, n, tn)
        out_specs.append(spec)
        out_shape.append(shape)
    outs = list(pl.pallas_call(
        functools.partial(_mm_norm_kernel, emit=emit, narrow=narrow is not None),
        grid=(n_i, n // tn),
        in_specs=in_specs,
        out_specs=out_specs,
        out_shape=out_shape,
        scratch_shapes=[pltpu.VMEM((tm, D_MODEL), MXU_DTYPE)],
        compiler_params=_cparams(("parallel", "arbitrary")),
        name="mm_norm",
    )(*args))
    out = outs.pop(0)
    out_narrow = outs.pop(0) if narrow is not None else None
    return out, out_narrow, (outs.pop(0) if emit else None)


def _mm_res_kernel(*refs, n_lhs, emit):
    a_refs, w_refs = refs[:n_lhs], refs[n_lhs:2 * n_lhs]
    xres_ref, gate_ref, o_ref = refs[2 * n_lhs:2 * n_lhs + 3]
    wq_refs = refs[2 * n_lhs + 3:] if emit else (None,) * n_lhs
    acc = None
    for a_ref, w_ref, wq_ref in zip(a_refs, w_refs, wq_refs):
        d = jnp.dot(a_ref[...], _weight_tile(w_ref, wq_ref), preferred_element_type=F32)
        acc = d if acc is None else acc + d
    o_ref[...] = xres_ref[...] + gate_ref[0] * acc.reshape(o_ref.shape)


def _mm_res(a_list, wops, x3, mod, layer, k_gate, geo, tn, emit=False):
    _, n_i, tm = _tiles(geo)
    per = D_MODEL // tn
    in_specs = [pl.BlockSpec((tm, a.shape[1]), lambda i, j: (i, 0)) for a in a_list]
    in_specs += [_weight_spec(wop, a.shape[1], tn) for wop, a in zip(wops, a_list)]
    in_specs += [_x_spec(geo, tn, lambda j: j),
                 _mod_spec(geo, layer, tn, lambda j: k_gate * per + j)]
    out_specs = [_x_spec(geo, tn, lambda j: j)]
    out_shape = [jax.ShapeDtypeStruct(x3.shape, F32)]
    if emit:
        assert n_i == 1
        for a in a_list:
            spec, shape = _emit_spec_shape(a.shape[1], D_MODEL, tn)
            out_specs.append(spec)
            out_shape.append(shape)
    outs = pl.pallas_call(
        functools.partial(_mm_res_kernel, n_lhs=len(a_list), emit=emit),
        grid=(n_i, per),
        in_specs=in_specs,
        out_specs=out_specs,
        out_shape=out_shape,
        compiler_params=_cparams(("parallel", "parallel")),
        name="mm_res",
    )(*a_list, *[wop[0] for wop in wops], x3, mod)
    return outs[0], (list(outs[1:]) if emit else None)


def _ffn_in_kernel(x_ref, nw_ref, sc_ref, sh_ref, wg_ref, wu_ref, cs_ref, cw_ref, cb_ref,
                   a_ref, cso_ref, *rest, nrt, valid, emit):
    wgq_ref, wuq_ref, h_scr, tail_scr = rest if emit else (None, None) + rest
    i, j = pl.program_id(0), pl.program_id(1)

    seq_blk, row_blk, tn = x_ref.shape[0], x_ref.shape[1], a_ref.shape[1]
    w_g, w_u = _weight_tile(wg_ref, wgq_ref), _weight_tile(wu_ref, wuq_ref)

    def conv_gate(g, u):
        g, u = g.reshape(seq_blk, row_blk, tn), u.reshape(seq_blk, row_blk, tn)
        prev = cs_ref[0]
        if nrt > 1:
            prev = jnp.where(i % nrt == 0, prev, tail_scr[j])
        t2, t1 = prev[:, 0:1, :], prev[:, 1:2, :]
        w = cw_ref[...]
        y = (cb_ref[...] + w[0:1] * _shifted(g, [t2, t1], 2, axis=1)
             + w[1:2] * _shifted(g, [t2, t1], 1, axis=1) + w[2:3] * g)
        a_ref[...] = (y * _sigmoid(y) * u).reshape(a_ref.shape).astype(a_ref.dtype)
        new_tail = g[:, valid - 2:valid, :]
        if nrt > 1:
            tail_scr[j] = new_tail
        cso_ref[0] = new_tail

    @pl.when(j == 0)
    def _():
        gs, us = [], []
        for rows, h in _norm_mod_chunks(x_ref, nw_ref, sc_ref, sh_ref):
            h_scr[rows, :] = h
            gs.append(jnp.dot(h, w_g, preferred_element_type=F32))
            us.append(jnp.dot(h, w_u, preferred_element_type=F32))
        conv_gate(jnp.concatenate(gs, axis=0), jnp.concatenate(us, axis=0))

    @pl.when(j > 0)
    def _():
        h = h_scr[...]
        conv_gate(jnp.dot(h, w_g, preferred_element_type=F32),
                  jnp.dot(h, w_u, preferred_element_type=F32))


def _ffn_in(x3, nw, mod, layer, wop_g, wop_u, conv_state, conv_w, conv_b, geo, tn, emit=False):
    nrt, n_i, tm = _tiles(geo)
    n_j = D_FF // tn
    valid = geo.row_blk if geo.valid == geo.chunk else geo.valid
    tail_shape = (n_j, geo.seq_blk, CONV_K_F - 1, tn) if nrt > 1 else (1, 1, CONV_K_F - 1, 128)
    cs_spec = pl.BlockSpec((1, geo.seq_blk, CONV_K_F - 1, tn), lambda i, j: (layer, i // nrt, 0, j))
    emit_specs, emit_shapes = [], []
    if emit:
        assert n_i == 1
        for _ in range(2):
            spec, shape = _emit_spec_shape(D_MODEL, D_FF, tn)
            emit_specs.append(spec)
            emit_shapes.append(shape)
    act, tails, *emitted = pl.pallas_call(
        functools.partial(_ffn_in_kernel, nrt=nrt, valid=valid, emit=emit),
        grid=(n_i, n_j),
        in_specs=[
            _x_spec(geo, D_MODEL, lambda j: 0),
            pl.BlockSpec((1, D_MODEL), lambda i, j: (0, 0)),
            _mod_spec(geo, layer, D_MODEL, lambda j: 4),
            _mod_spec(geo, layer, D_MODEL, lambda j: 3),
            _weight_spec(wop_g, D_MODEL, tn),
            _weight_spec(wop_u, D_MODEL, tn),
            cs_spec,
            pl.BlockSpec((CONV_K_F, tn), lambda i, j: (0, j)),
            pl.BlockSpec((1, tn), lambda i, j: (0, j)),
        ],
        out_specs=[
            pl.BlockSpec((tm, tn), lambda i, j: (i, j)),
            pl.BlockSpec((1, geo.seq_blk, CONV_K_F - 1, tn), lambda i, j: (i, 0, 0, j)),
        ] + emit_specs,
        out_shape=[
            jax.ShapeDtypeStruct((geo.n_seq * geo.seq_len, D_FF), MXU_DTYPE),
            jax.ShapeDtypeStruct((n_i, geo.seq_blk, CONV_K_F - 1, D_FF), F32),
        ] + emit_shapes,
        scratch_shapes=[pltpu.VMEM((tm, D_MODEL), MXU_DTYPE),
                        pltpu.VMEM(tail_shape, F32)],
        compiler_params=_cparams(("arbitrary", "arbitrary")),
        name="ffn_in",
    )(x3, nw.reshape(1, D_MODEL), mod, mod, wop_g[0], wop_u[0], conv_state, conv_w,
      conv_b.reshape(1, D_FF))
    last = tails.reshape(n_i // nrt, nrt, geo.seq_blk, CONV_K_F - 1, D_FF)[:, nrt - 1]
    return act, last.reshape(1, geo.n_seq, CONV_K_F - 1, D_FF), (emitted if emit else None)


def _norm_kernel(x_ref, nw_ref, o_ref):
    o_ref[...] = _rmsnorm_rows(x_ref[...]) * nw_ref[...]


def _final_norm(x3, nw, geo):
    _, n_i, _ = _tiles(geo)
    return pl.pallas_call(
        _norm_kernel,
        grid=(n_i, 1),
        in_specs=[_x_spec(geo, D_MODEL, lambda j: 0),
                  pl.BlockSpec((1, D_MODEL), lambda i, j: (0, 0))],
        out_specs=_x_spec(geo, D_MODEL, lambda j: 0),
        out_shape=jax.ShapeDtypeStruct(x3.shape, F32),
        compiler_params=_cparams(("parallel", "arbitrary")),
        name="final_norm",
    )(x3, nw.reshape(1, D_MODEL))


def _state_spec(shape, layer, geo):
    rest = tuple(shape[2:])
    zeros = (0,) * len(rest)
    return pl.BlockSpec((1, geo.scan_seqs) + rest, lambda i, c: (layer, i) + zeros)


def _row_spec(geo, width, col):
    return pl.BlockSpec((geo.scan_seqs, geo.chunk, width), lambda i, c: (i, c, col))


def _rows3(a, geo):
    return a.reshape(geo.n_seq, geo.seq_len, a.shape[-1])


def _gates_t_spec(geo):
    return pl.BlockSpec((geo.scan_seqs, 1, GATE_W, geo.chunk), lambda i, c: (i, c, 0, 0))


ROWS, LEAD, STATE, CONST = "rows", "lead", "state", "const"


def _per_sequence(body, geo, kinds, interleave):
    def view(ref, kind, k):
        if kind == ROWS:
            return ref.at[k]
        if kind == LEAD:
            return ref.at[pl.ds(k, 1)]
        if kind == STATE:
            return ref.at[:, pl.ds(k, 1)]
        return ref

    def wrapped(*refs):
        running = [body(*[view(r, kind, k) for r, kind in zip(refs, kinds)])
                   for k in range(geo.scan_seqs)]
        if not interleave:
            for gen in running:
                for _ in gen:
                    pass
            return
        while running:
            still = []
            for gen in running:
                if next(gen, StopIteration) is not StopIteration:
                    still.append(gen)
            running = still
    return wrapped


def _seq_call(kernel_fn, geo, name, in_specs, args, out_specs, out_shapes, scratch, kinds,
              stacked_prev, interleave=True):
    nc = geo.seq_len // geo.chunk
    kernel_fn = _per_sequence(kernel_fn, geo, kinds, interleave)
    in_specs, args = list(in_specs), list(args)
    aliases = {}
    assert len(stacked_prev) <= 1
    for out_idx, arr in stacked_prev.items():
        kernel_fn = _drop_ref(kernel_fn, len(in_specs))
        aliases[len(in_specs)] = out_idx
        in_specs.append(pl.BlockSpec(memory_space=pl.ANY))
        args.append(arr)
    return pl.pallas_call(
        kernel_fn,
        grid=(geo.n_seq // geo.scan_seqs, nc),
        in_specs=in_specs,
        out_specs=out_specs,
        out_shape=out_shapes,
        scratch_shapes=scratch,
        input_output_aliases=aliases,
        compiler_params=_cparams(("parallel", "arbitrary")),
        name=name,
    )(*args)


def _ssd_kernel(z_ref, xbc_ref, gc_ref, gr_ref, cs_ref, s0_ref, cw_ref, cb_ref,
                dtb_r_ref, dtb_c_ref, al_r_ref, al_c_ref, dsk_ref, nw_ref, exp_ref,
                y_ref, cso_ref, so_ref, tail_scr, *, lc, lv, single_chunk):
    if single_chunk:
        s_ref = s0_ref
        cs = cs_ref[0, 0]
        tails = [cs[0:1, :], cs[1:2, :], cs[2:3, :]]
    else:
        @pl.when(pl.program_id(1) == 0)
        def _():
            tail_scr[0, 5:8, :] = cs_ref[0, 0]
            so_ref[0, 0] = s0_ref[0, 0]

        s_ref = so_ref
        tails = [tail_scr[0, 5:6, :], tail_scr[0, 6:7, :], tail_scr[0, 7:8, :]]

    x = xbc_ref[...]
    w = cw_ref[0]
    xc = (cb_ref[0] + w[0:1] * _shifted(x, tails, 3) + w[1:2] * _shifted(x, tails, 2)
          + w[2:3] * _shifted(x, tails, 1) + w[3:4] * x)
    xc = xc * _sigmoid(xc)
    new_tail = x[lv - 3:lv, :]
    if not single_chunk:
        tail_scr[0, 5:8, :] = new_tail
    cso_ref[0, 0] = new_tail
    xa = xc[:, :D_A]
    bm = xc[:, D_A:D_A + G_A * N_A]
    cm = xc[:, D_A + G_A * N_A:]

    dt_c = _softplus(gc_ref[:, 0:H_A] + dtb_r_ref[...])
    dt_r = _softplus(gr_ref[0, 0:H_A, :] + dtb_c_ref[...])
    if lv < lc:
        dt_c = jnp.where(lax.broadcasted_iota(jnp.int32, dt_c.shape, 0) < lv, dt_c, 0.0)
        dt_r = jnp.where(lax.broadcasted_iota(jnp.int32, dt_r.shape, 1) < lv, dt_r, 0.0)
    lower, upper, mask = _tri(lc)
    cum_c = _select_dot(lower, dt_c * (-jnp.exp(al_r_ref[...])), 'a')
    cum_r = _select_dot(dt_r * (-jnp.exp(al_c_ref[...])), upper, 'b')
    cum_last = cum_c[lc - 1:lc, :]
    expand = exp_ref[...]
    ecum_x = _select_dot(jnp.exp(cum_c), expand, 'b')
    tail_x = _select_dot(jnp.exp(cum_last - cum_c) * dt_c, expand, 'b')

    hg = H_A // G_A
    gw = hg * P_A
    groups = range(G_A)
    gsl = lambda g: slice(g * gw, (g + 1) * gw)
    cg = [cm[:, g * N_A:(g + 1) * N_A] for g in groups]
    bg = [bm[:, g * N_A:(g + 1) * N_A] for g in groups]
    yield
    cb_ts = [lax.dot_general(cg[g], bg[g], _NT, preferred_element_type=F32) for g in groups]
    y_inter = [lax.dot_general(cg[g], s_ref[0, 0, g], _NT, preferred_element_type=F32)
               for g in groups]
    upd = [lax.dot_general(xa[:, gsl(g)] * tail_x[:, gsl(g)], bg[g], _TN,
                           preferred_element_type=F32) for g in groups]
    yield
    w_ts = [cb_ts[h // hg] * dt_r[h:h + 1, :]
            * jnp.exp(jnp.where(mask, cum_c[:, h:h + 1] - cum_r[h:h + 1, :], -jnp.inf))
            for h in range(H_A)]
    yield
    pieces = []
    for j in range(H_A // 2):
        xp = xa[:, j * 128:(j + 1) * 128]
        lane = lax.broadcasted_iota(jnp.int32, xp.shape, 1)
        pieces.append(
            jnp.dot(w_ts[2 * j], jnp.where(lane < P_A, xp, 0.0), preferred_element_type=F32)
            + jnp.dot(w_ts[2 * j + 1], jnp.where(lane >= P_A, xp, 0.0),
                      preferred_element_type=F32))
    yield
    y = (jnp.concatenate(pieces, axis=1) + jnp.concatenate(y_inter, axis=1) * ecum_x
         + dsk_ref[...] * xa)
    for g in groups:
        for hh in range(hg):
            h = g * hg + hh
            rows = slice(hh * P_A, (hh + 1) * P_A)
            so_ref[0, 0, g, rows, :] = (s_ref[0, 0, g, rows, :] * jnp.exp(cum_r[h:h + 1, lc - 1:lc])
                                        + upd[g][rows, :])

    z = z_ref[...]
    y = y * (z * _sigmoid(z))
    y = jnp.concatenate([_rmsnorm_rows(y[:, g * gw:(g + 1) * gw]) for g in range(G_A)], axis=1)
    y_ref[...] = (y * nw_ref[...]).astype(y_ref.dtype)


def _ssd(p1, gates, gates_t, conv_state, ssd_state, prm, e, geo, prev_state_out):
    b = geo.n_seq
    const2 = lambda shape: pl.BlockSpec(shape, lambda i, c: (0, 0))
    n_even = ssd_state.shape[0]
    s5 = ssd_state.reshape(n_even, b, G_A, (H_A // G_A) * P_A, N_A)
    expand = (jnp.arange(D_A)[None, :] // P_A == jnp.arange(H_A)[:, None]).astype(F32)
    cso_shape = (1,) + conv_state.shape[1:]
    y, cso, so = _seq_call(
        functools.partial(_ssd_kernel, lc=geo.chunk, lv=geo.valid,
                          single_chunk=geo.seq_len == geo.chunk), geo, "ssd",
        in_specs=[
            _row_spec(geo, D_A, 0),
            _row_spec(geo, CONV_DIM_A, 2),
            _row_spec(geo, GATE_W, 0),
            _gates_t_spec(geo),
            _state_spec(conv_state.shape, e, geo),
            _state_spec(s5.shape, e, geo),
            pl.BlockSpec((1, CONV_K_A, CONV_DIM_A), lambda i, c: (e, 0, 0)),
            pl.BlockSpec((1, 1, CONV_DIM_A), lambda i, c: (e, 0, 0)),
            const2((1, H_A)), const2((H_A, 1)), const2((1, H_A)), const2((H_A, 1)),
            const2((1, D_A)), const2((1, D_A)), const2((H_A, D_A)),
        ],
        args=(_rows3(p1, geo), _rows3(p1, geo), _rows3(gates, geo), gates_t, conv_state, s5,
              prm['conv_w_a'], prm['conv_b_a'].reshape(n_even, 1, CONV_DIM_A),
              prm['dt_bias'][e].reshape(1, H_A), prm['dt_bias'][e].reshape(H_A, 1),
              prm['a_log'][e].reshape(1, H_A), prm['a_log'][e].reshape(H_A, 1),
              jnp.repeat(prm['d_skip'][e], P_A).reshape(1, D_A),
              prm['norm_a'][e].reshape(1, D_A), expand),
        out_specs=[_row_spec(geo, D_A, 0), _state_spec(cso_shape, 0, geo),
                   _state_spec(s5.shape, e, geo)],
        out_shapes=[
            jax.ShapeDtypeStruct((b, geo.seq_len, D_A), MXU_DTYPE),
            jax.ShapeDtypeStruct(cso_shape, F32),
            jax.ShapeDtypeStruct(s5.shape, F32),
        ],
        scratch=[pltpu.VMEM((geo.scan_seqs, SUBLANES, CONV_DIM_A), F32)],
        kinds=[ROWS] * 4 + [STATE, STATE] + [CONST] * 9 + [ROWS, STATE, STATE, LEAD],
        stacked_prev={} if prev_state_out is None else {2: prev_state_out},
    )
    return y.reshape(b * geo.seq_len, D_A), cso, so


def _mlstm_kernel(q_ref, k_ref, v_ref, og_ref, gc_ref, gr_ref, c0_ref, n0_ref, m0_ref,
                  ib_r_ref, ib_c_ref, fb_r_ref, fb_c_ref, nw_ref,
                  h_ref, co_ref, no_ref, mo_ref, *, lc, lv, single_chunk):
    if single_chunk:
        c_ref, n_ref, m_ref = c0_ref, n0_ref, m0_ref
    else:
        @pl.when(pl.program_id(1) == 0)
        def _():
            co_ref[0, 0] = c0_ref[0, 0]
            no_ref[0, 0] = n0_ref[0, 0]
            mo_ref[0, 0] = m0_ref[0, 0]

        c_ref, n_ref, m_ref = co_ref, no_ref, mo_ref
    m_old, n_old = m_ref[0, 0], n_ref[0, 0]

    i0, f0 = H_A, H_A + H_B
    li_c = gc_ref[:, i0:i0 + H_B] + ib_r_ref[...]
    lf_c = _log_sigmoid(gc_ref[:, f0:f0 + H_B] + fb_r_ref[...])
    li_r = gr_ref[0, i0:i0 + H_B, :] + ib_c_ref[...]
    lf_r = _log_sigmoid(gr_ref[0, f0:f0 + H_B, :] + fb_c_ref[...])
    if lv < lc:
        vc = lax.broadcasted_iota(jnp.int32, li_c.shape, 0) < lv
        vr = lax.broadcasted_iota(jnp.int32, li_r.shape, 1) < lv
        li_c, lf_c = jnp.where(vc, li_c, NEG_BIG), jnp.where(vc, lf_c, 0.0)
        li_r, lf_r = jnp.where(vr, li_r, NEG_BIG), jnp.where(vr, lf_r, 0.0)
    lower, upper, mask = _tri(lc)
    bc_c = jnp.dot(lower, lf_c, precision=HI)
    bc_r = jnp.dot(lf_r, upper, precision=HI)

    heads = range(H_B)
    q = [q_ref[:, h * DK_B:(h + 1) * DK_B] * (DK_B ** -0.5) for h in heads]
    k = [k_ref[:, h * DK_B:(h + 1) * DK_B] for h in heads]
    v = [v_ref[:, h * DV_B:(h + 1) * DV_B] for h in heads]
    yield
    qk = [lax.dot_general(q[h], k[h], _NT, preferred_element_type=F32) for h in heads]
    yield
    m_t, w_in, w_ts = [], [], []
    for h in heads:
        bcc = bc_c[:, h:h + 1]
        dmat = jnp.where(mask, bcc - bc_r[h:h + 1, :] + li_r[h:h + 1, :], -jnp.inf)
        inter = bcc + m_old[:, h:h + 1]
        m_t.append(jnp.maximum(inter, jnp.max(dmat, axis=1, keepdims=True)))
        w_in.append(jnp.exp(inter - m_t[h]))
        w_ts.append(jnp.exp(dmat - m_t[h]) * qk[h])
    yield
    q_c = [jnp.dot(q[h], c_ref[0, 0, h], preferred_element_type=F32) for h in heads]
    wv = [jnp.dot(w_ts[h], v[h], preferred_element_type=F32) for h in heads]
    yield
    hs = []
    for h in heads:
        num = wv[h] + w_in[h] * q_c[h]
        den = (jnp.sum(w_ts[h], axis=1, keepdims=True)
               + w_in[h] * jnp.sum(q[h] * n_old[h:h + 1, :], axis=1, keepdims=True))
        hs.append(_rmsnorm_rows(num / jnp.maximum(jnp.abs(den), jnp.exp(-m_t[h]))))
    lane_h = lax.broadcasted_iota(jnp.int32, (1, H_B), 1)
    m_out = jnp.zeros((1, H_B), F32)
    ks, w_c = [], []
    for h in heads:
        bcc = bc_c[:, h:h + 1]
        m_new = m_t[h][lv - 1:lv, :]
        bc_last = bcc[lc - 1:lc, :]
        ks.append(k[h] * jnp.exp(bc_last - bcc + li_c[:, h:h + 1] - m_new))
        w_c.append(jnp.exp(bc_last + m_old[:, h:h + 1] - m_new))
        m_out = jnp.where(lane_h == h, m_new, m_out)
    yield
    kv = [lax.dot_general(ks[h], v[h], _TN, preferred_element_type=F32) for h in heads]
    yield
    for h in heads:
        co_ref[0, 0, h] = w_c[h] * c_ref[0, 0, h] + kv[h]
        no_ref[0, 0, h:h + 1, :] = w_c[h] * n_old[h:h + 1, :] + jnp.sum(ks[h], axis=0, keepdims=True)
    mo_ref[0, 0] = m_out
    hn = jnp.concatenate(hs, axis=1) * nw_ref[...]
    h_ref[...] = (hn * _sigmoid(og_ref[...])).astype(h_ref.dtype)


def _mlstm(p1, gates, gates_t, c_state, n_state, m_state, prm, e, geo, prev_state_out):
    geo = geo._replace(scan_seqs=geo.mlstm_seqs)
    b = geo.n_seq
    const2 = lambda shape: pl.BlockSpec(shape, lambda i, c: (0, 0))
    qk_w = H_B * DK_B
    m4 = m_state.reshape(m_state.shape[0], b, 1, H_B)
    one = lambda shape: (1,) + tuple(shape[1:])
    h, co, no, mo = _seq_call(
        functools.partial(_mlstm_kernel, lc=geo.chunk, lv=geo.valid,
                          single_chunk=geo.seq_len == geo.chunk), geo, "mlstm",
        in_specs=[
            _row_spec(geo, qk_w, 9), _row_spec(geo, qk_w, 10),
            _row_spec(geo, D_B, 1), _row_spec(geo, D_B, 2),
            _row_spec(geo, GATE_W, 0),
            _gates_t_spec(geo),
            _state_spec(c_state.shape, e, geo), _state_spec(n_state.shape, e, geo),
            _state_spec(m4.shape, e, geo),
            const2((1, H_B)), const2((H_B, 1)), const2((1, H_B)), const2((H_B, 1)),
            const2((1, D_B)),
        ],
        args=(_rows3(p1, geo),) * 4 + (_rows3(gates, geo), gates_t, c_state, n_state, m4,
              prm['i_bias'][e].reshape(1, H_B), prm['i_bias'][e].reshape(H_B, 1),
              prm['f_bias'][e].reshape(1, H_B), prm['f_bias'][e].reshape(H_B, 1),
              prm['norm_b'][e].reshape(1, D_B)),
        out_specs=[_row_spec(geo, D_B, 0), _state_spec(c_state.shape, e, geo),
                   _state_spec(one(n_state.shape), 0, geo), _state_spec(one(m4.shape), 0, geo)],
        out_shapes=[
            jax.ShapeDtypeStruct((b, geo.seq_len, D_B), MXU_DTYPE),
            jax.ShapeDtypeStruct(c_state.shape, F32),
            jax.ShapeDtypeStruct(one(n_state.shape), F32),
            jax.ShapeDtypeStruct(one(m4.shape), F32),
        ],
        scratch=[],
        kinds=[ROWS] * 6 + [STATE] * 3 + [CONST] * 5 + [ROWS] + [STATE] * 3,
        stacked_prev={} if prev_state_out is None else {1: prev_state_out},
        interleave=geo.seq_len != geo.chunk,
    )
    return h.reshape(b * geo.seq_len, D_B), co, no, mo.reshape(1, b, H_B)


def _hgrn_kernel(q_ref, f_ref, i_ref, g_ref, lbl_ref, s0_ref, nw_ref, o_ref, so_ref,
                 *, lc, lv, layer_o, bs, single_chunk):
    if single_chunk:
        s_ref = s0_ref
    else:
        @pl.when(pl.program_id(1) == 0)
        def _():
            so_ref[0, 0] = s0_ref[0, 0]

        s_ref = so_ref

    lbl = lbl_ref[...]
    ex = jnp.exp(lbl - jnp.max(lbl, axis=0, keepdims=True))
    sm = ex / jnp.sum(ex, axis=0, keepdims=True)
    lb_all = [sm[0:1, :]]
    for r in range(1, lbl.shape[0]):
        lb_all.append(lb_all[-1] + sm[r:r + 1, :])
    lb = lb_all[layer_o] - lb_all[0]

    fx = f_ref[...]
    e1 = jnp.exp(-jnp.abs(fx))
    log_sig = jnp.minimum(fx, 0.0) - jnp.log1p(e1)
    la = jnp.log(lb)
    lb_ = jnp.log1p(-lb) + log_sig
    logf = jnp.maximum(la, lb_) + jnp.log1p(jnp.exp(-jnp.abs(la - lb_)))
    kk = (1.0 - lb) * (jnp.where(fx >= 0.0, e1, 1.0) / (1.0 + e1))
    if lv < lc:
        valid = lax.broadcasted_iota(jnp.int32, fx.shape, 0) < lv
        logf = jnp.where(valid, logf, 0.0)
        kk = jnp.where(valid, kk, 0.0)

    nb = lc // bs
    r_i = lax.broadcasted_iota(jnp.int32, (lc, lc), 0)
    c_i = lax.broadcasted_iota(jnp.int32, (lc, lc), 1)
    sh = int(math.log2(bs))
    blk_lower = ((c_i <= r_i) & ((c_i >> sh) == (r_i >> sh))).astype(F32)
    gw = jnp.dot(blk_lower, logf, precision=HI)
    q = q_ref[...]
    v = i_ref[...]
    blk = lambda a, i: a[i * bs:(i + 1) * bs, :]
    tots = [gw[(i + 1) * bs - 1:(i + 1) * bs, :] for i in range(nb)]
    before = [jnp.zeros_like(tots[0])]
    for i in range(nb):
        before.append(before[-1] + tots[i])
    g_tot = before[nb]
    qt = q * jnp.exp(gw)
    kt = [blk(kk, j) * jnp.exp(tots[j] - blk(gw, j)) for j in range(nb)]
    q_in = jnp.concatenate([blk(qt, i) * jnp.exp(before[i]) for i in range(nb)], axis=0)
    k_out = jnp.concatenate([kt[j] * jnp.exp(g_tot - before[j + 1]) for j in range(nb)], axis=0)

    heads = range(H_C)
    hsl = lambda h: slice(h * DK_C, (h + 1) * DK_C)
    yield
    o_inter = [jnp.dot(q_in[:, hsl(h)], s_ref[0, 0, h], preferred_element_type=F32)
               for h in heads]
    kv = [lax.dot_general(k_out[:, hsl(h)], v[:, hsl(h)], _TN, preferred_element_type=F32)
          for h in heads]
    att_off = [None]
    for i in range(1, nb):
        k_hat = jnp.concatenate(
            [kt[j] if j == i - 1 else kt[j] * jnp.exp(before[i] - before[j + 1])
             for j in range(i)] + [jnp.zeros((lc - i * bs, D_C), F32)], axis=0)
        q_ti = blk(qt, i)
        att_off.append([lax.dot_general(q_ti[:, hsl(h)], k_hat[:, hsl(h)], _NT,
                                        preferred_element_type=F32) for h in heads])
    yield
    gw2 = gw * LOG2E
    ck = jnp.log2(kk) - gw2
    n_t = bs // SUBLANES
    lane_s = lax.broadcasted_iota(jnp.int32, (SUBLANES, lc), 1)
    row_t = lax.broadcasted_iota(jnp.int32, (bs, lc), 0)
    col_s = lax.broadcasted_iota(jnp.int32, (bs, lc), 1)
    att = []
    for i in range(nb):
        g_i, q_i, ck_i = blk(gw2, i), blk(q, i), blk(ck, i)
        att_d = [[jnp.zeros((SUBLANES, lc), F32) for _ in range(n_t)] for _ in heads]
        for s in range(bs):
            t0 = s // SUBLANES
            p = q_i[t0 * SUBLANES:, :] * jnp.exp2(g_i[t0 * SUBLANES:, :] + ck_i[s:s + 1, :])
            for h in heads:
                a = jnp.sum(p[:, hsl(h)], axis=1, keepdims=True)
                for tt in range(t0, n_t):
                    a_t = a[(tt - t0) * SUBLANES:(tt - t0 + 1) * SUBLANES, :]
                    att_d[h][tt] = jnp.where(lane_s == i * bs + s, a_t, att_d[h][tt])
        causal = (col_s - i * bs) <= row_t
        att_i = []
        for h in heads:
            a = att_d[h][0] if n_t == 1 else jnp.concatenate(att_d[h], axis=0)
            a = jnp.where(causal, a, 0.0)
            att_i.append(a if i == 0 else a + att_off[i][h])
        att.append(att_i)
    yield
    o_intra = [[jnp.dot(att[i][h], v[:, hsl(h)], preferred_element_type=F32) for h in heads]
               for i in range(nb)]
    yield
    outs = []
    for h in heads:
        o_h = o_intra[0][h] if nb == 1 else jnp.concatenate([o_intra[i][h] for i in range(nb)],
                                                             axis=0)
        outs.append(_rmsnorm_rows(o_h + o_inter[h]))
        dec_col = jnp.transpose(jnp.broadcast_to(jnp.exp(g_tot[:, hsl(h)]), (DK_C, DK_C)))
        so_ref[0, 0, h] = dec_col * s_ref[0, 0, h] + kv[h]
    gate = g_ref[...]
    o_ref[...] = (jnp.concatenate(outs, axis=1) * nw_ref[...]
                  * (gate * _sigmoid(gate))).astype(o_ref.dtype)


def _hgrn(p, state, prm, o, geo, prev_state_out):
    geo = geo._replace(scan_seqs=geo.hgrn_seqs)
    n_odd = prm['lb_logits'].shape[0]
    out, new_state = _seq_call(
        functools.partial(_hgrn_kernel, lc=geo.chunk, lv=geo.valid, layer_o=o,
                          bs=min(16, geo.chunk), single_chunk=geo.seq_len == geo.chunk), geo, "hgrn",
        in_specs=[
            _row_spec(geo, D_C, 0), _row_spec(geo, D_C, 1), _row_spec(geo, D_C, 2),
            _row_spec(geo, D_C, 3),
            pl.BlockSpec((n_odd, D_C), lambda i, c: (0, 0)),
            _state_spec(state.shape, o, geo),
            pl.BlockSpec((1, D_C), lambda i, c: (0, 0)),
        ],
        args=(_rows3(p, geo),) * 4 + (prm['lb_logits'], state, prm['norm_c'][o].reshape(1, D_C)),
        out_specs=[_row_spec(geo, D_C, 0), _state_spec(state.shape, o, geo)],
        out_shapes=[
            jax.ShapeDtypeStruct((geo.n_seq, geo.seq_len, D_C), MXU_DTYPE),
            jax.ShapeDtypeStruct(state.shape, F32),
        ],
        scratch=[],
        kinds=[ROWS] * 4 + [CONST, STATE, CONST, ROWS, STATE],
        stacked_prev={} if prev_state_out is None else {1: prev_state_out},
    )
    return out.reshape(geo.n_seq * geo.seq_len, D_C), new_state


def _in_ab_column_groups():
    a0, a1, a2 = D_A, D_A + CONV_DIM_A, D_A + CONV_DIM_A + H_A
    q1 = a2 + H_B * DK_B
    k1 = q1 + H_B * DK_B
    v1 = k1 + D_B
    o1 = v1 + D_B
    end = o1 + 2 * H_B
    return [(0, a0), (k1, v1), (v1, o1), (a0, a1), (a2, q1), (q1, k1)], [(a1, a2), (o1, end)]


def _regroup_in_ab(wab):
    wide, narrow = _in_ab_column_groups()
    n_gate = sum(b - a for a, b in narrow)
    in_ab = jnp.concatenate([wab[:, :, a:b] for a, b in wide], axis=2).astype(MXU_DTYPE)
    gates = jnp.pad(jnp.concatenate([wab[:, :, a:b] for a, b in narrow], axis=2),
                    ((0, 0), (0, 0), (0, GATE_W - n_gate))).astype(MXU_DTYPE)
    return in_ab, gates


def _prep_weights(prm):
    wab = prm['w_in_ab']
    in_ab, gates_ab = _regroup_in_ab(wab)
    n_even, n_odd = wab.shape[0], prm['w_in_c'].shape[0]
    return {
        'in_ab': [[(in_ab, e, 0)] for e in range(n_even)],
        'gates_ab': [[(gates_ab, e, 0)] for e in range(n_even)],
        'out_ab': [[(prm['w_out_ab'], e, 0), (prm['w_out_ab'], e, 1)] for e in range(n_even)],
        'in_c': [[(prm['w_in_c'], o, 0)] for o in range(n_odd)],
        'out_c': [[(prm['w_out_c'], o, 0)] for o in range(n_odd)],
        'ffn_g': [[(prm['w_ffn_g'], l, 0)] for l in range(DEPTH)],
        'ffn_u': [[(prm['w_ffn_u'], l, 0)] for l in range(DEPTH)],
        'ffn_d': [[(prm['w_ffn_d'], l, 0)] for l in range(DEPTH)],
    }


def _trunk(x3, mod, states, prm, w, geo, tn_in, tn_res, tn_ffn, tn_down, emit, rows_in=None):
    conv_a, ssd, mem_c, mem_n, mem_m, hgrn, ffn_buf = states
    n_conv, n_n, n_m, n_ffn = [], [], [], []
    new_ssd = new_c = new_hgrn = None
    nc = geo.seq_len // geo.chunk
    wq = {name: list(per_layer) for name, per_layer in w.items()}

    def record(name, idx, emitted):
        if emitted is not None:
            wq[name][idx] = [(arr, 0, 0) for arr in emitted]

    casts = lambda wops: emit and any(wop[0].dtype != MXU_DTYPE for wop in wops)
    for layer in range(DEPTH):
        geo_in = geo if rows_in is None else geo._replace(row_blk=rows_in[layer])
        mm_norm = functools.partial(_mm_norm, x3, prm['norm_mix'][layer], mod, layer, 1, 0,
                                    geo=geo_in)
        if layer % 2 == 0:
            e = layer // 2
            p1, gates, _ = mm_norm(w['in_ab'][e][0], tn=tn_in[layer], narrow=w['gates_ab'][e][0])
            gates_t = gates.reshape(geo.n_seq, nc, geo.chunk, GATE_W).transpose(0, 1, 3, 2)
            ya, cv, new_ssd = _ssd(p1, gates, gates_t, conv_a, ssd, prm, e, geo, new_ssd)
            hb, new_c, nn, mmm = _mlstm(p1, gates, gates_t, mem_c, mem_n, mem_m, prm, e, geo, new_c)
            n_conv.append(cv); n_n.append(nn); n_m.append(mmm)
            x3, em = _mm_res([ya, hb], w['out_ab'][e], x3, mod, layer, 2, geo, tn_res[layer],
                             emit=casts(w['out_ab'][e]))
            record('out_ab', e, em)
        else:
            o = layer // 2
            p, _, em = mm_norm(w['in_c'][o][0], tn=tn_in[layer], emit=casts(w['in_c'][o]))
            record('in_c', o, None if em is None else [em])
            oc, new_hgrn = _hgrn(p, hgrn, prm, o, geo, new_hgrn)
            x3, em = _mm_res([oc], w['out_c'][o], x3, mod, layer, 2, geo, tn_res[layer],
                             emit=casts(w['out_c'][o]))
            record('out_c', o, em)
        act, fb, em = _ffn_in(x3, prm['norm_ffn'][layer], mod, layer, w['ffn_g'][layer][0],
                              w['ffn_u'][layer][0], ffn_buf, prm['conv_w_f'][layer],
                              prm['conv_b_f'][layer], geo_in, tn_ffn,
                              emit=casts(w['ffn_g'][layer] + w['ffn_u'][layer]))
        if em is not None:
            record('ffn_g', layer, em[:1])
            record('ffn_u', layer, em[1:])
        n_ffn.append(fb)
        x3, em = _mm_res([act], w['ffn_d'][layer], x3, mod, layer, 5, geo, tn_down,
                         emit=casts(w['ffn_d'][layer]))
        record('ffn_d', layer, em)
    y = _final_norm(x3, prm['norm_f'], geo)
    cat = lambda xs: jnp.concatenate(xs, axis=0)
    return (y, cat(n_conv), new_ssd.reshape(ssd.shape), new_c, cat(n_n), cat(n_m), new_hgrn,
            cat(n_ffn)), wq


def kernel(x_prompt, x_sample, c_prompt, c_sample, state_ssd_conv, state_ssd, state_mlstm_c, state_mlstm_n, state_mlstm_m, state_hgrn, state_ffn_conv, w_ada, b_ada, norm_mix, norm_ffn, w_in_ab, conv_w_a, conv_b_a, dt_bias, a_log, d_skip, norm_a, i_bias, f_bias, norm_b, w_out_ab, w_in_c, lb_logits, norm_c, w_out_c, w_ffn_g, w_ffn_u, conv_w_f, conv_b_f, w_ffn_d, norm_f):
    prm = dict(norm_mix=norm_mix, norm_ffn=norm_ffn, w_in_ab=w_in_ab, conv_w_a=conv_w_a,
               conv_b_a=conv_b_a, dt_bias=dt_bias, a_log=a_log, d_skip=d_skip, norm_a=norm_a,
               i_bias=i_bias, f_bias=f_bias, norm_b=norm_b, w_out_ab=w_out_ab, w_in_c=w_in_c,
               lb_logits=lb_logits, norm_c=norm_c, w_out_c=w_out_c, w_ffn_g=w_ffn_g,
               w_ffn_u=w_ffn_u, conv_w_f=conv_w_f, conv_b_f=conv_b_f, w_ffn_d=w_ffn_d,
               norm_f=norm_f)
    bp, lp, _ = x_prompt.shape
    bs, ls, _ = x_sample.shape
    n_even, n_odd = state_ssd.shape[0], state_hgrn.shape[0]
    w = _prep_weights(prm)

    n_c = bs + bp
    n_c_pad = -(-n_c // SUBLANES) * SUBLANES
    c_all = jnp.pad(jnp.concatenate([c_sample, c_prompt], axis=0), ((0, n_c_pad - n_c), (0, 0)))
    mod = _ada(c_all, w_ada, b_ada).reshape(DEPTH, n_c_pad, 1, 6 * D_MODEL)

    geo_s = Geo(n_seq=bs, seq_len=SUBLANES, seq_blk=bs, row_blk=SUBLANES, chunk=SUBLANES,
                valid=ls, mod_off=0, scan_seqs=8, mlstm_seqs=4, hgrn_seqs=8)
    xs = jnp.pad(x_sample, ((0, 0), (0, SUBLANES - ls), (0, 0)))
    st_s = (state_ssd_conv, state_ssd, state_mlstm_c, state_mlstm_n, state_mlstm_m, state_hgrn,
            state_ffn_conv)
    out_s, w_bf16 = _trunk(xs, mod, st_s, prm, w, geo_s, tn_in=(512,) * DEPTH,
                           tn_res=(512,) * DEPTH, tn_ffn=256, tn_down=256, emit=True)

    zeros = lambda *s: jnp.zeros(s, F32)
    st_p = (zeros(n_even, bp, CONV_K_A - 1, CONV_DIM_A), zeros(n_even, bp, H_A, P_A, N_A),
            zeros(n_even, bp, H_B, DK_B, DV_B), zeros(n_even, bp, H_B, DK_B),
            zeros(n_even, bp, H_B), zeros(n_odd, bp, H_C, DK_C, DV_C),
            zeros(DEPTH, bp, CONV_K_F - 1, D_FF))
    lc_p = math.gcd(lp, PROMPT_CHUNK)
    geo_p = Geo(n_seq=bp, seq_len=lp, seq_blk=1, row_blk=min(lp, 1024), chunk=lc_p, valid=lc_p,
                mod_off=bs, scan_seqs=1, mlstm_seqs=1, hgrn_seqs=2 if bp % 2 == 0 else 1)
    out_p, _ = _trunk(x_prompt, mod, st_p, prm, w_bf16, geo_p, tn_in=(1024,) * DEPTH,
                      tn_res=(1024,) * DEPTH, tn_ffn=FFN_TN, tn_down=FFN_DOWN_TN, emit=False)
    return (out_p[0], out_s[0][:, :ls]) + tuple(out_p[1:]) + tuple(out_s[1:])
```

```python
import collections
import functools
import math

import jax
import jax.numpy as jnp
from jax import lax
from jax.experimental import pallas as pl
from jax.experimental.pallas import tpu as pltpu

F32 = jnp.float32
MXU_DTYPE = jnp.bfloat16
HI = lax.Precision.HIGHEST
NEG_BIG = -1e30
LOG2E = 1.4426950408889634

D_MODEL = 2048
DEPTH = 4
EPS = 1e-6
PROMPT_CHUNK = 128
H_A, P_A, G_A, N_A, CONV_K_A = 32, 64, 4, 128, 4
D_A = H_A * P_A
CONV_DIM_A = D_A + 2 * G_A * N_A
H_B, DK_B, DV_B = 8, 128, 256
D_B = H_B * DV_B
H_C, DK_C, DV_C = 16, 128, 128
D_C = H_C * DV_C
D_FF, CONV_K_F = 5632, 3
FFN_TN = 512
FFN_DOWN_TN = 512
PROLOGUE_SPLIT = 4
GATE_W = 128
SUBLANES = 8
VMEM_LIMIT = 56 * 1024 * 1024

_NT = (((1,), (1,)), ((), ()))
_TN = (((0,), (0,)), ((), ()))

Geo = collections.namedtuple(
    "Geo", "n_seq seq_len seq_blk row_blk chunk valid mod_off scan_seqs mlstm_seqs hgrn_seqs "
           "mlstm_chunk")


def _cparams(sem):
    return pltpu.CompilerParams(dimension_semantics=sem, vmem_limit_bytes=VMEM_LIMIT)


def _sigmoid(x):
    return 1.0 / (1.0 + jnp.exp(-x))


def _softplus(x):
    return jnp.maximum(x, 0.0) + jnp.log1p(jnp.exp(-jnp.abs(x)))


def _log_sigmoid(x):
    return jnp.minimum(x, 0.0) - jnp.log1p(jnp.exp(-jnp.abs(x)))


def _tri(n):
    r = lax.broadcasted_iota(jnp.int32, (n, n), 0)
    c = lax.broadcasted_iota(jnp.int32, (n, n), 1)
    mask = c <= r
    return mask.astype(F32), (r <= c).astype(F32), mask


def _split3(a):
    hi = a.astype(MXU_DTYPE)
    rest = a - hi.astype(F32)
    mid = rest.astype(MXU_DTYPE)
    lo = (rest - mid.astype(F32)).astype(MXU_DTYPE)
    return hi, mid, lo


def _select_dot(a, b, selector):
    if selector == 'a':
        s = a.astype(MXU_DTYPE)
        parts = [jnp.dot(s, t, preferred_element_type=F32) for t in _split3(b)]
    else:
        s = b.astype(MXU_DTYPE)
        parts = [jnp.dot(t, s, preferred_element_type=F32) for t in _split3(a)]
    return parts[0] + parts[1] + parts[2]


def _shifted(x, tails, k, axis=0):
    rolled = pltpu.roll(x, k, axis)
    head = lax.slice_in_dim(rolled, 0, SUBLANES, axis=axis)
    row = lax.broadcasted_iota(jnp.int32, head.shape, axis)
    for r in range(k):
        head = jnp.where(row == r, tails[len(tails) - k + r], head)
    if x.shape[axis] == SUBLANES:
        return head
    rest = lax.slice_in_dim(rolled, SUBLANES, x.shape[axis], axis=axis)
    return jnp.concatenate([head, rest], axis=axis)


def _drop_ref(fn, idx):
    def wrapped(*refs):
        return fn(*refs[:idx], *refs[idx + 1:])
    return wrapped


def _rmsnorm_rows(x):
    return x * lax.rsqrt(jnp.mean(x * x, axis=-1, keepdims=True) + EPS)


def _ada_kernel(c_ref, w_ref, b_ref, o_ref):
    c = c_ref[...]
    ca = (c * _sigmoid(c)).astype(MXU_DTYPE)
    o_ref[0] = jnp.dot(ca, w_ref[0].astype(MXU_DTYPE), preferred_element_type=F32) + b_ref[0]


def _ada(c_all, w_ada, b_ada, tn=1024):
    rows = c_all.shape[0]
    n = w_ada.shape[2]
    return pl.pallas_call(
        _ada_kernel,
        grid=(DEPTH, n // tn),
        in_specs=[
            pl.BlockSpec((rows, D_MODEL), lambda l, j: (0, 0)),
            pl.BlockSpec((1, D_MODEL, tn), lambda l, j: (l, 0, j)),
            pl.BlockSpec((1, 1, tn), lambda l, j: (l, 0, j)),
        ],
        out_specs=pl.BlockSpec((1, rows, tn), lambda l, j: (l, 0, j)),
        out_shape=jax.ShapeDtypeStruct((DEPTH, rows, n), F32),
        compiler_params=_cparams(("parallel", "parallel")),
        name="ada",
    )(c_all, w_ada, b_ada.reshape(DEPTH, 1, n))


def _tiles(geo):
    nrt = geo.seq_len // geo.row_blk
    return nrt, (geo.n_seq // geo.seq_blk) * nrt, geo.seq_blk * geo.row_blk


def _x_spec(geo, width, col_of):
    nrt = geo.seq_len // geo.row_blk
    return pl.BlockSpec((geo.seq_blk, geo.row_blk, width),
                        lambda i, j: (i // nrt, i % nrt, col_of(j)))


def _mod_spec(geo, layer, width, col_of):
    nrt = geo.seq_len // geo.row_blk
    return pl.BlockSpec((1, geo.seq_blk, 1, width),
                        lambda i, j: (layer, geo.mod_off + i // nrt, 0, col_of(j)))


def _norm_mod_chunks(x_ref, nw_ref, sc_ref, sh_ref):
    seq_blk, row_blk, d = x_ref.shape
    for r in range(PROLOGUE_SPLIT):
        if seq_blk == 1:
            n = row_blk // PROLOGUE_SPLIT
            x, sc, sh = x_ref[:, r * n:(r + 1) * n], sc_ref[0], sh_ref[0]
        else:
            n = seq_blk // PROLOGUE_SPLIT
            x, sc, sh = x_ref[r * n:(r + 1) * n], sc_ref[0, r * n:(r + 1) * n], sh_ref[0, r * n:(r + 1) * n]
        h = (_rmsnorm_rows(x) * nw_ref[...]) * (1.0 + sc) + sh
        rows = h.shape[0] * h.shape[1]
        yield slice(r * rows, (r + 1) * rows), h.reshape(rows, d).astype(MXU_DTYPE)


def _weight_spec(wop, k, tn):
    _, layer, row_block = wop
    return pl.BlockSpec((1, k, tn), lambda i, j: (layer, row_block, j))


def _emit_spec_shape(k, n, tn):
    return (pl.BlockSpec((1, k, tn), lambda i, j: (0, 0, j)),
            jax.ShapeDtypeStruct((1, k, n), MXU_DTYPE))


def _weight_tile(w_ref, wq_ref):
    w = w_ref[0].astype(MXU_DTYPE)
    if wq_ref is not None:
        wq_ref[0] = w
    return w


def _mm_norm_kernel(*refs, emit, narrow):
    x_ref, nw_ref, sc_ref, sh_ref, w_ref = refs[:5]
    rest = list(refs[5:])
    wn_ref = rest.pop(0) if narrow else None
    o_ref = rest.pop(0)
    on_ref = rest.pop(0) if narrow else None
    wq_ref = rest.pop(0) if emit else None
    (h_scr,) = rest
    w = _weight_tile(w_ref, wq_ref)

    @pl.when(pl.program_id(1) == 0)
    def _():
        for rows, h in _norm_mod_chunks(x_ref, nw_ref, sc_ref, sh_ref):
            h_scr[rows, :] = h
            o_ref[rows, :] = jnp.dot(h, w, preferred_element_type=F32)
            if narrow:
                on_ref[rows, :] = jnp.dot(h, wn_ref[0], preferred_element_type=F32)

    @pl.when(pl.program_id(1) > 0)
    def _():
        o_ref[...] = jnp.dot(h_scr[...], w, preferred_element_type=F32)


def _mm_norm(x3, nw, mod, layer, k_sc, k_sh, wop, geo, tn, emit=False, narrow=None):
    _, n_i, tm = _tiles(geo)
    n = wop[0].shape[2]
    in_specs = [
        _x_spec(geo, D_MODEL, lambda j: 0),
        pl.BlockSpec((1, D_MODEL), lambda i, j: (0, 0)),
        _mod_spec(geo, layer, D_MODEL, lambda j: k_sc),
        _mod_spec(geo, layer, D_MODEL, lambda j: k_sh),
        _weight_spec(wop, D_MODEL, tn),
    ]
    args = [x3, nw.reshape(1, D_MODEL), mod, mod, wop[0]]
    out_specs = [pl.BlockSpec((tm, tn), lambda i, j: (i, j))]
    out_shape = [jax.ShapeDtypeStruct((geo.n_seq * geo.seq_len, n), F32)]
    if narrow is not None:
        n_narrow = narrow[0].shape[2]
        in_specs.append(pl.BlockSpec((1, D_MODEL, n_narrow), lambda i, j: (narrow[1], 0, 0)))
        args.append(narrow[0])
        out_specs.append(pl.BlockSpec((tm, n_narrow), lambda i, j: (i, 0)))
        out_shape.append(jax.ShapeDtypeStruct((geo.n_seq * geo.seq_len, n_narrow), F32))
    if emit:
        assert n_i == 1
        spec, shape = _emit_spec_shape(D_MODEL, n, tn)
        out_specs.append(spec)
        out_shape.append(shape)
    outs = list(pl.pallas_call(
        functools.partial(_mm_norm_kernel, emit=emit, narrow=narrow is not None),
        grid=(n_i, n // tn),
        in_specs=in_specs,
        out_specs=out_specs,
        out_shape=out_shape,
        scratch_shapes=[pltpu.VMEM((tm, D_MODEL), MXU_DTYPE)],
        compiler_params=_cparams(("parallel", "arbitrary")),
        name="mm_norm",
    )(*args))
    out = outs.pop(0)
    out_narrow = outs.pop(0) if narrow is not None else None
    return out, out_narrow, (outs.pop(0) if emit else None)


def _mm_res_kernel(*refs, n_lhs, emit):
    a_refs, w_refs = refs[:n_lhs], refs[n_lhs:2 * n_lhs]
    xres_ref, gate_ref, o_ref = refs[2 * n_lhs:2 * n_lhs + 3]
    wq_refs = refs[2 * n_lhs + 3:] if emit else (None,) * n_lhs
    acc = None
    for a_ref, w_ref, wq_ref in zip(a_refs, w_refs, wq_refs):
        d = jnp.dot(a_ref[...], _weight_tile(w_ref, wq_ref), preferred_element_type=F32)
        acc = d if acc is None else acc + d
    o_ref[...] = xres_ref[...] + gate_ref[0] * acc.reshape(o_ref.shape)


def _mm_res(a_list, wops, x3, mod, layer, k_gate, geo, tn, emit=False):
    _, n_i, tm = _tiles(geo)
    per = D_MODEL // tn
    in_specs = [pl.BlockSpec((tm, a.shape[1]), lambda i, j: (i, 0)) for a in a_list]
    in_specs += [_weight_spec(wop, a.shape[1], tn) for wop, a in zip(wops, a_list)]
    in_specs += [_x_spec(geo, tn, lambda j: j),
                 _mod_spec(geo, layer, tn, lambda j: k_gate * per + j)]
    out_specs = [_x_spec(geo, tn, lambda j: j)]
    out_shape = [jax.ShapeDtypeStruct(x3.shape, F32)]
    if emit:
        assert n_i == 1
        for a in a_list:
            spec, shape = _emit_spec_shape(a.shape[1], D_MODEL, tn)
            out_specs.append(spec)
            out_shape.append(shape)
    outs = pl.pallas_call(
        functools.partial(_mm_res_kernel, n_lhs=len(a_list), emit=emit),
        grid=(n_i, per),
        in_specs=in_specs,
        out_specs=out_specs,
        out_shape=out_shape,
        compiler_params=_cparams(("parallel", "parallel")),
        name="mm_res",
    )(*a_list, *[wop[0] for wop in wops], x3, mod)
    return outs[0], (list(outs[1:]) if emit else None)


def _ffn_in_kernel(x_ref, nw_ref, sc_ref, sh_ref, wg_ref, wu_ref, cs_ref, cw_ref, cb_ref,
                   a_ref, cso_ref, *rest, nrt, valid, emit):
    wgq_ref, wuq_ref, h_scr, tail_scr = rest if emit else (None, None) + rest
    i, j = pl.program_id(0), pl.program_id(1)

    seq_blk, row_blk, tn = x_ref.shape[0], x_ref.shape[1], a_ref.shape[1]
    w_g, w_u = _weight_tile(wg_ref, wgq_ref), _weight_tile(wu_ref, wuq_ref)

    def conv_gate(g, u):
        g, u = g.reshape(seq_blk, row_blk, tn), u.reshape(seq_blk, row_blk, tn)
        prev = cs_ref[0]
        if nrt > 1:
            prev = jnp.where(i % nrt == 0, prev, tail_scr[j])
        t2, t1 = prev[:, 0:1, :], prev[:, 1:2, :]
        w = cw_ref[...]
        y = (cb_ref[...] + w[0:1] * _shifted(g, [t2, t1], 2, axis=1)
             + w[1:2] * _shifted(g, [t2, t1], 1, axis=1) + w[2:3] * g)
        a_ref[...] = (y * _sigmoid(y) * u).reshape(a_ref.shape).astype(a_ref.dtype)
        new_tail = g[:, valid - 2:valid, :]
        if nrt > 1:
            tail_scr[j] = new_tail
        cso_ref[0] = new_tail

    @pl.when(j == 0)
    def _():
        gs, us = [], []
        for rows, h in _norm_mod_chunks(x_ref, nw_ref, sc_ref, sh_ref):
            h_scr[rows, :] = h
            gs.append(jnp.dot(h, w_g, preferred_element_type=F32))
            us.append(jnp.dot(h, w_u, preferred_element_type=F32))
        conv_gate(jnp.concatenate(gs, axis=0), jnp.concatenate(us, axis=0))

    @pl.when(j > 0)
    def _():
        h = h_scr[...]
        conv_gate(jnp.dot(h, w_g, preferred_element_type=F32),
                  jnp.dot(h, w_u, preferred_element_type=F32))


def _ffn_in(x3, nw, mod, layer, wop_g, wop_u, conv_state, conv_w, conv_b, geo, tn, emit=False):
    nrt, n_i, tm = _tiles(geo)
    n_j = D_FF // tn
    valid = geo.row_blk if geo.valid == geo.chunk else geo.valid
    tail_shape = (n_j, geo.seq_blk, CONV_K_F - 1, tn) if nrt > 1 else (1, 1, CONV_K_F - 1, 128)
    cs_spec = pl.BlockSpec((1, geo.seq_blk, CONV_K_F - 1, tn), lambda i, j: (layer, i // nrt, 0, j))
    emit_specs, emit_shapes = [], []
    if emit:
        assert n_i == 1
        for _ in range(2):
            spec, shape = _emit_spec_shape(D_MODEL, D_FF, tn)
            emit_specs.append(spec)
            emit_shapes.append(shape)
    act, tails, *emitted = pl.pallas_call(
        functools.partial(_ffn_in_kernel, nrt=nrt, valid=valid, emit=emit),
        grid=(n_i, n_j),
        in_specs=[
            _x_spec(geo, D_MODEL, lambda j: 0),
            pl.BlockSpec((1, D_MODEL), lambda i, j: (0, 0)),
            _mod_spec(geo, layer, D_MODEL, lambda j: 4),
            _mod_spec(geo, layer, D_MODEL, lambda j: 3),
            _weight_spec(wop_g, D_MODEL, tn),
            _weight_spec(wop_u, D_MODEL, tn),
            cs_spec,
            pl.BlockSpec((CONV_K_F, tn), lambda i, j: (0, j)),
            pl.BlockSpec((1, tn), lambda i, j: (0, j)),
        ],
        out_specs=[
            pl.BlockSpec((tm, tn), lambda i, j: (i, j)),
            pl.BlockSpec((1, geo.seq_blk, CONV_K_F - 1, tn), lambda i, j: (i, 0, 0, j)),
        ] + emit_specs,
        out_shape=[
            jax.ShapeDtypeStruct((geo.n_seq * geo.seq_len, D_FF), MXU_DTYPE),
            jax.ShapeDtypeStruct((n_i, geo.seq_blk, CONV_K_F - 1, D_FF), F32),
        ] + emit_shapes,
        scratch_shapes=[pltpu.VMEM((tm, D_MODEL), MXU_DTYPE),
                        pltpu.VMEM(tail_shape, F32)],
        compiler_params=_cparams(("arbitrary", "arbitrary")),
        name="ffn_in",
    )(x3, nw.reshape(1, D_MODEL), mod, mod, wop_g[0], wop_u[0], conv_state, conv_w,
      conv_b.reshape(1, D_FF))
    last = tails.reshape(n_i // nrt, nrt, geo.seq_blk, CONV_K_F - 1, D_FF)[:, nrt - 1]
    return act, last.reshape(1, geo.n_seq, CONV_K_F - 1, D_FF), (emitted if emit else None)


def _norm_kernel(x_ref, nw_ref, o_ref):
    o_ref[...] = _rmsnorm_rows(x_ref[...]) * nw_ref[...]


def _final_norm(x3, nw, geo):
    _, n_i, _ = _tiles(geo)
    return pl.pallas_call(
        _norm_kernel,
        grid=(n_i, 1),
        in_specs=[_x_spec(geo, D_MODEL, lambda j: 0),
                  pl.BlockSpec((1, D_MODEL), lambda i, j: (0, 0))],
        out_specs=_x_spec(geo, D_MODEL, lambda j: 0),
        out_shape=jax.ShapeDtypeStruct(x3.shape, F32),
        compiler_params=_cparams(("parallel", "arbitrary")),
        name="final_norm",
    )(x3, nw.reshape(1, D_MODEL))


def _state_spec(shape, layer, geo):
    rest = tuple(shape[2:])
    zeros = (0,) * len(rest)
    return pl.BlockSpec((1, geo.scan_seqs) + rest, lambda i, c: (layer, i) + zeros)


def _row_spec(geo, width, col):
    return pl.BlockSpec((geo.scan_seqs, geo.chunk, width), lambda i, c: (i, c, col))


def _rows3(a, geo):
    return a.reshape(geo.n_seq, geo.seq_len, a.shape[-1])


def _gates_t_spec(geo):
    return pl.BlockSpec((geo.scan_seqs, 1, GATE_W, geo.chunk), lambda i, c: (i, c, 0, 0))


def _gates_t(gates, geo):
    nc = geo.seq_len // geo.chunk
    return gates.reshape(geo.n_seq, nc, geo.chunk, GATE_W).transpose(0, 1, 3, 2)


ROWS, LEAD, STATE, CONST = "rows", "lead", "state", "const"


def _per_sequence(body, geo, kinds, interleave):
    def view(ref, kind, k):
        if kind == ROWS:
            return ref.at[k]
        if kind == LEAD:
            return ref.at[pl.ds(k, 1)]
        if kind == STATE:
            return ref.at[:, pl.ds(k, 1)]
        return ref

    def wrapped(*refs):
        running = [body(*[view(r, kind, k) for r, kind in zip(refs, kinds)])
                   for k in range(geo.scan_seqs)]
        if not interleave:
            for gen in running:
                for _ in gen:
                    pass
            return
        while running:
            still = []
            for gen in running:
                if next(gen, StopIteration) is not StopIteration:
                    still.append(gen)
            running = still
    return wrapped


def _seq_call(kernel_fn, geo, name, in_specs, args, out_specs, out_shapes, scratch, kinds,
              stacked_prev, interleave=True):
    nc = geo.seq_len // geo.chunk
    kernel_fn = _per_sequence(kernel_fn, geo, kinds, interleave)
    in_specs, args = list(in_specs), list(args)
    aliases = {}
    assert len(stacked_prev) <= 1
    for out_idx, arr in stacked_prev.items():
        kernel_fn = _drop_ref(kernel_fn, len(in_specs))
        aliases[len(in_specs)] = out_idx
        in_specs.append(pl.BlockSpec(memory_space=pl.ANY))
        args.append(arr)
    return pl.pallas_call(
        kernel_fn,
        grid=(geo.n_seq // geo.scan_seqs, nc),
        in_specs=in_specs,
        out_specs=out_specs,
        out_shape=out_shapes,
        scratch_shapes=scratch,
        input_output_aliases=aliases,
        compiler_params=_cparams(("parallel", "arbitrary")),
        name=name,
    )(*args)


def _ssd_kernel(z_ref, xbc_ref, gc_ref, gr_ref, cs_ref, s0_ref, cw_ref, cb_ref,
                dtb_r_ref, dtb_c_ref, al_r_ref, al_c_ref, dsk_ref, nw_ref, exp_ref,
                y_ref, cso_ref, so_ref, tail_scr, *, lc, lv, single_chunk):
    if single_chunk:
        s_ref = s0_ref
        cs = cs_ref[0, 0]
        tails = [cs[0:1, :], cs[1:2, :], cs[2:3, :]]
    else:
        @pl.when(pl.program_id(1) == 0)
        def _():
            tail_scr[0, 5:8, :] = cs_ref[0, 0]
            so_ref[0, 0] = s0_ref[0, 0]

        s_ref = so_ref
        tails = [tail_scr[0, 5:6, :], tail_scr[0, 6:7, :], tail_scr[0, 7:8, :]]

    x = xbc_ref[...]
    w = cw_ref[0]
    xc = (cb_ref[0] + w[0:1] * _shifted(x, tails, 3) + w[1:2] * _shifted(x, tails, 2)
          + w[2:3] * _shifted(x, tails, 1) + w[3:4] * x)
    xc = xc * _sigmoid(xc)
    new_tail = x[lv - 3:lv, :]
    if not single_chunk:
        tail_scr[0, 5:8, :] = new_tail
    cso_ref[0, 0] = new_tail
    xa = xc[:, :D_A]
    bm = xc[:, D_A:D_A + G_A * N_A]
    cm = xc[:, D_A + G_A * N_A:]

    dt_c = _softplus(gc_ref[:, 0:H_A] + dtb_r_ref[...])
    dt_r = _softplus(gr_ref[0, 0:H_A, :] + dtb_c_ref[...])
    if lv < lc:
        dt_c = jnp.where(lax.broadcasted_iota(jnp.int32, dt_c.shape, 0) < lv, dt_c, 0.0)
        dt_r = jnp.where(lax.broadcasted_iota(jnp.int32, dt_r.shape, 1) < lv, dt_r, 0.0)
    lower, upper, mask = _tri(lc)
    cum_c = _select_dot(lower, dt_c * (-jnp.exp(al_r_ref[...])), 'a')
    cum_r = _select_dot(dt_r * (-jnp.exp(al_c_ref[...])), upper, 'b')
    cum_last = cum_c[lc - 1:lc, :]
    expand = exp_ref[...]
    ecum_x = _select_dot(jnp.exp(cum_c), expand, 'b')
    tail_x = _select_dot(jnp.exp(cum_last - cum_c) * dt_c, expand, 'b')

    hg = H_A // G_A
    gw = hg * P_A
    groups = range(G_A)
    gsl = lambda g: slice(g * gw, (g + 1) * gw)
    cg = [cm[:, g * N_A:(g + 1) * N_A] for g in groups]
    bg = [bm[:, g * N_A:(g + 1) * N_A] for g in groups]
    yield
    cb_ts = [lax.dot_general(cg[g], bg[g], _NT, preferred_element_type=F32) for g in groups]
    y_inter = [lax.dot_general(cg[g], s_ref[0, 0, g], _NT, preferred_element_type=F32)
               for g in groups]
    upd = [lax.dot_general(xa[:, gsl(g)] * tail_x[:, gsl(g)], bg[g], _TN,
                           preferred_element_type=F32) for g in groups]
    yield
    w_ts = [cb_ts[h // hg] * dt_r[h:h + 1, :]
            * jnp.exp(jnp.where(mask, cum_c[:, h:h + 1] - cum_r[h:h + 1, :], -jnp.inf))
            for h in range(H_A)]
    yield
    pieces = []
    for j in range(H_A // 2):
        xp = xa[:, j * 128:(j + 1) * 128]
        lane = lax.broadcasted_iota(jnp.int32, xp.shape, 1)
        pieces.append(
            jnp.dot(w_ts[2 * j], jnp.where(lane < P_A, xp, 0.0), preferred_element_type=F32)
            + jnp.dot(w_ts[2 * j + 1], jnp.where(lane >= P_A, xp, 0.0),
                      preferred_element_type=F32))
    yield
    y = (jnp.concatenate(pieces, axis=1) + jnp.concatenate(y_inter, axis=1) * ecum_x
         + dsk_ref[...] * xa)
    for g in groups:
        for hh in range(hg):
            h = g * hg + hh
            rows = slice(hh * P_A, (hh + 1) * P_A)
            so_ref[0, 0, g, rows, :] = (s_ref[0, 0, g, rows, :] * jnp.exp(cum_r[h:h + 1, lc - 1:lc])
                                        + upd[g][rows, :])

    z = z_ref[...]
    y = y * (z * _sigmoid(z))
    y = jnp.concatenate([_rmsnorm_rows(y[:, g * gw:(g + 1) * gw]) for g in range(G_A)], axis=1)
    y_ref[...] = (y * nw_ref[...]).astype(y_ref.dtype)


def _ssd(p1, gates, conv_state, ssd_state, prm, e, geo, prev_state_out):
    gates_t = _gates_t(gates, geo)
    b = geo.n_seq
    const2 = lambda shape: pl.BlockSpec(shape, lambda i, c: (0, 0))
    n_even = ssd_state.shape[0]
    s5 = ssd_state.reshape(n_even, b, G_A, (H_A // G_A) * P_A, N_A)
    expand = (jnp.arange(D_A)[None, :] // P_A == jnp.arange(H_A)[:, None]).astype(F32)
    cso_shape = (1,) + conv_state.shape[1:]
    y, cso, so = _seq_call(
        functools.partial(_ssd_kernel, lc=geo.chunk, lv=geo.valid,
                          single_chunk=geo.seq_len == geo.chunk), geo, "ssd",
        in_specs=[
            _row_spec(geo, D_A, 0),
            _row_spec(geo, CONV_DIM_A, 2),
            _row_spec(geo, GATE_W, 0),
            _gates_t_spec(geo),
            _state_spec(conv_state.shape, e, geo),
            _state_spec(s5.shape, e, geo),
            pl.BlockSpec((1, CONV_K_A, CONV_DIM_A), lambda i, c: (e, 0, 0)),
            pl.BlockSpec((1, 1, CONV_DIM_A), lambda i, c: (e, 0, 0)),
            const2((1, H_A)), const2((H_A, 1)), const2((1, H_A)), const2((H_A, 1)),
            const2((1, D_A)), const2((1, D_A)), const2((H_A, D_A)),
        ],
        args=(_rows3(p1, geo), _rows3(p1, geo), _rows3(gates, geo), gates_t, conv_state, s5,
              prm['conv_w_a'], prm['conv_b_a'].reshape(n_even, 1, CONV_DIM_A),
              prm['dt_bias'][e].reshape(1, H_A), prm['dt_bias'][e].reshape(H_A, 1),
              prm['a_log'][e].reshape(1, H_A), prm['a_log'][e].reshape(H_A, 1),
              jnp.repeat(prm['d_skip'][e], P_A).reshape(1, D_A),
              prm['norm_a'][e].reshape(1, D_A), expand),
        out_specs=[_row_spec(geo, D_A, 0), _state_spec(cso_shape, 0, geo),
                   _state_spec(s5.shape, e, geo)],
        out_shapes=[
            jax.ShapeDtypeStruct((b, geo.seq_len, D_A), MXU_DTYPE),
            jax.ShapeDtypeStruct(cso_shape, F32),
            jax.ShapeDtypeStruct(s5.shape, F32),
        ],
        scratch=[pltpu.VMEM((geo.scan_seqs, SUBLANES, CONV_DIM_A), F32)],
        kinds=[ROWS] * 4 + [STATE, STATE] + [CONST] * 9 + [ROWS, STATE, STATE, LEAD],
        stacked_prev={} if prev_state_out is None else {2: prev_state_out},
    )
    return y.reshape(b * geo.seq_len, D_A), cso, so


def _mlstm_kernel(q_ref, k_ref, v_ref, og_ref, gc_ref, gr_ref, c0_ref, n0_ref, m0_ref,
                  ib_r_ref, ib_c_ref, fb_r_ref, fb_c_ref, nw_ref,
                  h_ref, co_ref, no_ref, mo_ref, *, lc, lv, single_chunk):
    if single_chunk:
        c_ref, n_ref, m_ref = c0_ref, n0_ref, m0_ref
    else:
        @pl.when(pl.program_id(1) == 0)
        def _():
            co_ref[0, 0] = c0_ref[0, 0]
            no_ref[0, 0] = n0_ref[0, 0]
            mo_ref[0, 0] = m0_ref[0, 0]

        c_ref, n_ref, m_ref = co_ref, no_ref, mo_ref
    m_old, n_old = m_ref[0, 0], n_ref[0, 0]

    i0, f0 = H_A, H_A + H_B
    li_c = gc_ref[:, i0:i0 + H_B] + ib_r_ref[...]
    lf_c = _log_sigmoid(gc_ref[:, f0:f0 + H_B] + fb_r_ref[...])
    li_r = gr_ref[0, i0:i0 + H_B, :] + ib_c_ref[...]
    lf_r = _log_sigmoid(gr_ref[0, f0:f0 + H_B, :] + fb_c_ref[...])
    if lv < lc:
        vc = lax.broadcasted_iota(jnp.int32, li_c.shape, 0) < lv
        vr = lax.broadcasted_iota(jnp.int32, li_r.shape, 1) < lv
        li_c, lf_c = jnp.where(vc, li_c, NEG_BIG), jnp.where(vc, lf_c, 0.0)
        li_r, lf_r = jnp.where(vr, li_r, NEG_BIG), jnp.where(vr, lf_r, 0.0)
    lower, upper, mask = _tri(lc)
    bc_c = jnp.dot(lower, lf_c, precision=HI)
    bc_r = jnp.dot(lf_r, upper, precision=HI)

    heads = range(H_B)
    q = [q_ref[:, h * DK_B:(h + 1) * DK_B] * (DK_B ** -0.5) for h in heads]
    k = [k_ref[:, h * DK_B:(h + 1) * DK_B] for h in heads]
    v = [v_ref[:, h * DV_B:(h + 1) * DV_B] for h in heads]
    yield
    qk = [lax.dot_general(q[h], k[h], _NT, preferred_element_type=F32) for h in heads]
    yield
    m_t, w_in, w_ts = [], [], []
    for h in heads:
        bcc = bc_c[:, h:h + 1]
        dmat = jnp.where(mask, bcc - bc_r[h:h + 1, :] + li_r[h:h + 1, :], -jnp.inf)
        inter = bcc + m_old[:, h:h + 1]
        m_t.append(jnp.maximum(inter, jnp.max(dmat, axis=1, keepdims=True)))
        w_in.append(jnp.exp(inter - m_t[h]))
        w_ts.append(jnp.exp(dmat - m_t[h]) * qk[h])
    yield
    q_c = [jnp.dot(q[h], c_ref[0, 0, h], preferred_element_type=F32) for h in heads]
    wv = [jnp.dot(w_ts[h], v[h], preferred_element_type=F32) for h in heads]
    yield
    hs = []
    for h in heads:
        num = wv[h] + w_in[h] * q_c[h]
        den = (jnp.sum(w_ts[h], axis=1, keepdims=True)
               + w_in[h] * jnp.sum(q[h] * n_old[h:h + 1, :], axis=1, keepdims=True))
        hs.append(_rmsnorm_rows(num / jnp.maximum(jnp.abs(den), jnp.exp(-m_t[h]))))
    lane_h = lax.broadcasted_iota(jnp.int32, (1, H_B), 1)
    m_out = jnp.zeros((1, H_B), F32)
    ks, w_c = [], []
    for h in heads:
        bcc = bc_c[:, h:h + 1]
        m_new = m_t[h][lv - 1:lv, :]
        bc_last = bcc[lc - 1:lc, :]
        ks.append(k[h] * jnp.exp(bc_last - bcc + li_c[:, h:h + 1] - m_new))
        w_c.append(jnp.exp(bc_last + m_old[:, h:h + 1] - m_new))
        m_out = jnp.where(lane_h == h, m_new, m_out)
    yield
    kv = [lax.dot_general(ks[h], v[h], _TN, preferred_element_type=F32) for h in heads]
    yield
    for h in heads:
        co_ref[0, 0, h] = w_c[h] * c_ref[0, 0, h] + kv[h]
        no_ref[0, 0, h:h + 1, :] = w_c[h] * n_old[h:h + 1, :] + jnp.sum(ks[h], axis=0, keepdims=True)
    mo_ref[0, 0] = m_out
    hn = jnp.concatenate(hs, axis=1) * nw_ref[...]
    h_ref[...] = (hn * _sigmoid(og_ref[...])).astype(h_ref.dtype)


def _mlstm(p1, gates, c_state, n_state, m_state, prm, e, geo, prev_state_out):
    geo = geo._replace(scan_seqs=geo.mlstm_seqs, chunk=geo.mlstm_chunk,
                       valid=geo.mlstm_chunk if geo.valid == geo.chunk else geo.valid)
    gates_t = _gates_t(gates, geo)
    b = geo.n_seq
    const2 = lambda shape: pl.BlockSpec(shape, lambda i, c: (0, 0))
    qk_w = H_B * DK_B
    m4 = m_state.reshape(m_state.shape[0], b, 1, H_B)
    one = lambda shape: (1,) + tuple(shape[1:])
    h, co, no, mo = _seq_call(
        functools.partial(_mlstm_kernel, lc=geo.chunk, lv=geo.valid,
                          single_chunk=geo.seq_len == geo.chunk), geo, "mlstm",
        in_specs=[
            _row_spec(geo, qk_w, 9), _row_spec(geo, qk_w, 10),
            _row_spec(geo, D_B, 1), _row_spec(geo, D_B, 2),
            _row_spec(geo, GATE_W, 0),
            _gates_t_spec(geo),
            _state_spec(c_state.shape, e, geo), _state_spec(n_state.shape, e, geo),
            _state_spec(m4.shape, e, geo),
            const2((1, H_B)), const2((H_B, 1)), const2((1, H_B)), const2((H_B, 1)),
            const2((1, D_B)),
        ],
        args=(_rows3(p1, geo),) * 4 + (_rows3(gates, geo), gates_t, c_state, n_state, m4,
              prm['i_bias'][e].reshape(1, H_B), prm['i_bias'][e].reshape(H_B, 1),
              prm['f_bias'][e].reshape(1, H_B), prm['f_bias'][e].reshape(H_B, 1),
              prm['norm_b'][e].reshape(1, D_B)),
        out_specs=[_row_spec(geo, D_B, 0), _state_spec(c_state.shape, e, geo),
                   _state_spec(one(n_state.shape), 0, geo), _state_spec(one(m4.shape), 0, geo)],
        out_shapes=[
            jax.ShapeDtypeStruct((b, geo.seq_len, D_B), MXU_DTYPE),
            jax.ShapeDtypeStruct(c_state.shape, F32),
            jax.ShapeDtypeStruct(one(n_state.shape), F32),
            jax.ShapeDtypeStruct(one(m4.shape), F32),
        ],
        scratch=[],
        kinds=[ROWS] * 6 + [STATE] * 3 + [CONST] * 5 + [ROWS] + [STATE] * 3,
        stacked_prev={} if prev_state_out is None else {1: prev_state_out},
        interleave=geo.seq_len != geo.chunk,
    )
    return h.reshape(b * geo.seq_len, D_B), co, no, mo.reshape(1, b, H_B)


def _hgrn_kernel(q_ref, f_ref, i_ref, g_ref, lbl_ref, s0_ref, nw_ref, o_ref, so_ref,
                 *, lc, lv, layer_o, bs, single_chunk):
    if single_chunk:
        s_ref = s0_ref
    else:
        @pl.when(pl.program_id(1) == 0)
        def _():
            so_ref[0, 0] = s0_ref[0, 0]

        s_ref = so_ref

    lbl = lbl_ref[...]
    ex = jnp.exp(lbl - jnp.max(lbl, axis=0, keepdims=True))
    sm = ex / jnp.sum(ex, axis=0, keepdims=True)
    lb_all = [sm[0:1, :]]
    for r in range(1, lbl.shape[0]):
        lb_all.append(lb_all[-1] + sm[r:r + 1, :])
    lb = lb_all[layer_o] - lb_all[0]

    fx = f_ref[...]
    e1 = jnp.exp(-jnp.abs(fx))
    log_sig = jnp.minimum(fx, 0.0) - jnp.log1p(e1)
    la = jnp.log(lb)
    lb_ = jnp.log1p(-lb) + log_sig
    logf = jnp.maximum(la, lb_) + jnp.log1p(jnp.exp(-jnp.abs(la - lb_)))
    kk = (1.0 - lb) * (jnp.where(fx >= 0.0, e1, 1.0) / (1.0 + e1))
    if lv < lc:
        valid = lax.broadcasted_iota(jnp.int32, fx.shape, 0) < lv
        logf = jnp.where(valid, logf, 0.0)
        kk = jnp.where(valid, kk, 0.0)

    nb = lc // bs
    r_i = lax.broadcasted_iota(jnp.int32, (lc, lc), 0)
    c_i = lax.broadcasted_iota(jnp.int32, (lc, lc), 1)
    sh = int(math.log2(bs))
    blk_lower = ((c_i <= r_i) & ((c_i >> sh) == (r_i >> sh))).astype(F32)
    gw = jnp.dot(blk_lower, logf, precision=HI)
    q = q_ref[...]
    v = i_ref[...]
    blk = lambda a, i: a[i * bs:(i + 1) * bs, :]
    tots = [gw[(i + 1) * bs - 1:(i + 1) * bs, :] for i in range(nb)]
    before = [jnp.zeros_like(tots[0])]
    for i in range(nb):
        before.append(before[-1] + tots[i])
    g_tot = before[nb]
    qt = q * jnp.exp(gw)
    kt = [blk(kk, j) * jnp.exp(tots[j] - blk(gw, j)) for j in range(nb)]
    q_in = jnp.concatenate([blk(qt, i) * jnp.exp(before[i]) for i in range(nb)], axis=0)
    k_out = jnp.concatenate([kt[j] * jnp.exp(g_tot - before[j + 1]) for j in range(nb)], axis=0)

    heads = range(H_C)
    hsl = lambda h: slice(h * DK_C, (h + 1) * DK_C)
    yield
    o_inter = [jnp.dot(q_in[:, hsl(h)], s_ref[0, 0, h], preferred_element_type=F32)
               for h in heads]
    kv = [lax.dot_general(k_out[:, hsl(h)], v[:, hsl(h)], _TN, preferred_element_type=F32)
          for h in heads]
    att_off = [None]
    for i in range(1, nb):
        k_hat = jnp.concatenate(
            [kt[j] if j == i - 1 else kt[j] * jnp.exp(before[i] - before[j + 1])
             for j in range(i)] + [jnp.zeros((lc - i * bs, D_C), F32)], axis=0)
        q_ti = blk(qt, i)
        att_off.append([lax.dot_general(q_ti[:, hsl(h)], k_hat[:, hsl(h)], _NT,
                                        preferred_element_type=F32) for h in heads])
    yield
    gw2 = gw * LOG2E
    ck = jnp.log2(kk) - gw2
    n_t = bs // SUBLANES
    lane_s = lax.broadcasted_iota(jnp.int32, (SUBLANES, lc), 1)
    row_t = lax.broadcasted_iota(jnp.int32, (bs, lc), 0)
    col_s = lax.broadcasted_iota(jnp.int32, (bs, lc), 1)
    att = []
    for i in range(nb):
        g_i, q_i, ck_i = blk(gw2, i), blk(q, i), blk(ck, i)
        att_d = [[jnp.zeros((SUBLANES, lc), F32) for _ in range(n_t)] for _ in heads]
        for s in range(bs):
            t0 = s // SUBLANES
            p = q_i[t0 * SUBLANES:, :] * jnp.exp2(g_i[t0 * SUBLANES:, :] + ck_i[s:s + 1, :])
            for h in heads:
                a = jnp.sum(p[:, hsl(h)], axis=1, keepdims=True)
                for tt in range(t0, n_t):
                    a_t = a[(tt - t0) * SUBLANES:(tt - t0 + 1) * SUBLANES, :]
                    att_d[h][tt] = jnp.where(lane_s == i * bs + s, a_t, att_d[h][tt])
        causal = (col_s - i * bs) <= row_t
        att_i = []
        for h in heads:
            a = att_d[h][0] if n_t == 1 else jnp.concatenate(att_d[h], axis=0)
            a = jnp.where(causal, a, 0.0)
            att_i.append(a if i == 0 else a + att_off[i][h])
        att.append(att_i)
    yield
    o_intra = [[jnp.dot(att[i][h], v[:, hsl(h)], preferred_element_type=F32) for h in heads]
               for i in range(nb)]
    yield
    outs = []
    for h in heads:
        o_h = o_intra[0][h] if nb == 1 else jnp.concatenate([o_intra[i][h] for i in range(nb)],
                                                             axis=0)
        outs.append(_rmsnorm_rows(o_h + o_inter[h]))
        dec_col = jnp.transpose(jnp.broadcast_to(jnp.exp(g_tot[:, hsl(h)]), (DK_C, DK_C)))
        so_ref[0, 0, h] = dec_col * s_ref[0, 0, h] + kv[h]
    gate = g_ref[...]
    o_ref[...] = (jnp.concatenate(outs, axis=1) * nw_ref[...]
                  * (gate * _sigmoid(gate))).astype(o_ref.dtype)


def _hgrn(p, state, prm, o, geo, prev_state_out):
    geo = geo._replace(scan_seqs=geo.hgrn_seqs)
    n_odd = prm['lb_logits'].shape[0]
    out, new_state = _seq_call(
        functools.partial(_hgrn_kernel, lc=geo.chunk, lv=geo.valid, layer_o=o,
                          bs=min(16, geo.chunk), single_chunk=geo.seq_len == geo.chunk), geo, "hgrn",
        in_specs=[
            _row_spec(geo, D_C, 0), _row_spec(geo, D_C, 1), _row_spec(geo, D_C, 2),
            _row_spec(geo, D_C, 3),
            pl.BlockSpec((n_odd, D_C), lambda i, c: (0, 0)),
            _state_spec(state.shape, o, geo),
            pl.BlockSpec((1, D_C), lambda i, c: (0, 0)),
        ],
        args=(_rows3(p, geo),) * 4 + (prm['lb_logits'], state, prm['norm_c'][o].reshape(1, D_C)),
        out_specs=[_row_spec(geo, D_C, 0), _state_spec(state.shape, o, geo)],
        out_shapes=[
            jax.ShapeDtypeStruct((geo.n_seq, geo.seq_len, D_C), MXU_DTYPE),
            jax.ShapeDtypeStruct(state.shape, F32),
        ],
        scratch=[],
        kinds=[ROWS] * 4 + [CONST, STATE, CONST, ROWS, STATE],
        stacked_prev={} if prev_state_out is None else {1: prev_state_out},
    )
    return out.reshape(geo.n_seq * geo.seq_len, D_C), new_state


def _in_ab_column_groups():
    a0, a1, a2 = D_A, D_A + CONV_DIM_A, D_A + CONV_DIM_A + H_A
    q1 = a2 + H_B * DK_B
    k1 = q1 + H_B * DK_B
    v1 = k1 + D_B
    o1 = v1 + D_B
    end = o1 + 2 * H_B
    return [(0, a0), (k1, v1), (v1, o1), (a0, a1), (a2, q1), (q1, k1)], [(a1, a2), (o1, end)]


def _regroup_in_ab(wab):
    wide, narrow = _in_ab_column_groups()
    n_gate = sum(b - a for a, b in narrow)
    in_ab = jnp.concatenate([wab[:, :, a:b] for a, b in wide], axis=2).astype(MXU_DTYPE)
    gates = jnp.pad(jnp.concatenate([wab[:, :, a:b] for a, b in narrow], axis=2),
                    ((0, 0), (0, 0), (0, GATE_W - n_gate))).astype(MXU_DTYPE)
    return in_ab, gates


def _prep_weights(prm):
    wab = prm['w_in_ab']
    in_ab, gates_ab = _regroup_in_ab(wab)
    n_even, n_odd = wab.shape[0], prm['w_in_c'].shape[0]
    return {
        'in_ab': [[(in_ab, e, 0)] for e in range(n_even)],
        'gates_ab': [[(gates_ab, e, 0)] for e in range(n_even)],
        'out_ab': [[(prm['w_out_ab'], e, 0), (prm['w_out_ab'], e, 1)] for e in range(n_even)],
        'in_c': [[(prm['w_in_c'], o, 0)] for o in range(n_odd)],
        'out_c': [[(prm['w_out_c'], o, 0)] for o in range(n_odd)],
        'ffn_g': [[(prm['w_ffn_g'], l, 0)] for l in range(DEPTH)],
        'ffn_u': [[(prm['w_ffn_u'], l, 0)] for l in range(DEPTH)],
        'ffn_d': [[(prm['w_ffn_d'], l, 0)] for l in range(DEPTH)],
    }


def _trunk(x3, mod, states, prm, w, geo, tn_in, tn_res, tn_ffn, tn_down, emit, rows_in=None):
    conv_a, ssd, mem_c, mem_n, mem_m, hgrn, ffn_buf = states
    n_conv, n_n, n_m, n_ffn = [], [], [], []
    new_ssd = new_c = new_hgrn = None
    nc = geo.seq_len // geo.chunk
    wq = {name: list(per_layer) for name, per_layer in w.items()}

    def record(name, idx, emitted):
        if emitted is not None:
            wq[name][idx] = [(arr, 0, 0) for arr in emitted]

    casts = lambda wops: emit and any(wop[0].dtype != MXU_DTYPE for wop in wops)
    for layer in range(DEPTH):
        geo_in = geo if rows_in is None else geo._replace(row_blk=rows_in[layer])
        mm_norm = functools.partial(_mm_norm, x3, prm['norm_mix'][layer], mod, layer, 1, 0,
                                    geo=geo_in)
        if layer % 2 == 0:
            e = layer // 2
            p1, gates, _ = mm_norm(w['in_ab'][e][0], tn=tn_in[layer], narrow=w['gates_ab'][e][0])
            ya, cv, new_ssd = _ssd(p1, gates, conv_a, ssd, prm, e, geo, new_ssd)
            hb, new_c, nn, mmm = _mlstm(p1, gates, mem_c, mem_n, mem_m, prm, e, geo, new_c)
            n_conv.append(cv); n_n.append(nn); n_m.append(mmm)
            x3, em = _mm_res([ya, hb], w['out_ab'][e], x3, mod, layer, 2, geo, tn_res[layer],
                             emit=casts(w['out_ab'][e]))
            record('out_ab', e, em)
        else:
            o = layer // 2
            p, _, em = mm_norm(w['in_c'][o][0], tn=tn_in[layer], emit=casts(w['in_c'][o]))
            record('in_c', o, None if em is None else [em])
            oc, new_hgrn = _hgrn(p, hgrn, prm, o, geo, new_hgrn)
            x3, em = _mm_res([oc], w['out_c'][o], x3, mod, layer, 2, geo, tn_res[layer],
                             emit=casts(w['out_c'][o]))
            record('out_c', o, em)
        act, fb, em = _ffn_in(x3, prm['norm_ffn'][layer], mod, layer, w['ffn_g'][layer][0],
                              w['ffn_u'][layer][0], ffn_buf, prm['conv_w_f'][layer],
                              prm['conv_b_f'][layer], geo_in, tn_ffn,
                              emit=casts(w['ffn_g'][layer] + w['ffn_u'][layer]))
        if em is not None:
            record('ffn_g', layer, em[:1])
            record('ffn_u', layer, em[1:])
        n_ffn.append(fb)
        x3, em = _mm_res([act], w['ffn_d'][layer], x3, mod, layer, 5, geo, tn_down,
                         emit=casts(w['ffn_d'][layer]))
        record('ffn_d', layer, em)
    y = _final_norm(x3, prm['norm_f'], geo)
    cat = lambda xs: jnp.concatenate(xs, axis=0)
    return (y, cat(n_conv), new_ssd.reshape(ssd.shape), new_c, cat(n_n), cat(n_m), new_hgrn,
            cat(n_ffn)), wq


def kernel(x_prompt, x_sample, c_prompt, c_sample, state_ssd_conv, state_ssd, state_mlstm_c, state_mlstm_n, state_mlstm_m, state_hgrn, state_ffn_conv, w_ada, b_ada, norm_mix, norm_ffn, w_in_ab, conv_w_a, conv_b_a, dt_bias, a_log, d_skip, norm_a, i_bias, f_bias, norm_b, w_out_ab, w_in_c, lb_logits, norm_c, w_out_c, w_ffn_g, w_ffn_u, conv_w_f, conv_b_f, w_ffn_d, norm_f):
    prm = dict(norm_mix=norm_mix, norm_ffn=norm_ffn, w_in_ab=w_in_ab, conv_w_a=conv_w_a,
               conv_b_a=conv_b_a, dt_bias=dt_bias, a_log=a_log, d_skip=d_skip, norm_a=norm_a,
               i_bias=i_bias, f_bias=f_bias, norm_b=norm_b, w_out_ab=w_out_ab, w_in_c=w_in_c,
               lb_logits=lb_logits, norm_c=norm_c, w_out_c=w_out_c, w_ffn_g=w_ffn_g,
               w_ffn_u=w_ffn_u, conv_w_f=conv_w_f, conv_b_f=conv_b_f, w_ffn_d=w_ffn_d,
               norm_f=norm_f)
    bp, lp, _ = x_prompt.shape
    bs, ls, _ = x_sample.shape
    n_even, n_odd = state_ssd.shape[0], state_hgrn.shape[0]
    w = _prep_weights(prm)

    n_c = bs + bp
    n_c_pad = -(-n_c // SUBLANES) * SUBLANES
    c_all = jnp.pad(jnp.concatenate([c_sample, c_prompt], axis=0), ((0, n_c_pad - n_c), (0, 0)))
    mod = _ada(c_all, w_ada, b_ada).reshape(DEPTH, n_c_pad, 1, 6 * D_MODEL)

    geo_s = Geo(n_seq=bs, seq_len=SUBLANES, seq_blk=bs, row_blk=SUBLANES, chunk=SUBLANES,
                valid=ls, mod_off=0, scan_seqs=8, mlstm_seqs=4, hgrn_seqs=8, mlstm_chunk=SUBLANES)
    xs = jnp.pad(x_sample, ((0, 0), (0, SUBLANES - ls), (0, 0)))
    st_s = (state_ssd_conv, state_ssd, state_mlstm_c, state_mlstm_n, state_mlstm_m, state_hgrn,
            state_ffn_conv)
    out_s, w_bf16 = _trunk(xs, mod, st_s, prm, w, geo_s, tn_in=(512,) * DEPTH,
                           tn_res=(512,) * DEPTH, tn_ffn=256, tn_down=256, emit=True)

    zeros = lambda *s: jnp.zeros(s, F32)
    st_p = (zeros(n_even, bp, CONV_K_A - 1, CONV_DIM_A), zeros(n_even, bp, H_A, P_A, N_A),
            zeros(n_even, bp, H_B, DK_B, DV_B), zeros(n_even, bp, H_B, DK_B),
            zeros(n_even, bp, H_B), zeros(n_odd, bp, H_C, DK_C, DV_C),
            zeros(DEPTH, bp, CONV_K_F - 1, D_FF))
    lc_p = math.gcd(lp, PROMPT_CHUNK)
    geo_p = Geo(n_seq=bp, seq_len=lp, seq_blk=1, row_blk=min(lp, 1024), chunk=lc_p, valid=lc_p,
                mod_off=bs, scan_seqs=1, mlstm_seqs=1, hgrn_seqs=2 if bp % 2 == 0 else 1,
                mlstm_chunk=math.gcd(lp, 2 * PROMPT_CHUNK))
    out_p, _ = _trunk(x_prompt, mod, st_p, prm, w_bf16, geo_p, tn_in=(1024,) * DEPTH,
                      tn_res=(1024,) * DEPTH, tn_ffn=FFN_TN, tn_down=FFN_DOWN_TN, emit=False)
    return (out_p[0], out_s[0][:, :ls]) + tuple(out_p[1:]) + tuple(out_s[1:])
```

```python
import collections
import functools
import math

import jax
import jax.numpy as jnp
from jax import lax
from jax.experimental import pallas as pl
from jax.experimental.pallas import tpu as pltpu

F32 = jnp.float32
MXU_DTYPE = jnp.bfloat16
HI = lax.Precision.HIGHEST
NEG_BIG = -1e30
LOG2E = 1.4426950408889634

D_MODEL = 2048
DEPTH = 4
EPS = 1e-6
PROMPT_CHUNK = 128
H_A, P_A, G_A, N_A, CONV_K_A = 32, 64, 4, 128, 4
D_A = H_A * P_A
CONV_DIM_A = D_A + 2 * G_A * N_A
H_B, DK_B, DV_B = 8, 128, 256
D_B = H_B * DV_B
H_C, DK_C, DV_C = 16, 128, 128
D_C = H_C * DV_C
D_FF, CONV_K_F = 5632, 3
FFN_TN = 512
FFN_DOWN_TN = 512
PROLOGUE_SPLIT = 4
GATE_W = 128
SUBLANES = 8
VMEM_LIMIT = 56 * 1024 * 1024

_NT = (((1,), (1,)), ((), ()))
_TN = (((0,), (0,)), ((), ()))

Geo = collections.namedtuple(
    "Geo", "n_seq seq_len seq_blk row_blk chunk valid mod_off scan_seqs mlstm_seqs hgrn_seqs "
           "mlstm_chunk")


def _cparams(sem):
    return pltpu.CompilerParams(dimension_semantics=sem, vmem_limit_bytes=VMEM_LIMIT)


def _sigmoid(x):
    return 1.0 / (1.0 + jnp.exp(-x))


def _softplus(x):
    return jnp.maximum(x, 0.0) + jnp.log1p(jnp.exp(-jnp.abs(x)))


def _log_sigmoid(x):
    return jnp.minimum(x, 0.0) - jnp.log1p(jnp.exp(-jnp.abs(x)))


def _tri(n):
    r = lax.broadcasted_iota(jnp.int32, (n, n), 0)
    c = lax.broadcasted_iota(jnp.int32, (n, n), 1)
    mask = c <= r
    return mask.astype(F32), (r <= c).astype(F32), mask


def _split3(a):
    hi = a.astype(MXU_DTYPE)
    rest = a - hi.astype(F32)
    mid = rest.astype(MXU_DTYPE)
    lo = (rest - mid.astype(F32)).astype(MXU_DTYPE)
    return hi, mid, lo


def _select_dot(a, b, selector):
    if selector == 'a':
        s = a.astype(MXU_DTYPE)
        parts = [jnp.dot(s, t, preferred_element_type=F32) for t in _split3(b)]
    else:
        s = b.astype(MXU_DTYPE)
        parts = [jnp.dot(t, s, preferred_element_type=F32) for t in _split3(a)]
    return parts[0] + parts[1] + parts[2]


def _shifted(x, tails, k, axis=0):
    rolled = pltpu.roll(x, k, axis)
    head = lax.slice_in_dim(rolled, 0, SUBLANES, axis=axis)
    row = lax.broadcasted_iota(jnp.int32, head.shape, axis)
    for r in range(k):
        head = jnp.where(row == r, tails[len(tails) - k + r], head)
    if x.shape[axis] == SUBLANES:
        return head
    rest = lax.slice_in_dim(rolled, SUBLANES, x.shape[axis], axis=axis)
    return jnp.concatenate([head, rest], axis=axis)


def _drop_ref(fn, idx):
    def wrapped(*refs):
        return fn(*refs[:idx], *refs[idx + 1:])
    return wrapped


def _rmsnorm_rows(x):
    return x * lax.rsqrt(jnp.mean(x * x, axis=-1, keepdims=True) + EPS)


def _ada_kernel(c_ref, w_ref, b_ref, o_ref):
    c = c_ref[...]
    ca = (c * _sigmoid(c)).astype(MXU_DTYPE)
    o_ref[0] = jnp.dot(ca, w_ref[0].astype(MXU_DTYPE), preferred_element_type=F32) + b_ref[0]


def _ada(c_all, w_ada, b_ada, tn=1024):
    rows = c_all.shape[0]
    n = w_ada.shape[2]
    return pl.pallas_call(
        _ada_kernel,
        grid=(DEPTH, n // tn),
        in_specs=[
            pl.BlockSpec((rows, D_MODEL), lambda l, j: (0, 0)),
            pl.BlockSpec((1, D_MODEL, tn), lambda l, j: (l, 0, j)),
            pl.BlockSpec((1, 1, tn), lambda l, j: (l, 0, j)),
        ],
        out_specs=pl.BlockSpec((1, rows, tn), lambda l, j: (l, 0, j)),
        out_shape=jax.ShapeDtypeStruct((DEPTH, rows, n), F32),
        compiler_params=_cparams(("parallel", "parallel")),
        name="ada",
    )(c_all, w_ada, b_ada.reshape(DEPTH, 1, n))


def _tiles(geo):
    nrt = geo.seq_len // geo.row_blk
    return nrt, (geo.n_seq // geo.seq_blk) * nrt, geo.seq_blk * geo.row_blk


def _x_spec(geo, width, col_of):
    nrt = geo.seq_len // geo.row_blk
    return pl.BlockSpec((geo.seq_blk, geo.row_blk, width),
                        lambda i, j: (i // nrt, i % nrt, col_of(j)))


def _mod_spec(geo, layer, width, col_of):
    nrt = geo.seq_len // geo.row_blk
    return pl.BlockSpec((1, geo.seq_blk, 1, width),
                        lambda i, j: (layer, geo.mod_off + i // nrt, 0, col_of(j)))


def _norm_mod_chunks(x_ref, nw_ref, sc_ref, sh_ref):
    seq_blk, row_blk, d = x_ref.shape
    for r in range(PROLOGUE_SPLIT):
        if seq_blk == 1:
            n = row_blk // PROLOGUE_SPLIT
            x, sc, sh = x_ref[:, r * n:(r + 1) * n], sc_ref[0], sh_ref[0]
        else:
            n = seq_blk // PROLOGUE_SPLIT
            x, sc, sh = x_ref[r * n:(r + 1) * n], sc_ref[0, r * n:(r + 1) * n], sh_ref[0, r * n:(r + 1) * n]
        h = (_rmsnorm_rows(x) * nw_ref[...]) * (1.0 + sc) + sh
        rows = h.shape[0] * h.shape[1]
        yield slice(r * rows, (r + 1) * rows), h.reshape(rows, d).astype(MXU_DTYPE)


def _weight_spec(wop, k, tn):
    _, layer, row_block = wop
    return pl.BlockSpec((1, k, tn), lambda i, j: (layer, row_block, j))


def _emit_spec_shape(k, n, tn):
    return (pl.BlockSpec((1, k, tn), lambda i, j: (0, 0, j)),
            jax.ShapeDtypeStruct((1, k, n), MXU_DTYPE))


def _weight_tile(w_ref, wq_ref):
    w = w_ref[0].astype(MXU_DTYPE)
    if wq_ref is not None:
        wq_ref[0] = w
    return w


def _mm_norm_kernel(*refs, emit, narrow):
    x_ref, nw_ref, sc_ref, sh_ref, w_ref = refs[:5]
    rest = list(refs[5:])
    wn_ref = rest.pop(0) if narrow else None
    o_ref = rest.pop(0)
    on_ref = rest.pop(0) if narrow else None
    wq_ref = rest.pop(0) if emit else None
    (h_scr,) = rest
    w = _weight_tile(w_ref, wq_ref)

    @pl.when(pl.program_id(1) == 0)
    def _():
        for rows, h in _norm_mod_chunks(x_ref, nw_ref, sc_ref, sh_ref):
            h_scr[rows, :] = h
            o_ref[rows, :] = jnp.dot(h, w, preferred_element_type=F32)
            if narrow:
                on_ref[rows, :] = jnp.dot(h, wn_ref[0], preferred_element_type=F32)

    @pl.when(pl.program_id(1) > 0)
    def _():
        o_ref[...] = jnp.dot(h_scr[...], w, preferred_element_type=F32)


def _mm_norm(x3, nw, mod, layer, k_sc, k_sh, wop, geo, tn, emit=False, narrow=None):
    _, n_i, tm = _tiles(geo)
    n = wop[0].shape[2]
    in_specs = [
        _x_spec(geo, D_MODEL, lambda j: 0),
        pl.BlockSpec((1, D_MODEL), lambda i, j: (0, 0)),
        _mod_spec(geo, layer, D_MODEL, lambda j: k_sc),
        _mod_spec(geo, layer, D_MODEL, lambda j: k_sh),
        _weight_spec(wop, D_MODEL, tn),
    ]
    args = [x3, nw.reshape(1, D_MODEL), mod, mod, wop[0]]
    out_specs = [pl.BlockSpec((tm, tn), lambda i, j: (i, j))]
    out_shape = [jax.ShapeDtypeStruct((geo.n_seq * geo.seq_len, n), F32)]
    if narrow is not None:
        n_narrow = narrow[0].shape[2]
        in_specs.append(pl.BlockSpec((1, D_MODEL, n_narrow), lambda i, j: (narrow[1], 0, 0)))
        args.append(narrow[0])
        out_specs.append(pl.BlockSpec((tm, n_narrow), lambda i, j: (i, 0)))
        out_shape.append(jax.ShapeDtypeStruct((geo.n_seq * geo.seq_len, n_narrow), F32))
    if emit:
        assert n_i == 1
        spec, shape = _emit_spec_shape(D_MODEL, n, tn)
        out_specs.append(spec)
        out_shape.append(shape)
    outs = list(pl.pallas_call(
        functools.partial(_mm_norm_kernel, emit=emit, narrow=narrow is not None),
        grid=(n_i, n // tn),
        in_specs=in_specs,
        out_specs=out_specs,
        out_shape=out_shape,
        scratch_shapes=[pltpu.VMEM((tm, D_MODEL), MXU_DTYPE)],
        compiler_params=_cparams(("parallel", "arbitrary")),
        name="mm_norm",
    )(*args))
    out = outs.pop(0)
    out_narrow = outs.pop(0) if narrow is not None else None
    return out, out_narrow, (outs.pop(0) if emit else None)


def _mm_res_kernel(*refs, n_lhs, emit):
    a_refs, w_refs = refs[:n_lhs], refs[n_lhs:2 * n_lhs]
    xres_ref, gate_ref, o_ref = refs[2 * n_lhs:2 * n_lhs + 3]
    wq_refs = refs[2 * n_lhs + 3:] if emit else (None,) * n_lhs
    acc = None
    for a_ref, w_ref, wq_ref in zip(a_refs, w_refs, wq_refs):
        d = jnp.dot(a_ref[...], _weight_tile(w_ref, wq_ref), preferred_element_type=F32)
        acc = d if acc is None else acc + d
    o_ref[...] = xres_ref[...] + gate_ref[0] * acc.reshape(o_ref.shape)


def _mm_res(a_list, wops, x3, mod, layer, k_gate, geo, tn, emit=False):
    _, n_i, tm = _tiles(geo)
    per = D_MODEL // tn
    in_specs = [pl.BlockSpec((tm, a.shape[1]), lambda i, j: (i, 0)) for a in a_list]
    in_specs += [_weight_spec(wop, a.shape[1], tn) for wop, a in zip(wops, a_list)]
    in_specs += [_x_spec(geo, tn, lambda j: j),
                 _mod_spec(geo, layer, tn, lambda j: k_gate * per + j)]
    out_specs = [_x_spec(geo, tn, lambda j: j)]
    out_shape = [jax.ShapeDtypeStruct(x3.shape, F32)]
    if emit:
        assert n_i == 1
        for a in a_list:
            spec, shape = _emit_spec_shape(a.shape[1], D_MODEL, tn)
            out_specs.append(spec)
            out_shape.append(shape)
    outs = pl.pallas_call(
        functools.partial(_mm_res_kernel, n_lhs=len(a_list), emit=emit),
        grid=(n_i, per),
        in_specs=in_specs,
        out_specs=out_specs,
        out_shape=out_shape,
        compiler_params=_cparams(("parallel", "parallel")),
        name="mm_res",
    )(*a_list, *[wop[0] for wop in wops], x3, mod)
    return outs[0], (list(outs[1:]) if emit else None)


def _ffn_in_kernel(x_ref, nw_ref, sc_ref, sh_ref, wg_ref, wu_ref, cs_ref, cw_ref, cb_ref,
                   a_ref, cso_ref, *rest, nrt, valid, emit):
    wgq_ref, wuq_ref, h_scr, tail_scr = rest if emit else (None, None) + rest
    i, j = pl.program_id(0), pl.program_id(1)

    seq_blk, row_blk, tn = x_ref.shape[0], x_ref.shape[1], a_ref.shape[1]
    w_g, w_u = _weight_tile(wg_ref, wgq_ref), _weight_tile(wu_ref, wuq_ref)

    def conv_gate(g, u):
        g, u = g.reshape(seq_blk, row_blk, tn), u.reshape(seq_blk, row_blk, tn)
        prev = cs_ref[0]
        if nrt > 1:
            prev = jnp.where(i % nrt == 0, prev, tail_scr[j])
        t2, t1 = prev[:, 0:1, :], prev[:, 1:2, :]
        w = cw_ref[...]
        y = (cb_ref[...] + w[0:1] * _shifted(g, [t2, t1], 2, axis=1)
             + w[1:2] * _shifted(g, [t2, t1], 1, axis=1) + w[2:3] * g)
        a_ref[...] = (y * _sigmoid(y) * u).reshape(a_ref.shape).astype(a_ref.dtype)
        new_tail = g[:, valid - 2:valid, :]
        if nrt > 1:
            tail_scr[j] = new_tail
        cso_ref[0] = new_tail

    @pl.when(j == 0)
    def _():
        gs, us = [], []
        for rows, h in _norm_mod_chunks(x_ref, nw_ref, sc_ref, sh_ref):
            h_scr[rows, :] = h
            gs.append(jnp.dot(h, w_g, preferred_element_type=F32))
            us.append(jnp.dot(h, w_u, preferred_element_type=F32))
        conv_gate(jnp.concatenate(gs, axis=0), jnp.concatenate(us, axis=0))

    @pl.when(j > 0)
    def _():
        h = h_scr[...]
        conv_gate(jnp.dot(h, w_g, preferred_element_type=F32),
                  jnp.dot(h, w_u, preferred_element_type=F32))


def _ffn_in(x3, nw, mod, layer, wop_g, wop_u, conv_state, conv_w, conv_b, geo, tn, emit=False,
            stacked_prev=None):
    nrt, n_i, tm = _tiles(geo)
    n_j = D_FF // tn
    valid = geo.row_blk if geo.valid == geo.chunk else geo.valid
    tail_shape = (n_j, geo.seq_blk, CONV_K_F - 1, tn) if nrt > 1 else (1, 1, CONV_K_F - 1, 128)
    cs_spec = pl.BlockSpec((1, geo.seq_blk, CONV_K_F - 1, tn), lambda i, j: (layer, i // nrt, 0, j))
    stacked = n_i == 1
    emit_specs, emit_shapes = [], []
    if emit:
        assert n_i == 1
        for _ in range(2):
            spec, shape = _emit_spec_shape(D_MODEL, D_FF, tn)
            emit_specs.append(spec)
            emit_shapes.append(shape)
    kernel_fn = functools.partial(_ffn_in_kernel, nrt=nrt, valid=valid, emit=emit)
    in_specs = [
        _x_spec(geo, D_MODEL, lambda j: 0),
        pl.BlockSpec((1, D_MODEL), lambda i, j: (0, 0)),
        _mod_spec(geo, layer, D_MODEL, lambda j: 4),
        _mod_spec(geo, layer, D_MODEL, lambda j: 3),
        _weight_spec(wop_g, D_MODEL, tn),
        _weight_spec(wop_u, D_MODEL, tn),
        cs_spec,
        pl.BlockSpec((CONV_K_F, tn), lambda i, j: (0, j)),
        pl.BlockSpec((1, tn), lambda i, j: (0, j)),
    ]
    args = [x3, nw.reshape(1, D_MODEL), mod, mod, wop_g[0], wop_u[0], conv_state, conv_w,
            conv_b.reshape(1, D_FF)]
    aliases = {}
    if stacked:
        tails_spec = pl.BlockSpec((1, geo.seq_blk, CONV_K_F - 1, tn), lambda i, j: (layer, 0, 0, j))
        tails_shape = (DEPTH, geo.n_seq, CONV_K_F - 1, D_FF)
        if stacked_prev is not None:
            kernel_fn = _drop_ref(kernel_fn, len(in_specs))
            aliases[len(in_specs)] = 1
            in_specs.append(pl.BlockSpec(memory_space=pl.ANY))
            args.append(stacked_prev)
    else:
        tails_spec = pl.BlockSpec((1, geo.seq_blk, CONV_K_F - 1, tn), lambda i, j: (i, 0, 0, j))
        tails_shape = (n_i, geo.seq_blk, CONV_K_F - 1, D_FF)
    act, tails, *emitted = pl.pallas_call(
        kernel_fn,
        grid=(n_i, n_j),
        in_specs=in_specs,
        out_specs=[pl.BlockSpec((tm, tn), lambda i, j: (i, j)), tails_spec] + emit_specs,
        out_shape=[jax.ShapeDtypeStruct((geo.n_seq * geo.seq_len, D_FF), MXU_DTYPE),
                   jax.ShapeDtypeStruct(tails_shape, F32)] + emit_shapes,
        scratch_shapes=[pltpu.VMEM((tm, D_MODEL), MXU_DTYPE),
                        pltpu.VMEM(tail_shape, F32)],
        input_output_aliases=aliases,
        compiler_params=_cparams(("arbitrary", "arbitrary")),
        name="ffn_in",
    )(*args)
    if not stacked:
        last = tails.reshape(n_i // nrt, nrt, geo.seq_blk, CONV_K_F - 1, D_FF)[:, nrt - 1]
        tails = last.reshape(1, geo.n_seq, CONV_K_F - 1, D_FF)
    return act, tails, (emitted if emit else None)


def _norm_kernel(x_ref, nw_ref, o_ref):
    o_ref[...] = _rmsnorm_rows(x_ref[...]) * nw_ref[...]


def _final_norm(x3, nw, geo):
    _, n_i, _ = _tiles(geo)
    return pl.pallas_call(
        _norm_kernel,
        grid=(n_i, 1),
        in_specs=[_x_spec(geo, D_MODEL, lambda j: 0),
                  pl.BlockSpec((1, D_MODEL), lambda i, j: (0, 0))],
        out_specs=_x_spec(geo, D_MODEL, lambda j: 0),
        out_shape=jax.ShapeDtypeStruct(x3.shape, F32),
        compiler_params=_cparams(("parallel", "arbitrary")),
        name="final_norm",
    )(x3, nw.reshape(1, D_MODEL))


def _state_spec(shape, layer, geo):
    rest = tuple(shape[2:])
    zeros = (0,) * len(rest)
    return pl.BlockSpec((1, geo.scan_seqs) + rest, lambda i, c: (layer, i) + zeros)


def _row_spec(geo, width, col):
    return pl.BlockSpec((geo.scan_seqs, geo.chunk, width), lambda i, c: (i, c, col))


def _rows3(a, geo):
    return a.reshape(geo.n_seq, geo.seq_len, a.shape[-1])


def _gates_t_spec(geo):
    return pl.BlockSpec((geo.scan_seqs, 1, GATE_W, geo.chunk), lambda i, c: (i, c, 0, 0))


def _gates_t(gates, geo):
    nc = geo.seq_len // geo.chunk
    return gates.reshape(geo.n_seq, nc, geo.chunk, GATE_W).transpose(0, 1, 3, 2)


ROWS, LEAD, STATE, CONST = "rows", "lead", "state", "const"


def _per_sequence(body, geo, kinds, interleave):
    def view(ref, kind, k):
        if kind == ROWS:
            return ref.at[k]
        if kind == LEAD:
            return ref.at[pl.ds(k, 1)]
        if kind == STATE:
            return ref.at[:, pl.ds(k, 1)]
        return ref

    def wrapped(*refs):
        running = [body(*[view(r, kind, k) for r, kind in zip(refs, kinds)])
                   for k in range(geo.scan_seqs)]
        if not interleave:
            for gen in running:
                for _ in gen:
                    pass
            return
        while running:
            still = []
            for gen in running:
                if next(gen, StopIteration) is not StopIteration:
                    still.append(gen)
            running = still
    return wrapped


def _seq_call(kernel_fn, geo, name, in_specs, args, out_specs, out_shapes, scratch, kinds,
              stacked_prev, interleave=True):
    nc = geo.seq_len // geo.chunk
    kernel_fn = _per_sequence(kernel_fn, geo, kinds, interleave)
    in_specs, args = list(in_specs), list(args)
    aliases = {}
    assert len(stacked_prev) <= 1
    for out_idx, arr in stacked_prev.items():
        kernel_fn = _drop_ref(kernel_fn, len(in_specs))
        aliases[len(in_specs)] = out_idx
        in_specs.append(pl.BlockSpec(memory_space=pl.ANY))
        args.append(arr)
    return pl.pallas_call(
        kernel_fn,
        grid=(geo.n_seq // geo.scan_seqs, nc),
        in_specs=in_specs,
        out_specs=out_specs,
        out_shape=out_shapes,
        scratch_shapes=scratch,
        input_output_aliases=aliases,
        compiler_params=_cparams(("parallel", "arbitrary")),
        name=name,
    )(*args)


def _ssd_kernel(z_ref, xbc_ref, gc_ref, gr_ref, cs_ref, s0_ref, cw_ref, cb_ref,
                dtb_r_ref, dtb_c_ref, al_r_ref, al_c_ref, dsk_ref, nw_ref, exp_ref,
                y_ref, cso_ref, so_ref, tail_scr, *, lc, lv, single_chunk):
    if single_chunk:
        s_ref = s0_ref
        cs = cs_ref[0, 0]
        tails = [cs[0:1, :], cs[1:2, :], cs[2:3, :]]
    else:
        @pl.when(pl.program_id(1) == 0)
        def _():
            tail_scr[0, 5:8, :] = cs_ref[0, 0]
            so_ref[0, 0] = s0_ref[0, 0]

        s_ref = so_ref
        tails = [tail_scr[0, 5:6, :], tail_scr[0, 6:7, :], tail_scr[0, 7:8, :]]

    x = xbc_ref[...]
    w = cw_ref[0]
    xc = (cb_ref[0] + w[0:1] * _shifted(x, tails, 3) + w[1:2] * _shifted(x, tails, 2)
          + w[2:3] * _shifted(x, tails, 1) + w[3:4] * x)
    xc = xc * _sigmoid(xc)
    new_tail = x[lv - 3:lv, :]
    if not single_chunk:
        tail_scr[0, 5:8, :] = new_tail
    cso_ref[0, 0] = new_tail
    xa = xc[:, :D_A]
    bm = xc[:, D_A:D_A + G_A * N_A]
    cm = xc[:, D_A + G_A * N_A:]

    dt_c = _softplus(gc_ref[:, 0:H_A] + dtb_r_ref[...])
    dt_r = _softplus(gr_ref[0, 0:H_A, :] + dtb_c_ref[...])
    if lv < lc:
        dt_c = jnp.where(lax.broadcasted_iota(jnp.int32, dt_c.shape, 0) < lv, dt_c, 0.0)
        dt_r = jnp.where(lax.broadcasted_iota(jnp.int32, dt_r.shape, 1) < lv, dt_r, 0.0)
    lower, upper, mask = _tri(lc)
    cum_c = _select_dot(lower, dt_c * (-jnp.exp(al_r_ref[...])), 'a')
    cum_r = _select_dot(dt_r * (-jnp.exp(al_c_ref[...])), upper, 'b')
    cum_last = cum_c[lc - 1:lc, :]
    expand = exp_ref[...]
    ecum_x = _select_dot(jnp.exp(cum_c), expand, 'b')
    tail_x = _select_dot(jnp.exp(cum_last - cum_c) * dt_c, expand, 'b')

    hg = H_A // G_A
    gw = hg * P_A
    groups = range(G_A)
    gsl = lambda g: slice(g * gw, (g + 1) * gw)
    cg = [cm[:, g * N_A:(g + 1) * N_A] for g in groups]
    bg = [bm[:, g * N_A:(g + 1) * N_A] for g in groups]
    yield
    cb_ts = [lax.dot_general(cg[g], bg[g], _NT, preferred_element_type=F32) for g in groups]
    y_inter = [lax.dot_general(cg[g], s_ref[0, 0, g], _NT, preferred_element_type=F32)
               for g in groups]
    upd = [lax.dot_general(xa[:, gsl(g)] * tail_x[:, gsl(g)], bg[g], _TN,
                           preferred_element_type=F32) for g in groups]
    yield
    w_ts = [cb_ts[h // hg] * dt_r[h:h + 1, :]
            * jnp.exp(jnp.where(mask, cum_c[:, h:h + 1] - cum_r[h:h + 1, :], -jnp.inf))
            for h in range(H_A)]
    yield
    pieces = []
    for j in range(H_A // 2):
        xp = xa[:, j * 128:(j + 1) * 128]
        lane = lax.broadcasted_iota(jnp.int32, xp.shape, 1)
        pieces.append(
            jnp.dot(w_ts[2 * j], jnp.where(lane < P_A, xp, 0.0), preferred_element_type=F32)
            + jnp.dot(w_ts[2 * j + 1], jnp.where(lane >= P_A, xp, 0.0),
                      preferred_element_type=F32))
    yield
    y = (jnp.concatenate(pieces, axis=1) + jnp.concatenate(y_inter, axis=1) * ecum_x
         + dsk_ref[...] * xa)
    for g in groups:
        for hh in range(hg):
            h = g * hg + hh
            rows = slice(hh * P_A, (hh + 1) * P_A)
            so_ref[0, 0, g, rows, :] = (s_ref[0, 0, g, rows, :] * jnp.exp(cum_r[h:h + 1, lc - 1:lc])
                                        + upd[g][rows, :])

    z = z_ref[...]
    y = y * (z * _sigmoid(z))
    y = jnp.concatenate([_rmsnorm_rows(y[:, g * gw:(g + 1) * gw]) for g in range(G_A)], axis=1)
    y_ref[...] = (y * nw_ref[...]).astype(y_ref.dtype)


def _ssd(p1, gates, conv_state, ssd_state, prm, e, geo, prev_state_out):
    gates_t = _gates_t(gates, geo)
    b = geo.n_seq
    const2 = lambda shape: pl.BlockSpec(shape, lambda i, c: (0, 0))
    n_even = ssd_state.shape[0]
    s5 = ssd_state.reshape(n_even, b, G_A, (H_A // G_A) * P_A, N_A)
    expand = (jnp.arange(D_A)[None, :] // P_A == jnp.arange(H_A)[:, None]).astype(F32)
    cso_shape = (1,) + conv_state.shape[1:]
    y, cso, so = _seq_call(
        functools.partial(_ssd_kernel, lc=geo.chunk, lv=geo.valid,
                          single_chunk=geo.seq_len == geo.chunk), geo, "ssd",
        in_specs=[
            _row_spec(geo, D_A, 0),
            _row_spec(geo, CONV_DIM_A, 2),
            _row_spec(geo, GATE_W, 0),
            _gates_t_spec(geo),
            _state_spec(conv_state.shape, e, geo),
            _state_spec(s5.shape, e, geo),
            pl.BlockSpec((1, CONV_K_A, CONV_DIM_A), lambda i, c: (e, 0, 0)),
            pl.BlockSpec((1, 1, CONV_DIM_A), lambda i, c: (e, 0, 0)),
            const2((1, H_A)), const2((H_A, 1)), const2((1, H_A)), const2((H_A, 1)),
            const2((1, D_A)), const2((1, D_A)), const2((H_A, D_A)),
        ],
        args=(_rows3(p1, geo), _rows3(p1, geo), _rows3(gates, geo), gates_t, conv_state, s5,
              prm['conv_w_a'], prm['conv_b_a'].reshape(n_even, 1, CONV_DIM_A),
              prm['dt_bias'][e].reshape(1, H_A), prm['dt_bias'][e].reshape(H_A, 1),
              prm['a_log'][e].reshape(1, H_A), prm['a_log'][e].reshape(H_A, 1),
              jnp.repeat(prm['d_skip'][e], P_A).reshape(1, D_A),
              prm['norm_a'][e].reshape(1, D_A), expand),
        out_specs=[_row_spec(geo, D_A, 0), _state_spec(cso_shape, 0, geo),
                   _state_spec(s5.shape, e, geo)],
        out_shapes=[
            jax.ShapeDtypeStruct((b, geo.seq_len, D_A), MXU_DTYPE),
            jax.ShapeDtypeStruct(cso_shape, F32),
            jax.ShapeDtypeStruct(s5.shape, F32),
        ],
        scratch=[pltpu.VMEM((geo.scan_seqs, SUBLANES, CONV_DIM_A), F32)],
        kinds=[ROWS] * 4 + [STATE, STATE] + [CONST] * 9 + [ROWS, STATE, STATE, LEAD],
        stacked_prev={} if prev_state_out is None else {2: prev_state_out},
    )
    return y.reshape(b * geo.seq_len, D_A), cso, so


def _mlstm_kernel(q_ref, k_ref, v_ref, og_ref, gc_ref, gr_ref, c0_ref, n0_ref, m0_ref,
                  ib_r_ref, ib_c_ref, fb_r_ref, fb_c_ref, nw_ref,
                  h_ref, co_ref, no_ref, mo_ref, *, lc, lv, single_chunk):
    if single_chunk:
        c_ref, n_ref, m_ref = c0_ref, n0_ref, m0_ref
    else:
        @pl.when(pl.program_id(1) == 0)
        def _():
            co_ref[0, 0] = c0_ref[0, 0]
            no_ref[0, 0] = n0_ref[0, 0]
            mo_ref[0, 0] = m0_ref[0, 0]

        c_ref, n_ref, m_ref = co_ref, no_ref, mo_ref
    m_old, n_old = m_ref[0, 0], n_ref[0, 0]

    i0, f0 = H_A, H_A + H_B
    li_c = gc_ref[:, i0:i0 + H_B] + ib_r_ref[...]
    lf_c = _log_sigmoid(gc_ref[:, f0:f0 + H_B] + fb_r_ref[...])
    li_r = gr_ref[0, i0:i0 + H_B, :] + ib_c_ref[...]
    lf_r = _log_sigmoid(gr_ref[0, f0:f0 + H_B, :] + fb_c_ref[...])
    if lv < lc:
        vc = lax.broadcasted_iota(jnp.int32, li_c.shape, 0) < lv
        vr = lax.broadcasted_iota(jnp.int32, li_r.shape, 1) < lv
        li_c, lf_c = jnp.where(vc, li_c, NEG_BIG), jnp.where(vc, lf_c, 0.0)
        li_r, lf_r = jnp.where(vr, li_r, NEG_BIG), jnp.where(vr, lf_r, 0.0)
    lower, upper, mask = _tri(lc)
    bc_c = jnp.dot(lower, lf_c, precision=HI)
    bc_r = jnp.dot(lf_r, upper, precision=HI)

    heads = range(H_B)
    q = [q_ref[:, h * DK_B:(h + 1) * DK_B] * (DK_B ** -0.5) for h in heads]
    k = [k_ref[:, h * DK_B:(h + 1) * DK_B] for h in heads]
    v = [v_ref[:, h * DV_B:(h + 1) * DV_B] for h in heads]
    yield
    qk = [lax.dot_general(q[h], k[h], _NT, preferred_element_type=F32) for h in heads]
    yield
    m_t, w_in, w_ts = [], [], []
    for h in heads:
        bcc = bc_c[:, h:h + 1]
        dmat = jnp.where(mask, bcc - bc_r[h:h + 1, :] + li_r[h:h + 1, :], -jnp.inf)
        inter = bcc + m_old[:, h:h + 1]
        m_t.append(jnp.maximum(inter, jnp.max(dmat, axis=1, keepdims=True)))
        w_in.append(jnp.exp(inter - m_t[h]))
        w_ts.append(jnp.exp(dmat - m_t[h]) * qk[h])
    yield
    q_c = [jnp.dot(q[h], c_ref[0, 0, h], preferred_element_type=F32) for h in heads]
    wv = [jnp.dot(w_ts[h], v[h], preferred_element_type=F32) for h in heads]
    yield
    hs = []
    for h in heads:
        num = wv[h] + w_in[h] * q_c[h]
        den = (jnp.sum(w_ts[h], axis=1, keepdims=True)
               + w_in[h] * jnp.sum(q[h] * n_old[h:h + 1, :], axis=1, keepdims=True))
        hs.append(_rmsnorm_rows(num / jnp.maximum(jnp.abs(den), jnp.exp(-m_t[h]))))
    lane_h = lax.broadcasted_iota(jnp.int32, (1, H_B), 1)
    m_out = jnp.zeros((1, H_B), F32)
    ks, w_c = [], []
    for h in heads:
        bcc = bc_c[:, h:h + 1]
        m_new = m_t[h][lv - 1:lv, :]
        bc_last = bcc[lc - 1:lc, :]
        ks.append(k[h] * jnp.exp(bc_last - bcc + li_c[:, h:h + 1] - m_new))
        w_c.append(jnp.exp(bc_last + m_old[:, h:h + 1] - m_new))
        m_out = jnp.where(lane_h == h, m_new, m_out)
    yield
    kv = [lax.dot_general(ks[h], v[h], _TN, preferred_element_type=F32) for h in heads]
    yield
    for h in heads:
        co_ref[0, 0, h] = w_c[h] * c_ref[0, 0, h] + kv[h]
        no_ref[0, 0, h:h + 1, :] = w_c[h] * n_old[h:h + 1, :] + jnp.sum(ks[h], axis=0, keepdims=True)
    mo_ref[0, 0] = m_out
    hn = jnp.concatenate(hs, axis=1) * nw_ref[...]
    h_ref[...] = (hn * _sigmoid(og_ref[...])).astype(h_ref.dtype)


def _mlstm(p1, gates, c_state, n_state, m_state, prm, e, geo, prev_state_out):
    geo = geo._replace(scan_seqs=geo.mlstm_seqs, chunk=geo.mlstm_chunk,
                       valid=geo.mlstm_chunk if geo.valid == geo.chunk else geo.valid)
    gates_t = _gates_t(gates, geo)
    b = geo.n_seq
    const2 = lambda shape: pl.BlockSpec(shape, lambda i, c: (0, 0))
    qk_w = H_B * DK_B
    m4 = m_state.reshape(m_state.shape[0], b, 1, H_B)
    one = lambda shape: (1,) + tuple(shape[1:])
    h, co, no, mo = _seq_call(
        functools.partial(_mlstm_kernel, lc=geo.chunk, lv=geo.valid,
                          single_chunk=geo.seq_len == geo.chunk), geo, "mlstm",
        in_specs=[
            _row_spec(geo, qk_w, 9), _row_spec(geo, qk_w, 10),
            _row_spec(geo, D_B, 1), _row_spec(geo, D_B, 2),
            _row_spec(geo, GATE_W, 0),
            _gates_t_spec(geo),
            _state_spec(c_state.shape, e, geo), _state_spec(n_state.shape, e, geo),
            _state_spec(m4.shape, e, geo),
            const2((1, H_B)), const2((H_B, 1)), const2((1, H_B)), const2((H_B, 1)),
            const2((1, D_B)),
        ],
        args=(_rows3(p1, geo),) * 4 + (_rows3(gates, geo), gates_t, c_state, n_state, m4,
              prm['i_bias'][e].reshape(1, H_B), prm['i_bias'][e].reshape(H_B, 1),
              prm['f_bias'][e].reshape(1, H_B), prm['f_bias'][e].reshape(H_B, 1),
              prm['norm_b'][e].reshape(1, D_B)),
        out_specs=[_row_spec(geo, D_B, 0), _state_spec(c_state.shape, e, geo),
                   _state_spec(one(n_state.shape), 0, geo), _state_spec(one(m4.shape), 0, geo)],
        out_shapes=[
            jax.ShapeDtypeStruct((b, geo.seq_len, D_B), MXU_DTYPE),
            jax.ShapeDtypeStruct(c_state.shape, F32),
            jax.ShapeDtypeStruct(one(n_state.shape), F32),
            jax.ShapeDtypeStruct(one(m4.shape), F32),
        ],
        scratch=[],
        kinds=[ROWS] * 6 + [STATE] * 3 + [CONST] * 5 + [ROWS] + [STATE] * 3,
        stacked_prev={} if prev_state_out is None else {1: prev_state_out},
        interleave=geo.seq_len != geo.chunk,
    )
    return h.reshape(b * geo.seq_len, D_B), co, no, mo.reshape(1, b, H_B)


def _hgrn_kernel(q_ref, f_ref, i_ref, g_ref, lbl_ref, s0_ref, nw_ref, o_ref, so_ref,
                 *, lc, lv, layer_o, bs, single_chunk):
    if single_chunk:
        s_ref = s0_ref
    else:
        @pl.when(pl.program_id(1) == 0)
        def _():
            so_ref[0, 0] = s0_ref[0, 0]

        s_ref = so_ref

    lbl = lbl_ref[...]
    ex = jnp.exp(lbl - jnp.max(lbl, axis=0, keepdims=True))
    sm = ex / jnp.sum(ex, axis=0, keepdims=True)
    lb_all = [sm[0:1, :]]
    for r in range(1, lbl.shape[0]):
        lb_all.append(lb_all[-1] + sm[r:r + 1, :])
    lb = lb_all[layer_o] - lb_all[0]

    fx = f_ref[...]
    e1 = jnp.exp(-jnp.abs(fx))
    log_sig = jnp.minimum(fx, 0.0) - jnp.log1p(e1)
    la = jnp.log(lb)
    lb_ = jnp.log1p(-lb) + log_sig
    logf = jnp.maximum(la, lb_) + jnp.log1p(jnp.exp(-jnp.abs(la - lb_)))
    kk = (1.0 - lb) * (jnp.where(fx >= 0.0, e1, 1.0) / (1.0 + e1))
    if lv < lc:
        valid = lax.broadcasted_iota(jnp.int32, fx.shape, 0) < lv
        logf = jnp.where(valid, logf, 0.0)
        kk = jnp.where(valid, kk, 0.0)

    nb = lc // bs
    r_i = lax.broadcasted_iota(jnp.int32, (lc, lc), 0)
    c_i = lax.broadcasted_iota(jnp.int32, (lc, lc), 1)
    sh = int(math.log2(bs))
    blk_lower = ((c_i <= r_i) & ((c_i >> sh) == (r_i >> sh))).astype(F32)
    gw = jnp.dot(blk_lower, logf, precision=HI)
    q = q_ref[...]
    v = i_ref[...]
    blk = lambda a, i: a[i * bs:(i + 1) * bs, :]
    tots = [gw[(i + 1) * bs - 1:(i + 1) * bs, :] for i in range(nb)]
    before = [jnp.zeros_like(tots[0])]
    for i in range(nb):
        before.append(before[-1] + tots[i])
    g_tot = before[nb]
    qt = q * jnp.exp(gw)
    kt = [blk(kk, j) * jnp.exp(tots[j] - blk(gw, j)) for j in range(nb)]
    q_in = jnp.concatenate([blk(qt, i) * jnp.exp(before[i]) for i in range(nb)], axis=0)
    k_out = jnp.concatenate([kt[j] * jnp.exp(g_tot - before[j + 1]) for j in range(nb)], axis=0)

    heads = range(H_C)
    hsl = lambda h: slice(h * DK_C, (h + 1) * DK_C)
    yield
    o_inter = [jnp.dot(q_in[:, hsl(h)], s_ref[0, 0, h], preferred_element_type=F32)
               for h in heads]
    kv = [lax.dot_general(k_out[:, hsl(h)], v[:, hsl(h)], _TN, preferred_element_type=F32)
          for h in heads]
    att_off = [None]
    for i in range(1, nb):
        k_hat = jnp.concatenate(
            [kt[j] if j == i - 1 else kt[j] * jnp.exp(before[i] - before[j + 1])
             for j in range(i)] + [jnp.zeros((lc - i * bs, D_C), F32)], axis=0)
        q_ti = blk(qt, i)
        att_off.append([lax.dot_general(q_ti[:, hsl(h)], k_hat[:, hsl(h)], _NT,
                                        preferred_element_type=F32) for h in heads])
    yield
    gw2 = gw * LOG2E
    ck = jnp.log2(kk) - gw2
    n_t = bs // SUBLANES
    lane_s = lax.broadcasted_iota(jnp.int32, (SUBLANES, lc), 1)
    row_t = lax.broadcasted_iota(jnp.int32, (bs, lc), 0)
    col_s = lax.broadcasted_iota(jnp.int32, (bs, lc), 1)
    att = []
    for i in range(nb):
        g_i, q_i, ck_i = blk(gw2, i), blk(q, i), blk(ck, i)
        att_d = [[jnp.zeros((SUBLANES, lc), F32) for _ in range(n_t)] for _ in heads]
        for s in range(bs):
            t0 = s // SUBLANES
            p = q_i[t0 * SUBLANES:, :] * jnp.exp2(g_i[t0 * SUBLANES:, :] + ck_i[s:s + 1, :])
            for h in heads:
                a = jnp.sum(p[:, hsl(h)], axis=1, keepdims=True)
                for tt in range(t0, n_t):
                    a_t = a[(tt - t0) * SUBLANES:(tt - t0 + 1) * SUBLANES, :]
                    att_d[h][tt] = jnp.where(lane_s == i * bs + s, a_t, att_d[h][tt])
        causal = (col_s - i * bs) <= row_t
        att_i = []
        for h in heads:
            a = att_d[h][0] if n_t == 1 else jnp.concatenate(att_d[h], axis=0)
            a = jnp.where(causal, a, 0.0)
            att_i.append(a if i == 0 else a + att_off[i][h])
        att.append(att_i)
    yield
    o_intra = [[jnp.dot(att[i][h], v[:, hsl(h)], preferred_element_type=F32) for h in heads]
               for i in range(nb)]
    yield
    outs = []
    for h in heads:
        o_h = o_intra[0][h] if nb == 1 else jnp.concatenate([o_intra[i][h] for i in range(nb)],
                                                             axis=0)
        outs.append(_rmsnorm_rows(o_h + o_inter[h]))
        dec_col = jnp.transpose(jnp.broadcast_to(jnp.exp(g_tot[:, hsl(h)]), (DK_C, DK_C)))
        so_ref[0, 0, h] = dec_col * s_ref[0, 0, h] + kv[h]
    gate = g_ref[...]
    o_ref[...] = (jnp.concatenate(outs, axis=1) * nw_ref[...]
                  * (gate * _sigmoid(gate))).astype(o_ref.dtype)


def _hgrn(p, state, prm, o, geo, prev_state_out):
    geo = geo._replace(scan_seqs=geo.hgrn_seqs)
    n_odd = prm['lb_logits'].shape[0]
    out, new_state = _seq_call(
        functools.partial(_hgrn_kernel, lc=geo.chunk, lv=geo.valid, layer_o=o,
                          bs=min(16, geo.chunk), single_chunk=geo.seq_len == geo.chunk), geo, "hgrn",
        in_specs=[
            _row_spec(geo, D_C, 0), _row_spec(geo, D_C, 1), _row_spec(geo, D_C, 2),
            _row_spec(geo, D_C, 3),
            pl.BlockSpec((n_odd, D_C), lambda i, c: (0, 0)),
            _state_spec(state.shape, o, geo),
            pl.BlockSpec((1, D_C), lambda i, c: (0, 0)),
        ],
        args=(_rows3(p, geo),) * 4 + (prm['lb_logits'], state, prm['norm_c'][o].reshape(1, D_C)),
        out_specs=[_row_spec(geo, D_C, 0), _state_spec(state.shape, o, geo)],
        out_shapes=[
            jax.ShapeDtypeStruct((geo.n_seq, geo.seq_len, D_C), MXU_DTYPE),
            jax.ShapeDtypeStruct(state.shape, F32),
        ],
        scratch=[],
        kinds=[ROWS] * 4 + [CONST, STATE, CONST, ROWS, STATE],
        stacked_prev={} if prev_state_out is None else {1: prev_state_out},
    )
    return out.reshape(geo.n_seq * geo.seq_len, D_C), new_state


def _in_ab_column_groups():
    a0, a1, a2 = D_A, D_A + CONV_DIM_A, D_A + CONV_DIM_A + H_A
    q1 = a2 + H_B * DK_B
    k1 = q1 + H_B * DK_B
    v1 = k1 + D_B
    o1 = v1 + D_B
    end = o1 + 2 * H_B
    return [(0, a0), (k1, v1), (v1, o1), (a0, a1), (a2, q1), (q1, k1)], [(a1, a2), (o1, end)]


def _regroup_in_ab(wab):
    wide, narrow = _in_ab_column_groups()
    n_gate = sum(b - a for a, b in narrow)
    in_ab = jnp.concatenate([wab[:, :, a:b] for a, b in wide], axis=2).astype(MXU_DTYPE)
    gates = jnp.pad(jnp.concatenate([wab[:, :, a:b] for a, b in narrow], axis=2),
                    ((0, 0), (0, 0), (0, GATE_W - n_gate))).astype(MXU_DTYPE)
    return in_ab, gates


def _prep_weights(prm):
    wab = prm['w_in_ab']
    in_ab, gates_ab = _regroup_in_ab(wab)
    n_even, n_odd = wab.shape[0], prm['w_in_c'].shape[0]
    return {
        'in_ab': [[(in_ab, e, 0)] for e in range(n_even)],
        'gates_ab': [[(gates_ab, e, 0)] for e in range(n_even)],
        'out_ab': [[(prm['w_out_ab'], e, 0), (prm['w_out_ab'], e, 1)] for e in range(n_even)],
        'in_c': [[(prm['w_in_c'], o, 0)] for o in range(n_odd)],
        'out_c': [[(prm['w_out_c'], o, 0)] for o in range(n_odd)],
        'ffn_g': [[(prm['w_ffn_g'], l, 0)] for l in range(DEPTH)],
        'ffn_u': [[(prm['w_ffn_u'], l, 0)] for l in range(DEPTH)],
        'ffn_d': [[(prm['w_ffn_d'], l, 0)] for l in range(DEPTH)],
    }


def _trunk(x3, mod, states, prm, w, geo, tn_in, tn_res, tn_ffn, tn_down, emit, rows_in=None):
    conv_a, ssd, mem_c, mem_n, mem_m, hgrn, ffn_buf = states
    n_conv, n_n, n_m, n_ffn = [], [], [], []
    new_ssd = new_c = new_hgrn = ffn_stacked = None
    wq = {name: list(per_layer) for name, per_layer in w.items()}

    def record(name, idx, emitted):
        if emitted is not None:
            wq[name][idx] = [(arr, 0, 0) for arr in emitted]

    casts = lambda wops: emit and any(wop[0].dtype != MXU_DTYPE for wop in wops)
    for layer in range(DEPTH):
        geo_in = geo if rows_in is None else geo._replace(row_blk=rows_in[layer])
        mm_norm = functools.partial(_mm_norm, x3, prm['norm_mix'][layer], mod, layer, 1, 0,
                                    geo=geo_in)
        if layer % 2 == 0:
            e = layer // 2
            p1, gates, _ = mm_norm(w['in_ab'][e][0], tn=tn_in[layer], narrow=w['gates_ab'][e][0])
            ya, cv, new_ssd = _ssd(p1, gates, conv_a, ssd, prm, e, geo, new_ssd)
            hb, new_c, nn, mmm = _mlstm(p1, gates, mem_c, mem_n, mem_m, prm, e, geo, new_c)
            n_conv.append(cv); n_n.append(nn); n_m.append(mmm)
            x3, em = _mm_res([ya, hb], w['out_ab'][e], x3, mod, layer, 2, geo, tn_res[layer],
                             emit=casts(w['out_ab'][e]))
            record('out_ab', e, em)
        else:
            o = layer // 2
            p, _, em = mm_norm(w['in_c'][o][0], tn=tn_in[layer], emit=casts(w['in_c'][o]))
            record('in_c', o, None if em is None else [em])
            oc, new_hgrn = _hgrn(p, hgrn, prm, o, geo, new_hgrn)
            x3, em = _mm_res([oc], w['out_c'][o], x3, mod, layer, 2, geo, tn_res[layer],
                             emit=casts(w['out_c'][o]))
            record('out_c', o, em)
        act, fb, em = _ffn_in(x3, prm['norm_ffn'][layer], mod, layer, w['ffn_g'][layer][0],
                              w['ffn_u'][layer][0], ffn_buf, prm['conv_w_f'][layer],
                              prm['conv_b_f'][layer], geo_in, tn_ffn,
                              emit=casts(w['ffn_g'][layer] + w['ffn_u'][layer]),
                              stacked_prev=ffn_stacked)
        if em is not None:
            record('ffn_g', layer, em[:1])
            record('ffn_u', layer, em[1:])
        if fb.shape[0] == DEPTH:
            ffn_stacked = fb
        else:
            n_ffn.append(fb)
        x3, em = _mm_res([act], w['ffn_d'][layer], x3, mod, layer, 5, geo, tn_down,
                         emit=casts(w['ffn_d'][layer]))
        record('ffn_d', layer, em)
    y = _final_norm(x3, prm['norm_f'], geo)
    cat = lambda xs: jnp.concatenate(xs, axis=0)
    return (y, cat(n_conv), new_ssd.reshape(ssd.shape), new_c, cat(n_n), cat(n_m), new_hgrn,
            ffn_stacked if ffn_stacked is not None else cat(n_ffn)), wq


def kernel(x_prompt, x_sample, c_prompt, c_sample, state_ssd_conv, state_ssd, state_mlstm_c, state_mlstm_n, state_mlstm_m, state_hgrn, state_ffn_conv, w_ada, b_ada, norm_mix, norm_ffn, w_in_ab, conv_w_a, conv_b_a, dt_bias, a_log, d_skip, norm_a, i_bias, f_bias, norm_b, w_out_ab, w_in_c, lb_logits, norm_c, w_out_c, w_ffn_g, w_ffn_u, conv_w_f, conv_b_f, w_ffn_d, norm_f):
    prm = dict(norm_mix=norm_mix, norm_ffn=norm_ffn, w_in_ab=w_in_ab, conv_w_a=conv_w_a,
               conv_b_a=conv_b_a, dt_bias=dt_bias, a_log=a_log, d_skip=d_skip, norm_a=norm_a,
               i_bias=i_bias, f_bias=f_bias, norm_b=norm_b, w_out_ab=w_out_ab, w_in_c=w_in_c,
               lb_logits=lb_logits, norm_c=norm_c, w_out_c=w_out_c, w_ffn_g=w_ffn_g,
               w_ffn_u=w_ffn_u, conv_w_f=conv_w_f, conv_b_f=conv_b_f, w_ffn_d=w_ffn_d,
               norm_f=norm_f)
    bp, lp, _ = x_prompt.shape
    bs, ls, _ = x_sample.shape
    n_even, n_odd = state_ssd.shape[0], state_hgrn.shape[0]
    w = _prep_weights(prm)

    n_c = bs + bp
    n_c_pad = -(-n_c // SUBLANES) * SUBLANES
    c_all = jnp.pad(jnp.concatenate([c_sample, c_prompt], axis=0), ((0, n_c_pad - n_c), (0, 0)))
    mod = _ada(c_all, w_ada, b_ada).reshape(DEPTH, n_c_pad, 1, 6 * D_MODEL)

    geo_s = Geo(n_seq=bs, seq_len=SUBLANES, seq_blk=bs, row_blk=SUBLANES, chunk=SUBLANES,
                valid=ls, mod_off=0, scan_seqs=8, mlstm_seqs=4, hgrn_seqs=8, mlstm_chunk=SUBLANES)
    xs = jnp.pad(x_sample, ((0, 0), (0, SUBLANES - ls), (0, 0)))
    st_s = (state_ssd_conv, state_ssd, state_mlstm_c, state_mlstm_n, state_mlstm_m, state_hgrn,
            state_ffn_conv)
    out_s, w_bf16 = _trunk(xs, mod, st_s, prm, w, geo_s, tn_in=(512,) * DEPTH,
                           tn_res=(512,) * DEPTH, tn_ffn=256, tn_down=256, emit=True)

    zeros = lambda *s: jnp.zeros(s, F32)
    st_p = (zeros(n_even, bp, CONV_K_A - 1, CONV_DIM_A), zeros(n_even, bp, H_A, P_A, N_A),
            zeros(n_even, bp, H_B, DK_B, DV_B), zeros(n_even, bp, H_B, DK_B),
            zeros(n_even, bp, H_B), zeros(n_odd, bp, H_C, DK_C, DV_C),
            zeros(DEPTH, bp, CONV_K_F - 1, D_FF))
    lc_p = math.gcd(lp, PROMPT_CHUNK)
    geo_p = Geo(n_seq=bp, seq_len=lp, seq_blk=1, row_blk=min(lp, 1024), chunk=lc_p, valid=lc_p,
                mod_off=bs, scan_seqs=1, mlstm_seqs=1, hgrn_seqs=2 if bp % 2 == 0 else 1,
                mlstm_chunk=math.gcd(lp, 2 * PROMPT_CHUNK))
    out_p, _ = _trunk(x_prompt, mod, st_p, prm, w_bf16, geo_p, tn_in=(1024,) * DEPTH,
                      tn_res=(1024,) * DEPTH, tn_ffn=FFN_TN, tn_down=FFN_DOWN_TN, emit=False)
    return (out_p[0], out_s[0][:, :ls]) + tuple(out_p[1:]) + tuple(out_s[1:])
```

```python
import collections
import functools
import math

import jax
import jax.numpy as jnp
from jax import lax
from jax.experimental import pallas as pl
from jax.experimental.pallas import tpu as pltpu

F32 = jnp.float32
MXU_DTYPE = jnp.bfloat16
HI = lax.Precision.HIGHEST
NEG_BIG = -1e30
LOG2E = 1.4426950408889634

D_MODEL = 2048
DEPTH = 4
EPS = 1e-6
PROMPT_CHUNK = 128
H_A, P_A, G_A, N_A, CONV_K_A = 32, 64, 4, 128, 4
D_A = H_A * P_A
CONV_DIM_A = D_A + 2 * G_A * N_A
H_B, DK_B, DV_B = 8, 128, 256
D_B = H_B * DV_B
H_C, DK_C, DV_C = 16, 128, 128
D_C = H_C * DV_C
D_FF, CONV_K_F = 5632, 3
FFN_TN = 512
FFN_DOWN_TN = 512
PROLOGUE_SPLIT = 4
GATE_W = 128
SUBLANES = 8
VMEM_LIMIT = 56 * 1024 * 1024

_NT = (((1,), (1,)), ((), ()))
_TN = (((0,), (0,)), ((), ()))

Geo = collections.namedtuple(
    "Geo", "n_seq seq_len seq_blk row_blk chunk valid mod_off scan_seqs mlstm_seqs hgrn_seqs "
           "mlstm_chunk")


def _cparams(sem):
    return pltpu.CompilerParams(dimension_semantics=sem, vmem_limit_bytes=VMEM_LIMIT)


def _sigmoid(x):
    return 1.0 / (1.0 + jnp.exp(-x))


def _softplus(x):
    return jnp.maximum(x, 0.0) + jnp.log1p(jnp.exp(-jnp.abs(x)))


def _log_sigmoid(x):
    return jnp.minimum(x, 0.0) - jnp.log1p(jnp.exp(-jnp.abs(x)))


def _tri(n):
    r = lax.broadcasted_iota(jnp.int32, (n, n), 0)
    c = lax.broadcasted_iota(jnp.int32, (n, n), 1)
    mask = c <= r
    return mask.astype(F32), (r <= c).astype(F32), mask


def _split3(a):
    hi = a.astype(MXU_DTYPE)
    rest = a - hi.astype(F32)
    mid = rest.astype(MXU_DTYPE)
    lo = (rest - mid.astype(F32)).astype(MXU_DTYPE)
    return hi, mid, lo


def _select_dot(a, b, selector):
    if selector == 'a':
        s = a.astype(MXU_DTYPE)
        parts = [jnp.dot(s, t, preferred_element_type=F32) for t in _split3(b)]
    else:
        s = b.astype(MXU_DTYPE)
        parts = [jnp.dot(t, s, preferred_element_type=F32) for t in _split3(a)]
    return parts[0] + parts[1] + parts[2]


def _shifted(x, tails, k, axis=0):
    rolled = pltpu.roll(x, k, axis)
    head = lax.slice_in_dim(rolled, 0, SUBLANES, axis=axis)
    row = lax.broadcasted_iota(jnp.int32, head.shape, axis)
    for r in range(k):
        head = jnp.where(row == r, tails[len(tails) - k + r], head)
    if x.shape[axis] == SUBLANES:
        return head
    rest = lax.slice_in_dim(rolled, SUBLANES, x.shape[axis], axis=axis)
    return jnp.concatenate([head, rest], axis=axis)


def _drop_ref(fn, idx):
    def wrapped(*refs):
        return fn(*refs[:idx], *refs[idx + 1:])
    return wrapped


def _rmsnorm_rows(x):
    return x * lax.rsqrt(jnp.mean(x * x, axis=-1, keepdims=True) + EPS)


def _ada_kernel(c_ref, w_ref, b_ref, o_ref):
    c = c_ref[...]
    ca = (c * _sigmoid(c)).astype(MXU_DTYPE)
    o_ref[0] = jnp.dot(ca, w_ref[0].astype(MXU_DTYPE), preferred_element_type=F32) + b_ref[0]


def _ada(c_all, w_ada, b_ada, tn=1024):
    rows = c_all.shape[0]
    n = w_ada.shape[2]
    return pl.pallas_call(
        _ada_kernel,
        grid=(DEPTH, n // tn),
        in_specs=[
            pl.BlockSpec((rows, D_MODEL), lambda l, j: (0, 0)),
            pl.BlockSpec((1, D_MODEL, tn), lambda l, j: (l, 0, j)),
            pl.BlockSpec((1, 1, tn), lambda l, j: (l, 0, j)),
        ],
        out_specs=pl.BlockSpec((1, rows, tn), lambda l, j: (l, 0, j)),
        out_shape=jax.ShapeDtypeStruct((DEPTH, rows, n), F32),
        compiler_params=_cparams(("parallel", "parallel")),
        name="ada",
    )(c_all, w_ada, b_ada.reshape(DEPTH, 1, n))


def _tiles(geo):
    nrt = geo.seq_len // geo.row_blk
    return nrt, (geo.n_seq // geo.seq_blk) * nrt, geo.seq_blk * geo.row_blk


def _x_spec(geo, width, col_of):
    nrt = geo.seq_len // geo.row_blk
    return pl.BlockSpec((geo.seq_blk, geo.row_blk, width),
                        lambda i, j: (i // nrt, i % nrt, col_of(j)))


def _mod_spec(geo, layer, width, col_of):
    nrt = geo.seq_len // geo.row_blk
    return pl.BlockSpec((1, geo.seq_blk, 1, width),
                        lambda i, j: (layer, geo.mod_off + i // nrt, 0, col_of(j)))


def _norm_mod_chunks(x_ref, nw_ref, sc_ref, sh_ref):
    seq_blk, row_blk, d = x_ref.shape
    for r in range(PROLOGUE_SPLIT):
        if seq_blk == 1:
            n = row_blk // PROLOGUE_SPLIT
            x, sc, sh = x_ref[:, r * n:(r + 1) * n], sc_ref[0], sh_ref[0]
        else:
            n = seq_blk // PROLOGUE_SPLIT
            x, sc, sh = x_ref[r * n:(r + 1) * n], sc_ref[0, r * n:(r + 1) * n], sh_ref[0, r * n:(r + 1) * n]
        h = (_rmsnorm_rows(x) * nw_ref[...]) * (1.0 + sc) + sh
        rows = h.shape[0] * h.shape[1]
        yield slice(r * rows, (r + 1) * rows), h.reshape(rows, d).astype(MXU_DTYPE)


def _weight_spec(wop, k, tn):
    _, layer, row_block = wop
    return pl.BlockSpec((1, k, tn), lambda i, j: (layer, row_block, j))


def _emit_spec_shape(k, n, tn):
    return (pl.BlockSpec((1, k, tn), lambda i, j: (0, 0, j)),
            jax.ShapeDtypeStruct((1, k, n), MXU_DTYPE))


def _weight_tile(w_ref, wq_ref):
    w = w_ref[0].astype(MXU_DTYPE)
    if wq_ref is not None:
        wq_ref[0] = w
    return w


def _mm_norm_kernel(*refs, emit, narrow):
    x_ref, nw_ref, sc_ref, sh_ref, w_ref = refs[:5]
    rest = list(refs[5:])
    wn_ref = rest.pop(0) if narrow else None
    o_ref = rest.pop(0)
    on_ref = rest.pop(0) if narrow else None
    wq_ref = rest.pop(0) if emit else None
    (h_scr,) = rest
    w = _weight_tile(w_ref, wq_ref)

    @pl.when(pl.program_id(1) == 0)
    def _():
        for rows, h in _norm_mod_chunks(x_ref, nw_ref, sc_ref, sh_ref):
            h_scr[rows, :] = h
            o_ref[rows, :] = jnp.dot(h, w, preferred_element_type=F32)
            if narrow:
                on_ref[rows, :] = jnp.dot(h, wn_ref[0], preferred_element_type=F32)

    @pl.when(pl.program_id(1) > 0)
    def _():
        o_ref[...] = jnp.dot(h_scr[...], w, preferred_element_type=F32)


def _mm_norm(x3, nw, mod, layer, k_sc, k_sh, wop, geo, tn, emit=False, narrow=None):
    _, n_i, tm = _tiles(geo)
    n = wop[0].shape[2]
    in_specs = [
        _x_spec(geo, D_MODEL, lambda j: 0),
        pl.BlockSpec((1, D_MODEL), lambda i, j: (0, 0)),
        _mod_spec(geo, layer, D_MODEL, lambda j: k_sc),
        _mod_spec(geo, layer, D_MODEL, lambda j: k_sh),
        _weight_spec(wop, D_MODEL, tn),
    ]
    args = [x3, nw.reshape(1, D_MODEL), mod, mod, wop[0]]
    out_specs = [pl.BlockSpec((tm, tn), lambda i, j: (i, j))]
    out_shape = [jax.ShapeDtypeStruct((geo.n_seq * geo.seq_len, n), F32)]
    if narrow is not None:
        n_narrow = narrow[0].shape[2]
        in_specs.append(pl.BlockSpec((1, D_MODEL, n_narrow), lambda i, j: (narrow[1], 0, 0)))
        args.append(narrow[0])
        out_specs.append(pl.BlockSpec((tm, n_narrow), lambda i, j: (i, 0)))
        out_shape.append(jax.ShapeDtypeStruct((geo.n_seq * geo.seq_len, n_narrow), F32))
    if emit:
        assert n_i == 1
        spec, shape = _emit_spec_shape(D_MODEL, n, tn)
        out_specs.append(spec)
        out_shape.append(shape)
    outs = list(pl.pallas_call(
        functools.partial(_mm_norm_kernel, emit=emit, narrow=narrow is not None),
        grid=(n_i, n // tn),
        in_specs=in_specs,
        out_specs=out_specs,
        out_shape=out_shape,
        scratch_shapes=[pltpu.VMEM((tm, D_MODEL), MXU_DTYPE)],
        compiler_params=_cparams(("parallel", "arbitrary")),
        name="mm_norm",
    )(*args))
    out = outs.pop(0)
    out_narrow = outs.pop(0) if narrow is not None else None
    return out, out_narrow, (outs.pop(0) if emit else None)


def _mm_res_kernel(*refs, n_lhs, emit):
    a_refs, w_refs = refs[:n_lhs], refs[n_lhs:2 * n_lhs]
    xres_ref, gate_ref, o_ref = refs[2 * n_lhs:2 * n_lhs + 3]
    wq_refs = refs[2 * n_lhs + 3:] if emit else (None,) * n_lhs
    acc = None
    for a_ref, w_ref, wq_ref in zip(a_refs, w_refs, wq_refs):
        d = jnp.dot(a_ref[...], _weight_tile(w_ref, wq_ref), preferred_element_type=F32)
        acc = d if acc is None else acc + d
    o_ref[...] = xres_ref[...] + gate_ref[0] * acc.reshape(o_ref.shape)


def _mm_res(a_list, wops, x3, mod, layer, k_gate, geo, tn, emit=False):
    _, n_i, tm = _tiles(geo)
    per = D_MODEL // tn
    in_specs = [pl.BlockSpec((tm, a.shape[1]), lambda i, j: (i, 0)) for a in a_list]
    in_specs += [_weight_spec(wop, a.shape[1], tn) for wop, a in zip(wops, a_list)]
    in_specs += [_x_spec(geo, tn, lambda j: j),
                 _mod_spec(geo, layer, tn, lambda j: k_gate * per + j)]
    out_specs = [_x_spec(geo, tn, lambda j: j)]
    out_shape = [jax.ShapeDtypeStruct(x3.shape, F32)]
    if emit:
        assert n_i == 1
        for a in a_list:
            spec, shape = _emit_spec_shape(a.shape[1], D_MODEL, tn)
            out_specs.append(spec)
            out_shape.append(shape)
    outs = pl.pallas_call(
        functools.partial(_mm_res_kernel, n_lhs=len(a_list), emit=emit),
        grid=(n_i, per),
        in_specs=in_specs,
        out_specs=out_specs,
        out_shape=out_shape,
        compiler_params=_cparams(("parallel", "parallel")),
        name="mm_res",
    )(*a_list, *[wop[0] for wop in wops], x3, mod)
    return outs[0], (list(outs[1:]) if emit else None)


def _ffn_in_kernel(x_ref, nw_ref, sc_ref, sh_ref, wg_ref, wu_ref, cs_ref, cw_ref, cb_ref,
                   a_ref, cso_ref, *rest, nrt, valid, emit):
    wgq_ref, wuq_ref, h_scr, tail_scr = rest if emit else (None, None) + rest
    i, j = pl.program_id(0), pl.program_id(1)

    seq_blk, row_blk, tn = x_ref.shape[0], x_ref.shape[1], a_ref.shape[1]
    w_g, w_u = _weight_tile(wg_ref, wgq_ref), _weight_tile(wu_ref, wuq_ref)

    def conv_gate(g, u):
        g, u = g.reshape(seq_blk, row_blk, tn), u.reshape(seq_blk, row_blk, tn)
        prev = cs_ref[0]
        if nrt > 1:
            prev = jnp.where(i % nrt == 0, prev, tail_scr[j])
        t2, t1 = prev[:, 0:1, :], prev[:, 1:2, :]
        w = cw_ref[...]
        y = (cb_ref[...] + w[0:1] * _shifted(g, [t2, t1], 2, axis=1)
             + w[1:2] * _shifted(g, [t2, t1], 1, axis=1) + w[2:3] * g)
        a_ref[...] = (y * _sigmoid(y) * u).reshape(a_ref.shape).astype(a_ref.dtype)
        new_tail = g[:, valid - 2:valid, :]
        if nrt > 1:
            tail_scr[j] = new_tail
        cso_ref[0] = new_tail

    @pl.when(j == 0)
    def _():
        gs, us = [], []
        for rows, h in _norm_mod_chunks(x_ref, nw_ref, sc_ref, sh_ref):
            h_scr[rows, :] = h
            gs.append(jnp.dot(h, w_g, preferred_element_type=F32))
            us.append(jnp.dot(h, w_u, preferred_element_type=F32))
        conv_gate(jnp.concatenate(gs, axis=0), jnp.concatenate(us, axis=0))

    @pl.when(j > 0)
    def _():
        h = h_scr[...]
        conv_gate(jnp.dot(h, w_g, preferred_element_type=F32),
                  jnp.dot(h, w_u, preferred_element_type=F32))


def _ffn_in(x3, nw, mod, layer, wop_g, wop_u, conv_state, conv_w, conv_b, geo, tn, emit=False,
            stacked_prev=None):
    nrt, n_i, tm = _tiles(geo)
    n_j = D_FF // tn
    valid = geo.row_blk if geo.valid == geo.chunk else geo.valid
    tail_shape = (n_j, geo.seq_blk, CONV_K_F - 1, tn) if nrt > 1 else (1, 1, CONV_K_F - 1, 128)
    cs_spec = pl.BlockSpec((1, geo.seq_blk, CONV_K_F - 1, tn), lambda i, j: (layer, i // nrt, 0, j))
    stacked = n_i == 1
    emit_specs, emit_shapes = [], []
    if emit:
        assert n_i == 1
        for _ in range(2):
            spec, shape = _emit_spec_shape(D_MODEL, D_FF, tn)
            emit_specs.append(spec)
            emit_shapes.append(shape)
    kernel_fn = functools.partial(_ffn_in_kernel, nrt=nrt, valid=valid, emit=emit)
    in_specs = [
        _x_spec(geo, D_MODEL, lambda j: 0),
        pl.BlockSpec((1, D_MODEL), lambda i, j: (0, 0)),
        _mod_spec(geo, layer, D_MODEL, lambda j: 4),
        _mod_spec(geo, layer, D_MODEL, lambda j: 3),
        _weight_spec(wop_g, D_MODEL, tn),
        _weight_spec(wop_u, D_MODEL, tn),
        cs_spec,
        pl.BlockSpec((CONV_K_F, tn), lambda i, j: (0, j)),
        pl.BlockSpec((1, tn), lambda i, j: (0, j)),
    ]
    args = [x3, nw.reshape(1, D_MODEL), mod, mod, wop_g[0], wop_u[0], conv_state, conv_w,
            conv_b.reshape(1, D_FF)]
    aliases = {}
    if stacked:
        tails_spec = pl.BlockSpec((1, geo.seq_blk, CONV_K_F - 1, tn), lambda i, j: (layer, 0, 0, j))
        tails_shape = (DEPTH, geo.n_seq, CONV_K_F - 1, D_FF)
        if stacked_prev is not None:
            kernel_fn = _drop_ref(kernel_fn, len(in_specs))
            aliases[len(in_specs)] = 1
            in_specs.append(pl.BlockSpec(memory_space=pl.ANY))
            args.append(stacked_prev)
    else:
        tails_spec = pl.BlockSpec((1, geo.seq_blk, CONV_K_F - 1, tn), lambda i, j: (i, 0, 0, j))
        tails_shape = (n_i, geo.seq_blk, CONV_K_F - 1, D_FF)
    act, tails, *emitted = pl.pallas_call(
        kernel_fn,
        grid=(n_i, n_j),
        in_specs=in_specs,
        out_specs=[pl.BlockSpec((tm, tn), lambda i, j: (i, j)), tails_spec] + emit_specs,
        out_shape=[jax.ShapeDtypeStruct((geo.n_seq * geo.seq_len, D_FF), MXU_DTYPE),
                   jax.ShapeDtypeStruct(tails_shape, F32)] + emit_shapes,
        scratch_shapes=[pltpu.VMEM((tm, D_MODEL), MXU_DTYPE),
                        pltpu.VMEM(tail_shape, F32)],
        input_output_aliases=aliases,
        compiler_params=_cparams(("arbitrary", "arbitrary")),
        name="ffn_in",
    )(*args)
    if not stacked:
        last = tails.reshape(n_i // nrt, nrt, geo.seq_blk, CONV_K_F - 1, D_FF)[:, nrt - 1]
        tails = last.reshape(1, geo.n_seq, CONV_K_F - 1, D_FF)
    return act, tails, (emitted if emit else None)


def _norm_kernel(x_ref, nw_ref, o_ref):
    o_ref[...] = _rmsnorm_rows(x_ref[...]) * nw_ref[...]


def _final_norm(x3, nw, geo):
    _, n_i, _ = _tiles(geo)
    return pl.pallas_call(
        _norm_kernel,
        grid=(n_i, 1),
        in_specs=[_x_spec(geo, D_MODEL, lambda j: 0),
                  pl.BlockSpec((1, D_MODEL), lambda i, j: (0, 0))],
        out_specs=_x_spec(geo, D_MODEL, lambda j: 0),
        out_shape=jax.ShapeDtypeStruct(x3.shape, F32),
        compiler_params=_cparams(("parallel", "arbitrary")),
        name="final_norm",
    )(x3, nw.reshape(1, D_MODEL))


def _state_spec(shape, layer, geo):
    rest = tuple(shape[2:])
    zeros = (0,) * len(rest)
    return pl.BlockSpec((1, geo.scan_seqs) + rest, lambda i, c: (layer, i) + zeros)


def _row_spec(geo, width, col):
    return pl.BlockSpec((geo.scan_seqs, geo.chunk, width), lambda i, c: (i, c, col))


def _rows3(a, geo):
    return a.reshape(geo.n_seq, geo.seq_len, a.shape[-1])


def _gates_t_spec(geo):
    return pl.BlockSpec((geo.scan_seqs, 1, GATE_W, geo.chunk), lambda i, c: (i, c, 0, 0))


def _gates_t(gates, geo):
    nc = geo.seq_len // geo.chunk
    return gates.reshape(geo.n_seq, nc, geo.chunk, GATE_W).transpose(0, 1, 3, 2)


ROWS, LEAD, STATE, CONST = "rows", "lead", "state", "const"


def _per_sequence(body, geo, kinds, interleave):
    def view(ref, kind, k):
        if kind == ROWS:
            return ref.at[k]
        if kind == LEAD:
            return ref.at[pl.ds(k, 1)]
        if kind == STATE:
            return ref.at[:, pl.ds(k, 1)]
        return ref

    def wrapped(*refs):
        running = [body(*[view(r, kind, k) for r, kind in zip(refs, kinds)])
                   for k in range(geo.scan_seqs)]
        if not interleave:
            for gen in running:
                for _ in gen:
                    pass
            return
        while running:
            still = []
            for gen in running:
                if next(gen, StopIteration) is not StopIteration:
                    still.append(gen)
            running = still
    return wrapped


def _seq_call(kernel_fn, geo, name, in_specs, args, out_specs, out_shapes, scratch, kinds,
              stacked_prev, interleave=True):
    nc = geo.seq_len // geo.chunk
    kernel_fn = _per_sequence(kernel_fn, geo, kinds, interleave)
    in_specs, args = list(in_specs), list(args)
    aliases = {}
    assert len(stacked_prev) <= 1
    for out_idx, arr in stacked_prev.items():
        kernel_fn = _drop_ref(kernel_fn, len(in_specs))
        aliases[len(in_specs)] = out_idx
        in_specs.append(pl.BlockSpec(memory_space=pl.ANY))
        args.append(arr)
    return pl.pallas_call(
        kernel_fn,
        grid=(geo.n_seq // geo.scan_seqs, nc),
        in_specs=in_specs,
        out_specs=out_specs,
        out_shape=out_shapes,
        scratch_shapes=scratch,
        input_output_aliases=aliases,
        compiler_params=_cparams(("parallel", "arbitrary")),
        name=name,
    )(*args)


def _ssd_kernel(z_ref, xbc_ref, gc_ref, gr_ref, cs_ref, s0_ref, cw_ref, cb_ref,
                dtb_r_ref, dtb_c_ref, al_r_ref, al_c_ref, dsk_ref, nw_ref, exp_ref,
                y_ref, cso_ref, so_ref, tail_scr, *, lc, lv, single_chunk):
    if single_chunk:
        s_ref = s0_ref
        cs = cs_ref[0, 0]
        tails = [cs[0:1, :], cs[1:2, :], cs[2:3, :]]
    else:
        @pl.when(pl.program_id(1) == 0)
        def _():
            tail_scr[0, 5:8, :] = cs_ref[0, 0]
            so_ref[0, 0] = s0_ref[0, 0]

        s_ref = so_ref
        tails = [tail_scr[0, 5:6, :], tail_scr[0, 6:7, :], tail_scr[0, 7:8, :]]

    x = xbc_ref[...]
    w = cw_ref[0]
    xc = (cb_ref[0] + w[0:1] * _shifted(x, tails, 3) + w[1:2] * _shifted(x, tails, 2)
          + w[2:3] * _shifted(x, tails, 1) + w[3:4] * x)
    xc = xc * _sigmoid(xc)
    new_tail = x[lv - 3:lv, :]
    if not single_chunk:
        tail_scr[0, 5:8, :] = new_tail
    cso_ref[0, 0] = new_tail
    xa = xc[:, :D_A]
    bm = xc[:, D_A:D_A + G_A * N_A]
    cm = xc[:, D_A + G_A * N_A:]

    dt_c = _softplus(gc_ref[:, 0:H_A] + dtb_r_ref[...])
    dt_r = _softplus(gr_ref[0, 0:H_A, :] + dtb_c_ref[...])
    if lv < lc:
        dt_c = jnp.where(lax.broadcasted_iota(jnp.int32, dt_c.shape, 0) < lv, dt_c, 0.0)
        dt_r = jnp.where(lax.broadcasted_iota(jnp.int32, dt_r.shape, 1) < lv, dt_r, 0.0)
    lower, upper, mask = _tri(lc)
    cum_c = _select_dot(lower, dt_c * (-jnp.exp(al_r_ref[...])), 'a')
    cum_r = _select_dot(dt_r * (-jnp.exp(al_c_ref[...])), upper, 'b')
    cum_last = cum_c[lc - 1:lc, :]
    expand = exp_ref[...]
    ecum_x = _select_dot(jnp.exp(cum_c), expand, 'b')
    tail_x = _select_dot(jnp.exp(cum_last - cum_c) * dt_c, expand, 'b')

    hg = H_A // G_A
    gw = hg * P_A
    groups = range(G_A)
    gsl = lambda g: slice(g * gw, (g + 1) * gw)
    cg = [cm[:, g * N_A:(g + 1) * N_A] for g in groups]
    bg = [bm[:, g * N_A:(g + 1) * N_A] for g in groups]
    yield
    cb_ts = [lax.dot_general(cg[g], bg[g], _NT, preferred_element_type=F32) for g in groups]
    y_inter = [lax.dot_general(cg[g], s_ref[0, 0, g], _NT, preferred_element_type=F32)
               for g in groups]
    upd = [lax.dot_general(xa[:, gsl(g)] * tail_x[:, gsl(g)], bg[g], _TN,
                           preferred_element_type=F32) for g in groups]
    yield
    w_ts = [cb_ts[h // hg] * dt_r[h:h + 1, :]
            * jnp.exp(jnp.where(mask, cum_c[:, h:h + 1] - cum_r[h:h + 1, :], -jnp.inf))
            for h in range(H_A)]
    yield
    pieces = []
    for j in range(H_A // 2):
        xp = xa[:, j * 128:(j + 1) * 128]
        lane = lax.broadcasted_iota(jnp.int32, xp.shape, 1)
        pieces.append(
            jnp.dot(w_ts[2 * j], jnp.where(lane < P_A, xp, 0.0), preferred_element_type=F32)
            + jnp.dot(w_ts[2 * j + 1], jnp.where(lane >= P_A, xp, 0.0),
                      preferred_element_type=F32))
    yield
    y = (jnp.concatenate(pieces, axis=1) + jnp.concatenate(y_inter, axis=1) * ecum_x
         + dsk_ref[...] * xa)
    for g in groups:
        for hh in range(hg):
            h = g * hg + hh
            rows = slice(hh * P_A, (hh + 1) * P_A)
            so_ref[0, 0, g, rows, :] = (s_ref[0, 0, g, rows, :] * jnp.exp(cum_r[h:h + 1, lc - 1:lc])
                                        + upd[g][rows, :])

    z = z_ref[...]
    y = y * (z * _sigmoid(z))
    y = jnp.concatenate([_rmsnorm_rows(y[:, g * gw:(g + 1) * gw]) for g in range(G_A)], axis=1)
    y_ref[...] = (y * nw_ref[...]).astype(y_ref.dtype)


def _ssd(p1, gates, conv_state, ssd_state, prm, e, geo, prev_state_out):
    gates_t = _gates_t(gates, geo)
    b = geo.n_seq
    const2 = lambda shape: pl.BlockSpec(shape, lambda i, c: (0, 0))
    n_even = ssd_state.shape[0]
    s5 = ssd_state.reshape(n_even, b, G_A, (H_A // G_A) * P_A, N_A)
    expand = (jnp.arange(D_A)[None, :] // P_A == jnp.arange(H_A)[:, None]).astype(F32)
    cso_shape = (1,) + conv_state.shape[1:]
    y, cso, so = _seq_call(
        functools.partial(_ssd_kernel, lc=geo.chunk, lv=geo.valid,
                          single_chunk=geo.seq_len == geo.chunk), geo, "ssd",
        in_specs=[
            _row_spec(geo, D_A, 0),
            _row_spec(geo, CONV_DIM_A, 2),
            _row_spec(geo, GATE_W, 0),
            _gates_t_spec(geo),
            _state_spec(conv_state.shape, e, geo),
            _state_spec(s5.shape, e, geo),
            pl.BlockSpec((1, CONV_K_A, CONV_DIM_A), lambda i, c: (e, 0, 0)),
            pl.BlockSpec((1, 1, CONV_DIM_A), lambda i, c: (e, 0, 0)),
            const2((1, H_A)), const2((H_A, 1)), const2((1, H_A)), const2((H_A, 1)),
            const2((1, D_A)), const2((1, D_A)), const2((H_A, D_A)),
        ],
        args=(_rows3(p1, geo), _rows3(p1, geo), _rows3(gates, geo), gates_t, conv_state, s5,
              prm['conv_w_a'], prm['conv_b_a'].reshape(n_even, 1, CONV_DIM_A),
              prm['dt_bias'][e].reshape(1, H_A), prm['dt_bias'][e].reshape(H_A, 1),
              prm['a_log'][e].reshape(1, H_A), prm['a_log'][e].reshape(H_A, 1),
              jnp.repeat(prm['d_skip'][e], P_A).reshape(1, D_A),
              prm['norm_a'][e].reshape(1, D_A), expand),
        out_specs=[_row_spec(geo, D_A, 0), _state_spec(cso_shape, 0, geo),
                   _state_spec(s5.shape, e, geo)],
        out_shapes=[
            jax.ShapeDtypeStruct((b, geo.seq_len, D_A), MXU_DTYPE),
            jax.ShapeDtypeStruct(cso_shape, F32),
            jax.ShapeDtypeStruct(s5.shape, F32),
        ],
        scratch=[pltpu.VMEM((geo.scan_seqs, SUBLANES, CONV_DIM_A), F32)],
        kinds=[ROWS] * 4 + [STATE, STATE] + [CONST] * 9 + [ROWS, STATE, STATE, LEAD],
        stacked_prev={} if prev_state_out is None else {2: prev_state_out},
    )
    return y.reshape(b * geo.seq_len, D_A), cso, so


def _mlstm_kernel(q_ref, k_ref, v_ref, og_ref, gc_ref, gr_ref, c0_ref, n0_ref, m0_ref,
                  ib_r_ref, ib_c_ref, fb_r_ref, fb_c_ref, nw_ref,
                  h_ref, co_ref, no_ref, mo_ref, *, lc, lv, single_chunk):
    if single_chunk:
        c_ref, n_ref, m_ref = c0_ref, n0_ref, m0_ref
    else:
        @pl.when(pl.program_id(1) == 0)
        def _():
            co_ref[0, 0] = c0_ref[0, 0]
            no_ref[0, 0] = n0_ref[0, 0]
            mo_ref[0, 0] = m0_ref[0, 0]

        c_ref, n_ref, m_ref = co_ref, no_ref, mo_ref
    m_old, n_old = m_ref[0, 0], n_ref[0, 0]

    i0, f0 = H_A, H_A + H_B
    li_c = gc_ref[:, i0:i0 + H_B] + ib_r_ref[...]
    lf_c = _log_sigmoid(gc_ref[:, f0:f0 + H_B] + fb_r_ref[...])
    li_r = gr_ref[0, i0:i0 + H_B, :] + ib_c_ref[...]
    lf_r = _log_sigmoid(gr_ref[0, f0:f0 + H_B, :] + fb_c_ref[...])
    if lv < lc:
        vc = lax.broadcasted_iota(jnp.int32, li_c.shape, 0) < lv
        vr = lax.broadcasted_iota(jnp.int32, li_r.shape, 1) < lv
        li_c, lf_c = jnp.where(vc, li_c, NEG_BIG), jnp.where(vc, lf_c, 0.0)
        li_r, lf_r = jnp.where(vr, li_r, NEG_BIG), jnp.where(vr, lf_r, 0.0)
    lower, upper, mask = _tri(lc)
    bc_c = jnp.dot(lower, lf_c, precision=HI)
    bc_r = jnp.dot(lf_r, upper, precision=HI)

    heads = range(H_B)
    q = [q_ref[:, h * DK_B:(h + 1) * DK_B] * (DK_B ** -0.5) for h in heads]
    k = [k_ref[:, h * DK_B:(h + 1) * DK_B] for h in heads]
    v = [v_ref[:, h * DV_B:(h + 1) * DV_B] for h in heads]
    yield
    qk = [lax.dot_general(q[h], k[h], _NT, preferred_element_type=F32) for h in heads]
    yield
    m_t, w_in, w_ts = [], [], []
    for h in heads:
        bcc = bc_c[:, h:h + 1]
        dmat = jnp.where(mask, bcc - bc_r[h:h + 1, :] + li_r[h:h + 1, :], -jnp.inf)
        inter = bcc + m_old[:, h:h + 1]
        m_t.append(jnp.maximum(inter, jnp.max(dmat, axis=1, keepdims=True)))
        w_in.append(jnp.exp(inter - m_t[h]))
        w_ts.append(jnp.exp(dmat - m_t[h]) * qk[h])
    yield
    q_c = [jnp.dot(q[h], c_ref[0, 0, h], preferred_element_type=F32) for h in heads]
    wv = [jnp.dot(w_ts[h], v[h], preferred_element_type=F32) for h in heads]
    yield
    hs = []
    for h in heads:
        num = wv[h] + w_in[h] * q_c[h]
        den = (jnp.sum(w_ts[h], axis=1, keepdims=True)
               + w_in[h] * jnp.sum(q[h] * n_old[h:h + 1, :], axis=1, keepdims=True))
        hs.append(_rmsnorm_rows(num / jnp.maximum(jnp.abs(den), jnp.exp(-m_t[h]))))
    lane_h = lax.broadcasted_iota(jnp.int32, (1, H_B), 1)
    m_out = jnp.zeros((1, H_B), F32)
    ks, w_c = [], []
    for h in heads:
        bcc = bc_c[:, h:h + 1]
        m_new = m_t[h][lv - 1:lv, :]
        bc_last = bcc[lc - 1:lc, :]
        ks.append(k[h] * jnp.exp(bc_last - bcc + li_c[:, h:h + 1] - m_new))
        w_c.append(jnp.exp(bc_last + m_old[:, h:h + 1] - m_new))
        m_out = jnp.where(lane_h == h, m_new, m_out)
    yield
    kv = [lax.dot_general(ks[h], v[h], _TN, preferred_element_type=F32) for h in heads]
    yield
    for h in heads:
        co_ref[0, 0, h] = w_c[h] * c_ref[0, 0, h] + kv[h]
        no_ref[0, 0, h:h + 1, :] = w_c[h] * n_old[h:h + 1, :] + jnp.sum(ks[h], axis=0, keepdims=True)
    mo_ref[0, 0] = m_out
    hn = jnp.concatenate(hs, axis=1) * nw_ref[...]
    h_ref[...] = (hn * _sigmoid(og_ref[...])).astype(h_ref.dtype)


def _mlstm(p1, gates, c_state, n_state, m_state, prm, e, geo, prev_state_out):
    geo = geo._replace(scan_seqs=geo.mlstm_seqs, chunk=geo.mlstm_chunk,
                       valid=geo.mlstm_chunk if geo.valid == geo.chunk else geo.valid)
    gates_t = _gates_t(gates, geo)
    b = geo.n_seq
    const2 = lambda shape: pl.BlockSpec(shape, lambda i, c: (0, 0))
    qk_w = H_B * DK_B
    m4 = m_state.reshape(m_state.shape[0], b, 1, H_B)
    one = lambda shape: (1,) + tuple(shape[1:])
    h, co, no, mo = _seq_call(
        functools.partial(_mlstm_kernel, lc=geo.chunk, lv=geo.valid,
                          single_chunk=geo.seq_len == geo.chunk), geo, "mlstm",
        in_specs=[
            _row_spec(geo, qk_w, 9), _row_spec(geo, qk_w, 10),
            _row_spec(geo, D_B, 1), _row_spec(geo, D_B, 2),
            _row_spec(geo, GATE_W, 0),
            _gates_t_spec(geo),
            _state_spec(c_state.shape, e, geo), _state_spec(n_state.shape, e, geo),
            _state_spec(m4.shape, e, geo),
            const2((1, H_B)), const2((H_B, 1)), const2((1, H_B)), const2((H_B, 1)),
            const2((1, D_B)),
        ],
        args=(_rows3(p1, geo),) * 4 + (_rows3(gates, geo), gates_t, c_state, n_state, m4,
              prm['i_bias'][e].reshape(1, H_B), prm['i_bias'][e].reshape(H_B, 1),
              prm['f_bias'][e].reshape(1, H_B), prm['f_bias'][e].reshape(H_B, 1),
              prm['norm_b'][e].reshape(1, D_B)),
        out_specs=[_row_spec(geo, D_B, 0), _state_spec(c_state.shape, e, geo),
                   _state_spec(one(n_state.shape), 0, geo), _state_spec(one(m4.shape), 0, geo)],
        out_shapes=[
            jax.ShapeDtypeStruct((b, geo.seq_len, D_B), MXU_DTYPE),
            jax.ShapeDtypeStruct(c_state.shape, F32),
            jax.ShapeDtypeStruct(one(n_state.shape), F32),
            jax.ShapeDtypeStruct(one(m4.shape), F32),
        ],
        scratch=[],
        kinds=[ROWS] * 6 + [STATE] * 3 + [CONST] * 5 + [ROWS] + [STATE] * 3,
        stacked_prev={} if prev_state_out is None else {1: prev_state_out},
        interleave=geo.seq_len != geo.chunk,
    )
    return h.reshape(b * geo.seq_len, D_B), co, no, mo.reshape(1, b, H_B)


def _hgrn_kernel(q_ref, f_ref, i_ref, g_ref, lbl_ref, s0_ref, nw_ref, o_ref, so_ref,
                 *, lc, lv, layer_o, bs, single_chunk):
    if single_chunk:
        s_ref = s0_ref
    else:
        @pl.when(pl.program_id(1) == 0)
        def _():
            so_ref[0, 0] = s0_ref[0, 0]

        s_ref = so_ref

    lbl = lbl_ref[...]
    ex = jnp.exp(lbl - jnp.max(lbl, axis=0, keepdims=True))
    sm = ex / jnp.sum(ex, axis=0, keepdims=True)
    lb_all = [sm[0:1, :]]
    for r in range(1, lbl.shape[0]):
        lb_all.append(lb_all[-1] + sm[r:r + 1, :])
    lb = lb_all[layer_o] - lb_all[0]

    fx = f_ref[...]
    e1 = jnp.exp(-jnp.abs(fx))
    log_sig = jnp.minimum(fx, 0.0) - jnp.log1p(e1)
    la = jnp.log(lb)
    lb_ = jnp.log1p(-lb) + log_sig
    logf = jnp.maximum(la, lb_) + jnp.log1p(jnp.exp(-jnp.abs(la - lb_)))
    kk = (1.0 - lb) * (jnp.where(fx >= 0.0, e1, 1.0) / (1.0 + e1))
    if lv < lc:
        valid = lax.broadcasted_iota(jnp.int32, fx.shape, 0) < lv
        logf = jnp.where(valid, logf, 0.0)
        kk = jnp.where(valid, kk, 0.0)

    nb = lc // bs
    r_i = lax.broadcasted_iota(jnp.int32, (lc, lc), 0)
    c_i = lax.broadcasted_iota(jnp.int32, (lc, lc), 1)
    sh = int(math.log2(bs))
    blk_lower = ((c_i <= r_i) & ((c_i >> sh) == (r_i >> sh))).astype(F32)
    gw = jnp.dot(blk_lower, logf, precision=HI)
    q = q_ref[...]
    v = i_ref[...]
    blk = lambda a, i: a[i * bs:(i + 1) * bs, :]
    tots = [gw[(i + 1) * bs - 1:(i + 1) * bs, :] for i in range(nb)]
    before = [jnp.zeros_like(tots[0])]
    for i in range(nb):
        before.append(before[-1] + tots[i])
    g_tot = before[nb]
    qt = q * jnp.exp(gw)
    kt = [blk(kk, j) * jnp.exp(tots[j] - blk(gw, j)) for j in range(nb)]
    q_in = jnp.concatenate([blk(qt, i) * jnp.exp(before[i]) for i in range(nb)], axis=0)
    k_out = jnp.concatenate([kt[j] * jnp.exp(g_tot - before[j + 1]) for j in range(nb)], axis=0)

    heads = range(H_C)
    hsl = lambda h: slice(h * DK_C, (h + 1) * DK_C)
    yield
    o_inter = [jnp.dot(q_in[:, hsl(h)], s_ref[0, 0, h], preferred_element_type=F32)
               for h in heads]
    kv = [lax.dot_general(k_out[:, hsl(h)], v[:, hsl(h)], _TN, preferred_element_type=F32)
          for h in heads]
    att_off = [None]
    for i in range(1, nb):
        k_hat = jnp.concatenate(
            [kt[j] if j == i - 1 else kt[j] * jnp.exp(before[i] - before[j + 1])
             for j in range(i)] + [jnp.zeros((lc - i * bs, D_C), F32)], axis=0)
        q_ti = blk(qt, i)
        att_off.append([lax.dot_general(q_ti[:, hsl(h)], k_hat[:, hsl(h)], _NT,
                                        preferred_element_type=F32) for h in heads])
    yield
    gw2 = gw * LOG2E
    ck = jnp.log2(kk) - gw2
    n_t = bs // SUBLANES
    lane_s = lax.broadcasted_iota(jnp.int32, (SUBLANES, lc), 1)
    row_t = lax.broadcasted_iota(jnp.int32, (bs, lc), 0)
    col_s = lax.broadcasted_iota(jnp.int32, (bs, lc), 1)
    att = []
    for i in range(nb):
        g_i, q_i, ck_i = blk(gw2, i), blk(q, i), blk(ck, i)
        att_d = [[jnp.zeros((SUBLANES, lc), F32) for _ in range(n_t)] for _ in heads]
        for s in range(bs):
            t0 = s // SUBLANES
            p = q_i[t0 * SUBLANES:, :] * jnp.exp2(g_i[t0 * SUBLANES:, :] + ck_i[s:s + 1, :])
            for h in heads:
                a = jnp.sum(p[:, hsl(h)], axis=1, keepdims=True)
                for tt in range(t0, n_t):
                    a_t = a[(tt - t0) * SUBLANES:(tt - t0 + 1) * SUBLANES, :]
                    att_d[h][tt] = jnp.where(lane_s == i * bs + s, a_t, att_d[h][tt])
        causal = (col_s - i * bs) <= row_t
        att_i = []
        for h in heads:
            a = att_d[h][0] if n_t == 1 else jnp.concatenate(att_d[h], axis=0)
            a = jnp.where(causal, a, 0.0)
            att_i.append(a if i == 0 else a + att_off[i][h])
        att.append(att_i)
    yield
    o_intra = [[jnp.dot(att[i][h], v[:, hsl(h)], preferred_element_type=F32) for h in heads]
               for i in range(nb)]
    yield
    outs = []
    for h in heads:
        o_h = o_intra[0][h] if nb == 1 else jnp.concatenate([o_intra[i][h] for i in range(nb)],
                                                             axis=0)
        outs.append(_rmsnorm_rows(o_h + o_inter[h]))
        dec_col = jnp.transpose(jnp.broadcast_to(jnp.exp(g_tot[:, hsl(h)]), (DK_C, DK_C)))
        so_ref[0, 0, h] = dec_col * s_ref[0, 0, h] + kv[h]
    gate = g_ref[...]
    o_ref[...] = (jnp.concatenate(outs, axis=1) * nw_ref[...]
                  * (gate * _sigmoid(gate))).astype(o_ref.dtype)


def _hgrn(p, state, prm, o, geo, prev_state_out):
    geo = geo._replace(scan_seqs=geo.hgrn_seqs)
    n_odd = prm['lb_logits'].shape[0]
    out, new_state = _seq_call(
        functools.partial(_hgrn_kernel, lc=geo.chunk, lv=geo.valid, layer_o=o,
                          bs=min(16, geo.chunk), single_chunk=geo.seq_len == geo.chunk), geo, "hgrn",
        in_specs=[
            _row_spec(geo, D_C, 0), _row_spec(geo, D_C, 1), _row_spec(geo, D_C, 2),
            _row_spec(geo, D_C, 3),
            pl.BlockSpec((n_odd, D_C), lambda i, c: (0, 0)),
            _state_spec(state.shape, o, geo),
            pl.BlockSpec((1, D_C), lambda i, c: (0, 0)),
        ],
        args=(_rows3(p, geo),) * 4 + (prm['lb_logits'], state, prm['norm_c'][o].reshape(1, D_C)),
        out_specs=[_row_spec(geo, D_C, 0), _state_spec(state.shape, o, geo)],
        out_shapes=[
            jax.ShapeDtypeStruct((geo.n_seq, geo.seq_len, D_C), MXU_DTYPE),
            jax.ShapeDtypeStruct(state.shape, F32),
        ],
        scratch=[],
        kinds=[ROWS] * 4 + [CONST, STATE, CONST, ROWS, STATE],
        stacked_prev={} if prev_state_out is None else {1: prev_state_out},
    )
    return out.reshape(geo.n_seq * geo.seq_len, D_C), new_state


def _in_ab_column_groups():
    a0, a1, a2 = D_A, D_A + CONV_DIM_A, D_A + CONV_DIM_A + H_A
    q1 = a2 + H_B * DK_B
    k1 = q1 + H_B * DK_B
    v1 = k1 + D_B
    o1 = v1 + D_B
    end = o1 + 2 * H_B
    return [(0, a0), (k1, v1), (v1, o1), (a0, a1), (a2, q1), (q1, k1)], [(a1, a2), (o1, end)]


def _regroup_kernel(w_ref, o_ref, g_ref):
    wide, narrow = _in_ab_column_groups()
    w = w_ref[0]
    o_ref[0] = jnp.concatenate([w[:, a:b] for a, b in wide], axis=1)
    n_gate = sum(b - a for a, b in narrow)
    pad = jnp.zeros((w.shape[0], GATE_W - n_gate), w.dtype)
    g_ref[0] = jnp.concatenate([w[:, a:b] for a, b in narrow] + [pad], axis=1)


def _regroup_in_ab(wab, rows=256):
    n_even, k, n_in = wab.shape
    wide, _ = _in_ab_column_groups()
    n_wide = sum(b - a for a, b in wide)
    return pl.pallas_call(
        _regroup_kernel,
        grid=(n_even, k // rows),
        in_specs=[pl.BlockSpec((1, rows, n_in), lambda e, r: (e, r, 0))],
        out_specs=[pl.BlockSpec((1, rows, n_wide), lambda e, r: (e, r, 0)),
                   pl.BlockSpec((1, rows, GATE_W), lambda e, r: (e, r, 0))],
        out_shape=[jax.ShapeDtypeStruct((n_even, k, n_wide), MXU_DTYPE),
                   jax.ShapeDtypeStruct((n_even, k, GATE_W), MXU_DTYPE)],
        compiler_params=_cparams(("parallel", "parallel")),
        name="regroup_in_ab",
    )(wab.astype(MXU_DTYPE))


def _prep_weights(prm):
    wab = prm['w_in_ab']
    in_ab, gates_ab = _regroup_in_ab(wab)
    n_even, n_odd = wab.shape[0], prm['w_in_c'].shape[0]
    return {
        'in_ab': [[(in_ab, e, 0)] for e in range(n_even)],
        'gates_ab': [[(gates_ab, e, 0)] for e in range(n_even)],
        'out_ab': [[(prm['w_out_ab'], e, 0), (prm['w_out_ab'], e, 1)] for e in range(n_even)],
        'in_c': [[(prm['w_in_c'], o, 0)] for o in range(n_odd)],
        'out_c': [[(prm['w_out_c'], o, 0)] for o in range(n_odd)],
        'ffn_g': [[(prm['w_ffn_g'], l, 0)] for l in range(DEPTH)],
        'ffn_u': [[(prm['w_ffn_u'], l, 0)] for l in range(DEPTH)],
        'ffn_d': [[(prm['w_ffn_d'], l, 0)] for l in range(DEPTH)],
    }


def _trunk(x3, mod, states, prm, w, geo, tn_in, tn_res, tn_ffn, tn_down, emit, rows_in=None):
    conv_a, ssd, mem_c, mem_n, mem_m, hgrn, ffn_buf = states
    n_conv, n_n, n_m, n_ffn = [], [], [], []
    new_ssd = new_c = new_hgrn = ffn_stacked = None
    wq = {name: list(per_layer) for name, per_layer in w.items()}

    def record(name, idx, emitted):
        if emitted is not None:
            wq[name][idx] = [(arr, 0, 0) for arr in emitted]

    casts = lambda wops: emit and any(wop[0].dtype != MXU_DTYPE for wop in wops)
    for layer in range(DEPTH):
        geo_in = geo if rows_in is None else geo._replace(row_blk=rows_in[layer])
        mm_norm = functools.partial(_mm_norm, x3, prm['norm_mix'][layer], mod, layer, 1, 0,
                                    geo=geo_in)
        if layer % 2 == 0:
            e = layer // 2
            p1, gates, _ = mm_norm(w['in_ab'][e][0], tn=tn_in[layer], narrow=w['gates_ab'][e][0])
            ya, cv, new_ssd = _ssd(p1, gates, conv_a, ssd, prm, e, geo, new_ssd)
            hb, new_c, nn, mmm = _mlstm(p1, gates, mem_c, mem_n, mem_m, prm, e, geo, new_c)
            n_conv.append(cv); n_n.append(nn); n_m.append(mmm)
            x3, em = _mm_res([ya, hb], w['out_ab'][e], x3, mod, layer, 2, geo, tn_res[layer],
                             emit=casts(w['out_ab'][e]))
            record('out_ab', e, em)
        else:
            o = layer // 2
            p, _, em = mm_norm(w['in_c'][o][0], tn=tn_in[layer], emit=casts(w['in_c'][o]))
            record('in_c', o, None if em is None else [em])
            oc, new_hgrn = _hgrn(p, hgrn, prm, o, geo, new_hgrn)
            x3, em = _mm_res([oc], w['out_c'][o], x3, mod, layer, 2, geo, tn_res[layer],
                             emit=casts(w['out_c'][o]))
            record('out_c', o, em)
        act, fb, em = _ffn_in(x3, prm['norm_ffn'][layer], mod, layer, w['ffn_g'][layer][0],
                              w['ffn_u'][layer][0], ffn_buf, prm['conv_w_f'][layer],
                              prm['conv_b_f'][layer], geo_in, tn_ffn,
                              emit=casts(w['ffn_g'][layer] + w['ffn_u'][layer]),
                              stacked_prev=ffn_stacked)
        if em is not None:
            record('ffn_g', layer, em[:1])
            record('ffn_u', layer, em[1:])
        if fb.shape[0] == DEPTH:
            ffn_stacked = fb
        else:
            n_ffn.append(fb)
        x3, em = _mm_res([act], w['ffn_d'][layer], x3, mod, layer, 5, geo, tn_down,
                         emit=casts(w['ffn_d'][layer]))
        record('ffn_d', layer, em)
    y = _final_norm(x3, prm['norm_f'], geo)
    cat = lambda xs: jnp.concatenate(xs, axis=0)
    return (y, cat(n_conv), new_ssd.reshape(ssd.shape), new_c, cat(n_n), cat(n_m), new_hgrn,
            ffn_stacked if ffn_stacked is not None else cat(n_ffn)), wq


def kernel(x_prompt, x_sample, c_prompt, c_sample, state_ssd_conv, state_ssd, state_mlstm_c, state_mlstm_n, state_mlstm_m, state_hgrn, state_ffn_conv, w_ada, b_ada, norm_mix, norm_ffn, w_in_ab, conv_w_a, conv_b_a, dt_bias, a_log, d_skip, norm_a, i_bias, f_bias, norm_b, w_out_ab, w_in_c, lb_logits, norm_c, w_out_c, w_ffn_g, w_ffn_u, conv_w_f, conv_b_f, w_ffn_d, norm_f):
    prm = dict(norm_mix=norm_mix, norm_ffn=norm_ffn, w_in_ab=w_in_ab, conv_w_a=conv_w_a,
               conv_b_a=conv_b_a, dt_bias=dt_bias, a_log=a_log, d_skip=d_skip, norm_a=norm_a,
               i_bias=i_bias, f_bias=f_bias, norm_b=norm_b, w_out_ab=w_out_ab, w_in_c=w_in_c,
               lb_logits=lb_logits, norm_c=norm_c, w_out_c=w_out_c, w_ffn_g=w_ffn_g,
               w_ffn_u=w_ffn_u, conv_w_f=conv_w_f, conv_b_f=conv_b_f, w_ffn_d=w_ffn_d,
               norm_f=norm_f)
    bp, lp, _ = x_prompt.shape
    bs, ls, _ = x_sample.shape
    n_even, n_odd = state_ssd.shape[0], state_hgrn.shape[0]
    w = _prep_weights(prm)

    n_c = bs + bp
    n_c_pad = -(-n_c // SUBLANES) * SUBLANES
    c_all = jnp.pad(jnp.concatenate([c_sample, c_prompt], axis=0), ((0, n_c_pad - n_c), (0, 0)))
    mod = _ada(c_all, w_ada, b_ada).reshape(DEPTH, n_c_pad, 1, 6 * D_MODEL)

    geo_s = Geo(n_seq=bs, seq_len=SUBLANES, seq_blk=bs, row_blk=SUBLANES, chunk=SUBLANES,
                valid=ls, mod_off=0, scan_seqs=8, mlstm_seqs=4, hgrn_seqs=8, mlstm_chunk=SUBLANES)
    xs = jnp.pad(x_sample, ((0, 0), (0, SUBLANES - ls), (0, 0)))
    st_s = (state_ssd_conv, state_ssd, state_mlstm_c, state_mlstm_n, state_mlstm_m, state_hgrn,
            state_ffn_conv)
    out_s, w_bf16 = _trunk(xs, mod, st_s, prm, w, geo_s, tn_in=(512,) * DEPTH,
                           tn_res=(512,) * DEPTH, tn_ffn=256, tn_down=256, emit=True)

    zeros = lambda *s: jnp.zeros(s, F32)
    st_p = (zeros(n_even, bp, CONV_K_A - 1, CONV_DIM_A), zeros(n_even, bp, H_A, P_A, N_A),
            zeros(n_even, bp, H_B, DK_B, DV_B), zeros(n_even, bp, H_B, DK_B),
            zeros(n_even, bp, H_B), zeros(n_odd, bp, H_C, DK_C, DV_C),
            zeros(DEPTH, bp, CONV_K_F - 1, D_FF))
    lc_p = math.gcd(lp, PROMPT_CHUNK)
    geo_p = Geo(n_seq=bp, seq_len=lp, seq_blk=1, row_blk=min(lp, 1024), chunk=lc_p, valid=lc_p,
                mod_off=bs, scan_seqs=1, mlstm_seqs=1, hgrn_seqs=2 if bp % 2 == 0 else 1,
                mlstm_chunk=math.gcd(lp, 2 * PROMPT_CHUNK))
    out_p, _ = _trunk(x_prompt, mod, st_p, prm, w_bf16, geo_p, tn_in=(1024,) * DEPTH,
                      tn_res=(1024,) * DEPTH, tn_ffn=FFN_TN, tn_down=FFN_DOWN_TN, emit=False)
    return (out_p[0], out_s[0][:, :ls]) + tuple(out_p[1:]) + tuple(out_s[1:])
```
